```python
import math
import jax, jax.numpy as jnp
from jax import lax
import numpy as np

D_MODEL = 1024
BATCH = 4
SEQ = 4096
DEPTH = 1

N_HEADS = 8
N_KV_HEADS = 2
HEAD_DIM = 64
GROUP = N_HEADS // N_KV_HEADS
ATTN_WIDTH = N_HEADS * HEAD_DIM
KV_WIDTH = N_KV_HEADS * HEAD_DIM
WINDOW = 128
BLOCK = 128
SSM_WIDTH = D_MODEL // 2
SSM_CH = 16
SSM_GROUPS = SSM_WIDTH // SSM_CH
SSM_STATE = 64
DT_MIN = 1e-3
DT_MAX = 1e-1
D_FF = 2816
CONV_WIDTH = 3
N_BRANCH = 2
IN_WIDTH = ATTN_WIDTH + 2 * KV_WIDTH + SSM_WIDTH + N_BRANCH * D_MODEL
EPS = 1e-6
NEG_INF = -1e30

kernel_name = "hybrid_gqa_s5_convffn_encoder"


def rmsnorm(x, g):
    xf = x.astype(jnp.float32)
    xf = xf * lax.rsqrt(jnp.mean(xf * xf, axis=-1, keepdims=True) + EPS)
    return (xf * g.astype(jnp.float32)).astype(x.dtype)


def alibi_slopes():
    h = jnp.arange(N_HEADS, dtype=jnp.float32) + 1.0
    return 2.0 ** (-8.0 * h / N_HEADS)


def windowed_gqa_sink(q, k, v, sink):
    bsz, seq = q.shape[0], q.shape[1]
    nb = seq // BLOCK
    qb = q.reshape(bsz, nb, BLOCK, N_KV_HEADS, GROUP, HEAD_DIM)
    pad = ((0, 0), (BLOCK, BLOCK), (0, 0), (0, 0))
    kp = jnp.pad(k, pad).reshape(bsz, nb + 2, BLOCK, N_KV_HEADS, HEAD_DIM)
    vp = jnp.pad(v, pad).reshape(bsz, nb + 2, BLOCK, N_KV_HEADS, HEAD_DIM)
    kb = jnp.concatenate([kp[:, :-2], kp[:, 1:-1], kp[:, 2:]], axis=2)
    vb = jnp.concatenate([vp[:, :-2], vp[:, 1:-1], vp[:, 2:]], axis=2)
    scores = jnp.einsum('bnqkgd,bnskd->bnkgqs', qb, kb).astype(jnp.float32) * (HEAD_DIM ** -0.5)
    rel = jnp.arange(BLOCK)[:, None] - (jnp.arange(3 * BLOCK)[None, :] - BLOCK)
    dist = jnp.abs(rel).astype(jnp.float32)
    key_pos = (jnp.arange(nb)[:, None] - 1) * BLOCK + jnp.arange(3 * BLOCK)[None, :]
    valid = (jnp.abs(rel) <= WINDOW)[None] & ((key_pos >= 0) & (key_pos < seq))[:, None, :]
    slopes = alibi_slopes().reshape(N_KV_HEADS, GROUP)
    scores = scores - slopes[:, :, None, None] * dist
    scores = jnp.where(valid[None, :, None, None], scores, NEG_INF)
    sink_logit = jnp.broadcast_to(
        sink.astype(jnp.float32).reshape(N_KV_HEADS, GROUP)[None, None, :, :, None, None],
        scores.shape[:-1] + (1,))
    probs = jax.nn.softmax(jnp.concatenate([scores, sink_logit], axis=-1), axis=-1)[..., :-1]
    out = jnp.einsum('bnkgqs,bnskd->bnqkgd', probs.astype(v.dtype), vb)
    return out.reshape(bsz, seq, ATTN_WIDTH)


def _scan_op(e1, e2):
    a1, b1 = e1
    a2, b2 = e2
    return a1 * a2, a2 * b1 + b2


def s5_scan(u, lam_re, lam_im, log_dt, b_re, b_im, c_re, c_im):
    dt = jnp.exp(log_dt.astype(jnp.float32))[:, None]
    lam = lax.complex(lam_re.astype(jnp.float32), lam_im.astype(jnp.float32))
    lam_bar = jnp.exp(lam * dt)
    b = lax.complex(b_re.astype(jnp.float32), b_im.astype(jnp.float32))
    b_bar = ((lam_bar - 1.0) / lam)[..., None] * b
    bu = jnp.einsum('blgc,gnc->blgn', u.astype(jnp.complex64), b_bar)
    a = jnp.broadcast_to(lam_bar, bu.shape)
    _, xs = lax.associative_scan(_scan_op, (a, bu), axis=1)
    c = lax.complex(c_re.astype(jnp.float32), c_im.astype(jnp.float32))
    return jnp.einsum('blgn,gcn->blgc', xs, c).real


def s5_bidirectional(u, lam_re_f, lam_im_f, log_dt_f, b_re_f, b_im_f, c_re_f, c_im_f,
                     lam_re_b, lam_im_b, log_dt_b, b_re_b, b_im_b, c_re_b, c_im_b,
                     ssm_d, w_glu, b_glu):
    bsz, seq, _ = u.shape
    uf = u.astype(jnp.float32)
    ug = uf.reshape(bsz, seq, SSM_GROUPS, SSM_CH)
    y_f = s5_scan(ug, lam_re_f, lam_im_f, log_dt_f, b_re_f, b_im_f, c_re_f, c_im_f)
    y_b = s5_scan(ug[:, ::-1], lam_re_b, lam_im_b, log_dt_b, b_re_b, b_im_b, c_re_b, c_im_b)[:, ::-1]
    y = (y_f + y_b).reshape(bsz, seq, SSM_WIDTH) + ssm_d.astype(jnp.float32) * uf
    y = jax.nn.gelu(y).astype(u.dtype)
    return y * jax.nn.sigmoid(y @ w_glu + b_glu)


def hybrid_layer(x, norm_mix_pre, w_in, attn_sink,
                 lam_re_f, lam_im_f, log_dt_f, b_re_f, b_im_f, c_re_f, c_im_f,
                 lam_re_b, lam_im_b, log_dt_b, b_re_b, b_im_b, c_re_b, c_im_b,
                 ssm_d, w_glu, b_glu, w_branch_attn, w_branch_ssm, w_out, norm_mix_post,
                 norm_ffn_pre, w_ffn_gate, w_ffn_up, conv_w, conv_b, w_ffn_down, norm_ffn_post):
    bsz, seq, _ = x.shape
    h = rmsnorm(x, norm_mix_pre)
    z = h @ w_in
    splits = np.cumsum([ATTN_WIDTH, KV_WIDTH, KV_WIDTH, SSM_WIDTH]).tolist()
    q, k, v, u, gates = jnp.split(z, splits, axis=-1)
    k = k.reshape(bsz, seq, N_KV_HEADS, HEAD_DIM)
    v = v.reshape(bsz, seq, N_KV_HEADS, HEAD_DIM)
    y_attn = windowed_gqa_sink(q, k, v, attn_sink) @ w_branch_attn
    y_ssm = s5_bidirectional(u, lam_re_f, lam_im_f, log_dt_f, b_re_f, b_im_f, c_re_f, c_im_f,
                             lam_re_b, lam_im_b, log_dt_b, b_re_b, b_im_b, c_re_b, c_im_b,
                             ssm_d, w_glu, b_glu) @ w_branch_ssm
    g = jax.nn.sigmoid(gates.reshape(bsz, seq, N_BRANCH, D_MODEL))
    merged = g[:, :, 0] * y_attn + g[:, :, 1] * y_ssm
    x = x + rmsnorm(merged @ w_out, norm_mix_post)
    h2 = rmsnorm(x, norm_ffn_pre)
    gate = h2 @ w_ffn_gate
    up = h2 @ w_ffn_up
    gp = jnp.pad(gate, ((0, 0), (1, 1), (0, 0)))
    gate = gp[:, :-2] * conv_w[0] + gp[:, 1:-1] * conv_w[1] + gp[:, 2:] * conv_w[2] + conv_b
    f = jax.nn.gelu(gate) * up
    return x + rmsnorm(f @ w_ffn_down, norm_ffn_post)


def setup_inputs(seed: int = 0) -> dict:
    key = jax.random.key(seed)
    ks = iter(jax.random.split(key, 48))

    def nrm(shape, scale):
        return scale * jax.random.normal(next(ks), shape, jnp.float32)

    L = DEPTH
    n = jnp.arange(SSM_STATE, dtype=jnp.float32)

    def gain(dim):
        return 1.0 + nrm((L, dim), 0.05)

    def lam_re():
        return -0.5 + nrm((L, SSM_GROUPS, SSM_STATE), 0.01)

    def lam_im():
        return math.pi * n + nrm((L, SSM_GROUPS, SSM_STATE), 0.01)

    def log_dt():
        return jax.random.uniform(next(ks), (L, SSM_GROUPS), jnp.float32,
                                  math.log(DT_MIN), math.log(DT_MAX))

    b_scale = (2.0 * SSM_CH) ** -0.5
    c_scale = (2.0 * SSM_STATE) ** -0.5
    return {
        "x": nrm((BATCH, SEQ, D_MODEL), 1.0),
        "norm_mix_pre": gain(D_MODEL),
        "w_in": nrm((L, D_MODEL, IN_WIDTH), D_MODEL ** -0.5),
        "attn_sink": nrm((L, N_HEADS), 0.5),
        "lam_re_f": lam_re(),
        "lam_im_f": lam_im(),
        "log_dt_f": log_dt(),
        "b_re_f": nrm((L, SSM_GROUPS, SSM_STATE, SSM_CH), b_scale),
        "b_im_f": nrm((L, SSM_GROUPS, SSM_STATE, SSM_CH), b_scale),
        "c_re_f": nrm((L, SSM_GROUPS, SSM_CH, SSM_STATE), c_scale),
        "c_im_f": nrm((L, SSM_GROUPS, SSM_CH, SSM_STATE), c_scale),
        "lam_re_b": lam_re(),
        "lam_im_b": lam_im(),
        "log_dt_b": log_dt(),
        "b_re_b": nrm((L, SSM_GROUPS, SSM_STATE, SSM_CH), b_scale),
        "b_im_b": nrm((L, SSM_GROUPS, SSM_STATE, SSM_CH), b_scale),
        "c_re_b": nrm((L, SSM_GROUPS, SSM_CH, SSM_STATE), c_scale),
        "c_im_b": nrm((L, SSM_GROUPS, SSM_CH, SSM_STATE), c_scale),
        "ssm_d": nrm((L, SSM_WIDTH), 1.0),
        "w_glu": nrm((L, SSM_WIDTH, SSM_WIDTH), SSM_WIDTH ** -0.5),
        "b_glu": nrm((L, SSM_WIDTH), 0.02),
        "w_branch_attn": nrm((L, ATTN_WIDTH, D_MODEL), ATTN_WIDTH ** -0.5),
        "w_branch_ssm": nrm((L, SSM_WIDTH, D_MODEL), SSM_WIDTH ** -0.5),
        "w_out": nrm((L, D_MODEL, D_MODEL), D_MODEL ** -0.5),
        "norm_mix_post": gain(D_MODEL),
        "norm_ffn_pre": gain(D_MODEL),
        "w_ffn_gate": nrm((L, D_MODEL, D_FF), D_MODEL ** -0.5),
        "w_ffn_up": nrm((L, D_MODEL, D_FF), D_MODEL ** -0.5),
        "conv_w": nrm((L, CONV_WIDTH, D_FF), CONV_WIDTH ** -0.5),
        "conv_b": nrm((L, D_FF), 0.02),
        "w_ffn_down": nrm((L, D_FF, D_MODEL), D_FF ** -0.5),
        "norm_ffn_post": gain(D_MODEL),
    }


def reference(x, norm_mix_pre, w_in, attn_sink,
              lam_re_f, lam_im_f, log_dt_f, b_re_f, b_im_f, c_re_f, c_im_f,
              lam_re_b, lam_im_b, log_dt_b, b_re_b, b_im_b, c_re_b, c_im_b,
              ssm_d, w_glu, b_glu, w_branch_attn, w_branch_ssm, w_out, norm_mix_post,
              norm_ffn_pre, w_ffn_gate, w_ffn_up, conv_w, conv_b, w_ffn_down, norm_ffn_post):
    for l in range(DEPTH):
        x = hybrid_layer(
            x, norm_mix_pre[l], w_in[l], attn_sink[l],
            lam_re_f[l], lam_im_f[l], log_dt_f[l], b_re_f[l], b_im_f[l], c_re_f[l], c_im_f[l],
            lam_re_b[l], lam_im_b[l], log_dt_b[l], b_re_b[l], b_im_b[l], c_re_b[l], c_im_b[l],
            ssm_d[l], w_glu[l], b_glu[l], w_branch_attn[l], w_branch_ssm[l], w_out[l], norm_mix_post[l],
            norm_ffn_pre[l], w_ffn_gate[l], w_ffn_up[l], conv_w[l], conv_b[l], w_ffn_down[l], norm_ffn_post[l])
    return x
```

```python
import functools
import math

import numpy as np
import jax
import jax.numpy as jnp
from jax import lax
from jax.experimental import pallas as pl
from jax.experimental.pallas import tpu as pltpu

F32 = jnp.float32
BF16 = jnp.bfloat16

N_HEADS = 8
N_KV_HEADS = 2
HEAD_DIM = 64
GROUP = N_HEADS // N_KV_HEADS
WINDOW = 128
BLOCK = 128
SSM_CH = 16
SSM_STATE = 64
CHUNK = 16
PAIR = 2
N_BRANCH = 2
EPS = 1e-6
NEG_INF = -1e30

LANES = 128
SUBLANES = 8
HALO = 16
FF_TILE = 256
VMEM_LIMIT = 56 * 1024 * 1024


def _gelu_tanh(x):
    return 0.5 * x * (1.0 + jnp.tanh(math.sqrt(2.0 / math.pi) * (x + 0.044715 * (x * x * x))))


def _rms(x, gain):
    return x * lax.rsqrt(jnp.mean(x * x, axis=-1, keepdims=True) + EPS) * gain


def _params(semantics):
    return pltpu.CompilerParams(dimension_semantics=semantics, vmem_limit_bytes=VMEM_LIMIT)


def _const_spec(shape):
    nd = len(shape)
    return pl.BlockSpec(shape, lambda *_: (0,) * nd)


def _inproj_body(splits, x_ref, g_ref, w_ref, q_ref, k_ref, v_ref, u_ref, gate_ref):
    h = _rms(x_ref[...], g_ref[...]).astype(BF16)

    def proj(i):
        return jnp.dot(h, w_ref[:, splits[i]:splits[i + 1]], preferred_element_type=F32)

    q_ref[...] = (proj(0) * (HEAD_DIM ** -0.5)).astype(BF16)
    k_ref[...] = proj(1).astype(BF16)
    v_ref[...] = proj(2).astype(BF16)
    u_ref[...] = proj(3).astype(BF16)
    gate_ref[...] = jax.nn.sigmoid(proj(4)).astype(BF16)


def _inproj(x2, gain, w_in, tm):
    T, D = x2.shape
    attn_w = N_HEADS * HEAD_DIM
    kv_w = N_KV_HEADS * HEAD_DIM
    widths = [attn_w, kv_w, kv_w, D // 2, N_BRANCH * D]
    splits = [0] + np.cumsum(widths).tolist()
    row = lambda w: pl.BlockSpec((tm, w), lambda i: (i, 0))
    return pl.pallas_call(
        functools.partial(_inproj_body, tuple(splits)),
        grid=(T // tm,),
        in_specs=[row(D), _const_spec((1, D)), _const_spec(w_in.shape)],
        out_specs=[row(w) for w in widths],
        out_shape=[jax.ShapeDtypeStruct((T, w), BF16) for w in widths],
        compiler_params=_params(("parallel",)),
        name="inproj",
    )(x2, gain, w_in)


def _attn_body(q_ref, kp_ref, kc_ref, kn_ref, vp_ref, vc_ref, vn_ref, bias_ref, sink_ref, o_ref):
    lane = lax.broadcasted_iota(jnp.int32, (1, LANES), 1)
    left = lane < HEAD_DIM
    zero = jnp.zeros((), BF16)
    kcat = jnp.concatenate([kp_ref[...], kc_ref[...], kn_ref[...]], axis=0)
    vcat = jnp.concatenate([vp_ref[...], vc_ref[...], vn_ref[...]], axis=0)
    acc = None
    for g in range(N_KV_HEADS):
        keep = left if g == 0 else jnp.logical_not(left)
        qg = jnp.concatenate(
            [jnp.where(keep, q_ref[:, LANES * i:LANES * (i + 1)], zero) for i in range(GROUP)], axis=0)
        s = lax.dot_general(qg, kcat, (((1,), (1,)), ((), ())), preferred_element_type=F32)
        s = s + bias_ref[g]
        sink = sink_ref[g]
        m = jnp.maximum(jnp.max(s, axis=-1, keepdims=True), sink)
        p = jnp.exp(s - m)
        denom = jnp.sum(p, axis=-1, keepdims=True) + jnp.exp(sink - m)
        vg = jnp.where(keep, vcat, zero)
        o = jnp.dot(p.astype(BF16), vg, preferred_element_type=F32) / denom
        acc = o if acc is None else acc + o
    for i in range(GROUP):
        o_ref[:, LANES * i:LANES * (i + 1)] = acc[BLOCK * i:BLOCK * (i + 1), :].astype(BF16)


def _attn_bias():
    t = np.arange(BLOCK)[:, None]
    s = np.arange(3 * BLOCK)[None, :] - BLOCK
    dist = np.abs(t - s).astype(np.float64)
    valid = dist <= WINDOW
    slopes = 2.0 ** (-8.0 * (np.arange(N_HEADS) + 1.0) / N_HEADS)
    out = np.zeros((3, N_KV_HEADS, GROUP * BLOCK, 3 * BLOCK), np.float32)
    for variant in range(3):
        ok = valid.copy()
        if variant == 0:
            ok[:, :BLOCK] = False
        if variant == 2:
            ok[:, 2 * BLOCK:] = False
        for g in range(N_KV_HEADS):
            for i in range(GROUP):
                out[variant, g, i * BLOCK:(i + 1) * BLOCK] = np.where(
                    ok, -slopes[g * GROUP + i] * dist, NEG_INF)
    return out


def _head_perm():
    cols = []
    for i in range(GROUP):
        for g in range(N_KV_HEADS):
            h = g * GROUP + i
            cols.extend(range(h * HEAD_DIM, (h + 1) * HEAD_DIM))
    return np.asarray(cols, np.int32)


def _attention(q, k, v, bias, sink_rows, bsz, seq):
    nb = seq // BLOCK
    assert nb >= 2
    kvw = k.shape[1]
    aw = q.shape[1]

    def kv_spec(off):
        return pl.BlockSpec((BLOCK, kvw), lambda b, n: (b * nb + jnp.clip(n + off, 0, nb - 1), 0))

    def variant(b, n):
        return (jnp.where(n == 0, 0, jnp.where(n == nb - 1, 2, 1)), 0, 0, 0)

    qspec = pl.BlockSpec((BLOCK, aw), lambda b, n: (b * nb + n, 0))
    return pl.pallas_call(
        _attn_body,
        grid=(bsz, nb),
        in_specs=[qspec, kv_spec(-1), kv_spec(0), kv_spec(1), kv_spec(-1), kv_spec(0), kv_spec(1),
                  pl.BlockSpec((None,) + bias.shape[1:], variant),
                  _const_spec(sink_rows.shape)],
        out_specs=qspec,
        out_shape=jax.ShapeDtypeStruct(q.shape, BF16),
        compiler_params=_params(("parallel", "arbitrary")),
        name="attn",
    )(q, k, k, k, v, v, v, bias, sink_rows)


def _s5prep_body(lre_ref, lim_ref, ldt_ref, btr_ref, bti_ref, cre_ref, cim_ref,
                 pb_ref, pe_ref, tf_ref, a_ref):
    lre, lim = lre_ref[...], lim_ref[...]
    dt = jnp.exp(ldt_ref[...])
    mag = jnp.exp(lre * dt)
    are, aim = mag * jnp.cos(lim * dt), mag * jnp.sin(lim * dt)
    den = lre * lre + lim * lim
    cfr = ((are - 1.0) * lre + aim * lim) / den
    cfi = (aim * lre - (are - 1.0) * lim) / den
    btr, bti = btr_ref[...], bti_ref[...]
    bbr = btr * cfr - bti * cfi
    bbi = btr * cfi + bti * cfr
    cre, cim = cre_ref[...], cim_ref[...]

    pwr, pwi = jnp.ones_like(are), jnp.zeros_like(are)
    pb_r, pb_i, pe_r, pe_i, e_r, e_i = [], [], [], [], [], []
    for p in range(CHUNK + 1):
        er = cre * pwr - cim * pwi
        ei = cre * pwi + cim * pwr
        if p < CHUNK:
            pb_r.append(bbr * pwr - bbi * pwi)
            pb_i.append(bbr * pwi + bbi * pwr)
            e_r.append(er)
            e_i.append(ei)
        if p >= 1:
            pe_r.append(er)
            pe_i.append(ei)
        if p == CHUNK:
            a_ref[0] = jnp.broadcast_to(pwr, a_ref.shape[1:])
            a_ref[1] = jnp.broadcast_to(pwi, a_ref.shape[1:])
        pwr, pwi = pwr * are - pwi * aim, pwr * aim + pwi * are
    pb_ref[0] = jnp.concatenate(pb_r, axis=0)
    pb_ref[1] = jnp.concatenate(pb_i, axis=0)
    pe_ref[0] = jnp.concatenate(pe_r, axis=0)
    pe_ref[1] = jnp.concatenate(pe_i, axis=0)
    contract = (((1,), (1,)), ((), ()))
    hi = lax.Precision.HIGHEST
    krow = (lax.dot_general(bbr, jnp.concatenate(e_r, axis=0), contract, precision=hi,
                            preferred_element_type=F32)
            - lax.dot_general(bbi, jnp.concatenate(e_i, axis=0), contract, precision=hi,
                              preferred_element_type=F32))
    lane = lax.broadcasted_iota(jnp.int32, krow.shape, 1)
    rows = [krow]
    for kk in range(1, CHUNK):
        rows.append(jnp.where(lane >= SSM_CH * kk, pltpu.roll(krow, SSM_CH * kk, 1), 0.0))
    tf_ref[...] = jnp.concatenate(rows, axis=0)


def _s5prep(lre, lim, ldt, btr, bti, cre, cim):
    n = lre.shape[0]
    width = CHUNK * SSM_CH
    vec = pl.BlockSpec((None, 1, SSM_STATE), lambda i: (i, 0, 0))
    mat = pl.BlockSpec((None, SSM_CH, SSM_STATE), lambda i: (i, 0, 0))
    return pl.pallas_call(
        _s5prep_body,
        grid=(n,),
        in_specs=[vec, vec, vec, mat, mat, mat, mat],
        out_specs=[pl.BlockSpec((None, 2, width, SSM_STATE), lambda i: (i, 0, 0, 0)),
                   pl.BlockSpec((None, 2, width, SSM_STATE), lambda i: (i, 0, 0, 0)),
                   pl.BlockSpec((None, width, width), lambda i: (i, 0, 0)),
                   pl.BlockSpec((None, 2, SUBLANES, SSM_STATE), lambda i: (i, 0, 0, 0))],
        out_shape=[jax.ShapeDtypeStruct((n, 2, width, SSM_STATE), F32),
                   jax.ShapeDtypeStruct((n, 2, width, SSM_STATE), F32),
                   jax.ShapeDtypeStruct((n, width, width), F32),
                   jax.ShapeDtypeStruct((n, 2, SUBLANES, SSM_STATE), F32)],
        compiler_params=_params(("parallel",)),
        name="s5prep",
    )(lre, lim, ldt, btr, bti, cre, cim)


def _s5_body(bsz, u_ref, wb_ref, tf_ref, tb_ref, wc_ref, a_ref, y_ref, w_scr, s_scr):
    u = u_ref[...]
    w_scr[...] = jnp.dot(u, wb_ref[...], preferred_element_type=F32)
    rows = u.shape[0]
    nsteps = rows // SUBLANES
    half = SUBLANES // 2
    assert bsz == half
    lo = lax.broadcasted_iota(jnp.int32, (SUBLANES, LANES), 0) < half
    afr, afi, abr, abi = a_ref[0], a_ref[1], a_ref[2], a_ref[3]
    col = lambda i: slice(LANES * i, LANES * (i + 1))

    def advance(ar, ai, xr, xi, wr, wi):
        return ar * xr - ai * xi + wr, ar * xi + ai * xr + wi

    def swap(x):
        return pltpu.roll(x, half, 0)

    def step(i, carry):
        fr, fi, br, bi = carry
        rf = pl.ds(pl.multiple_of(i * SUBLANES, SUBLANES), SUBLANES)
        wr, wi = w_scr[rf, col(0)], w_scr[rf, col(1)]
        r0r, r0i = swap(fr), swap(fi)
        t1r, t1i = advance(afr, afi, r0r, r0i, wr, wi)
        r1r, r1i = swap(t1r), swap(t1i)
        t2r, t2i = advance(afr, afi, r1r, r1i, wr, wi)
        s_scr[rf, col(0)] = jnp.where(lo, r0r, r1r)
        s_scr[rf, col(1)] = jnp.where(lo, r0i, r1i)
        fr, fi = jnp.where(lo, t1r, t2r), jnp.where(lo, t1i, t2i)

        rb = pl.ds(pl.multiple_of((nsteps - 1 - i) * SUBLANES, SUBLANES), SUBLANES)
        wr, wi = w_scr[rb, col(2)], w_scr[rb, col(3)]
        r0r, r0i = swap(br), swap(bi)
        t1r, t1i = advance(abr, abi, r0r, r0i, wr, wi)
        r1r, r1i = swap(t1r), swap(t1i)
        t2r, t2i = advance(abr, abi, r1r, r1i, wr, wi)
        s_scr[rb, col(2)] = jnp.where(lo, r1r, r0r)
        s_scr[rb, col(3)] = jnp.where(lo, r1i, r0i)
        br, bi = jnp.where(lo, t2r, t1r), jnp.where(lo, t2i, t1i)
        return fr, fi, br, bi

    z = jnp.zeros((SUBLANES, LANES), F32)
    lax.fori_loop(0, nsteps, step, (z, z, z, z))

    y = jnp.dot(s_scr[...].astype(BF16), wc_ref[...], preferred_element_type=F32)
    width = CHUNK * SSM_CH
    for g in range(PAIR):
        sl = slice(width * g, width * (g + 1))
        t = (tf_ref[g] + tb_ref[g]).astype(BF16)
        y_ref[:, sl] = y[:, sl] + jnp.dot(u[:, sl], t, preferred_element_type=F32)


def _s5(u_r, wb, tf, tb, wc, a16, bsz):
    rows, lanes = u_r.shape
    width = CHUNK * SSM_CH
    pw = PAIR * width
    npair = lanes // pw
    return pl.pallas_call(
        functools.partial(_s5_body, bsz),
        grid=(npair,),
        in_specs=[pl.BlockSpec((rows, pw), lambda p: (0, p)),
                  pl.BlockSpec((None, pw, 4 * LANES), lambda p: (p, 0, 0)),
                  pl.BlockSpec((PAIR, width, width), lambda p: (p, 0, 0)),
                  pl.BlockSpec((PAIR, width, width), lambda p: (p, 0, 0)),
                  pl.BlockSpec((None, 4 * LANES, pw), lambda p: (p, 0, 0)),
                  pl.BlockSpec((None, 4, SUBLANES, LANES), lambda p: (p, 0, 0, 0))],
        out_specs=pl.BlockSpec((rows, pw), lambda p: (0, p)),
        out_shape=jax.ShapeDtypeStruct((rows, lanes), F32),
        scratch_shapes=[pltpu.VMEM((rows, 4 * LANES), F32), pltpu.VMEM((rows, 4 * LANES), F32)],
        compiler_params=_params(("parallel",)),
        name="s5",
    )(u_r, wb, tf, tb, wc, a16)


def _s5_operators(pb, pe, tf, a16, ngroups):
    width = CHUNK * SSM_CH
    npair = ngroups // PAIR
    eye = jnp.eye(PAIR, dtype=F32)
    pb = pb.reshape(2, npair, PAIR, 2, CHUNK, SSM_CH, SSM_STATE)
    pe = pe.reshape(2, npair, PAIR, 2, CHUNK, SSM_CH, SSM_STATE)
    wb_dir = jnp.stack([pb[0][:, :, :, ::-1], pb[1]], axis=0)
    wb_dir = wb_dir.reshape(2, npair, PAIR, 2, width, SSM_STATE)
    wb = jnp.einsum('dpgrkn,gh->pgkdrhn', wb_dir, eye).reshape(npair, PAIR * width, 4 * LANES)
    wc_dir = jnp.stack([pe[0], pe[1][:, :, :, ::-1]], axis=0)
    wc_dir = wc_dir.reshape(2, npair, PAIR, 2, width, SSM_STATE)
    sign = jnp.asarray([1.0, -1.0], F32)
    wc = jnp.einsum('dpgrkn,gh,r->pdrgnhk', wc_dir, eye, sign).reshape(npair, 4 * LANES, PAIR * width)
    tf_f = tf[0].reshape(npair * PAIR, width, width)
    tf_b = tf[1].reshape(ngroups, CHUNK, SSM_CH, CHUNK, SSM_CH)[:, ::-1, :, ::-1, :]
    tf_b = tf_b.reshape(npair * PAIR, width, width)
    a = a16.reshape(2, npair, PAIR, 2, SUBLANES, SSM_STATE)
    a = a.transpose(1, 0, 3, 4, 2, 5).reshape(npair, 4, SUBLANES, LANES)
    return wb.astype(BF16), tf_f, tf_b, wc.astype(BF16), a


def _merge_body(attn_ref, ys_ref, u_ref, g_ref, x_ref, wba_ref, wglu_ref, bglu_ref, d_ref, wbs_ref,
                wout_ref, gpost_ref, gpre_ref, x1_ref, h2_ref):
    d = x_ref.shape[1]
    y = ys_ref[...] + d_ref[...] * u_ref[...].astype(F32)
    y = _gelu_tanh(y)
    gl = jnp.dot(y.astype(BF16), wglu_ref[...], preferred_element_type=F32) + bglu_ref[...]
    s = y * jax.nn.sigmoid(gl)
    y_ssm = jnp.dot(s.astype(BF16), wbs_ref[...], preferred_element_type=F32)
    y_attn = jnp.dot(attn_ref[...], wba_ref[...], preferred_element_type=F32)
    merged = g_ref[:, :d].astype(F32) * y_attn + g_ref[:, d:].astype(F32) * y_ssm
    o = jnp.dot(merged.astype(BF16), wout_ref[...], preferred_element_type=F32)
    x1 = x_ref[...] + _rms(o, gpost_ref[...])
    x1_ref[...] = x1
    h2_ref[...] = _rms(x1, gpre_ref[...]).astype(BF16)


def _merge(attn, ys, u, gates, x2, wba, wglu, bglu, ssm_d, wbs, wout, gpost, gpre, tm):
    T, D = x2.shape
    row = lambda w: pl.BlockSpec((tm, w), lambda i: (i, 0))
    consts = [wba, wglu, bglu, ssm_d, wbs, wout, gpost, gpre]
    return pl.pallas_call(
        _merge_body,
        grid=(T // tm,),
        in_specs=[row(attn.shape[1]), row(ys.shape[1]), row(u.shape[1]), row(gates.shape[1]), row(D)]
                 + [_const_spec(c.shape) for c in consts],
        out_specs=[row(D), row(D)],
        out_shape=[jax.ShapeDtypeStruct((T, D), F32), jax.ShapeDtypeStruct((T, D), BF16)],
        compiler_params=_params(("parallel",)),
        name="merge",
    )(attn, ys, u, gates, x2, *consts)


def _ffn_body(tiles_per_seq, hp_ref, hc_ref, hn_ref, x1_ref, wg_ref, wu_ref, cw_ref, cb_ref, wd_ref,
              gn_ref, o_ref, hext, fscr):
    tm = hc_ref.shape[0]
    i = pl.program_id(0) % tiles_per_seq
    zero = jnp.zeros((), BF16)
    hext[0:HALO] = jnp.where(i == 0, zero, hp_ref[...])
    hext[HALO:HALO + tm] = hc_ref[...]
    hext[HALO + tm:] = jnp.where(i == tiles_per_seq - 1, zero, hn_ref[...])
    ext = tm + 2 * HALO
    for c in range(wg_ref.shape[1] // FF_TILE):
        sl = slice(FF_TILE * c, FF_TILE * (c + 1))
        g = jnp.dot(hext[...], wg_ref[:, sl], preferred_element_type=F32)
        up = jnp.dot(hc_ref[...], wu_ref[:, sl], preferred_element_type=F32)
        g_prev = pltpu.roll(g, 1, 0)[HALO:HALO + tm]
        g_next = pltpu.roll(g, ext - 1, 0)[HALO:HALO + tm]
        conv = (g_prev * cw_ref[0:1, sl] + g[HALO:HALO + tm] * cw_ref[1:2, sl]
                + g_next * cw_ref[2:3, sl] + cb_ref[:, sl])
        fscr[:, sl] = (_gelu_tanh(conv) * up).astype(BF16)
    o = jnp.dot(fscr[...], wd_ref[...], preferred_element_type=F32)
    o_ref[...] = x1_ref[...] + _rms(o, gn_ref[...])


def _ffn(h2, x1, wg, wu, cw, cb, wd, gn, seq, tm):
    T, D = x1.shape
    dff = wg.shape[1]
    assert dff % FF_TILE == 0 and tm % HALO == 0 and seq % tm == 0
    per = tm // HALO
    nh = T // HALO
    row = lambda w: pl.BlockSpec((tm, w), lambda i: (i, 0))
    once = lambda shape: pl.BlockSpec(shape, lambda i: (0,) * len(shape), pipeline_mode=pl.Buffered(1))
    return pl.pallas_call(
        functools.partial(_ffn_body, seq // tm),
        grid=(T // tm,),
        in_specs=[pl.BlockSpec((HALO, D), lambda i: (jnp.maximum(i * per - 1, 0), 0)),
                  row(D),
                  pl.BlockSpec((HALO, D), lambda i: (jnp.minimum((i + 1) * per, nh - 1), 0)),
                  row(D),
                  once(wg.shape), once(wu.shape), _const_spec(cw.shape), _const_spec(cb.shape),
                  once(wd.shape), _const_spec(gn.shape)],
        out_specs=row(D),
        out_shape=jax.ShapeDtypeStruct((T, D), F32),
        scratch_shapes=[pltpu.VMEM((tm + 2 * HALO, D), BF16), pltpu.VMEM((tm, dff), BF16)],
        compiler_params=_params(("parallel",)),
        name="ffn",
    )(h2, h2, h2, x1, wg, wu, cw, cb, wd, gn)


def _row_tile(seq):
    return min(512, seq)


def _layer(x, p):
    bsz, seq, d = x.shape
    T = bsz * seq
    tm = _row_tile(seq)
    ssm_w = d // 2
    ngroups = ssm_w // SSM_CH
    nchunk = seq // CHUNK
    attn_w = N_HEADS * HEAD_DIM
    row2 = lambda v: v.reshape(1, -1).astype(F32)

    perm = _head_perm()
    w_in = jnp.concatenate([p["w_in"][:, :attn_w][:, perm], p["w_in"][:, attn_w:]], axis=1).astype(BF16)
    x2 = x.reshape(T, d)
    q, k, v, u, gates = _inproj(x2, row2(p["norm_mix_pre"]), w_in, tm)

    sink = p["attn_sink"].astype(F32).reshape(N_KV_HEADS, GROUP)
    sink_rows = jnp.repeat(sink, BLOCK, axis=1)[:, :, None]
    attn = _attention(q, k, v, jnp.asarray(_attn_bias()), sink_rows, bsz, seq)

    both = lambda name: jnp.stack([p[name + "_f"], p[name + "_b"]], axis=0).astype(F32)
    n2 = 2 * ngroups
    lre = both("lam_re").reshape(n2, 1, SSM_STATE)
    lim = both("lam_im").reshape(n2, 1, SSM_STATE)
    ldt = jnp.broadcast_to(both("log_dt").reshape(n2, 1, 1), (n2, 1, SSM_STATE))
    btr = both("b_re").transpose(0, 1, 3, 2).reshape(n2, SSM_CH, SSM_STATE)
    bti = both("b_im").transpose(0, 1, 3, 2).reshape(n2, SSM_CH, SSM_STATE)
    cre = both("c_re").reshape(n2, SSM_CH, SSM_STATE)
    cim = both("c_im").reshape(n2, SSM_CH, SSM_STATE)
    pb, pe, tf, a16 = _s5prep(lre, lim, ldt, btr, bti, cre, cim)
    wb, tf_f, tf_b, wc, a = _s5_operators(
        pb.reshape(2, ngroups, 2, CHUNK * SSM_CH, SSM_STATE), pe.reshape(2, ngroups, 2, CHUNK * SSM_CH, SSM_STATE),
        tf.reshape(2, ngroups, CHUNK * SSM_CH, CHUNK * SSM_CH), a16.reshape(2, ngroups, 2, SUBLANES, SSM_STATE),
        ngroups)

    u_r = u.reshape(bsz, nchunk, CHUNK, ngroups, SSM_CH).transpose(1, 0, 3, 2, 4)
    u_r = u_r.reshape(nchunk * bsz, ngroups * CHUNK * SSM_CH)
    y_r = _s5(u_r, wb, tf_f, tf_b, wc, a, bsz)
    ys = y_r.reshape(nchunk, bsz, ngroups, CHUNK, SSM_CH).transpose(1, 0, 3, 2, 4).reshape(T, ssm_w)

    x1, h2 = _merge(attn, ys, u, gates, x2,
                    p["w_branch_attn"][perm, :].astype(BF16), p["w_glu"].astype(BF16), row2(p["b_glu"]),
                    row2(p["ssm_d"]), p["w_branch_ssm"].astype(BF16), p["w_out"].astype(BF16),
                    row2(p["norm_mix_post"]), row2(p["norm_ffn_pre"]), tm)

    out = _ffn(h2, x1, p["w_ffn_gate"].astype(BF16), p["w_ffn_up"].astype(BF16), p["conv_w"].astype(F32),
               row2(p["conv_b"]), p["w_ffn_down"].astype(BF16), row2(p["norm_ffn_post"]), seq, tm)
    return out.reshape(bsz, seq, d)


_PARAM_NAMES = (
    "norm_mix_pre", "w_in", "attn_sink",
    "lam_re_f", "lam_im_f", "log_dt_f", "b_re_f", "b_im_f", "c_re_f", "c_im_f",
    "lam_re_b", "lam_im_b", "log_dt_b", "b_re_b", "b_im_b", "c_re_b", "c_im_b",
    "ssm_d", "w_glu", "b_glu", "w_branch_attn", "w_branch_ssm", "w_out", "norm_mix_post",
    "norm_ffn_pre", "w_ffn_gate", "w_ffn_up", "conv_w", "conv_b", "w_ffn_down", "norm_ffn_post")


def kernel(x, norm_mix_pre, w_in, attn_sink, lam_re_f, lam_im_f, log_dt_f, b_re_f, b_im_f, c_re_f, c_im_f, lam_re_b, lam_im_b, log_dt_b, b_re_b, b_im_b, c_re_b, c_im_b, ssm_d, w_glu, b_glu, w_branch_attn, w_branch_ssm, w_out, norm_mix_post, norm_ffn_pre, w_ffn_gate, w_ffn_up, conv_w, conv_b, w_ffn_down, norm_ffn_post):
    stacked = dict(zip(_PARAM_NAMES, (
        norm_mix_pre, w_in, attn_sink,
        lam_re_f, lam_im_f, log_dt_f, b_re_f, b_im_f, c_re_f, c_im_f,
        lam_re_b, lam_im_b, log_dt_b, b_re_b, b_im_b, c_re_b, c_im_b,
        ssm_d, w_glu, b_glu, w_branch_attn, w_branch_ssm, w_out, norm_mix_post,
        norm_ffn_pre, w_ffn_gate, w_ffn_up, conv_w, conv_b, w_ffn_down, norm_ffn_post)))
    for layer in range(w_in.shape[0]):
        x = _layer(x, {name: value[layer] for name, value in stacked.items()})
    return x
```

```python
import functools
import math

import numpy as np
import jax
import jax.numpy as jnp
from jax import lax
from jax.experimental import pallas as pl
from jax.experimental.pallas import tpu as pltpu

F32 = jnp.float32
BF16 = jnp.bfloat16

N_HEADS = 8
N_KV_HEADS = 2
HEAD_DIM = 64
GROUP = N_HEADS // N_KV_HEADS
WINDOW = 128
BLOCK = 128
SSM_CH = 16
SSM_STATE = 64
CHUNK = 16
PAIR = 2
N_BRANCH = 2
EPS = 1e-6
NEG_INF = -1e30

LANES = 128
SUBLANES = 8
HALO = 16
FF_TILE = 256
VMEM_LIMIT = 56 * 1024 * 1024


def _gelu_tanh(x):
    return 0.5 * x * (1.0 + jnp.tanh(math.sqrt(2.0 / math.pi) * (x + 0.044715 * (x * x * x))))


def _rms(x, gain):
    return x * lax.rsqrt(jnp.mean(x * x, axis=-1, keepdims=True) + EPS) * gain


def _params(semantics):
    return pltpu.CompilerParams(dimension_semantics=semantics, vmem_limit_bytes=VMEM_LIMIT)


def _const_spec(shape):
    nd = len(shape)
    return pl.BlockSpec(shape, lambda *_: (0,) * nd)


def _inproj_body(splits, x_ref, g_ref, w_ref, q_ref, k_ref, v_ref, u_ref, gate_ref):
    h = _rms(x_ref[...], g_ref[...]).astype(BF16)

    def proj(i):
        return jnp.dot(h, w_ref[:, splits[i]:splits[i + 1]], preferred_element_type=F32)

    q_ref[...] = (proj(0) * (HEAD_DIM ** -0.5)).astype(BF16)
    k_ref[...] = proj(1).astype(BF16)
    v_ref[...] = proj(2).astype(BF16)
    u_ref[...] = proj(3).astype(BF16).reshape(u_ref.shape)
    gate_ref[...] = jax.nn.sigmoid(proj(4)).astype(BF16)


def _chunk_rows_spec(tm, tiles_per_seq, width):
    return pl.BlockSpec((tm // CHUNK, None, CHUNK, width),
                        lambda i: (i % tiles_per_seq, i // tiles_per_seq, 0, 0))


def _inproj(x2, gain, w_in, bsz, seq, tm):
    T, D = x2.shape
    attn_w = N_HEADS * HEAD_DIM
    kv_w = N_KV_HEADS * HEAD_DIM
    widths = [attn_w, kv_w, kv_w, D // 2, N_BRANCH * D]
    splits = [0] + np.cumsum(widths).tolist()
    row = lambda w: pl.BlockSpec((tm, w), lambda i: (i, 0))
    out_specs = [row(w) for w in widths]
    out_shape = [jax.ShapeDtypeStruct((T, w), BF16) for w in widths]
    out_specs[3] = _chunk_rows_spec(tm, seq // tm, widths[3])
    out_shape[3] = jax.ShapeDtypeStruct((seq // CHUNK, bsz, CHUNK, widths[3]), BF16)
    return pl.pallas_call(
        functools.partial(_inproj_body, tuple(splits)),
        grid=(T // tm,),
        in_specs=[row(D), _const_spec((1, D)), _const_spec(w_in.shape)],
        out_specs=out_specs,
        out_shape=out_shape,
        compiler_params=_params(("parallel",)),
        name="inproj",
    )(x2, gain, w_in)


def _attn_body(q_ref, kp_ref, kc_ref, kn_ref, vp_ref, vc_ref, vn_ref, bias_ref, sink_ref, o_ref):
    lane = lax.broadcasted_iota(jnp.int32, (1, LANES), 1)
    left = lane < HEAD_DIM
    zero = jnp.zeros((), BF16)
    kcat = jnp.concatenate([kp_ref[...], kc_ref[...], kn_ref[...]], axis=0)
    vcat = jnp.concatenate([vp_ref[...], vc_ref[...], vn_ref[...]], axis=0)
    acc = None
    for g in range(N_KV_HEADS):
        keep = left if g == 0 else jnp.logical_not(left)
        qg = jnp.concatenate(
            [jnp.where(keep, q_ref[:, LANES * i:LANES * (i + 1)], zero) for i in range(GROUP)], axis=0)
        s = lax.dot_general(qg, kcat, (((1,), (1,)), ((), ())), preferred_element_type=F32)
        s = s + bias_ref[g]
        sink = sink_ref[g]
        m = jnp.maximum(jnp.max(s, axis=-1, keepdims=True), sink)
        p = jnp.exp(s - m)
        denom = jnp.sum(p, axis=-1, keepdims=True) + jnp.exp(sink - m)
        vg = jnp.where(keep, vcat, zero)
        o = jnp.dot(p.astype(BF16), vg, preferred_element_type=F32) / denom
        acc = o if acc is None else acc + o
    for i in range(GROUP):
        o_ref[:, LANES * i:LANES * (i + 1)] = acc[BLOCK * i:BLOCK * (i + 1), :].astype(BF16)


def _attn_bias():
    t = np.arange(BLOCK)[:, None]
    s = np.arange(3 * BLOCK)[None, :] - BLOCK
    dist = np.abs(t - s).astype(np.float64)
    valid = dist <= WINDOW
    slopes = 2.0 ** (-8.0 * (np.arange(N_HEADS) + 1.0) / N_HEADS)
    out = np.zeros((3, N_KV_HEADS, GROUP * BLOCK, 3 * BLOCK), np.float32)
    for variant in range(3):
        ok = valid.copy()
        if variant == 0:
            ok[:, :BLOCK] = False
        if variant == 2:
            ok[:, 2 * BLOCK:] = False
        for g in range(N_KV_HEADS):
            for i in range(GROUP):
                out[variant, g, i * BLOCK:(i + 1) * BLOCK] = np.where(
                    ok, -slopes[g * GROUP + i] * dist, NEG_INF)
    return out


def _head_perm():
    cols = []
    for i in range(GROUP):
        for g in range(N_KV_HEADS):
            h = g * GROUP + i
            cols.extend(range(h * HEAD_DIM, (h + 1) * HEAD_DIM))
    return np.asarray(cols, np.int32)


def _attention(q, k, v, bias, sink_rows, bsz, seq):
    nb = seq // BLOCK
    assert nb >= 2
    kvw = k.shape[1]
    aw = q.shape[1]

    def kv_spec(off):
        return pl.BlockSpec((BLOCK, kvw), lambda b, n: (b * nb + jnp.clip(n + off, 0, nb - 1), 0))

    def variant(b, n):
        return (jnp.where(n == 0, 0, jnp.where(n == nb - 1, 2, 1)), 0, 0, 0)

    qspec = pl.BlockSpec((BLOCK, aw), lambda b, n: (b * nb + n, 0))
    return pl.pallas_call(
        _attn_body,
        grid=(bsz, nb),
        in_specs=[qspec, kv_spec(-1), kv_spec(0), kv_spec(1), kv_spec(-1), kv_spec(0), kv_spec(1),
                  pl.BlockSpec((None,) + bias.shape[1:], variant),
                  _const_spec(sink_rows.shape)],
        out_specs=qspec,
        out_shape=jax.ShapeDtypeStruct(q.shape, BF16),
        compiler_params=_params(("parallel", "arbitrary")),
        name="attn",
    )(q, k, k, k, v, v, v, bias, sink_rows)


def _s5prep_body(lre_ref, lim_ref, ldt_ref, btr_ref, bti_ref, cre_ref, cim_ref,
                 pb_ref, pe_ref, tf_ref, a_ref):
    lre, lim = lre_ref[...], lim_ref[...]
    dt = jnp.exp(ldt_ref[...])
    mag = jnp.exp(lre * dt)
    are, aim = mag * jnp.cos(lim * dt), mag * jnp.sin(lim * dt)
    den = lre * lre + lim * lim
    cfr = ((are - 1.0) * lre + aim * lim) / den
    cfi = (aim * lre - (are - 1.0) * lim) / den
    btr, bti = btr_ref[...], bti_ref[...]
    bbr = btr * cfr - bti * cfi
    bbi = btr * cfi + bti * cfr
    cre, cim = cre_ref[...], cim_ref[...]

    pwr, pwi = jnp.ones_like(are), jnp.zeros_like(are)
    pb_r, pb_i, pe_r, pe_i, e_r, e_i = [], [], [], [], [], []
    for p in range(CHUNK + 1):
        er = cre * pwr - cim * pwi
        ei = cre * pwi + cim * pwr
        if p < CHUNK:
            pb_r.append(bbr * pwr - bbi * pwi)
            pb_i.append(bbr * pwi + bbi * pwr)
            e_r.append(er)
            e_i.append(ei)
        if p >= 1:
            pe_r.append(er)
            pe_i.append(ei)
        if p == CHUNK:
            a_ref[0] = jnp.broadcast_to(pwr, a_ref.shape[1:])
            a_ref[1] = jnp.broadcast_to(pwi, a_ref.shape[1:])
        pwr, pwi = pwr * are - pwi * aim, pwr * aim + pwi * are
    pb_ref[0] = jnp.concatenate(pb_r, axis=0)
    pb_ref[1] = jnp.concatenate(pb_i, axis=0)
    pe_ref[0] = jnp.concatenate(pe_r, axis=0)
    pe_ref[1] = jnp.concatenate(pe_i, axis=0)
    contract = (((1,), (1,)), ((), ()))
    hi = lax.Precision.HIGHEST
    krow = (lax.dot_general(bbr, jnp.concatenate(e_r, axis=0), contract, precision=hi,
                            preferred_element_type=F32)
            - lax.dot_general(bbi, jnp.concatenate(e_i, axis=0), contract, precision=hi,
                              preferred_element_type=F32))
    lane = lax.broadcasted_iota(jnp.int32, krow.shape, 1)
    rows = [krow]
    for kk in range(1, CHUNK):
        rows.append(jnp.where(lane >= SSM_CH * kk, pltpu.roll(krow, SSM_CH * kk, 1), 0.0))
    tf_ref[...] = jnp.concatenate(rows, axis=0)


def _s5prep(lre, lim, ldt, btr, bti, cre, cim):
    n = lre.shape[0]
    width = CHUNK * SSM_CH
    vec = pl.BlockSpec((None, 1, SSM_STATE), lambda i: (i, 0, 0))
    mat = pl.BlockSpec((None, SSM_CH, SSM_STATE), lambda i: (i, 0, 0))
    return pl.pallas_call(
        _s5prep_body,
        grid=(n,),
        in_specs=[vec, vec, vec, mat, mat, mat, mat],
        out_specs=[pl.BlockSpec((None, 2, width, SSM_STATE), lambda i: (i, 0, 0, 0)),
                   pl.BlockSpec((None, 2, width, SSM_STATE), lambda i: (i, 0, 0, 0)),
                   pl.BlockSpec((None, width, width), lambda i: (i, 0, 0)),
                   pl.BlockSpec((None, 2, SUBLANES, SSM_STATE), lambda i: (i, 0, 0, 0))],
        out_shape=[jax.ShapeDtypeStruct((n, 2, width, SSM_STATE), F32),
                   jax.ShapeDtypeStruct((n, 2, width, SSM_STATE), F32),
                   jax.ShapeDtypeStruct((n, width, width), F32),
                   jax.ShapeDtypeStruct((n, 2, SUBLANES, SSM_STATE), F32)],
        compiler_params=_params(("parallel",)),
        name="s5prep",
    )(lre, lim, ldt, btr, bti, cre, cim)


def _block_transpose(xs):
    n = len(xs)
    assert n * SSM_CH == LANES and n & (n - 1) == 0
    blk = lax.broadcasted_iota(jnp.int32, xs[0].shape, 1) // SSM_CH
    d = n // 2
    while d:
        upper = (blk & d) != 0
        ys = list(xs)
        for a in range(n):
            if a & d:
                continue
            b = a + d
            ys[a] = jnp.where(upper, pltpu.roll(xs[b], SSM_CH * d, 1), xs[a])
            ys[b] = jnp.where(upper, xs[b], pltpu.roll(xs[a], LANES - SSM_CH * d, 1))
        xs = ys
        d //= 2
    return xs


def _s5_scan(a_ref, w_scr, s_scr):
    rows = w_scr.shape[0]
    nsteps = rows // SUBLANES
    half = SUBLANES // 2
    lo = lax.broadcasted_iota(jnp.int32, (SUBLANES, LANES), 0) < half
    afr, afi, abr, abi = a_ref[0], a_ref[1], a_ref[2], a_ref[3]
    col = lambda i: slice(LANES * i, LANES * (i + 1))

    def advance(ar, ai, xr, xi, wr, wi):
        return ar * xr - ai * xi + wr, ar * xi + ai * xr + wi

    def swap(x):
        return pltpu.roll(x, half, 0)

    def step(i, carry):
        fr, fi, br, bi = carry
        rf = pl.ds(pl.multiple_of(i * SUBLANES, SUBLANES), SUBLANES)
        wr, wi = w_scr[rf, col(0)], w_scr[rf, col(1)]
        r0r, r0i = swap(fr), swap(fi)
        t1r, t1i = advance(afr, afi, r0r, r0i, wr, wi)
        r1r, r1i = swap(t1r), swap(t1i)
        t2r, t2i = advance(afr, afi, r1r, r1i, wr, wi)
        s_scr[rf, col(0)] = jnp.where(lo, r0r, r1r)
        s_scr[rf, col(1)] = jnp.where(lo, r0i, r1i)
        fr, fi = jnp.where(lo, t1r, t2r), jnp.where(lo, t1i, t2i)

        rb = pl.ds(pl.multiple_of((nsteps - 1 - i) * SUBLANES, SUBLANES), SUBLANES)
        wr, wi = w_scr[rb, col(2)], w_scr[rb, col(3)]
        r0r, r0i = swap(br), swap(bi)
        t1r, t1i = advance(abr, abi, r0r, r0i, wr, wi)
        r1r, r1i = swap(t1r), swap(t1i)
        t2r, t2i = advance(abr, abi, r1r, r1i, wr, wi)
        s_scr[rb, col(2)] = jnp.where(lo, r1r, r0r)
        s_scr[rb, col(3)] = jnp.where(lo, r1i, r0i)
        br, bi = jnp.where(lo, t2r, t1r), jnp.where(lo, t2i, t1i)
        return fr, fi, br, bi

    z = jnp.zeros((SUBLANES, LANES), F32)
    lax.fori_loop(0, nsteps, step, (z, z, z, z))


S5_SLAB_GROUPS = LANES // SSM_CH
S5_ROW_CHUNK = 256
S5_SUB_ROWS = 32


def _s5_body(u_ref, wb_ref, t_ref, wc_ref, a_ref, y_ref, ur_scr, w_scr, s_scr, sall_scr, yr_scr, tok_scr):
    rows = ur_scr.shape[0]
    width = CHUNK * SSM_CH
    pw = PAIR * width
    npair = S5_SLAB_GROUPS // PAIR
    half_steps = CHUNK // 2
    tok_rows = S5_ROW_CHUNK * CHUNK

    def relayout_in(c):
        tok_scr[...] = u_ref[tok_rows * c:tok_rows * (c + 1), :].astype(F32)

        def sub(i, carry):
            r0 = pl.multiple_of(i * S5_SUB_ROWS, S5_SUB_ROWS)
            for h in range(2):
                xs = [tok_scr[pl.ds(r0 * CHUNK + half_steps * h + kk, S5_SUB_ROWS, stride=CHUNK), :]
                      for kk in range(half_steps)]
                ys = _block_transpose(xs)
                for g in range(S5_SLAB_GROUPS):
                    ur_scr[pl.ds(S5_ROW_CHUNK * c + r0, S5_SUB_ROWS),
                           width * g + LANES * h:width * g + LANES * (h + 1)] = ys[g].astype(BF16)
            return carry

        lax.fori_loop(0, S5_ROW_CHUNK // S5_SUB_ROWS, sub, 0)

    for c in range(rows // S5_ROW_CHUNK):
        relayout_in(c)

    for pp in range(npair):
        u = ur_scr[:, pw * pp:pw * (pp + 1)]
        w_scr[...] = jnp.dot(u, wb_ref[pp], preferred_element_type=F32)
        _s5_scan(a_ref.at[pp], w_scr, s_scr)
        sall_scr[:, 4 * LANES * pp:4 * LANES * (pp + 1)] = s_scr[...].astype(BF16)

    def outputs(c):
        rc = slice(S5_ROW_CHUNK * c, S5_ROW_CHUNK * (c + 1))
        for pp in range(npair):
            y = jnp.dot(sall_scr[rc, 4 * LANES * pp:4 * LANES * (pp + 1)], wc_ref[pp],
                        preferred_element_type=F32)
            for g in range(PAIR):
                lo = pw * pp + width * g
                yr_scr[:, lo:lo + width] = y[:, width * g:width * (g + 1)] + jnp.dot(
                    ur_scr[rc, lo:lo + width], t_ref[PAIR * pp + g], preferred_element_type=F32)

        def sub(i, carry):
            r0 = pl.multiple_of(i * S5_SUB_ROWS, S5_SUB_ROWS)
            for h in range(2):
                xs = [yr_scr[pl.ds(r0, S5_SUB_ROWS), width * g + LANES * h:width * g + LANES * (h + 1)]
                      for g in range(S5_SLAB_GROUPS)]
                ys = _block_transpose(xs)
                for kk in range(half_steps):
                    tok_scr[pl.ds(r0 * CHUNK + half_steps * h + kk, S5_SUB_ROWS, stride=CHUNK), :] = ys[kk]
            return carry

        lax.fori_loop(0, S5_ROW_CHUNK // S5_SUB_ROWS, sub, 0)
        y_ref[tok_rows * c:tok_rows * (c + 1), :] = tok_scr[...].astype(BF16)

    for c in range(rows // S5_ROW_CHUNK):
        outputs(c)


def _s5(u_tok, wb, tsum, wc, a16):
    T, ssm_w = u_tok.shape
    rows = T // CHUNK
    assert rows % S5_ROW_CHUNK == 0 and S5_SLAB_GROUPS == CHUNK // 2
    width = CHUNK * SSM_CH
    pw = PAIR * width
    npair = S5_SLAB_GROUPS // PAIR
    slab = pl.BlockSpec((T, LANES), lambda s: (0, s))
    return pl.pallas_call(
        _s5_body,
        grid=(ssm_w // LANES,),
        in_specs=[slab,
                  pl.BlockSpec((npair, pw, 4 * LANES), lambda s: (s, 0, 0)),
                  pl.BlockSpec((S5_SLAB_GROUPS, width, width), lambda s: (s, 0, 0)),
                  pl.BlockSpec((npair, 4 * LANES, pw), lambda s: (s, 0, 0)),
                  pl.BlockSpec((npair, 4, SUBLANES, LANES), lambda s: (s, 0, 0, 0))],
        out_specs=slab,
        out_shape=jax.ShapeDtypeStruct((T, ssm_w), BF16),
        scratch_shapes=[pltpu.VMEM((rows, S5_SLAB_GROUPS * width), BF16),
                        pltpu.VMEM((rows, 4 * LANES), F32),
                        pltpu.VMEM((rows, 4 * LANES), F32),
                        pltpu.VMEM((rows, npair * 4 * LANES), BF16),
                        pltpu.VMEM((S5_ROW_CHUNK, S5_SLAB_GROUPS * width), F32),
                        pltpu.VMEM((S5_ROW_CHUNK * CHUNK, LANES), F32)],
        compiler_params=_params(("parallel",)),
        name="s5",
    )(u_tok, wb, tsum, wc, a16)


def _s5_operators(pb, pe, tf, a16, ngroups):
    width = CHUNK * SSM_CH
    npair = ngroups // PAIR
    eye = jnp.eye(PAIR, dtype=F32)
    pb = pb.reshape(2, npair, PAIR, 2, CHUNK, SSM_CH, SSM_STATE)
    pe = pe.reshape(2, npair, PAIR, 2, CHUNK, SSM_CH, SSM_STATE)
    wb_dir = jnp.stack([pb[0][:, :, :, ::-1], pb[1]], axis=0)
    wb_dir = wb_dir.reshape(2, npair, PAIR, 2, width, SSM_STATE)
    wb = jnp.einsum('dpgrkn,gh->pgkdrhn', wb_dir, eye).reshape(npair, PAIR * width, 4 * LANES)
    wc_dir = jnp.stack([pe[0], pe[1][:, :, :, ::-1]], axis=0)
    wc_dir = wc_dir.reshape(2, npair, PAIR, 2, width, SSM_STATE)
    sign = jnp.asarray([1.0, -1.0], F32)
    wc = jnp.einsum('dpgrkn,gh,r->pdrgnhk', wc_dir, eye, sign).reshape(npair, 4 * LANES, PAIR * width)
    tf_f = tf[0].reshape(npair * PAIR, width, width)
    tf_b = tf[1].reshape(ngroups, CHUNK, SSM_CH, CHUNK, SSM_CH)[:, ::-1, :, ::-1, :]
    tf_b = tf_b.reshape(npair * PAIR, width, width)
    a = a16.reshape(2, npair, PAIR, 2, SUBLANES, SSM_STATE)
    a = a.transpose(1, 0, 3, 4, 2, 5).reshape(npair, 4, SUBLANES, LANES)
    return wb.astype(BF16), (tf_f + tf_b).astype(BF16), wc.astype(BF16), a


def _merge_body(attn_ref, ys_ref, u_ref, g_ref, x_ref, wba_ref, wglu_ref, bglu_ref, d_ref, wbs_ref,
                wout_ref, gpost_ref, gpre_ref, x1_ref, h2_ref):
    tm, d = x_ref.shape
    u = u_ref[...].reshape(tm, -1).astype(F32)
    y = ys_ref[...].reshape(tm, -1).astype(F32) + d_ref[...] * u
    y = _gelu_tanh(y)
    gl = jnp.dot(y.astype(BF16), wglu_ref[...], preferred_element_type=F32) + bglu_ref[...]
    s = y * jax.nn.sigmoid(gl)
    y_ssm = jnp.dot(s.astype(BF16), wbs_ref[...], preferred_element_type=F32)
    y_attn = jnp.dot(attn_ref[...], wba_ref[...], preferred_element_type=F32)
    merged = g_ref[:, :d].astype(F32) * y_attn + g_ref[:, d:].astype(F32) * y_ssm
    o = jnp.dot(merged.astype(BF16), wout_ref[...], preferred_element_type=F32)
    x1 = x_ref[...] + _rms(o, gpost_ref[...])
    x1_ref[...] = x1
    h2_ref[...] = _rms(x1, gpre_ref[...]).astype(BF16)


def _merge(attn, ys, u, gates, x2, wba, wglu, bglu, ssm_d, wbs, wout, gpost, gpre, seq, tm):
    T, D = x2.shape
    row = lambda w: pl.BlockSpec((tm, w), lambda i: (i, 0))
    chunked = _chunk_rows_spec(tm, seq // tm, u.shape[-1])
    consts = [wba, wglu, bglu, ssm_d, wbs, wout, gpost, gpre]
    return pl.pallas_call(
        _merge_body,
        grid=(T // tm,),
        in_specs=[row(attn.shape[1]), chunked, chunked, row(gates.shape[1]), row(D)]
                 + [_const_spec(c.shape) for c in consts],
        out_specs=[row(D), row(D)],
        out_shape=[jax.ShapeDtypeStruct((T, D), F32), jax.ShapeDtypeStruct((T, D), BF16)],
        compiler_params=_params(("parallel",)),
        name="merge",
    )(attn, ys, u, gates, x2, *consts)


def _ffn_body(tiles_per_seq, hp_ref, hc_ref, hn_ref, x1_ref, wg_ref, wu_ref, cw_ref, cb_ref, wd_ref,
              gn_ref, o_ref, hext, fscr):
    tm = hc_ref.shape[0]
    i = pl.program_id(0) % tiles_per_seq
    zero = jnp.zeros((), BF16)
    hext[0:HALO] = jnp.where(i == 0, zero, hp_ref[...])
    hext[HALO:HALO + tm] = hc_ref[...]
    hext[HALO + tm:] = jnp.where(i == tiles_per_seq - 1, zero, hn_ref[...])
    ext = tm + 2 * HALO
    for c in range(wg_ref.shape[1] // FF_TILE):
        sl = slice(FF_TILE * c, FF_TILE * (c + 1))
        g = jnp.dot(hext[...], wg_ref[:, sl], preferred_element_type=F32)
        up = jnp.dot(hc_ref[...], wu_ref[:, sl], preferred_element_type=F32)
        g_prev = pltpu.roll(g, 1, 0)[HALO:HALO + tm]
        g_next = pltpu.roll(g, ext - 1, 0)[HALO:HALO + tm]
        conv = (g_prev * cw_ref[0:1, sl] + g[HALO:HALO + tm] * cw_ref[1:2, sl]
                + g_next * cw_ref[2:3, sl] + cb_ref[:, sl])
        fscr[:, sl] = (_gelu_tanh(conv) * up).astype(BF16)
    o = jnp.dot(fscr[...], wd_ref[...], preferred_element_type=F32)
    o_ref[...] = x1_ref[...] + _rms(o, gn_ref[...])


def _ffn(h2, x1, wg, wu, cw, cb, wd, gn, seq, tm):
    T, D = x1.shape
    dff = wg.shape[1]
    assert dff % FF_TILE == 0 and tm % HALO == 0 and seq % tm == 0
    per = tm // HALO
    nh = T // HALO
    row = lambda w: pl.BlockSpec((tm, w), lambda i: (i, 0))
    once = lambda shape: pl.BlockSpec(shape, lambda i: (0,) * len(shape), pipeline_mode=pl.Buffered(1))
    return pl.pallas_call(
        functools.partial(_ffn_body, seq // tm),
        grid=(T // tm,),
        in_specs=[pl.BlockSpec((HALO, D), lambda i: (jnp.maximum(i * per - 1, 0), 0)),
                  row(D),
                  pl.BlockSpec((HALO, D), lambda i: (jnp.minimum((i + 1) * per, nh - 1), 0)),
                  row(D),
                  once(wg.shape), once(wu.shape), _const_spec(cw.shape), _const_spec(cb.shape),
                  once(wd.shape), _const_spec(gn.shape)],
        out_specs=row(D),
        out_shape=jax.ShapeDtypeStruct((T, D), F32),
        scratch_shapes=[pltpu.VMEM((tm + 2 * HALO, D), BF16), pltpu.VMEM((tm, dff), BF16)],
        compiler_params=_params(("parallel",)),
        name="ffn",
    )(h2, h2, h2, x1, wg, wu, cw, cb, wd, gn)


def _row_tile(seq):
    return min(512, seq)


def _layer(x, p):
    bsz, seq, d = x.shape
    T = bsz * seq
    tm = _row_tile(seq)
    ssm_w = d // 2
    ngroups = ssm_w // SSM_CH
    attn_w = N_HEADS * HEAD_DIM
    row2 = lambda v: v.reshape(1, -1).astype(F32)

    perm = _head_perm()
    w_in = jnp.concatenate([p["w_in"][:, :attn_w][:, perm], p["w_in"][:, attn_w:]], axis=1).astype(BF16)
    x2 = x.reshape(T, d)
    q, k, v, u, gates = _inproj(x2, row2(p["norm_mix_pre"]), w_in, bsz, seq, tm)

    sink = p["attn_sink"].astype(F32).reshape(N_KV_HEADS, GROUP)
    sink_rows = jnp.repeat(sink, BLOCK, axis=1)[:, :, None]
    attn = _attention(q, k, v, jnp.asarray(_attn_bias()), sink_rows, bsz, seq)

    both = lambda name: jnp.stack([p[name + "_f"], p[name + "_b"]], axis=0).astype(F32)
    n2 = 2 * ngroups
    lre = both("lam_re").reshape(n2, 1, SSM_STATE)
    lim = both("lam_im").reshape(n2, 1, SSM_STATE)
    ldt = jnp.broadcast_to(both("log_dt").reshape(n2, 1, 1), (n2, 1, SSM_STATE))
    btr = both("b_re").transpose(0, 1, 3, 2).reshape(n2, SSM_CH, SSM_STATE)
    bti = both("b_im").transpose(0, 1, 3, 2).reshape(n2, SSM_CH, SSM_STATE)
    cre = both("c_re").reshape(n2, SSM_CH, SSM_STATE)
    cim = both("c_im").reshape(n2, SSM_CH, SSM_STATE)
    pb, pe, tf, a16 = _s5prep(lre, lim, ldt, btr, bti, cre, cim)
    wb, tsum, wc, a = _s5_operators(
        pb.reshape(2, ngroups, 2, CHUNK * SSM_CH, SSM_STATE), pe.reshape(2, ngroups, 2, CHUNK * SSM_CH, SSM_STATE),
        tf.reshape(2, ngroups, CHUNK * SSM_CH, CHUNK * SSM_CH), a16.reshape(2, ngroups, 2, SUBLANES, SSM_STATE),
        ngroups)

    assert bsz == SUBLANES // 2
    ys = _s5(u.reshape(T, ssm_w), wb, tsum, wc, a).reshape(u.shape)

    x1, h2 = _merge(attn, ys, u, gates, x2,
                    p["w_branch_attn"][perm, :].astype(BF16), p["w_glu"].astype(BF16), row2(p["b_glu"]),
                    row2(p["ssm_d"]), p["w_branch_ssm"].astype(BF16), p["w_out"].astype(BF16),
                    row2(p["norm_mix_post"]), row2(p["norm_ffn_pre"]), seq, tm)

    out = _ffn(h2, x1, p["w_ffn_gate"].astype(BF16), p["w_ffn_up"].astype(BF16), p["conv_w"].astype(F32),
               row2(p["conv_b"]), p["w_ffn_down"].astype(BF16), row2(p["norm_ffn_post"]), seq, tm)
    return out.reshape(bsz, seq, d)


_PARAM_NAMES = (
    "norm_mix_pre", "w_in", "attn_sink",
    "lam_re_f", "lam_im_f", "log_dt_f", "b_re_f", "b_im_f", "c_re_f", "c_im_f",
    "lam_re_b", "lam_im_b", "log_dt_b", "b_re_b", "b_im_b", "c_re_b", "c_im_b",
    "ssm_d", "w_glu", "b_glu", "w_branch_attn", "w_branch_ssm", "w_out", "norm_mix_post",
    "norm_ffn_pre", "w_ffn_gate", "w_ffn_up", "conv_w", "conv_b", "w_ffn_down", "norm_ffn_post")


def kernel(x, norm_mix_pre, w_in, attn_sink, lam_re_f, lam_im_f, log_dt_f, b_re_f, b_im_f, c_re_f, c_im_f, lam_re_b, lam_im_b, log_dt_b, b_re_b, b_im_b, c_re_b, c_im_b, ssm_d, w_glu, b_glu, w_branch_attn, w_branch_ssm, w_out, norm_mix_post, norm_ffn_pre, w_ffn_gate, w_ffn_up, conv_w, conv_b, w_ffn_down, norm_ffn_post):
    stacked = dict(zip(_PARAM_NAMES, (
        norm_mix_pre, w_in, attn_sink,
        lam_re_f, lam_im_f, log_dt_f, b_re_f, b_im_f, c_re_f, c_im_f,
        lam_re_b, lam_im_b, log_dt_b, b_re_b, b_im_b, c_re_b, c_im_b,
        ssm_d, w_glu, b_glu, w_branch_attn, w_branch_ssm, w_out, norm_mix_post,
        norm_ffn_pre, w_ffn_gate, w_ffn_up, conv_w, conv_b, w_ffn_down, norm_ffn_post)))
    for layer in range(w_in.shape[0]):
        x = _layer(x, {name: value[layer] for name, value in stacked.items()})
    return x
```

```python
import functools
import math

import numpy as np
import jax
import jax.numpy as jnp
from jax import lax
from jax.experimental import pallas as pl
from jax.experimental.pallas import tpu as pltpu

F32 = jnp.float32
BF16 = jnp.bfloat16

N_HEADS = 8
N_KV_HEADS = 2
HEAD_DIM = 64
GROUP = N_HEADS // N_KV_HEADS
WINDOW = 128
BLOCK = 128
SSM_CH = 16
SSM_STATE = 64
CHUNK = 16
PAIR = 2
N_BRANCH = 2
EPS = 1e-6
NEG_INF = -1e30

LANES = 128
SUBLANES = 8
HALO = 16
FF_TILE = 256
VMEM_LIMIT = 56 * 1024 * 1024


def _gelu_tanh(x):
    return 0.5 * x * (1.0 + jnp.tanh(math.sqrt(2.0 / math.pi) * (x + 0.044715 * (x * x * x))))


def _rms(x, gain):
    return x * lax.rsqrt(jnp.mean(x * x, axis=-1, keepdims=True) + EPS) * gain


def _params(semantics):
    return pltpu.CompilerParams(dimension_semantics=semantics, vmem_limit_bytes=VMEM_LIMIT)


def _const_spec(shape):
    nd = len(shape)
    return pl.BlockSpec(shape, lambda *_: (0,) * nd)


def _inproj_body(splits, x_ref, g_ref, w_ref, q_ref, k_ref, v_ref, u_ref, gate_ref):
    h = _rms(x_ref[...], g_ref[...]).astype(BF16)

    def proj(i):
        return jnp.dot(h, w_ref[:, splits[i]:splits[i + 1]], preferred_element_type=F32)

    q_ref[...] = (proj(0) * (HEAD_DIM ** -0.5)).astype(BF16)
    k_ref[...] = proj(1).astype(BF16)
    v_ref[...] = proj(2).astype(BF16)
    u_ref[...] = proj(3).astype(BF16).reshape(u_ref.shape)
    gate_ref[...] = jax.nn.sigmoid(proj(4)).astype(BF16)


def _chunk_rows_spec(tm, tiles_per_seq, width):
    return pl.BlockSpec((tm // CHUNK, None, CHUNK, width),
                        lambda i: (i % tiles_per_seq, i // tiles_per_seq, 0, 0))


def _inproj(x2, gain, w_in, bsz, seq, tm):
    T, D = x2.shape
    attn_w = N_HEADS * HEAD_DIM
    kv_w = N_KV_HEADS * HEAD_DIM
    widths = [attn_w, kv_w, kv_w, D // 2, N_BRANCH * D]
    splits = [0] + np.cumsum(widths).tolist()
    row = lambda w: pl.BlockSpec((tm, w), lambda i: (i, 0))
    out_specs = [row(w) for w in widths]
    out_shape = [jax.ShapeDtypeStruct((T, w), BF16) for w in widths]
    out_specs[3] = _chunk_rows_spec(tm, seq // tm, widths[3])
    out_shape[3] = jax.ShapeDtypeStruct((seq // CHUNK, bsz, CHUNK, widths[3]), BF16)
    return pl.pallas_call(
        functools.partial(_inproj_body, tuple(splits)),
        grid=(T // tm,),
        in_specs=[row(D), _const_spec((1, D)), _const_spec(w_in.shape)],
        out_specs=out_specs,
        out_shape=out_shape,
        compiler_params=_params(("parallel",)),
        name="inproj",
    )(x2, gain, w_in)


def _attn_body(q_ref, kp_ref, kc_ref, kn_ref, vp_ref, vc_ref, vn_ref, bias_ref, sink_ref, o_ref):
    lane = lax.broadcasted_iota(jnp.int32, (1, LANES), 1)
    left = lane < HEAD_DIM
    top = lax.broadcasted_iota(jnp.int32, (LANES, 1), 0) < HEAD_DIM
    zero = jnp.zeros((), BF16)
    kcat = jnp.concatenate([kp_ref[...], kc_ref[...], kn_ref[...]], axis=0)
    vcat = jnp.concatenate([vp_ref[...], vc_ref[...], vn_ref[...]], axis=0)
    for i in range(GROUP):
        qi = q_ref[:, LANES * i:LANES * (i + 1)]
        halves = []
        for g in range(N_KV_HEADS):
            h = g * GROUP + i
            qm = jnp.where(left if g == 0 else jnp.logical_not(left), qi, zero)
            st = lax.dot_general(kcat, qm, (((1,), (1,)), ((), ())), preferred_element_type=F32)
            st = st + bias_ref[h]
            sink = sink_ref[h]
            m = jnp.maximum(jnp.max(st, axis=0, keepdims=True), sink)
            p = jnp.exp(st - m)
            denom = jnp.sum(p, axis=0, keepdims=True) + jnp.exp(sink - m)
            ot = lax.dot_general(vcat, p.astype(BF16), (((0,), (0,)), ((), ())),
                                 preferred_element_type=F32)
            halves.append(ot / denom)
        o_ref[:, LANES * i:LANES * (i + 1)] = jnp.where(top, halves[0], halves[1]).T.astype(BF16)


def _attn_bias():
    t = np.arange(BLOCK)[None, :]
    s = np.arange(3 * BLOCK)[:, None] - BLOCK
    dist = np.abs(t - s).astype(np.float64)
    valid = dist <= WINDOW
    slopes = 2.0 ** (-8.0 * (np.arange(N_HEADS) + 1.0) / N_HEADS)
    out = np.zeros((3, N_HEADS, 3 * BLOCK, BLOCK), np.float32)
    for variant in range(3):
        ok = valid.copy()
        if variant == 0:
            ok[:BLOCK] = False
        if variant == 2:
            ok[2 * BLOCK:] = False
        for h in range(N_HEADS):
            out[variant, h] = np.where(ok, -slopes[h] * dist, NEG_INF)
    return out


def _head_perm():
    cols = []
    for i in range(GROUP):
        for g in range(N_KV_HEADS):
            h = g * GROUP + i
            cols.extend(range(h * HEAD_DIM, (h + 1) * HEAD_DIM))
    return np.asarray(cols, np.int32)


def _attention(q, k, v, bias, sink, bsz, seq):
    nb = seq // BLOCK
    assert nb >= 2
    kvw = k.shape[1]
    aw = q.shape[1]

    def kv_spec(off):
        return pl.BlockSpec((BLOCK, kvw), lambda b, n: (b * nb + jnp.clip(n + off, 0, nb - 1), 0))

    def variant(b, n):
        return (jnp.where(n == 0, 0, jnp.where(n == nb - 1, 2, 1)), 0, 0, 0)

    qspec = pl.BlockSpec((BLOCK, aw), lambda b, n: (b * nb + n, 0))
    return pl.pallas_call(
        _attn_body,
        grid=(bsz, nb),
        in_specs=[qspec, kv_spec(-1), kv_spec(0), kv_spec(1), kv_spec(-1), kv_spec(0), kv_spec(1),
                  pl.BlockSpec((None,) + bias.shape[1:], variant),
                  pl.BlockSpec(memory_space=pltpu.SMEM)],
        out_specs=qspec,
        out_shape=jax.ShapeDtypeStruct(q.shape, BF16),
        compiler_params=_params(("parallel", "arbitrary")),
        name="attn",
    )(q, k, k, k, v, v, v, bias, sink)


def _s5prep_body(lre_ref, lim_ref, ldt_ref, btr_ref, bti_ref, cre_ref, cim_ref,
                 pb_ref, pe_ref, tf_ref, a_ref):
    lre, lim = lre_ref[...], lim_ref[...]
    dt = jnp.exp(ldt_ref[...])
    mag = jnp.exp(lre * dt)
    are, aim = mag * jnp.cos(lim * dt), mag * jnp.sin(lim * dt)
    den = lre * lre + lim * lim
    cfr = ((are - 1.0) * lre + aim * lim) / den
    cfi = (aim * lre - (are - 1.0) * lim) / den
    btr, bti = btr_ref[...], bti_ref[...]
    bbr = btr * cfr - bti * cfi
    bbi = btr * cfi + bti * cfr
    cre, cim = cre_ref[...], cim_ref[...]

    pwr, pwi = jnp.ones_like(are), jnp.zeros_like(are)
    pb_r, pb_i, pe_r, pe_i, e_r, e_i = [], [], [], [], [], []
    for p in range(CHUNK + 1):
        er = cre * pwr - cim * pwi
        ei = cre * pwi + cim * pwr
        if p < CHUNK:
            pb_r.append(bbr * pwr - bbi * pwi)
            pb_i.append(bbr * pwi + bbi * pwr)
            e_r.append(er)
            e_i.append(ei)
        if p >= 1:
            pe_r.append(er)
            pe_i.append(ei)
        if p == CHUNK:
            a_ref[0] = jnp.broadcast_to(pwr, a_ref.shape[1:])
            a_ref[1] = jnp.broadcast_to(pwi, a_ref.shape[1:])
        pwr, pwi = pwr * are - pwi * aim, pwr * aim + pwi * are
    pb_ref[0] = jnp.concatenate(pb_r, axis=0)
    pb_ref[1] = jnp.concatenate(pb_i, axis=0)
    pe_ref[0] = jnp.concatenate(pe_r, axis=0)
    pe_ref[1] = jnp.concatenate(pe_i, axis=0)
    contract = (((1,), (1,)), ((), ()))
    hi = lax.Precision.HIGHEST
    krow = (lax.dot_general(bbr, jnp.concatenate(e_r, axis=0), contract, precision=hi,
                            preferred_element_type=F32)
            - lax.dot_general(bbi, jnp.concatenate(e_i, axis=0), contract, precision=hi,
                              preferred_element_type=F32))
    lane = lax.broadcasted_iota(jnp.int32, krow.shape, 1)
    rows = [krow]
    for kk in range(1, CHUNK):
        rows.append(jnp.where(lane >= SSM_CH * kk, pltpu.roll(krow, SSM_CH * kk, 1), 0.0))
    tf_ref[...] = jnp.concatenate(rows, axis=0)


def _s5prep(lre, lim, ldt, btr, bti, cre, cim):
    n = lre.shape[0]
    width = CHUNK * SSM_CH
    vec = pl.BlockSpec((None, 1, SSM_STATE), lambda i: (i, 0, 0))
    mat = pl.BlockSpec((None, SSM_CH, SSM_STATE), lambda i: (i, 0, 0))
    return pl.pallas_call(
        _s5prep_body,
        grid=(n,),
        in_specs=[vec, vec, vec, mat, mat, mat, mat],
        out_specs=[pl.BlockSpec((None, 2, width, SSM_STATE), lambda i: (i, 0, 0, 0)),
                   pl.BlockSpec((None, 2, width, SSM_STATE), lambda i: (i, 0, 0, 0)),
                   pl.BlockSpec((None, width, width), lambda i: (i, 0, 0)),
                   pl.BlockSpec((None, 2, SUBLANES, SSM_STATE), lambda i: (i, 0, 0, 0))],
        out_shape=[jax.ShapeDtypeStruct((n, 2, width, SSM_STATE), F32),
                   jax.ShapeDtypeStruct((n, 2, width, SSM_STATE), F32),
                   jax.ShapeDtypeStruct((n, width, width), F32),
                   jax.ShapeDtypeStruct((n, 2, SUBLANES, SSM_STATE), F32)],
        compiler_params=_params(("parallel",)),
        name="s5prep",
    )(lre, lim, ldt, btr, bti, cre, cim)


def _block_transpose(xs):
    n = len(xs)
    assert n * SSM_CH == LANES and n & (n - 1) == 0
    blk = lax.broadcasted_iota(jnp.int32, xs[0].shape, 1) // SSM_CH
    d = n // 2
    while d:
        upper = (blk & d) != 0
        ys = list(xs)
        for a in range(n):
            if a & d:
                continue
            b = a + d
            ys[a] = jnp.where(upper, pltpu.roll(xs[b], SSM_CH * d, 1), xs[a])
            ys[b] = jnp.where(upper, xs[b], pltpu.roll(xs[a], LANES - SSM_CH * d, 1))
        xs = ys
        d //= 2
    return xs


def _s5_scan(a_ref, w_scr, s_scr):
    rows = w_scr.shape[0]
    nsteps = rows // SUBLANES
    half = SUBLANES // 2
    lo = lax.broadcasted_iota(jnp.int32, (SUBLANES, LANES), 0) < half
    afr, afi, abr, abi = a_ref[0], a_ref[1], a_ref[2], a_ref[3]
    col = lambda i: slice(LANES * i, LANES * (i + 1))

    def advance(ar, ai, xr, xi, wr, wi):
        return ar * xr - ai * xi + wr, ar * xi + ai * xr + wi

    def swap(x):
        return pltpu.roll(x, half, 0)

    def step(i, carry):
        fr, fi, br, bi = carry
        rf = pl.ds(pl.multiple_of(i * SUBLANES, SUBLANES), SUBLANES)
        wr, wi = w_scr[rf, col(0)], w_scr[rf, col(1)]
        r0r, r0i = swap(fr), swap(fi)
        t1r, t1i = advance(afr, afi, r0r, r0i, wr, wi)
        r1r, r1i = swap(t1r), swap(t1i)
        t2r, t2i = advance(afr, afi, r1r, r1i, wr, wi)
        s_scr[rf, col(0)] = jnp.where(lo, r0r, r1r)
        s_scr[rf, col(1)] = jnp.where(lo, r0i, r1i)
        fr, fi = jnp.where(lo, t1r, t2r), jnp.where(lo, t1i, t2i)

        rb = pl.ds(pl.multiple_of((nsteps - 1 - i) * SUBLANES, SUBLANES), SUBLANES)
        wr, wi = w_scr[rb, col(2)], w_scr[rb, col(3)]
        r0r, r0i = swap(br), swap(bi)
        t1r, t1i = advance(abr, abi, r0r, r0i, wr, wi)
        r1r, r1i = swap(t1r), swap(t1i)
        t2r, t2i = advance(abr, abi, r1r, r1i, wr, wi)
        s_scr[rb, col(2)] = jnp.where(lo, r1r, r0r)
        s_scr[rb, col(3)] = jnp.where(lo, r1i, r0i)
        br, bi = jnp.where(lo, t2r, t1r), jnp.where(lo, t2i, t1i)
        return fr, fi, br, bi

    z = jnp.zeros((SUBLANES, LANES), F32)
    lax.fori_loop(0, nsteps, step, (z, z, z, z))


S5_SLAB_GROUPS = LANES // SSM_CH
S5_ROW_CHUNK = 256
S5_SUB_ROWS = 32


def _s5_body(u_ref, wb_ref, t_ref, wc_ref, a_ref, y_ref, ur_scr, w_scr, s_scr, sall_scr, yr_scr, tok_scr):
    rows = ur_scr.shape[0]
    width = CHUNK * SSM_CH
    pw = PAIR * width
    npair = S5_SLAB_GROUPS // PAIR
    half_steps = CHUNK // 2
    tok_rows = S5_ROW_CHUNK * CHUNK

    def relayout_in(c):
        tok_scr[...] = u_ref[tok_rows * c:tok_rows * (c + 1), :].astype(F32)

        def sub(i, carry):
            r0 = pl.multiple_of(i * S5_SUB_ROWS, S5_SUB_ROWS)
            for h in range(2):
                xs = [tok_scr[pl.ds(r0 * CHUNK + half_steps * h + kk, S5_SUB_ROWS, stride=CHUNK), :]
                      for kk in range(half_steps)]
                ys = _block_transpose(xs)
                for g in range(S5_SLAB_GROUPS):
                    ur_scr[pl.ds(S5_ROW_CHUNK * c + r0, S5_SUB_ROWS),
                           width * g + LANES * h:width * g + LANES * (h + 1)] = ys[g].astype(BF16)
            return carry

        lax.fori_loop(0, S5_ROW_CHUNK // S5_SUB_ROWS, sub, 0)

    for c in range(rows // S5_ROW_CHUNK):
        relayout_in(c)

    for pp in range(npair):
        u = ur_scr[:, pw * pp:pw * (pp + 1)]
        w_scr[...] = jnp.dot(u, wb_ref[pp], preferred_element_type=F32)
        _s5_scan(a_ref.at[pp], w_scr, s_scr)
        sall_scr[:, 4 * LANES * pp:4 * LANES * (pp + 1)] = s_scr[...].astype(BF16)

    def outputs(c):
        rc = slice(S5_ROW_CHUNK * c, S5_ROW_CHUNK * (c + 1))
        for pp in range(npair):
            y = jnp.dot(sall_scr[rc, 4 * LANES * pp:4 * LANES * (pp + 1)], wc_ref[pp],
                        preferred_element_type=F32)
            for g in range(PAIR):
                lo = pw * pp + width * g
                yr_scr[:, lo:lo + width] = y[:, width * g:width * (g + 1)] + jnp.dot(
                    ur_scr[rc, lo:lo + width], t_ref[PAIR * pp + g], preferred_element_type=F32)

        def sub(i, carry):
            r0 = pl.multiple_of(i * S5_SUB_ROWS, S5_SUB_ROWS)
            for h in range(2):
                xs = [yr_scr[pl.ds(r0, S5_SUB_ROWS), width * g + LANES * h:width * g + LANES * (h + 1)]
                      for g in range(S5_SLAB_GROUPS)]
                ys = _block_transpose(xs)
                for kk in range(half_steps):
                    tok_scr[pl.ds(r0 * CHUNK + half_steps * h + kk, S5_SUB_ROWS, stride=CHUNK), :] = ys[kk]
            return carry

        lax.fori_loop(0, S5_ROW_CHUNK // S5_SUB_ROWS, sub, 0)
        y_ref[tok_rows * c:tok_rows * (c + 1), :] = tok_scr[...].astype(BF16)

    for c in range(rows // S5_ROW_CHUNK):
        outputs(c)


def _s5(u_tok, wb, tsum, wc, a16):
    T, ssm_w = u_tok.shape
    rows = T // CHUNK
    assert rows % S5_ROW_CHUNK == 0 and S5_SLAB_GROUPS == CHUNK // 2
    width = CHUNK * SSM_CH
    pw = PAIR * width
    npair = S5_SLAB_GROUPS // PAIR
    slab = pl.BlockSpec((T, LANES), lambda s: (0, s))
    return pl.pallas_call(
        _s5_body,
        grid=(ssm_w // LANES,),
        in_specs=[slab,
                  pl.BlockSpec((npair, pw, 4 * LANES), lambda s: (s, 0, 0)),
                  pl.BlockSpec((S5_SLAB_GROUPS, width, width), lambda s: (s, 0, 0)),
                  pl.BlockSpec((npair, 4 * LANES, pw), lambda s: (s, 0, 0)),
                  pl.BlockSpec((npair, 4, SUBLANES, LANES), lambda s: (s, 0, 0, 0))],
        out_specs=slab,
        out_shape=jax.ShapeDtypeStruct((T, ssm_w), BF16),
        scratch_shapes=[pltpu.VMEM((rows, S5_SLAB_GROUPS * width), BF16),
                        pltpu.VMEM((rows, 4 * LANES), F32),
                        pltpu.VMEM((rows, 4 * LANES), F32),
                        pltpu.VMEM((rows, npair * 4 * LANES), BF16),
                        pltpu.VMEM((S5_ROW_CHUNK, S5_SLAB_GROUPS * width), F32),
                        pltpu.VMEM((S5_ROW_CHUNK * CHUNK, LANES), F32)],
        compiler_params=_params(("parallel",)),
        name="s5",
    )(u_tok, wb, tsum, wc, a16)


def _s5_operators(pb, pe, tf, a16, ngroups):
    width = CHUNK * SSM_CH
    npair = ngroups // PAIR
    eye = jnp.eye(PAIR, dtype=F32)
    pb = pb.reshape(2, npair, PAIR, 2, CHUNK, SSM_CH, SSM_STATE)
    pe = pe.reshape(2, npair, PAIR, 2, CHUNK, SSM_CH, SSM_STATE)
    wb_dir = jnp.stack([pb[0][:, :, :, ::-1], pb[1]], axis=0)
    wb_dir = wb_dir.reshape(2, npair, PAIR, 2, width, SSM_STATE)
    wb = jnp.einsum('dpgrkn,gh->pgkdrhn', wb_dir, eye).reshape(npair, PAIR * width, 4 * LANES)
    wc_dir = jnp.stack([pe[0], pe[1][:, :, :, ::-1]], axis=0)
    wc_dir = wc_dir.reshape(2, npair, PAIR, 2, width, SSM_STATE)
    sign = jnp.asarray([1.0, -1.0], F32)
    wc = jnp.einsum('dpgrkn,gh,r->pdrgnhk', wc_dir, eye, sign).reshape(npair, 4 * LANES, PAIR * width)
    tf_f = tf[0].reshape(npair * PAIR, width, width)
    tf_b = tf[1].reshape(ngroups, CHUNK, SSM_CH, CHUNK, SSM_CH)[:, ::-1, :, ::-1, :]
    tf_b = tf_b.reshape(npair * PAIR, width, width)
    a = a16.reshape(2, npair, PAIR, 2, SUBLANES, SSM_STATE)
    a = a.transpose(1, 0, 3, 4, 2, 5).reshape(npair, 4, SUBLANES, LANES)
    return wb.astype(BF16), (tf_f + tf_b).astype(BF16), wc.astype(BF16), a


def _merge_body(attn_ref, ys_ref, u_ref, g_ref, x_ref, wba_ref, wglu_ref, bglu_ref, d_ref, wbs_ref,
                wout_ref, gpost_ref, gpre_ref, x1_ref, h2_ref):
    tm, d = x_ref.shape
    u = u_ref[...].reshape(tm, -1).astype(F32)
    y = ys_ref[...].reshape(tm, -1).astype(F32) + d_ref[...] * u
    y = _gelu_tanh(y)
    gl = jnp.dot(y.astype(BF16), wglu_ref[...], preferred_element_type=F32) + bglu_ref[...]
    s = y * jax.nn.sigmoid(gl)
    y_ssm = jnp.dot(s.astype(BF16), wbs_ref[...], preferred_element_type=F32)
    y_attn = jnp.dot(attn_ref[...], wba_ref[...], preferred_element_type=F32)
    merged = g_ref[:, :d].astype(F32) * y_attn + g_ref[:, d:].astype(F32) * y_ssm
    o = jnp.dot(merged.astype(BF16), wout_ref[...], preferred_element_type=F32)
    x1 = x_ref[...] + _rms(o, gpost_ref[...])
    x1_ref[...] = x1
    h2_ref[...] = _rms(x1, gpre_ref[...]).astype(BF16)


def _merge(attn, ys, u, gates, x2, wba, wglu, bglu, ssm_d, wbs, wout, gpost, gpre, seq, tm):
    T, D = x2.shape
    row = lambda w: pl.BlockSpec((tm, w), lambda i: (i, 0))
    chunked = _chunk_rows_spec(tm, seq // tm, u.shape[-1])
    consts = [wba, wglu, bglu, ssm_d, wbs, wout, gpost, gpre]
    return pl.pallas_call(
        _merge_body,
        grid=(T // tm,),
        in_specs=[row(attn.shape[1]), chunked, chunked, row(gates.shape[1]), row(D)]
                 + [_const_spec(c.shape) for c in consts],
        out_specs=[row(D), row(D)],
        out_shape=[jax.ShapeDtypeStruct((T, D), F32), jax.ShapeDtypeStruct((T, D), BF16)],
        compiler_params=_params(("parallel",)),
        name="merge",
    )(attn, ys, u, gates, x2, *consts)


def _ffn_body(tiles_per_seq, hp_ref, hc_ref, hn_ref, x1_ref, wg_ref, wu_ref, cw_ref, cb_ref, wd_ref,
              gn_ref, o_ref, hext, fscr):
    tm = hc_ref.shape[0]
    i = pl.program_id(0) % tiles_per_seq
    zero = jnp.zeros((), BF16)
    hext[0:HALO] = jnp.where(i == 0, zero, hp_ref[...])
    hext[HALO:HALO + tm] = hc_ref[...]
    hext[HALO + tm:] = jnp.where(i == tiles_per_seq - 1, zero, hn_ref[...])
    ext = tm + 2 * HALO
    for c in range(wg_ref.shape[1] // FF_TILE):
        sl = slice(FF_TILE * c, FF_TILE * (c + 1))
        g = jnp.dot(hext[...], wg_ref[:, sl], preferred_element_type=F32)
        up = jnp.dot(hc_ref[...], wu_ref[:, sl], preferred_element_type=F32)
        g_prev = pltpu.roll(g, 1, 0)[HALO:HALO + tm]
        g_next = pltpu.roll(g, ext - 1, 0)[HALO:HALO + tm]
        conv = (g_prev * cw_ref[0:1, sl] + g[HALO:HALO + tm] * cw_ref[1:2, sl]
                + g_next * cw_ref[2:3, sl] + cb_ref[:, sl])
        fscr[:, sl] = (_gelu_tanh(conv) * up).astype(BF16)
    o = jnp.dot(fscr[...], wd_ref[...], preferred_element_type=F32)
    o_ref[...] = x1_ref[...] + _rms(o, gn_ref[...])


def _ffn(h2, x1, wg, wu, cw, cb, wd, gn, seq, tm):
    T, D = x1.shape
    dff = wg.shape[1]
    assert dff % FF_TILE == 0 and tm % HALO == 0 and seq % tm == 0
    per = tm // HALO
    nh = T // HALO
    row = lambda w: pl.BlockSpec((tm, w), lambda i: (i, 0))
    once = lambda shape: pl.BlockSpec(shape, lambda i: (0,) * len(shape), pipeline_mode=pl.Buffered(1))
    return pl.pallas_call(
        functools.partial(_ffn_body, seq // tm),
        grid=(T // tm,),
        in_specs=[pl.BlockSpec((HALO, D), lambda i: (jnp.maximum(i * per - 1, 0), 0)),
                  row(D),
                  pl.BlockSpec((HALO, D), lambda i: (jnp.minimum((i + 1) * per, nh - 1), 0)),
                  row(D),
                  once(wg.shape), once(wu.shape), _const_spec(cw.shape), _const_spec(cb.shape),
                  once(wd.shape), _const_spec(gn.shape)],
        out_specs=row(D),
        out_shape=jax.ShapeDtypeStruct((T, D), F32),
        scratch_shapes=[pltpu.VMEM((tm + 2 * HALO, D), BF16), pltpu.VMEM((tm, dff), BF16)],
        compiler_params=_params(("parallel",)),
        name="ffn",
    )(h2, h2, h2, x1, wg, wu, cw, cb, wd, gn)


def _row_tile(seq):
    return min(512, seq)


def _layer(x, p):
    bsz, seq, d = x.shape
    T = bsz * seq
    tm = _row_tile(seq)
    ssm_w = d // 2
    ngroups = ssm_w // SSM_CH
    attn_w = N_HEADS * HEAD_DIM
    row2 = lambda v: v.reshape(1, -1).astype(F32)

    perm = _head_perm()
    w_in = jnp.concatenate([p["w_in"][:, :attn_w][:, perm], p["w_in"][:, attn_w:]], axis=1).astype(BF16)
    x2 = x.reshape(T, d)
    q, k, v, u, gates = _inproj(x2, row2(p["norm_mix_pre"]), w_in, bsz, seq, tm)

    attn = _attention(q, k, v, jnp.asarray(_attn_bias()), p["attn_sink"].astype(F32), bsz, seq)

    both = lambda name: jnp.stack([p[name + "_f"], p[name + "_b"]], axis=0).astype(F32)
    n2 = 2 * ngroups
    lre = both("lam_re").reshape(n2, 1, SSM_STATE)
    lim = both("lam_im").reshape(n2, 1, SSM_STATE)
    ldt = jnp.broadcast_to(both("log_dt").reshape(n2, 1, 1), (n2, 1, SSM_STATE))
    btr = both("b_re").transpose(0, 1, 3, 2).reshape(n2, SSM_CH, SSM_STATE)
    bti = both("b_im").transpose(0, 1, 3, 2).reshape(n2, SSM_CH, SSM_STATE)
    cre = both("c_re").reshape(n2, SSM_CH, SSM_STATE)
    cim = both("c_im").reshape(n2, SSM_CH, SSM_STATE)
    pb, pe, tf, a16 = _s5prep(lre, lim, ldt, btr, bti, cre, cim)
    wb, tsum, wc, a = _s5_operators(
        pb.reshape(2, ngroups, 2, CHUNK * SSM_CH, SSM_STATE), pe.reshape(2, ngroups, 2, CHUNK * SSM_CH, SSM_STATE),
        tf.reshape(2, ngroups, CHUNK * SSM_CH, CHUNK * SSM_CH), a16.reshape(2, ngroups, 2, SUBLANES, SSM_STATE),
        ngroups)

    assert bsz == SUBLANES // 2
    ys = _s5(u.reshape(T, ssm_w), wb, tsum, wc, a).reshape(u.shape)

    x1, h2 = _merge(attn, ys, u, gates, x2,
                    p["w_branch_attn"][perm, :].astype(BF16), p["w_glu"].astype(BF16), row2(p["b_glu"]),
                    row2(p["ssm_d"]), p["w_branch_ssm"].astype(BF16), p["w_out"].astype(BF16),
                    row2(p["norm_mix_post"]), row2(p["norm_ffn_pre"]), seq, tm)

    out = _ffn(h2, x1, p["w_ffn_gate"].astype(BF16), p["w_ffn_up"].astype(BF16), p["conv_w"].astype(F32),
               row2(p["conv_b"]), p["w_ffn_down"].astype(BF16), row2(p["norm_ffn_post"]), seq, tm)
    return out.reshape(bsz, seq, d)


_PARAM_NAMES = (
    "norm_mix_pre", "w_in", "attn_sink",
    "lam_re_f", "lam_im_f", "log_dt_f", "b_re_f", "b_im_f", "c_re_f", "c_im_f",
    "lam_re_b", "lam_im_b", "log_dt_b", "b_re_b", "b_im_b", "c_re_b", "c_im_b",
    "ssm_d", "w_glu", "b_glu", "w_branch_attn", "w_branch_ssm", "w_out", "norm_mix_post",
    "norm_ffn_pre", "w_ffn_gate", "w_ffn_up", "conv_w", "conv_b", "w_ffn_down", "norm_ffn_post")


def kernel(x, norm_mix_pre, w_in, attn_sink, lam_re_f, lam_im_f, log_dt_f, b_re_f, b_im_f, c_re_f, c_im_f, lam_re_b, lam_im_b, log_dt_b, b_re_b, b_im_b, c_re_b, c_im_b, ssm_d, w_glu, b_glu, w_branch_attn, w_branch_ssm, w_out, norm_mix_post, norm_ffn_pre, w_ffn_gate, w_ffn_up, conv_w, conv_b, w_ffn_down, norm_ffn_post):
    stacked = dict(zip(_PARAM_NAMES, (
        norm_mix_pre, w_in, attn_sink,
        lam_re_f, lam_im_f, log_dt_f, b_re_f, b_im_f, c_re_f, c_im_f,
        lam_re_b, lam_im_b, log_dt_b, b_re_b, b_im_b, c_re_b, c_im_b,
        ssm_d, w_glu, b_glu, w_branch_attn, w_branch_ssm, w_out, norm_mix_post,
        norm_ffn_pre, w_ffn_gate, w_ffn_up, conv_w, conv_b, w_ffn_down, norm_ffn_post)))
    for layer in range(w_in.shape[0]):
        x = _layer(x, {name: value[layer] for name, value in stacked.items()})
    return x
```

```python
import functools
import math

import numpy as np
import jax
import jax.numpy as jnp
from jax import lax
from jax.experimental import pallas as pl
from jax.experimental.pallas import tpu as pltpu

F32 = jnp.float32
BF16 = jnp.bfloat16

N_HEADS = 8
N_KV_HEADS = 2
HEAD_DIM = 64
GROUP = N_HEADS // N_KV_HEADS
WINDOW = 128
BLOCK = 128
SSM_CH = 16
SSM_STATE = 64
CHUNK = 16
PAIR = 2
N_BRANCH = 2
EPS = 1e-6
NEG_INF = -1e30

LANES = 128
SUBLANES = 8
HALO = 16
FF_TILE = 256
VMEM_LIMIT = 56 * 1024 * 1024


def _gelu_tanh(x):
    return 0.5 * x * (1.0 + jnp.tanh(math.sqrt(2.0 / math.pi) * (x + 0.044715 * (x * x * x))))


def _rms(x, gain):
    return x * lax.rsqrt(jnp.mean(x * x, axis=-1, keepdims=True) + EPS) * gain


def _params(semantics):
    return pltpu.CompilerParams(dimension_semantics=semantics, vmem_limit_bytes=VMEM_LIMIT)


def _const_spec(shape):
    nd = len(shape)
    return pl.BlockSpec(shape, lambda *_: (0,) * nd)


def _inproj_body(splits, x_ref, g_ref, w_ref, q_ref, k_ref, v_ref, u_ref, gate_ref):
    h = _rms(x_ref[...], g_ref[...]).astype(BF16)

    def proj(i):
        return jnp.dot(h, w_ref[:, splits[i]:splits[i + 1]], preferred_element_type=F32)

    q_ref[...] = (proj(0) * (HEAD_DIM ** -0.5)).astype(BF16)
    k_ref[...] = proj(1).astype(BF16)
    v_ref[...] = proj(2).astype(BF16)
    u_ref[...] = proj(3).astype(BF16).reshape(u_ref.shape)
    gate_ref[...] = jax.nn.sigmoid(proj(4)).astype(BF16)


def _chunk_rows_spec(tm, tiles_per_seq, width):
    return pl.BlockSpec((tm // CHUNK, None, CHUNK, width),
                        lambda i: (i % tiles_per_seq, i // tiles_per_seq, 0, 0))


def _inproj(x2, gain, w_in, bsz, seq, tm):
    T, D = x2.shape
    attn_w = N_HEADS * HEAD_DIM
    kv_w = N_KV_HEADS * HEAD_DIM
    widths = [attn_w, kv_w, kv_w, D // 2, N_BRANCH * D]
    splits = [0] + np.cumsum(widths).tolist()
    row = lambda w: pl.BlockSpec((tm, w), lambda i: (i, 0))
    out_specs = [row(w) for w in widths]
    out_shape = [jax.ShapeDtypeStruct((T, w), BF16) for w in widths]
    out_specs[3] = _chunk_rows_spec(tm, seq // tm, widths[3])
    out_shape[3] = jax.ShapeDtypeStruct((seq // CHUNK, bsz, CHUNK, widths[3]), BF16)
    return pl.pallas_call(
        functools.partial(_inproj_body, tuple(splits)),
        grid=(T // tm,),
        in_specs=[row(D), _const_spec((1, D)), _const_spec(w_in.shape)],
        out_specs=out_specs,
        out_shape=out_shape,
        compiler_params=_params(("parallel",)),
        name="inproj",
    )(x2, gain, w_in)


def _attn_body(q_ref, kp_ref, kc_ref, kn_ref, vp_ref, vc_ref, vn_ref, bias_ref, sink_ref, o_ref):
    lane = lax.broadcasted_iota(jnp.int32, (1, LANES), 1)
    left = lane < HEAD_DIM
    top = lax.broadcasted_iota(jnp.int32, (LANES, 1), 0) < HEAD_DIM
    zero = jnp.zeros((), BF16)
    kcat = jnp.concatenate([kp_ref[...], kc_ref[...], kn_ref[...]], axis=0)
    vcat = jnp.concatenate([vp_ref[...], vc_ref[...], vn_ref[...]], axis=0)
    for i in range(GROUP):
        qi = q_ref[:, LANES * i:LANES * (i + 1)]
        halves = []
        for g in range(N_KV_HEADS):
            h = g * GROUP + i
            qm = jnp.where(left if g == 0 else jnp.logical_not(left), qi, zero)
            st = lax.dot_general(kcat, qm, (((1,), (1,)), ((), ())), preferred_element_type=F32)
            st = st + bias_ref[h]
            sink = sink_ref[h]
            m = jnp.maximum(jnp.max(st, axis=0, keepdims=True), sink)
            p = jnp.exp(st - m)
            denom = jnp.sum(p, axis=0, keepdims=True) + jnp.exp(sink - m)
            ot = lax.dot_general(vcat, p.astype(BF16), (((0,), (0,)), ((), ())),
                                 preferred_element_type=F32)
            halves.append(ot / denom)
        o_ref[:, LANES * i:LANES * (i + 1)] = jnp.where(top, halves[0], halves[1]).T.astype(BF16)


def _attn_bias():
    t = np.arange(BLOCK)[None, :]
    s = np.arange(3 * BLOCK)[:, None] - BLOCK
    dist = np.abs(t - s).astype(np.float64)
    valid = dist <= WINDOW
    slopes = 2.0 ** (-8.0 * (np.arange(N_HEADS) + 1.0) / N_HEADS)
    out = np.zeros((3, N_HEADS, 3 * BLOCK, BLOCK), np.float32)
    for variant in range(3):
        ok = valid.copy()
        if variant == 0:
            ok[:BLOCK] = False
        if variant == 2:
            ok[2 * BLOCK:] = False
        for h in range(N_HEADS):
            out[variant, h] = np.where(ok, -slopes[h] * dist, NEG_INF)
    return out


def _head_perm():
    cols = []
    for i in range(GROUP):
        for g in range(N_KV_HEADS):
            h = g * GROUP + i
            cols.extend(range(h * HEAD_DIM, (h + 1) * HEAD_DIM))
    return np.asarray(cols, np.int32)


def _attention(q, k, v, bias, sink, bsz, seq):
    nb = seq // BLOCK
    assert nb >= 2
    kvw = k.shape[1]
    aw = q.shape[1]

    def kv_spec(off):
        return pl.BlockSpec((BLOCK, kvw), lambda b, n: (b * nb + jnp.clip(n + off, 0, nb - 1), 0))

    def variant(b, n):
        return (jnp.where(n == 0, 0, jnp.where(n == nb - 1, 2, 1)), 0, 0, 0)

    qspec = pl.BlockSpec((BLOCK, aw), lambda b, n: (b * nb + n, 0))
    return pl.pallas_call(
        _attn_body,
        grid=(bsz, nb),
        in_specs=[qspec, kv_spec(-1), kv_spec(0), kv_spec(1), kv_spec(-1), kv_spec(0), kv_spec(1),
                  pl.BlockSpec((None,) + bias.shape[1:], variant),
                  pl.BlockSpec(memory_space=pltpu.SMEM)],
        out_specs=qspec,
        out_shape=jax.ShapeDtypeStruct(q.shape, BF16),
        compiler_params=_params(("parallel", "arbitrary")),
        name="attn",
    )(q, k, k, k, v, v, v, bias, sink)


def _s5prep_body(lam_ref, ldt_ref, bt_ref, c_ref, wb_ref, wct_ref, t_ref, a_ref):
    width = CHUNK * SSM_CH
    lane = lax.broadcasted_iota(jnp.int32, (1, LANES), 1)
    in_group = [lane < SSM_STATE, lane >= SSM_STATE]
    lane_t = lax.broadcasted_iota(jnp.int32, (SSM_CH, width), 1)
    contract = (((1,), (1,)), ((), ()))
    hi = lax.Precision.HIGHEST
    toeplitz = [[None] * CHUNK for _ in range(PAIR)]
    for d in range(2):
        lre, lim = lam_ref[d, 0], lam_ref[d, 1]
        dt = jnp.exp(ldt_ref[d])
        mag = jnp.exp(lre * dt)
        are, aim = mag * jnp.cos(lim * dt), mag * jnp.sin(lim * dt)
        den = lre * lre + lim * lim
        cfr = ((are - 1.0) * lre + aim * lim) / den
        cfi = (aim * lre - (are - 1.0) * lim) / den
        btr, bti = bt_ref[d, 0], bt_ref[d, 1]
        bbr = btr * cfr - bti * cfi
        bbi = btr * cfi + bti * cfr
        cre, cim = c_ref[d, 0], c_ref[d, 1]

        pwr, pwi = jnp.ones_like(are), jnp.zeros_like(are)
        e_r, e_i = [], []
        for p in range(CHUNK + 1):
            er = cre * pwr - cim * pwi
            ei = cre * pwi + cim * pwr
            if p < CHUNK:
                e_r.append(er)
                e_i.append(ei)
                k = CHUNK - 1 - p if d == 0 else p
                for ri, val in enumerate((bbr * pwr - bbi * pwi, bbr * pwi + bbi * pwr)):
                    for g in range(PAIR):
                        wb_ref[width * g + SSM_CH * k:width * g + SSM_CH * (k + 1),
                               LANES * (2 * d + ri):LANES * (2 * d + ri + 1)] = jnp.where(
                                   in_group[g], val, 0.0).astype(BF16)
            if p >= 1:
                r = p - 1 if d == 0 else CHUNK - p
                for ri, val in enumerate((er, -ei)):
                    for g in range(PAIR):
                        wct_ref[width * g + SSM_CH * r:width * g + SSM_CH * (r + 1),
                                LANES * (2 * d + ri):LANES * (2 * d + ri + 1)] = jnp.where(
                                    in_group[g], val, 0.0).astype(BF16)
            if p == CHUNK:
                a_ref[2 * d] = jnp.broadcast_to(pwr, a_ref.shape[1:])
                a_ref[2 * d + 1] = jnp.broadcast_to(pwi, a_ref.shape[1:])
            pwr, pwi = pwr * are - pwi * aim, pwr * aim + pwi * are

        if d == 1:
            e_r, e_i = e_r[::-1], e_i[::-1]
        ecat_r, ecat_i = jnp.concatenate(e_r, axis=0), jnp.concatenate(e_i, axis=0)
        for g in range(PAIR):
            gr, gi = jnp.where(in_group[g], bbr, 0.0), jnp.where(in_group[g], bbi, 0.0)
            krow = (lax.dot_general(gr, ecat_r, contract, precision=hi, preferred_element_type=F32)
                    - lax.dot_general(gi, ecat_i, contract, precision=hi, preferred_element_type=F32))
            for k in range(CHUNK):
                if d == 0:
                    shift, keep = SSM_CH * k, lane_t >= SSM_CH * k
                else:
                    shift, keep = (SSM_CH * (k + 1)) % width, lane_t < SSM_CH * (k + 1)
                blk = jnp.where(keep, pltpu.roll(krow, shift, 1) if shift else krow, 0.0)
                toeplitz[g][k] = blk if d == 0 else toeplitz[g][k] + blk
    for g in range(PAIR):
        t_ref[g] = jnp.concatenate(toeplitz[g], axis=0).astype(BF16)


def _s5prep(lam, ldt, bt, c):
    npair = lam.shape[1]
    width = CHUNK * SSM_CH
    pw = PAIR * width
    return pl.pallas_call(
        _s5prep_body,
        grid=(npair,),
        in_specs=[pl.BlockSpec((2, None, 2, 1, LANES), lambda i: (0, i, 0, 0, 0)),
                  pl.BlockSpec((2, None, 1, LANES), lambda i: (0, i, 0, 0)),
                  pl.BlockSpec((2, None, 2, SSM_CH, LANES), lambda i: (0, i, 0, 0, 0)),
                  pl.BlockSpec((2, None, 2, SSM_CH, LANES), lambda i: (0, i, 0, 0, 0))],
        out_specs=[pl.BlockSpec((None, pw, 4 * LANES), lambda i: (i, 0, 0)),
                   pl.BlockSpec((None, pw, 4 * LANES), lambda i: (i, 0, 0)),
                   pl.BlockSpec((PAIR, width, width), lambda i: (i, 0, 0)),
                   pl.BlockSpec((None, 4, SUBLANES, LANES), lambda i: (i, 0, 0, 0))],
        out_shape=[jax.ShapeDtypeStruct((npair, pw, 4 * LANES), BF16),
                   jax.ShapeDtypeStruct((npair, pw, 4 * LANES), BF16),
                   jax.ShapeDtypeStruct((npair * PAIR, width, width), BF16),
                   jax.ShapeDtypeStruct((npair, 4, SUBLANES, LANES), F32)],
        compiler_params=_params(("parallel",)),
        name="s5prep",
    )(lam, ldt, bt, c)


def _block_transpose(xs):
    n = len(xs)
    assert n * SSM_CH == LANES and n & (n - 1) == 0
    blk = lax.broadcasted_iota(jnp.int32, xs[0].shape, 1) // SSM_CH
    d = n // 2
    while d:
        upper = (blk & d) != 0
        ys = list(xs)
        for a in range(n):
            if a & d:
                continue
            b = a + d
            ys[a] = jnp.where(upper, pltpu.roll(xs[b], SSM_CH * d, 1), xs[a])
            ys[b] = jnp.where(upper, xs[b], pltpu.roll(xs[a], LANES - SSM_CH * d, 1))
        xs = ys
        d //= 2
    return xs


def _s5_scan(a_ref, w_scr, s_scr):
    rows = w_scr.shape[0]
    nsteps = rows // SUBLANES
    half = SUBLANES // 2
    lo = lax.broadcasted_iota(jnp.int32, (SUBLANES, LANES), 0) < half
    afr, afi, abr, abi = a_ref[0], a_ref[1], a_ref[2], a_ref[3]
    col = lambda i: slice(LANES * i, LANES * (i + 1))

    def advance(ar, ai, xr, xi, wr, wi):
        return ar * xr - ai * xi + wr, ar * xi + ai * xr + wi

    def swap(x):
        return pltpu.roll(x, half, 0)

    def step(i, carry):
        fr, fi, br, bi = carry
        rf = pl.ds(pl.multiple_of(i * SUBLANES, SUBLANES), SUBLANES)
        wr, wi = w_scr[rf, col(0)], w_scr[rf, col(1)]
        r0r, r0i = swap(fr), swap(fi)
        t1r, t1i = advance(afr, afi, r0r, r0i, wr, wi)
        r1r, r1i = swap(t1r), swap(t1i)
        t2r, t2i = advance(afr, afi, r1r, r1i, wr, wi)
        s_scr[rf, col(0)] = jnp.where(lo, r0r, r1r)
        s_scr[rf, col(1)] = jnp.where(lo, r0i, r1i)
        fr, fi = jnp.where(lo, t1r, t2r), jnp.where(lo, t1i, t2i)

        rb = pl.ds(pl.multiple_of((nsteps - 1 - i) * SUBLANES, SUBLANES), SUBLANES)
        wr, wi = w_scr[rb, col(2)], w_scr[rb, col(3)]
        r0r, r0i = swap(br), swap(bi)
        t1r, t1i = advance(abr, abi, r0r, r0i, wr, wi)
        r1r, r1i = swap(t1r), swap(t1i)
        t2r, t2i = advance(abr, abi, r1r, r1i, wr, wi)
        s_scr[rb, col(2)] = jnp.where(lo, r1r, r0r)
        s_scr[rb, col(3)] = jnp.where(lo, r1i, r0i)
        br, bi = jnp.where(lo, t2r, t1r), jnp.where(lo, t2i, t1i)
        return fr, fi, br, bi

    z = jnp.zeros((SUBLANES, LANES), F32)
    lax.fori_loop(0, nsteps, step, (z, z, z, z))


S5_SLAB_GROUPS = LANES // SSM_CH
S5_ROW_CHUNK = 256
S5_SUB_ROWS = 32


def _s5_body(u_ref, wb_ref, t_ref, wct_ref, a_ref, y_ref, ur_scr, w_scr, s_scr, sall_scr, yr_scr, tok_scr):
    rows = ur_scr.shape[0]
    width = CHUNK * SSM_CH
    pw = PAIR * width
    npair = S5_SLAB_GROUPS // PAIR
    half_steps = CHUNK // 2
    tok_rows = S5_ROW_CHUNK * CHUNK

    def relayout_in(c):
        tok_scr[...] = u_ref[tok_rows * c:tok_rows * (c + 1), :].astype(F32)

        def sub(i, carry):
            r0 = pl.multiple_of(i * S5_SUB_ROWS, S5_SUB_ROWS)
            for h in range(2):
                xs = [tok_scr[pl.ds(r0 * CHUNK + half_steps * h + kk, S5_SUB_ROWS, stride=CHUNK), :]
                      for kk in range(half_steps)]
                ys = _block_transpose(xs)
                for g in range(S5_SLAB_GROUPS):
                    ur_scr[pl.ds(S5_ROW_CHUNK * c + r0, S5_SUB_ROWS),
                           width * g + LANES * h:width * g + LANES * (h + 1)] = ys[g].astype(BF16)
            return carry

        lax.fori_loop(0, S5_ROW_CHUNK // S5_SUB_ROWS, sub, 0)

    for c in range(rows // S5_ROW_CHUNK):
        relayout_in(c)

    for pp in range(npair):
        u = ur_scr[:, pw * pp:pw * (pp + 1)]
        w_scr[...] = jnp.dot(u, wb_ref[pp], preferred_element_type=F32)
        _s5_scan(a_ref.at[pp], w_scr, s_scr)
        sall_scr[:, 4 * LANES * pp:4 * LANES * (pp + 1)] = s_scr[...].astype(BF16)

    def outputs(c):
        rc = slice(S5_ROW_CHUNK * c, S5_ROW_CHUNK * (c + 1))
        for pp in range(npair):
            y = lax.dot_general(sall_scr[rc, 4 * LANES * pp:4 * LANES * (pp + 1)], wct_ref[pp],
                                (((1,), (1,)), ((), ())), preferred_element_type=F32)
            for g in range(PAIR):
                lo = pw * pp + width * g
                yr_scr[:, lo:lo + width] = y[:, width * g:width * (g + 1)] + jnp.dot(
                    ur_scr[rc, lo:lo + width], t_ref[PAIR * pp + g], preferred_element_type=F32)

        def sub(i, carry):
            r0 = pl.multiple_of(i * S5_SUB_ROWS, S5_SUB_ROWS)
            for h in range(2):
                xs = [yr_scr[pl.ds(r0, S5_SUB_ROWS), width * g + LANES * h:width * g + LANES * (h + 1)]
                      for g in range(S5_SLAB_GROUPS)]
                ys = _block_transpose(xs)
                for kk in range(half_steps):
                    tok_scr[pl.ds(r0 * CHUNK + half_steps * h + kk, S5_SUB_ROWS, stride=CHUNK), :] = ys[kk]
            return carry

        lax.fori_loop(0, S5_ROW_CHUNK // S5_SUB_ROWS, sub, 0)
        y_ref[tok_rows * c:tok_rows * (c + 1), :] = tok_scr[...].astype(BF16)

    for c in range(rows // S5_ROW_CHUNK):
        outputs(c)


def _s5(u_tok, wb, tsum, wct, a16):
    T, ssm_w = u_tok.shape
    rows = T // CHUNK
    assert rows % S5_ROW_CHUNK == 0 and S5_SLAB_GROUPS == CHUNK // 2
    width = CHUNK * SSM_CH
    pw = PAIR * width
    npair = S5_SLAB_GROUPS // PAIR
    slab = pl.BlockSpec((T, LANES), lambda s: (0, s))
    return pl.pallas_call(
        _s5_body,
        grid=(ssm_w // LANES,),
        in_specs=[slab,
                  pl.BlockSpec((npair, pw, 4 * LANES), lambda s: (s, 0, 0)),
                  pl.BlockSpec((S5_SLAB_GROUPS, width, width), lambda s: (s, 0, 0)),
                  pl.BlockSpec((npair, pw, 4 * LANES), lambda s: (s, 0, 0)),
                  pl.BlockSpec((npair, 4, SUBLANES, LANES), lambda s: (s, 0, 0, 0))],
        out_specs=slab,
        out_shape=jax.ShapeDtypeStruct((T, ssm_w), BF16),
        scratch_shapes=[pltpu.VMEM((rows, S5_SLAB_GROUPS * width), BF16),
                        pltpu.VMEM((rows, 4 * LANES), F32),
                        pltpu.VMEM((rows, 4 * LANES), F32),
                        pltpu.VMEM((rows, npair * 4 * LANES), BF16),
                        pltpu.VMEM((S5_ROW_CHUNK, S5_SLAB_GROUPS * width), F32),
                        pltpu.VMEM((S5_ROW_CHUNK * CHUNK, LANES), F32)],
        compiler_params=_params(("parallel",)),
        name="s5",
    )(u_tok, wb, tsum, wct, a16)


def _merge_body(attn_ref, ys_ref, u_ref, g_ref, x_ref, wba_ref, wglu_ref, bglu_ref, d_ref, wbs_ref,
                wout_ref, gpost_ref, gpre_ref, x1_ref, h2_ref):
    tm, d = x_ref.shape
    u = u_ref[...].reshape(tm, -1).astype(F32)
    y = ys_ref[...].reshape(tm, -1).astype(F32) + d_ref[...] * u
    y = _gelu_tanh(y)
    gl = jnp.dot(y.astype(BF16), wglu_ref[...], preferred_element_type=F32) + bglu_ref[...]
    s = y * jax.nn.sigmoid(gl)
    y_ssm = jnp.dot(s.astype(BF16), wbs_ref[...], preferred_element_type=F32)
    y_attn = jnp.dot(attn_ref[...], wba_ref[...], preferred_element_type=F32)
    merged = g_ref[:, :d].astype(F32) * y_attn + g_ref[:, d:].astype(F32) * y_ssm
    o = jnp.dot(merged.astype(BF16), wout_ref[...], preferred_element_type=F32)
    x1 = x_ref[...] + _rms(o, gpost_ref[...])
    x1_ref[...] = x1
    h2_ref[...] = _rms(x1, gpre_ref[...]).astype(BF16)


def _merge(attn, ys, u, gates, x2, wba, wglu, bglu, ssm_d, wbs, wout, gpost, gpre, seq, tm):
    T, D = x2.shape
    row = lambda w: pl.BlockSpec((tm, w), lambda i: (i, 0))
    chunked = _chunk_rows_spec(tm, seq // tm, u.shape[-1])
    consts = [wba, wglu, bglu, ssm_d, wbs, wout, gpost, gpre]
    return pl.pallas_call(
        _merge_body,
        grid=(T // tm,),
        in_specs=[row(attn.shape[1]), chunked, chunked, row(gates.shape[1]), row(D)]
                 + [_const_spec(c.shape) for c in consts],
        out_specs=[row(D), row(D)],
        out_shape=[jax.ShapeDtypeStruct((T, D), F32), jax.ShapeDtypeStruct((T, D), BF16)],
        compiler_params=_params(("parallel",)),
        name="merge",
    )(attn, ys, u, gates, x2, *consts)


def _ffn_body(tiles_per_seq, hp_ref, hc_ref, hn_ref, x1_ref, wg_ref, wu_ref, cw_ref, cb_ref, wd_ref,
              gn_ref, o_ref, hext, fscr):
    tm = hc_ref.shape[0]
    i = pl.program_id(0) % tiles_per_seq
    zero = jnp.zeros((), BF16)
    hext[0:HALO] = jnp.where(i == 0, zero, hp_ref[...])
    hext[HALO:HALO + tm] = hc_ref[...]
    hext[HALO + tm:] = jnp.where(i == tiles_per_seq - 1, zero, hn_ref[...])
    ext = tm + 2 * HALO
    for c in range(wg_ref.shape[1] // FF_TILE):
        sl = slice(FF_TILE * c, FF_TILE * (c + 1))
        g = jnp.dot(hext[...], wg_ref[:, sl], preferred_element_type=F32)
        up = jnp.dot(hc_ref[...], wu_ref[:, sl], preferred_element_type=F32)
        g_prev = pltpu.roll(g, 1, 0)[HALO:HALO + tm]
        g_next = pltpu.roll(g, ext - 1, 0)[HALO:HALO + tm]
        conv = (g_prev * cw_ref[0:1, sl] + g[HALO:HALO + tm] * cw_ref[1:2, sl]
                + g_next * cw_ref[2:3, sl] + cb_ref[:, sl])
        fscr[:, sl] = (_gelu_tanh(conv) * up).astype(BF16)
    o = jnp.dot(fscr[...], wd_ref[...], preferred_element_type=F32)
    o_ref[...] = x1_ref[...] + _rms(o, gn_ref[...])


def _ffn(h2, x1, wg, wu, cw, cb, wd, gn, seq, tm):
    T, D = x1.shape
    dff = wg.shape[1]
    assert dff % FF_TILE == 0 and tm % HALO == 0 and seq % tm == 0
    per = tm // HALO
    nh = T // HALO
    row = lambda w: pl.BlockSpec((tm, w), lambda i: (i, 0))
    once = lambda shape: pl.BlockSpec(shape, lambda i: (0,) * len(shape), pipeline_mode=pl.Buffered(1))
    return pl.pallas_call(
        functools.partial(_ffn_body, seq // tm),
        grid=(T // tm,),
        in_specs=[pl.BlockSpec((HALO, D), lambda i: (jnp.maximum(i * per - 1, 0), 0)),
                  row(D),
                  pl.BlockSpec((HALO, D), lambda i: (jnp.minimum((i + 1) * per, nh - 1), 0)),
                  row(D),
                  once(wg.shape), once(wu.shape), _const_spec(cw.shape), _const_spec(cb.shape),
                  once(wd.shape), _const_spec(gn.shape)],
        out_specs=row(D),
        out_shape=jax.ShapeDtypeStruct((T, D), F32),
        scratch_shapes=[pltpu.VMEM((tm + 2 * HALO, D), BF16), pltpu.VMEM((tm, dff), BF16)],
        compiler_params=_params(("parallel",)),
        name="ffn",
    )(h2, h2, h2, x1, wg, wu, cw, cb, wd, gn)


def _row_tile(seq):
    return min(512, seq)


def _layer(x, p):
    bsz, seq, d = x.shape
    T = bsz * seq
    tm = _row_tile(seq)
    ssm_w = d // 2
    ngroups = ssm_w // SSM_CH
    attn_w = N_HEADS * HEAD_DIM
    row2 = lambda v: v.reshape(1, -1).astype(F32)

    perm = _head_perm()
    w_in = jnp.concatenate([p["w_in"][:, :attn_w][:, perm], p["w_in"][:, attn_w:]], axis=1).astype(BF16)
    x2 = x.reshape(T, d)
    q, k, v, u, gates = _inproj(x2, row2(p["norm_mix_pre"]), w_in, bsz, seq, tm)

    attn = _attention(q, k, v, jnp.asarray(_attn_bias()), p["attn_sink"].astype(F32), bsz, seq)

    npair = ngroups // PAIR
    both = lambda re, im: jnp.stack([jnp.stack([p[re + "_f"], p[im + "_f"]]),
                                     jnp.stack([p[re + "_b"], p[im + "_b"]])]).astype(F32)
    lam = both("lam_re", "lam_im").reshape(2, 2, npair, 1, LANES).transpose(0, 2, 1, 3, 4)
    ldt = jnp.repeat(jnp.stack([p["log_dt_f"], p["log_dt_b"]]).astype(F32), SSM_STATE, axis=-1)
    ldt = ldt.reshape(2, npair, 1, LANES)
    bt = both("b_re", "b_im").reshape(2, 2, npair, PAIR, SSM_STATE, SSM_CH)
    bt = bt.transpose(0, 2, 1, 5, 3, 4).reshape(2, npair, 2, SSM_CH, LANES)
    cc = both("c_re", "c_im").reshape(2, 2, npair, PAIR, SSM_CH, SSM_STATE)
    cc = cc.transpose(0, 2, 1, 4, 3, 5).reshape(2, npair, 2, SSM_CH, LANES)
    wb, wct, tsum, a = _s5prep(lam, ldt, bt, cc)

    assert bsz == SUBLANES // 2
    ys = _s5(u.reshape(T, ssm_w), wb, tsum, wct, a).reshape(u.shape)

    x1, h2 = _merge(attn, ys, u, gates, x2,
                    p["w_branch_attn"][perm, :].astype(BF16), p["w_glu"].astype(BF16), row2(p["b_glu"]),
                    row2(p["ssm_d"]), p["w_branch_ssm"].astype(BF16), p["w_out"].astype(BF16),
                    row2(p["norm_mix_post"]), row2(p["norm_ffn_pre"]), seq, tm)

    out = _ffn(h2, x1, p["w_ffn_gate"].astype(BF16), p["w_ffn_up"].astype(BF16), p["conv_w"].astype(F32),
               row2(p["conv_b"]), p["w_ffn_down"].astype(BF16), row2(p["norm_ffn_post"]), seq, tm)
    return out.reshape(bsz, seq, d)


_PARAM_NAMES = (
    "norm_mix_pre", "w_in", "attn_sink",
    "lam_re_f", "lam_im_f", "log_dt_f", "b_re_f", "b_im_f", "c_re_f", "c_im_f",
    "lam_re_b", "lam_im_b", "log_dt_b", "b_re_b", "b_im_b", "c_re_b", "c_im_b",
    "ssm_d", "w_glu", "b_glu", "w_branch_attn", "w_branch_ssm", "w_out", "norm_mix_post",
    "norm_ffn_pre", "w_ffn_gate", "w_ffn_up", "conv_w", "conv_b", "w_ffn_down", "norm_ffn_post")


def kernel(x, norm_mix_pre, w_in, attn_sink, lam_re_f, lam_im_f, log_dt_f, b_re_f, b_im_f, c_re_f, c_im_f, lam_re_b, lam_im_b, log_dt_b, b_re_b, b_im_b, c_re_b, c_im_b, ssm_d, w_glu, b_glu, w_branch_attn, w_branch_ssm, w_out, norm_mix_post, norm_ffn_pre, w_ffn_gate, w_ffn_up, conv_w, conv_b, w_ffn_down, norm_ffn_post):
    stacked = dict(zip(_PARAM_NAMES, (
        norm_mix_pre, w_in, attn_sink,
        lam_re_f, lam_im_f, log_dt_f, b_re_f, b_im_f, c_re_f, c_im_f,
        lam_re_b, lam_im_b, log_dt_b, b_re_b, b_im_b, c_re_b, c_im_b,
        ssm_d, w_glu, b_glu, w_branch_attn, w_branch_ssm, w_out, norm_mix_post,
        norm_ffn_pre, w_ffn_gate, w_ffn_up, conv_w, conv_b, w_ffn_down, norm_ffn_post)))
    for layer in range(w_in.shape[0]):
        x = _layer(x, {name: value[layer] for name, value in stacked.items()})
    return x
```

```python
import functools
import math

import numpy as np
import jax
import jax.numpy as jnp
from jax import lax
from jax.experimental import pallas as pl
from jax.experimental.pallas import tpu as pltpu

F32 = jnp.float32
BF16 = jnp.bfloat16

N_HEADS = 8
N_KV_HEADS = 2
HEAD_DIM = 64
GROUP = N_HEADS // N_KV_HEADS
WINDOW = 128
BLOCK = 128
SSM_CH = 16
SSM_STATE = 64
CHUNK = 16
PAIR = 2
N_BRANCH = 2
EPS = 1e-6
NEG_INF = -1e30

LANES = 128
SUBLANES = 8
HALO = 16
FF_TILE = 256
VMEM_LIMIT = 56 * 1024 * 1024


def _gelu_tanh(x):
    return 0.5 * x * (1.0 + jnp.tanh(math.sqrt(2.0 / math.pi) * (x + 0.044715 * (x * x * x))))


def _rms(x, gain):
    return x * lax.rsqrt(jnp.mean(x * x, axis=-1, keepdims=True) + EPS) * gain


def _params(semantics):
    return pltpu.CompilerParams(dimension_semantics=semantics, vmem_limit_bytes=VMEM_LIMIT)


def _const_spec(shape):
    nd = len(shape)
    return pl.BlockSpec(shape, lambda *_: (0,) * nd)


def _inproj_body(splits, x_ref, g_ref, w_ref, q_ref, k_ref, v_ref, u_ref, gate_ref):
    h = _rms(x_ref[...], g_ref[...]).astype(BF16)

    def proj(i):
        return jnp.dot(h, w_ref[:, splits[i]:splits[i + 1]], preferred_element_type=F32)

    q_ref[...] = (proj(0) * (HEAD_DIM ** -0.5)).astype(BF16)
    k_ref[...] = proj(1).astype(BF16)
    v_ref[...] = proj(2).astype(BF16)
    u_ref[...] = proj(3).astype(BF16).reshape(u_ref.shape)
    gate_ref[...] = jax.nn.sigmoid(proj(4)).astype(BF16)


def _chunk_rows_spec(tm, tiles_per_seq, width):
    return pl.BlockSpec((tm // CHUNK, None, CHUNK, width),
                        lambda i: (i % tiles_per_seq, i // tiles_per_seq, 0, 0))


def _inproj(x2, gain, w_in, bsz, seq, tm):
    T, D = x2.shape
    attn_w = N_HEADS * HEAD_DIM
    kv_w = N_KV_HEADS * HEAD_DIM
    widths = [attn_w, kv_w, kv_w, D // 2, N_BRANCH * D]
    splits = [0] + np.cumsum(widths).tolist()
    row = lambda w: pl.BlockSpec((tm, w), lambda i: (i, 0))
    out_specs = [row(w) for w in widths]
    out_shape = [jax.ShapeDtypeStruct((T, w), BF16) for w in widths]
    out_specs[3] = _chunk_rows_spec(tm, seq // tm, widths[3])
    out_shape[3] = jax.ShapeDtypeStruct((seq // CHUNK, bsz, CHUNK, widths[3]), BF16)
    return pl.pallas_call(
        functools.partial(_inproj_body, tuple(splits)),
        grid=(T // tm,),
        in_specs=[row(D), _const_spec((1, D)), _const_spec(w_in.shape)],
        out_specs=out_specs,
        out_shape=out_shape,
        compiler_params=_params(("parallel",)),
        name="inproj",
    )(x2, gain, w_in)


def _attn_body(q_ref, kp_ref, kc_ref, kn_ref, vp_ref, vc_ref, vn_ref, bias_ref, sink_ref, o_ref):
    lane = lax.broadcasted_iota(jnp.int32, (1, LANES), 1)
    left = lane < HEAD_DIM
    top = lax.broadcasted_iota(jnp.int32, (LANES, 1), 0) < HEAD_DIM
    zero = jnp.zeros((), BF16)
    kcat = jnp.concatenate([kp_ref[...], kc_ref[...], kn_ref[...]], axis=0)
    vcat = jnp.concatenate([vp_ref[...], vc_ref[...], vn_ref[...]], axis=0)
    for i in range(GROUP):
        qi = q_ref[:, LANES * i:LANES * (i + 1)]
        halves = []
        for g in range(N_KV_HEADS):
            h = g * GROUP + i
            qm = jnp.where(left if g == 0 else jnp.logical_not(left), qi, zero)
            st = lax.dot_general(kcat, qm, (((1,), (1,)), ((), ())), preferred_element_type=F32)
            st = st + bias_ref[h]
            sink = sink_ref[h]
            m = jnp.maximum(jnp.max(st, axis=0, keepdims=True), sink)
            p = jnp.exp(st - m)
            denom = jnp.sum(p, axis=0, keepdims=True) + jnp.exp(sink - m)
            ot = lax.dot_general(vcat, p.astype(BF16), (((0,), (0,)), ((), ())),
                                 preferred_element_type=F32)
            halves.append(ot / denom)
        o_ref[:, LANES * i:LANES * (i + 1)] = jnp.where(top, halves[0], halves[1]).T.astype(BF16)


def _attn_bias():
    t = np.arange(BLOCK)[None, :]
    s = np.arange(3 * BLOCK)[:, None] - BLOCK
    dist = np.abs(t - s).astype(np.float64)
    valid = dist <= WINDOW
    slopes = 2.0 ** (-8.0 * (np.arange(N_HEADS) + 1.0) / N_HEADS)
    out = np.zeros((3, N_HEADS, 3 * BLOCK, BLOCK), np.float32)
    for variant in range(3):
        ok = valid.copy()
        if variant == 0:
            ok[:BLOCK] = False
        if variant == 2:
            ok[2 * BLOCK:] = False
        for h in range(N_HEADS):
            out[variant, h] = np.where(ok, -slopes[h] * dist, NEG_INF)
    return out


def _head_perm():
    cols = []
    for i in range(GROUP):
        for g in range(N_KV_HEADS):
            h = g * GROUP + i
            cols.extend(range(h * HEAD_DIM, (h + 1) * HEAD_DIM))
    return np.asarray(cols, np.int32)


def _attention(q, k, v, bias, sink, bsz, seq):
    nb = seq // BLOCK
    assert nb >= 2
    kvw = k.shape[1]
    aw = q.shape[1]

    def kv_spec(off):
        return pl.BlockSpec((BLOCK, kvw), lambda b, n: (b * nb + jnp.clip(n + off, 0, nb - 1), 0))

    def variant(b, n):
        return (jnp.where(n == 0, 0, jnp.where(n == nb - 1, 2, 1)), 0, 0, 0)

    qspec = pl.BlockSpec((BLOCK, aw), lambda b, n: (b * nb + n, 0))
    return pl.pallas_call(
        _attn_body,
        grid=(bsz, nb),
        in_specs=[qspec, kv_spec(-1), kv_spec(0), kv_spec(1), kv_spec(-1), kv_spec(0), kv_spec(1),
                  pl.BlockSpec((None,) + bias.shape[1:], variant),
                  pl.BlockSpec(memory_space=pltpu.SMEM)],
        out_specs=qspec,
        out_shape=jax.ShapeDtypeStruct(q.shape, BF16),
        compiler_params=_params(("parallel", "arbitrary")),
        name="attn",
    )(q, k, k, k, v, v, v, bias, sink)


def _s5prep_body(lam_ref, ldt_ref, bt_ref, c_ref, wb_ref, wct_ref, t_ref, a_ref):
    width = CHUNK * SSM_CH
    lane = lax.broadcasted_iota(jnp.int32, (1, LANES), 1)
    in_group = [lane < SSM_STATE, lane >= SSM_STATE]
    lane_t = lax.broadcasted_iota(jnp.int32, (SSM_CH, width), 1)
    contract = (((1,), (1,)), ((), ()))
    hi = lax.Precision.HIGHEST
    toeplitz = [[None] * CHUNK for _ in range(PAIR)]
    for d in range(2):
        lre, lim = lam_ref[d, 0], lam_ref[d, 1]
        dt = jnp.exp(ldt_ref[d])
        mag = jnp.exp(lre * dt)
        are, aim = mag * jnp.cos(lim * dt), mag * jnp.sin(lim * dt)
        den = lre * lre + lim * lim
        cfr = ((are - 1.0) * lre + aim * lim) / den
        cfi = (aim * lre - (are - 1.0) * lim) / den
        btr, bti = bt_ref[d, 0], bt_ref[d, 1]
        bbr = btr * cfr - bti * cfi
        bbi = btr * cfi + bti * cfr
        cre, cim = c_ref[d, 0], c_ref[d, 1]

        pwr, pwi = jnp.ones_like(are), jnp.zeros_like(are)
        e_r, e_i = [], []
        for p in range(CHUNK + 1):
            er = cre * pwr - cim * pwi
            ei = cre * pwi + cim * pwr
            if p < CHUNK:
                e_r.append(er)
                e_i.append(ei)
                k = CHUNK - 1 - p if d == 0 else p
                for ri, val in enumerate((bbr * pwr - bbi * pwi, bbr * pwi + bbi * pwr)):
                    for g in range(PAIR):
                        wb_ref[width * g + SSM_CH * k:width * g + SSM_CH * (k + 1),
                               LANES * (2 * d + ri):LANES * (2 * d + ri + 1)] = jnp.where(
                                   in_group[g], val, 0.0).astype(BF16)
            if p >= 1:
                r = p - 1 if d == 0 else CHUNK - p
                for ri, val in enumerate((er, -ei)):
                    for g in range(PAIR):
                        wct_ref[width * g + SSM_CH * r:width * g + SSM_CH * (r + 1),
                                LANES * (2 * d + ri):LANES * (2 * d + ri + 1)] = jnp.where(
                                    in_group[g], val, 0.0).astype(BF16)
            if p == CHUNK:
                a_ref[2 * d] = jnp.broadcast_to(pwr, a_ref.shape[1:])
                a_ref[2 * d + 1] = jnp.broadcast_to(pwi, a_ref.shape[1:])
            pwr, pwi = pwr * are - pwi * aim, pwr * aim + pwi * are

        if d == 1:
            e_r, e_i = e_r[::-1], e_i[::-1]
        ecat_r, ecat_i = jnp.concatenate(e_r, axis=0), jnp.concatenate(e_i, axis=0)
        for g in range(PAIR):
            gr, gi = jnp.where(in_group[g], bbr, 0.0), jnp.where(in_group[g], bbi, 0.0)
            krow = (lax.dot_general(gr, ecat_r, contract, precision=hi, preferred_element_type=F32)
                    - lax.dot_general(gi, ecat_i, contract, precision=hi, preferred_element_type=F32))
            for k in range(CHUNK):
                if d == 0:
                    shift, keep = SSM_CH * k, lane_t >= SSM_CH * k
                else:
                    shift, keep = (SSM_CH * (k + 1)) % width, lane_t < SSM_CH * (k + 1)
                blk = jnp.where(keep, pltpu.roll(krow, shift, 1) if shift else krow, 0.0)
                toeplitz[g][k] = blk if d == 0 else toeplitz[g][k] + blk
    for g in range(PAIR):
        t_ref[g] = jnp.concatenate(toeplitz[g], axis=0).astype(BF16)


def _s5prep(lam, ldt, bt, c):
    npair = lam.shape[1]
    width = CHUNK * SSM_CH
    pw = PAIR * width
    return pl.pallas_call(
        _s5prep_body,
        grid=(npair,),
        in_specs=[pl.BlockSpec((2, None, 2, 1, LANES), lambda i: (0, i, 0, 0, 0)),
                  pl.BlockSpec((2, None, 1, LANES), lambda i: (0, i, 0, 0)),
                  pl.BlockSpec((2, None, 2, SSM_CH, LANES), lambda i: (0, i, 0, 0, 0)),
                  pl.BlockSpec((2, None, 2, SSM_CH, LANES), lambda i: (0, i, 0, 0, 0))],
        out_specs=[pl.BlockSpec((None, pw, 4 * LANES), lambda i: (i, 0, 0)),
                   pl.BlockSpec((None, pw, 4 * LANES), lambda i: (i, 0, 0)),
                   pl.BlockSpec((PAIR, width, width), lambda i: (i, 0, 0)),
                   pl.BlockSpec((None, 4, SUBLANES, LANES), lambda i: (i, 0, 0, 0))],
        out_shape=[jax.ShapeDtypeStruct((npair, pw, 4 * LANES), BF16),
                   jax.ShapeDtypeStruct((npair, pw, 4 * LANES), BF16),
                   jax.ShapeDtypeStruct((npair * PAIR, width, width), BF16),
                   jax.ShapeDtypeStruct((npair, 4, SUBLANES, LANES), F32)],
        compiler_params=_params(("parallel",)),
        name="s5prep",
    )(lam, ldt, bt, c)


def _block_transpose(xs):
    n = len(xs)
    assert n * SSM_CH == LANES and n & (n - 1) == 0
    blk = lax.broadcasted_iota(jnp.int32, xs[0].shape, 1) // SSM_CH
    d = n // 2
    while d:
        upper = (blk & d) != 0
        ys = list(xs)
        for a in range(n):
            if a & d:
                continue
            b = a + d
            ys[a] = jnp.where(upper, pltpu.roll(xs[b], SSM_CH * d, 1), xs[a])
            ys[b] = jnp.where(upper, xs[b], pltpu.roll(xs[a], LANES - SSM_CH * d, 1))
        xs = ys
        d //= 2
    return xs


def _s5_scan(a_ref, w_scr, s_scr):
    rows = w_scr.shape[0]
    nsteps = rows // SUBLANES
    nstream = w_scr.shape[1] // (2 * LANES)
    half = SUBLANES // 2
    lo = lax.broadcasted_iota(jnp.int32, (SUBLANES, LANES), 0) < half
    first = [lo, jnp.logical_not(lo)]
    coef = [(a_ref[s // 2, 2 * (s % 2)], a_ref[s // 2, 2 * (s % 2) + 1]) for s in range(nstream)]
    col = lambda i: slice(LANES * i, LANES * (i + 1))

    def advance(a, xr, xi, wr, wi):
        return a[0] * xr - a[1] * xi + wr, a[0] * xi + a[1] * xr + wi

    def swap(x):
        return pltpu.roll(x, half, 0)

    def step(i, carry):
        out = []
        for s in range(nstream):
            d = s % 2
            cr, ci = carry[2 * s], carry[2 * s + 1]
            vreg = i if d == 0 else nsteps - 1 - i
            rs = pl.ds(pl.multiple_of(vreg * SUBLANES, SUBLANES), SUBLANES)
            wr, wi = w_scr[rs, col(2 * s)], w_scr[rs, col(2 * s + 1)]
            r0r, r0i = swap(cr), swap(ci)
            t1r, t1i = advance(coef[s], r0r, r0i, wr, wi)
            r1r, r1i = swap(t1r), swap(t1i)
            t2r, t2i = advance(coef[s], r1r, r1i, wr, wi)
            s_scr[rs, col(2 * s)] = jnp.where(first[d], r0r, r1r)
            s_scr[rs, col(2 * s + 1)] = jnp.where(first[d], r0i, r1i)
            out += [jnp.where(first[d], t1r, t2r), jnp.where(first[d], t1i, t2i)]
        return tuple(out)

    z = jnp.zeros((SUBLANES, LANES), F32)
    lax.fori_loop(0, nsteps, step, (z,) * (2 * nstream))


S5_SLAB_GROUPS = LANES // SSM_CH
S5_ROW_CHUNK = 256
S5_SUB_ROWS = 32
S5_SCAN_PAIRS = 2


def _s5_body(u_ref, wb_ref, t_ref, wct_ref, a_ref, y_ref, ur_scr, w_scr, s_scr, sall_scr, yr_scr, tok_scr):
    rows = ur_scr.shape[0]
    width = CHUNK * SSM_CH
    pw = PAIR * width
    npair = S5_SLAB_GROUPS // PAIR
    half_steps = CHUNK // 2
    tok_rows = S5_ROW_CHUNK * CHUNK

    nchunks = rows // S5_ROW_CHUNK
    nsub = S5_ROW_CHUNK // S5_SUB_ROWS
    sw = 4 * LANES
    row_chunk = lambda c: slice(S5_ROW_CHUNK * c, S5_ROW_CHUNK * (c + 1))
    tok_chunk = lambda c: slice(tok_rows * c, tok_rows * (c + 1))
    step_rows = lambda i, h, kk: pl.ds(S5_SUB_ROWS * i * CHUNK + half_steps * h + kk, S5_SUB_ROWS,
                                       stride=CHUNK)
    lane_blk = lambda g, h: slice(width * g + LANES * h, width * g + LANES * (h + 1))

    def relayout_in(c):
        tok = tok_scr.at[c % 2]
        tok[...] = u_ref[tok_chunk(c), :].astype(F32)
        for i in range(nsub):
            for h in range(2):
                xs = [tok[step_rows(i, h, kk), :] for kk in range(half_steps)]
                ys = _block_transpose(xs)
                for g in range(S5_SLAB_GROUPS):
                    r0 = S5_ROW_CHUNK * c + S5_SUB_ROWS * i
                    ur_scr[r0:r0 + S5_SUB_ROWS, lane_blk(g, h)] = ys[g].astype(BF16)

    def state_in(c, p0):
        for q in range(S5_SCAN_PAIRS):
            pp = p0 + q
            w_scr[row_chunk(c), sw * q:sw * (q + 1)] = jnp.dot(
                ur_scr[row_chunk(c), pw * pp:pw * (pp + 1)], wb_ref[pp], preferred_element_type=F32)

    for p0 in range(0, npair, S5_SCAN_PAIRS):
        for c in range(nchunks):
            if p0 == 0:
                relayout_in(c)
            state_in(c, p0)
        _s5_scan(a_ref.at[p0:p0 + S5_SCAN_PAIRS], w_scr, s_scr)
        sall_scr[:, sw * p0:sw * (p0 + S5_SCAN_PAIRS)] = s_scr[...].astype(BF16)

    def outputs(c):
        yr, tok = yr_scr.at[c % 2], tok_scr.at[c % 2]
        for pp in range(npair):
            y = lax.dot_general(sall_scr[row_chunk(c), sw * pp:sw * (pp + 1)], wct_ref[pp],
                                (((1,), (1,)), ((), ())), preferred_element_type=F32)
            for g in range(PAIR):
                lo = pw * pp + width * g
                yr[:, lo:lo + width] = y[:, width * g:width * (g + 1)] + jnp.dot(
                    ur_scr[row_chunk(c), lo:lo + width], t_ref[PAIR * pp + g], preferred_element_type=F32)
        for i in range(nsub):
            for h in range(2):
                xs = [yr[S5_SUB_ROWS * i:S5_SUB_ROWS * (i + 1), lane_blk(g, h)].astype(BF16)
                      for g in range(S5_SLAB_GROUPS)]
                ys = _block_transpose(xs)
                for kk in range(half_steps):
                    tok[step_rows(i, h, kk), :] = ys[kk].astype(F32)
        y_ref[tok_chunk(c), :] = tok[...].astype(BF16)

    for c in range(nchunks):
        outputs(c)


def _s5(u_tok, wb, tsum, wct, a16):
    T, ssm_w = u_tok.shape
    rows = T // CHUNK
    assert rows % S5_ROW_CHUNK == 0 and S5_SLAB_GROUPS == CHUNK // 2
    width = CHUNK * SSM_CH
    pw = PAIR * width
    npair = S5_SLAB_GROUPS // PAIR
    slab = pl.BlockSpec((T, LANES), lambda s: (0, s))
    return pl.pallas_call(
        _s5_body,
        grid=(ssm_w // LANES,),
        in_specs=[slab,
                  pl.BlockSpec((npair, pw, 4 * LANES), lambda s: (s, 0, 0)),
                  pl.BlockSpec((S5_SLAB_GROUPS, width, width), lambda s: (s, 0, 0)),
                  pl.BlockSpec((npair, pw, 4 * LANES), lambda s: (s, 0, 0)),
                  pl.BlockSpec((npair, 4, SUBLANES, LANES), lambda s: (s, 0, 0, 0))],
        out_specs=slab,
        out_shape=jax.ShapeDtypeStruct((T, ssm_w), BF16),
        scratch_shapes=[pltpu.VMEM((rows, S5_SLAB_GROUPS * width), BF16),
                        pltpu.VMEM((rows, S5_SCAN_PAIRS * 4 * LANES), F32),
                        pltpu.VMEM((rows, S5_SCAN_PAIRS * 4 * LANES), F32),
                        pltpu.VMEM((rows, npair * 4 * LANES), BF16),
                        pltpu.VMEM((2, S5_ROW_CHUNK, S5_SLAB_GROUPS * width), F32),
                        pltpu.VMEM((2, S5_ROW_CHUNK * CHUNK, LANES), F32)],
        compiler_params=_params(("parallel",)),
        name="s5",
    )(u_tok, wb, tsum, wct, a16)


def _merge_body(attn_ref, ys_ref, u_ref, g_ref, x_ref, wba_ref, wglu_ref, bglu_ref, d_ref, wbs_ref,
                wout_ref, gpost_ref, gpre_ref, x1_ref, h2_ref):
    tm, d = x_ref.shape
    u = u_ref[...].reshape(tm, -1).astype(F32)
    y = ys_ref[...].reshape(tm, -1).astype(F32) + d_ref[...] * u
    y = _gelu_tanh(y)
    gl = jnp.dot(y.astype(BF16), wglu_ref[...], preferred_element_type=F32) + bglu_ref[...]
    s = y * jax.nn.sigmoid(gl)
    y_ssm = jnp.dot(s.astype(BF16), wbs_ref[...], preferred_element_type=F32)
    y_attn = jnp.dot(attn_ref[...], wba_ref[...], preferred_element_type=F32)
    merged = g_ref[:, :d].astype(F32) * y_attn + g_ref[:, d:].astype(F32) * y_ssm
    o = jnp.dot(merged.astype(BF16), wout_ref[...], preferred_element_type=F32)
    x1 = x_ref[...] + _rms(o, gpost_ref[...])
    x1_ref[...] = x1
    h2_ref[...] = _rms(x1, gpre_ref[...]).astype(BF16)


def _merge(attn, ys, u, gates, x2, wba, wglu, bglu, ssm_d, wbs, wout, gpost, gpre, seq, tm):
    T, D = x2.shape
    row = lambda w: pl.BlockSpec((tm, w), lambda i: (i, 0))
    chunked = _chunk_rows_spec(tm, seq // tm, u.shape[-1])
    consts = [wba, wglu, bglu, ssm_d, wbs, wout, gpost, gpre]
    return pl.pallas_call(
        _merge_body,
        grid=(T // tm,),
        in_specs=[row(attn.shape[1]), chunked, chunked, row(gates.shape[1]), row(D)]
                 + [_const_spec(c.shape) for c in consts],
        out_specs=[row(D), row(D)],
        out_shape=[jax.ShapeDtypeStruct((T, D), F32), jax.ShapeDtypeStruct((T, D), BF16)],
        compiler_params=_params(("parallel",)),
        name="merge",
    )(attn, ys, u, gates, x2, *consts)


def _ffn_body(tiles_per_seq, hp_ref, hc_ref, hn_ref, x1_ref, wg_ref, wu_ref, cw_ref, cb_ref, wd_ref,
              gn_ref, o_ref, hext, fscr):
    tm = hc_ref.shape[0]
    i = pl.program_id(0) % tiles_per_seq
    zero = jnp.zeros((), BF16)
    hext[0:HALO] = jnp.where(i == 0, zero, hp_ref[...])
    hext[HALO:HALO + tm] = hc_ref[...]
    hext[HALO + tm:] = jnp.where(i == tiles_per_seq - 1, zero, hn_ref[...])
    ext = tm + 2 * HALO
    for c in range(wg_ref.shape[1] // FF_TILE):
        sl = slice(FF_TILE * c, FF_TILE * (c + 1))
        g = jnp.dot(hext[...], wg_ref[:, sl], preferred_element_type=F32)
        up = jnp.dot(hc_ref[...], wu_ref[:, sl], preferred_element_type=F32)
        g_prev = pltpu.roll(g, 1, 0)[HALO:HALO + tm]
        g_next = pltpu.roll(g, ext - 1, 0)[HALO:HALO + tm]
        conv = (g_prev * cw_ref[0:1, sl] + g[HALO:HALO + tm] * cw_ref[1:2, sl]
                + g_next * cw_ref[2:3, sl] + cb_ref[:, sl])
        fscr[:, sl] = (_gelu_tanh(conv) * up).astype(BF16)
    o = jnp.dot(fscr[...], wd_ref[...], preferred_element_type=F32)
    o_ref[...] = x1_ref[...] + _rms(o, gn_ref[...])


def _ffn(h2, x1, wg, wu, cw, cb, wd, gn, seq, tm):
    T, D = x1.shape
    dff = wg.shape[1]
    assert dff % FF_TILE == 0 and tm % HALO == 0 and seq % tm == 0
    per = tm // HALO
    nh = T // HALO
    row = lambda w: pl.BlockSpec((tm, w), lambda i: (i, 0))
    once = lambda shape: pl.BlockSpec(shape, lambda i: (0,) * len(shape), pipeline_mode=pl.Buffered(1))
    return pl.pallas_call(
        functools.partial(_ffn_body, seq // tm),
        grid=(T // tm,),
        in_specs=[pl.BlockSpec((HALO, D), lambda i: (jnp.maximum(i * per - 1, 0), 0)),
                  row(D),
                  pl.BlockSpec((HALO, D), lambda i: (jnp.minimum((i + 1) * per, nh - 1), 0)),
                  row(D),
                  once(wg.shape), once(wu.shape), _const_spec(cw.shape), _const_spec(cb.shape),
                  once(wd.shape), _const_spec(gn.shape)],
        out_specs=row(D),
        out_shape=jax.ShapeDtypeStruct((T, D), F32),
        scratch_shapes=[pltpu.VMEM((tm + 2 * HALO, D), BF16), pltpu.VMEM((tm, dff), BF16)],
        compiler_params=_params(("parallel",)),
        name="ffn",
    )(h2, h2, h2, x1, wg, wu, cw, cb, wd, gn)


def _row_tile(seq):
    return min(512, seq)


def _layer(x, p):
    bsz, seq, d = x.shape
    T = bsz * seq
    tm = _row_tile(seq)
    ssm_w = d // 2
    ngroups = ssm_w // SSM_CH
    attn_w = N_HEADS * HEAD_DIM
    row2 = lambda v: v.reshape(1, -1).astype(F32)

    perm = _head_perm()
    w_in = jnp.concatenate([p["w_in"][:, :attn_w][:, perm], p["w_in"][:, attn_w:]], axis=1).astype(BF16)
    x2 = x.reshape(T, d)
    q, k, v, u, gates = _inproj(x2, row2(p["norm_mix_pre"]), w_in, bsz, seq, tm)

    attn = _attention(q, k, v, jnp.asarray(_attn_bias()), p["attn_sink"].astype(F32), bsz, seq)

    npair = ngroups // PAIR
    both = lambda re, im: jnp.stack([jnp.stack([p[re + "_f"], p[im + "_f"]]),
                                     jnp.stack([p[re + "_b"], p[im + "_b"]])]).astype(F32)
    lam = both("lam_re", "lam_im").reshape(2, 2, npair, 1, LANES).transpose(0, 2, 1, 3, 4)
    ldt = jnp.repeat(jnp.stack([p["log_dt_f"], p["log_dt_b"]]).astype(F32), SSM_STATE, axis=-1)
    ldt = ldt.reshape(2, npair, 1, LANES)
    bt = both("b_re", "b_im").reshape(2, 2, npair, PAIR, SSM_STATE, SSM_CH)
    bt = bt.transpose(0, 2, 1, 5, 3, 4).reshape(2, npair, 2, SSM_CH, LANES)
    cc = both("c_re", "c_im").reshape(2, 2, npair, PAIR, SSM_CH, SSM_STATE)
    cc = cc.transpose(0, 2, 1, 4, 3, 5).reshape(2, npair, 2, SSM_CH, LANES)
    wb, wct, tsum, a = _s5prep(lam, ldt, bt, cc)

    assert bsz == SUBLANES // 2
    ys = _s5(u.reshape(T, ssm_w), wb, tsum, wct, a).reshape(u.shape)

    x1, h2 = _merge(attn, ys, u, gates, x2,
                    p["w_branch_attn"][perm, :].astype(BF16), p["w_glu"].astype(BF16), row2(p["b_glu"]),
                    row2(p["ssm_d"]), p["w_branch_ssm"].astype(BF16), p["w_out"].astype(BF16),
                    row2(p["norm_mix_post"]), row2(p["norm_ffn_pre"]), seq, tm)

    out = _ffn(h2, x1, p["w_ffn_gate"].astype(BF16), p["w_ffn_up"].astype(BF16), p["conv_w"].astype(F32),
               row2(p["conv_b"]), p["w_ffn_down"].astype(BF16), row2(p["norm_ffn_post"]), seq, tm)
    return out.reshape(bsz, seq, d)


_PARAM_NAMES = (
    "norm_mix_pre", "w_in", "attn_sink",
    "lam_re_f", "lam_im_f", "log_dt_f", "b_re_f", "b_im_f", "c_re_f", "c_im_f",
    "lam_re_b", "lam_im_b", "log_dt_b", "b_re_b", "b_im_b", "c_re_b", "c_im_b",
    "ssm_d", "w_glu", "b_glu", "w_branch_attn", "w_branch_ssm", "w_out", "norm_mix_post",
    "norm_ffn_pre", "w_ffn_gate", "w_ffn_up", "conv_w", "conv_b", "w_ffn_down", "norm_ffn_post")


def kernel(x, norm_mix_pre, w_in, attn_sink, lam_re_f, lam_im_f, log_dt_f, b_re_f, b_im_f, c_re_f, c_im_f, lam_re_b, lam_im_b, log_dt_b, b_re_b, b_im_b, c_re_b, c_im_b, ssm_d, w_glu, b_glu, w_branch_attn, w_branch_ssm, w_out, norm_mix_post, norm_ffn_pre, w_ffn_gate, w_ffn_up, conv_w, conv_b, w_ffn_down, norm_ffn_post):
    stacked = dict(zip(_PARAM_NAMES, (
        norm_mix_pre, w_in, attn_sink,
        lam_re_f, lam_im_f, log_dt_f, b_re_f, b_im_f, c_re_f, c_im_f,
        lam_re_b, lam_im_b, log_dt_b, b_re_b, b_im_b, c_re_b, c_im_b,
        ssm_d, w_glu, b_glu, w_branch_attn, w_branch_ssm, w_out, norm_mix_post,
        norm_ffn_pre, w_ffn_gate, w_ffn_up, conv_w, conv_b, w_ffn_down, norm_ffn_post)))
    for layer in range(w_in.shape[0]):
        x = _layer(x, {name: value[layer] for name, value in stacked.items()})
    return x
```

```python
import functools
import math

import numpy as np
import jax
import jax.numpy as jnp
from jax import lax
from jax.experimental import pallas as pl
from jax.experimental.pallas import tpu as pltpu

F32 = jnp.float32
BF16 = jnp.bfloat16

N_HEADS = 8
N_KV_HEADS = 2
HEAD_DIM = 64
GROUP = N_HEADS // N_KV_HEADS
WINDOW = 128
BLOCK = 128
SSM_CH = 16
SSM_STATE = 64
CHUNK = 16
PAIR = 2
N_BRANCH = 2
EPS = 1e-6
NEG_INF = -1e30

LANES = 128
SUBLANES = 8
HALO = 16
FF_TILE = 256
VMEM_LIMIT = 56 * 1024 * 1024


def _gelu_tanh(x):
    return 0.5 * x * (1.0 + jnp.tanh(math.sqrt(2.0 / math.pi) * (x + 0.044715 * (x * x * x))))


def _rms(x, gain):
    return x * lax.rsqrt(jnp.mean(x * x, axis=-1, keepdims=True) + EPS) * gain


def _params(semantics):
    return pltpu.CompilerParams(dimension_semantics=semantics, vmem_limit_bytes=VMEM_LIMIT)


def _const_spec(shape):
    nd = len(shape)
    return pl.BlockSpec(shape, lambda *_: (0,) * nd)


def _inproj_body(splits, x_ref, g_ref, w_ref, q_ref, k_ref, v_ref, u_ref, gate_ref):
    h = _rms(x_ref[...], g_ref[...]).astype(BF16)

    def proj(i):
        return jnp.dot(h, w_ref[:, splits[i]:splits[i + 1]], preferred_element_type=F32)

    q_ref[...] = (proj(0) * (HEAD_DIM ** -0.5)).astype(BF16)
    k_ref[...] = proj(1).astype(BF16)
    v_ref[...] = proj(2).astype(BF16)
    u_ref[...] = proj(3).astype(BF16).reshape(u_ref.shape)
    gate_ref[...] = jax.nn.sigmoid(proj(4)).astype(BF16)


def _chunk_rows_spec(tm, tiles_per_seq, width):
    return pl.BlockSpec((tm // CHUNK, None, CHUNK, width),
                        lambda i: (i % tiles_per_seq, i // tiles_per_seq, 0, 0))


def _inproj(x2, gain, w_in, bsz, seq, tm):
    T, D = x2.shape
    attn_w = N_HEADS * HEAD_DIM
    kv_w = N_KV_HEADS * HEAD_DIM
    widths = [attn_w, kv_w, kv_w, D // 2, N_BRANCH * D]
    splits = [0] + np.cumsum(widths).tolist()
    row = lambda w: pl.BlockSpec((tm, w), lambda i: (i, 0))
    out_specs = [row(w) for w in widths]
    out_shape = [jax.ShapeDtypeStruct((T, w), BF16) for w in widths]
    out_specs[3] = _chunk_rows_spec(tm, seq // tm, widths[3])
    out_shape[3] = jax.ShapeDtypeStruct((seq // CHUNK, bsz, CHUNK, widths[3]), BF16)
    return pl.pallas_call(
        functools.partial(_inproj_body, tuple(splits)),
        grid=(T // tm,),
        in_specs=[row(D), _const_spec((1, D)), _const_spec(w_in.shape)],
        out_specs=out_specs,
        out_shape=out_shape,
        compiler_params=_params(("parallel",)),
        name="inproj",
    )(x2, gain, w_in)


def _attn_body(q_ref, kp_ref, kc_ref, kn_ref, vp_ref, vc_ref, vn_ref, bias_ref, sink_ref, o_ref):
    lane = lax.broadcasted_iota(jnp.int32, (1, LANES), 1)
    left = lane < HEAD_DIM
    top = lax.broadcasted_iota(jnp.int32, (LANES, 1), 0) < HEAD_DIM
    zero = jnp.zeros((), BF16)
    kcat = jnp.concatenate([kp_ref[...], kc_ref[...], kn_ref[...]], axis=0)
    vcat = jnp.concatenate([vp_ref[...], vc_ref[...], vn_ref[...]], axis=0)
    for i in range(GROUP):
        qi = q_ref[:, LANES * i:LANES * (i + 1)]
        halves = []
        for g in range(N_KV_HEADS):
            h = g * GROUP + i
            qm = jnp.where(left if g == 0 else jnp.logical_not(left), qi, zero)
            st = lax.dot_general(kcat, qm, (((1,), (1,)), ((), ())), preferred_element_type=F32)
            st = st + bias_ref[h]
            sink = sink_ref[h]
            m = jnp.maximum(jnp.max(st, axis=0, keepdims=True), sink)
            p = jnp.exp(st - m)
            denom = jnp.sum(p, axis=0, keepdims=True) + jnp.exp(sink - m)
            ot = lax.dot_general(vcat, p.astype(BF16), (((0,), (0,)), ((), ())),
                                 preferred_element_type=F32)
            halves.append(ot / denom)
        o_ref[:, LANES * i:LANES * (i + 1)] = jnp.where(top, halves[0], halves[1]).T.astype(BF16)


def _attn_bias():
    t = np.arange(BLOCK)[None, :]
    s = np.arange(3 * BLOCK)[:, None] - BLOCK
    dist = np.abs(t - s).astype(np.float64)
    valid = dist <= WINDOW
    slopes = 2.0 ** (-8.0 * (np.arange(N_HEADS) + 1.0) / N_HEADS)
    out = np.zeros((3, N_HEADS, 3 * BLOCK, BLOCK), np.float32)
    for variant in range(3):
        ok = valid.copy()
        if variant == 0:
            ok[:BLOCK] = False
        if variant == 2:
            ok[2 * BLOCK:] = False
        for h in range(N_HEADS):
            out[variant, h] = np.where(ok, -slopes[h] * dist, NEG_INF)
    return out


def _head_perm():
    cols = []
    for i in range(GROUP):
        for g in range(N_KV_HEADS):
            h = g * GROUP + i
            cols.extend(range(h * HEAD_DIM, (h + 1) * HEAD_DIM))
    return np.asarray(cols, np.int32)


def _attention(q, k, v, bias, sink, bsz, seq):
    nb = seq // BLOCK
    assert nb >= 2
    kvw = k.shape[1]
    aw = q.shape[1]

    def kv_spec(off):
        return pl.BlockSpec((BLOCK, kvw), lambda b, n: (b * nb + jnp.clip(n + off, 0, nb - 1), 0))

    def variant(b, n):
        return (jnp.where(n == 0, 0, jnp.where(n == nb - 1, 2, 1)), 0, 0, 0)

    qspec = pl.BlockSpec((BLOCK, aw), lambda b, n: (b * nb + n, 0))
    return pl.pallas_call(
        _attn_body,
        grid=(bsz, nb),
        in_specs=[qspec, kv_spec(-1), kv_spec(0), kv_spec(1), kv_spec(-1), kv_spec(0), kv_spec(1),
                  pl.BlockSpec((None,) + bias.shape[1:], variant),
                  pl.BlockSpec(memory_space=pltpu.SMEM)],
        out_specs=qspec,
        out_shape=jax.ShapeDtypeStruct(q.shape, BF16),
        compiler_params=_params(("parallel", "arbitrary")),
        name="attn",
    )(q, k, k, k, v, v, v, bias, sink)


def _s5prep_body(lam_ref, ldt_ref, bt_ref, c_ref, wb_ref, wct_ref, t_ref, a_ref):
    width = CHUNK * SSM_CH
    lane = lax.broadcasted_iota(jnp.int32, (1, LANES), 1)
    in_group = [lane < SSM_STATE, lane >= SSM_STATE]
    lane_t = lax.broadcasted_iota(jnp.int32, (SSM_CH, width), 1)
    contract = (((1,), (1,)), ((), ()))
    hi = lax.Precision.HIGHEST
    toeplitz = [[None] * CHUNK for _ in range(PAIR)]
    for d in range(2):
        lre, lim = lam_ref[d, 0], lam_ref[d, 1]
        dt = jnp.exp(ldt_ref[d])
        mag = jnp.exp(lre * dt)
        are, aim = mag * jnp.cos(lim * dt), mag * jnp.sin(lim * dt)
        den = lre * lre + lim * lim
        cfr = ((are - 1.0) * lre + aim * lim) / den
        cfi = (aim * lre - (are - 1.0) * lim) / den
        btr, bti = bt_ref[d, 0], bt_ref[d, 1]
        bbr = btr * cfr - bti * cfi
        bbi = btr * cfi + bti * cfr
        cre, cim = c_ref[d, 0], c_ref[d, 1]

        pwr, pwi = jnp.ones_like(are), jnp.zeros_like(are)
        e_r, e_i = [], []
        for p in range(CHUNK + 1):
            er = cre * pwr - cim * pwi
            ei = cre * pwi + cim * pwr
            if p < CHUNK:
                e_r.append(er)
                e_i.append(ei)
                k = CHUNK - 1 - p if d == 0 else p
                for ri, val in enumerate((bbr * pwr - bbi * pwi, bbr * pwi + bbi * pwr)):
                    for g in range(PAIR):
                        wb_ref[width * g + SSM_CH * k:width * g + SSM_CH * (k + 1),
                               LANES * (2 * d + ri):LANES * (2 * d + ri + 1)] = jnp.where(
                                   in_group[g], val, 0.0).astype(BF16)
            if p >= 1:
                r = p - 1 if d == 0 else CHUNK - p
                for ri, val in enumerate((er, -ei)):
                    for g in range(PAIR):
                        wct_ref[width * g + SSM_CH * r:width * g + SSM_CH * (r + 1),
                                LANES * (2 * d + ri):LANES * (2 * d + ri + 1)] = jnp.where(
                                    in_group[g], val, 0.0).astype(BF16)
            if p == CHUNK:
                a_ref[2 * d] = jnp.broadcast_to(pwr, a_ref.shape[1:])
                a_ref[2 * d + 1] = jnp.broadcast_to(pwi, a_ref.shape[1:])
            pwr, pwi = pwr * are - pwi * aim, pwr * aim + pwi * are

        if d == 1:
            e_r, e_i = e_r[::-1], e_i[::-1]
        ecat_r, ecat_i = jnp.concatenate(e_r, axis=0), jnp.concatenate(e_i, axis=0)
        for g in range(PAIR):
            gr, gi = jnp.where(in_group[g], bbr, 0.0), jnp.where(in_group[g], bbi, 0.0)
            krow = (lax.dot_general(gr, ecat_r, contract, precision=hi, preferred_element_type=F32)
                    - lax.dot_general(gi, ecat_i, contract, precision=hi, preferred_element_type=F32))
            for k in range(CHUNK):
                if d == 0:
                    shift, keep = SSM_CH * k, lane_t >= SSM_CH * k
                else:
                    shift, keep = (SSM_CH * (k + 1)) % width, lane_t < SSM_CH * (k + 1)
                blk = jnp.where(keep, pltpu.roll(krow, shift, 1) if shift else krow, 0.0)
                toeplitz[g][k] = blk if d == 0 else toeplitz[g][k] + blk
    for g in range(PAIR):
        t_ref[g] = jnp.concatenate(toeplitz[g], axis=0).astype(BF16)


def _s5prep(lam, ldt, bt, c):
    npair = lam.shape[1]
    width = CHUNK * SSM_CH
    pw = PAIR * width
    return pl.pallas_call(
        _s5prep_body,
        grid=(npair,),
        in_specs=[pl.BlockSpec((2, None, 2, 1, LANES), lambda i: (0, i, 0, 0, 0)),
                  pl.BlockSpec((2, None, 1, LANES), lambda i: (0, i, 0, 0)),
                  pl.BlockSpec((2, None, 2, SSM_CH, LANES), lambda i: (0, i, 0, 0, 0)),
                  pl.BlockSpec((2, None, 2, SSM_CH, LANES), lambda i: (0, i, 0, 0, 0))],
        out_specs=[pl.BlockSpec((None, pw, 4 * LANES), lambda i: (i, 0, 0)),
                   pl.BlockSpec((None, pw, 4 * LANES), lambda i: (i, 0, 0)),
                   pl.BlockSpec((PAIR, width, width), lambda i: (i, 0, 0)),
                   pl.BlockSpec((None, 4, SUBLANES, LANES), lambda i: (i, 0, 0, 0))],
        out_shape=[jax.ShapeDtypeStruct((npair, pw, 4 * LANES), BF16),
                   jax.ShapeDtypeStruct((npair, pw, 4 * LANES), BF16),
                   jax.ShapeDtypeStruct((npair * PAIR, width, width), BF16),
                   jax.ShapeDtypeStruct((npair, 4, SUBLANES, LANES), F32)],
        compiler_params=_params(("parallel",)),
        name="s5prep",
    )(lam, ldt, bt, c)


def _block_transpose(xs):
    n = len(xs)
    assert n * SSM_CH == LANES and n & (n - 1) == 0
    blk = lax.broadcasted_iota(jnp.int32, xs[0].shape, 1) // SSM_CH
    d = n // 2
    while d:
        upper = (blk & d) != 0
        ys = list(xs)
        for a in range(n):
            if a & d:
                continue
            b = a + d
            ys[a] = jnp.where(upper, pltpu.roll(xs[b], SSM_CH * d, 1), xs[a])
            ys[b] = jnp.where(upper, xs[b], pltpu.roll(xs[a], LANES - SSM_CH * d, 1))
        xs = ys
        d //= 2
    return xs


def _s5_scan(a_ref, w_scr, s_scr):
    rows = w_scr.shape[0]
    nsteps = rows // SUBLANES
    nstream = w_scr.shape[1] // (2 * LANES)
    half = SUBLANES // 2
    lo = lax.broadcasted_iota(jnp.int32, (SUBLANES, LANES), 0) < half
    first = [lo, jnp.logical_not(lo)]
    coef = [(a_ref[s // 2, 2 * (s % 2)], a_ref[s // 2, 2 * (s % 2) + 1]) for s in range(nstream)]
    col = lambda i: slice(LANES * i, LANES * (i + 1))

    def advance(a, xr, xi, wr, wi):
        return a[0] * xr - a[1] * xi + wr, a[0] * xi + a[1] * xr + wi

    def swap(x):
        return pltpu.roll(x, half, 0)

    def step(i, carry):
        out = []
        for s in range(nstream):
            d = s % 2
            cr, ci = carry[2 * s], carry[2 * s + 1]
            vreg = i if d == 0 else nsteps - 1 - i
            rs = pl.ds(pl.multiple_of(vreg * SUBLANES, SUBLANES), SUBLANES)
            wr, wi = w_scr[rs, col(2 * s)], w_scr[rs, col(2 * s + 1)]
            r0r, r0i = swap(cr), swap(ci)
            t1r, t1i = advance(coef[s], r0r, r0i, wr, wi)
            r1r, r1i = swap(t1r), swap(t1i)
            t2r, t2i = advance(coef[s], r1r, r1i, wr, wi)
            s_scr[rs, col(2 * s)] = jnp.where(first[d], r0r, r1r)
            s_scr[rs, col(2 * s + 1)] = jnp.where(first[d], r0i, r1i)
            out += [jnp.where(first[d], t1r, t2r), jnp.where(first[d], t1i, t2i)]
        return tuple(out)

    z = jnp.zeros((SUBLANES, LANES), F32)
    lax.fori_loop(0, nsteps, step, (z,) * (2 * nstream))


S5_SLAB_GROUPS = LANES // SSM_CH
S5_ROW_CHUNK = 256
S5_SUB_ROWS = 32
S5_SCAN_PAIRS = 2


def _s5_body(u_ref, wb_ref, t_ref, wct_ref, a_ref, y_ref, ur_scr, w_scr, s_scr, sall_scr, yr_scr, tok_scr):
    rows = ur_scr.shape[0]
    width = CHUNK * SSM_CH
    pw = PAIR * width
    npair = S5_SLAB_GROUPS // PAIR
    half_steps = CHUNK // 2
    tok_rows = S5_ROW_CHUNK * CHUNK

    nchunks = rows // S5_ROW_CHUNK
    nsub = S5_ROW_CHUNK // S5_SUB_ROWS
    sw = 4 * LANES
    row_chunk = lambda c: slice(S5_ROW_CHUNK * c, S5_ROW_CHUNK * (c + 1))
    tok_chunk = lambda c: slice(tok_rows * c, tok_rows * (c + 1))
    step_rows = lambda i, h, kk: pl.ds(S5_SUB_ROWS * i * CHUNK + half_steps * h + kk, S5_SUB_ROWS,
                                       stride=CHUNK)
    lane_blk = lambda g, h: slice(width * g + LANES * h, width * g + LANES * (h + 1))

    def relayout_in(c):
        tok = tok_scr.at[c % 2]
        tok[...] = u_ref[tok_chunk(c), :].astype(F32)
        for i in range(nsub):
            for h in range(2):
                xs = [tok[step_rows(i, h, kk), :] for kk in range(half_steps)]
                ys = _block_transpose(xs)
                for g in range(S5_SLAB_GROUPS):
                    r0 = S5_ROW_CHUNK * c + S5_SUB_ROWS * i
                    ur_scr[r0:r0 + S5_SUB_ROWS, lane_blk(g, h)] = ys[g].astype(BF16)

    def state_in(c, p0):
        for q in range(S5_SCAN_PAIRS):
            pp = p0 + q
            w_scr[row_chunk(c), sw * q:sw * (q + 1)] = jnp.dot(
                ur_scr[row_chunk(c), pw * pp:pw * (pp + 1)], wb_ref[pp], preferred_element_type=F32)

    for p0 in range(0, npair, S5_SCAN_PAIRS):
        for c in range(nchunks):
            if p0 == 0:
                relayout_in(c)
            state_in(c, p0)
        _s5_scan(a_ref.at[p0:p0 + S5_SCAN_PAIRS], w_scr, s_scr)
        sall_scr[:, sw * p0:sw * (p0 + S5_SCAN_PAIRS)] = s_scr[...].astype(BF16)

    def outputs(c):
        yr, tok = yr_scr.at[c % 2], tok_scr.at[c % 2]
        for pp in range(npair):
            y = lax.dot_general(sall_scr[row_chunk(c), sw * pp:sw * (pp + 1)], wct_ref[pp],
                                (((1,), (1,)), ((), ())), preferred_element_type=F32)
            for g in range(PAIR):
                lo = pw * pp + width * g
                yr[:, lo:lo + width] = y[:, width * g:width * (g + 1)] + jnp.dot(
                    ur_scr[row_chunk(c), lo:lo + width], t_ref[PAIR * pp + g], preferred_element_type=F32)
        for i in range(nsub):
            for h in range(2):
                xs = [yr[S5_SUB_ROWS * i:S5_SUB_ROWS * (i + 1), lane_blk(g, h)].astype(BF16)
                      for g in range(S5_SLAB_GROUPS)]
                ys = _block_transpose(xs)
                for kk in range(half_steps):
                    tok[step_rows(i, h, kk), :] = ys[kk].astype(F32)
        y_ref[tok_chunk(c), :] = tok[...].astype(BF16)

    for c in range(nchunks):
        outputs(c)


def _s5(u_tok, wb, tsum, wct, a16):
    T, ssm_w = u_tok.shape
    rows = T // CHUNK
    assert rows % S5_ROW_CHUNK == 0 and S5_SLAB_GROUPS == CHUNK // 2
    width = CHUNK * SSM_CH
    pw = PAIR * width
    npair = S5_SLAB_GROUPS // PAIR
    slab = pl.BlockSpec((T, LANES), lambda s: (0, s))
    return pl.pallas_call(
        _s5_body,
        grid=(ssm_w // LANES,),
        in_specs=[slab,
                  pl.BlockSpec((npair, pw, 4 * LANES), lambda s: (s, 0, 0)),
                  pl.BlockSpec((S5_SLAB_GROUPS, width, width), lambda s: (s, 0, 0)),
                  pl.BlockSpec((npair, pw, 4 * LANES), lambda s: (s, 0, 0)),
                  pl.BlockSpec((npair, 4, SUBLANES, LANES), lambda s: (s, 0, 0, 0))],
        out_specs=slab,
        out_shape=jax.ShapeDtypeStruct((T, ssm_w), BF16),
        scratch_shapes=[pltpu.VMEM((rows, S5_SLAB_GROUPS * width), BF16),
                        pltpu.VMEM((rows, S5_SCAN_PAIRS * 4 * LANES), F32),
                        pltpu.VMEM((rows, S5_SCAN_PAIRS * 4 * LANES), F32),
                        pltpu.VMEM((rows, npair * 4 * LANES), BF16),
                        pltpu.VMEM((2, S5_ROW_CHUNK, S5_SLAB_GROUPS * width), F32),
                        pltpu.VMEM((2, S5_ROW_CHUNK * CHUNK, LANES), F32)],
        compiler_params=_params(("parallel",)),
        name="s5",
    )(u_tok, wb, tsum, wct, a16)


def _merge_body(attn_ref, ys_ref, u_ref, g_ref, x_ref, wba_ref, wglu_ref, bglu_ref, d_ref, wbs_ref,
                wout_ref, gpost_ref, gpre_ref, x1_ref, h2_ref):
    tm, d = x_ref.shape
    u = u_ref[...].reshape(tm, -1).astype(F32)
    y = ys_ref[...].reshape(tm, -1).astype(F32) + d_ref[...] * u
    y = _gelu_tanh(y)
    gl = jnp.dot(y.astype(BF16), wglu_ref[...], preferred_element_type=F32) + bglu_ref[...]
    s = y * jax.nn.sigmoid(gl)
    y_ssm = jnp.dot(s.astype(BF16), wbs_ref[...], preferred_element_type=F32)
    y_attn = jnp.dot(attn_ref[...], wba_ref[...], preferred_element_type=F32)
    merged = g_ref[:, :d].astype(F32) * y_attn + g_ref[:, d:].astype(F32) * y_ssm
    o = jnp.dot(merged.astype(BF16), wout_ref[...], preferred_element_type=F32)
    x1 = x_ref[...] + _rms(o, gpost_ref[...])
    x1_ref[...] = x1
    h2_ref[...] = _rms(x1, gpre_ref[...]).astype(BF16)


def _merge(attn, ys, u, gates, x2, wba, wglu, bglu, ssm_d, wbs, wout, gpost, gpre, seq, tm):
    T, D = x2.shape
    row = lambda w: pl.BlockSpec((tm, w), lambda i: (i, 0))
    chunked = _chunk_rows_spec(tm, seq // tm, u.shape[-1])
    consts = [wba, wglu, bglu, ssm_d, wbs, wout, gpost, gpre]
    return pl.pallas_call(
        _merge_body,
        grid=(T // tm,),
        in_specs=[row(attn.shape[1]), chunked, chunked, row(gates.shape[1]), row(D)]
                 + [_const_spec(c.shape) for c in consts],
        out_specs=[row(D), row(D)],
        out_shape=[jax.ShapeDtypeStruct((T, D), F32), jax.ShapeDtypeStruct((T, D), BF16)],
        compiler_params=_params(("parallel",)),
        name="merge",
    )(attn, ys, u, gates, x2, *consts)


def _ffn_body(tiles_per_seq, hp_ref, hc_ref, hn_ref, x1_ref, wg_ref, wu_ref, cw_ref, cb_ref, wd_ref,
              gn_ref, o_ref, hext, fscr):
    tm = hc_ref.shape[0]
    i = pl.program_id(0) % tiles_per_seq
    zero = jnp.zeros((), BF16)
    hext[0:HALO] = jnp.where(i == 0, zero, hp_ref[...])
    hext[HALO:HALO + tm] = hc_ref[...]
    hext[HALO + tm:] = jnp.where(i == tiles_per_seq - 1, zero, hn_ref[...])
    ext = tm + 2 * HALO
    for c in range(wg_ref.shape[1] // FF_TILE):
        sl = slice(FF_TILE * c, FF_TILE * (c + 1))
        g = jnp.dot(hext[...], wg_ref[:, sl], preferred_element_type=F32)
        up = jnp.dot(hc_ref[...], wu_ref[:, sl], preferred_element_type=F32)
        g_prev = pltpu.roll(g, 1, 0)[HALO:HALO + tm]
        g_next = pltpu.roll(g, ext - 1, 0)[HALO:HALO + tm]
        conv = (g_prev * cw_ref[0:1, sl] + g[HALO:HALO + tm] * cw_ref[1:2, sl]
                + g_next * cw_ref[2:3, sl] + cb_ref[:, sl])
        fscr[:, sl] = (_gelu_tanh(conv) * up).astype(BF16)
    o = jnp.dot(fscr[...], wd_ref[...], preferred_element_type=F32)
    o_ref[...] = x1_ref[...] + _rms(o, gn_ref[...])


def _ffn(h2, x1, wg, wu, cw, cb, wd, gn, seq, tm):
    T, D = x1.shape
    dff = wg.shape[1]
    assert dff % FF_TILE == 0 and tm % HALO == 0 and seq % tm == 0
    per = tm // HALO
    nh = T // HALO
    row = lambda w: pl.BlockSpec((tm, w), lambda i: (i, 0))
    once = lambda shape: pl.BlockSpec(shape, lambda i: (0,) * len(shape), pipeline_mode=pl.Buffered(1))
    return pl.pallas_call(
        functools.partial(_ffn_body, seq // tm),
        grid=(T // tm,),
        in_specs=[pl.BlockSpec((HALO, D), lambda i: (jnp.maximum(i * per - 1, 0), 0)),
                  row(D),
                  pl.BlockSpec((HALO, D), lambda i: (jnp.minimum((i + 1) * per, nh - 1), 0)),
                  row(D),
                  once(wg.shape), once(wu.shape), _const_spec(cw.shape), _const_spec(cb.shape),
                  once(wd.shape), _const_spec(gn.shape)],
        out_specs=row(D),
        out_shape=jax.ShapeDtypeStruct((T, D), F32),
        scratch_shapes=[pltpu.VMEM((tm + 2 * HALO, D), BF16), pltpu.VMEM((tm, dff), BF16)],
        compiler_params=_params(("parallel",)),
        name="ffn",
    )(h2, h2, h2, x1, wg, wu, cw, cb, wd, gn)


def _row_tile(seq):
    return min(1024, seq)


def _layer(x, p):
    bsz, seq, d = x.shape
    T = bsz * seq
    tm = _row_tile(seq)
    ssm_w = d // 2
    ngroups = ssm_w // SSM_CH
    attn_w = N_HEADS * HEAD_DIM
    row2 = lambda v: v.reshape(1, -1).astype(F32)

    perm = _head_perm()
    w_in = jnp.concatenate([p["w_in"][:, :attn_w][:, perm], p["w_in"][:, attn_w:]], axis=1).astype(BF16)
    x2 = x.reshape(T, d)
    q, k, v, u, gates = _inproj(x2, row2(p["norm_mix_pre"]), w_in, bsz, seq, tm)

    attn = _attention(q, k, v, jnp.asarray(_attn_bias()), p["attn_sink"].astype(F32), bsz, seq)

    npair = ngroups // PAIR
    both = lambda re, im: jnp.stack([jnp.stack([p[re + "_f"], p[im + "_f"]]),
                                     jnp.stack([p[re + "_b"], p[im + "_b"]])]).astype(F32)
    lam = both("lam_re", "lam_im").reshape(2, 2, npair, 1, LANES).transpose(0, 2, 1, 3, 4)
    ldt = jnp.repeat(jnp.stack([p["log_dt_f"], p["log_dt_b"]]).astype(F32), SSM_STATE, axis=-1)
    ldt = ldt.reshape(2, npair, 1, LANES)
    bt = both("b_re", "b_im").reshape(2, 2, npair, PAIR, SSM_STATE, SSM_CH)
    bt = bt.transpose(0, 2, 1, 5, 3, 4).reshape(2, npair, 2, SSM_CH, LANES)
    cc = both("c_re", "c_im").reshape(2, 2, npair, PAIR, SSM_CH, SSM_STATE)
    cc = cc.transpose(0, 2, 1, 4, 3, 5).reshape(2, npair, 2, SSM_CH, LANES)
    wb, wct, tsum, a = _s5prep(lam, ldt, bt, cc)

    assert bsz == SUBLANES // 2
    ys = _s5(u.reshape(T, ssm_w), wb, tsum, wct, a).reshape(u.shape)

    x1, h2 = _merge(attn, ys, u, gates, x2,
                    p["w_branch_attn"][perm, :].astype(BF16), p["w_glu"].astype(BF16), row2(p["b_glu"]),
                    row2(p["ssm_d"]), p["w_branch_ssm"].astype(BF16), p["w_out"].astype(BF16),
                    row2(p["norm_mix_post"]), row2(p["norm_ffn_pre"]), seq, tm)

    out = _ffn(h2, x1, p["w_ffn_gate"].astype(BF16), p["w_ffn_up"].astype(BF16), p["conv_w"].astype(F32),
               row2(p["conv_b"]), p["w_ffn_down"].astype(BF16), row2(p["norm_ffn_post"]), seq, tm)
    return out.reshape(bsz, seq, d)


_PARAM_NAMES = (
    "norm_mix_pre", "w_in", "attn_sink",
    "lam_re_f", "lam_im_f", "log_dt_f", "b_re_f", "b_im_f", "c_re_f", "c_im_f",
    "lam_re_b", "lam_im_b", "log_dt_b", "b_re_b", "b_im_b", "c_re_b", "c_im_b",
    "ssm_d", "w_glu", "b_glu", "w_branch_attn", "w_branch_ssm", "w_out", "norm_mix_post",
    "norm_ffn_pre", "w_ffn_gate", "w_ffn_up", "conv_w", "conv_b", "w_ffn_down", "norm_ffn_post")


def kernel(x, norm_mix_pre, w_in, attn_sink, lam_re_f, lam_im_f, log_dt_f, b_re_f, b_im_f, c_re_f, c_im_f, lam_re_b, lam_im_b, log_dt_b, b_re_b, b_im_b, c_re_b, c_im_b, ssm_d, w_glu, b_glu, w_branch_attn, w_branch_ssm, w_out, norm_mix_post, norm_ffn_pre, w_ffn_gate, w_ffn_up, conv_w, conv_b, w_ffn_down, norm_ffn_post):
    stacked = dict(zip(_PARAM_NAMES, (
        norm_mix_pre, w_in, attn_sink,
        lam_re_f, lam_im_f, log_dt_f, b_re_f, b_im_f, c_re_f, c_im_f,
        lam_re_b, lam_im_b, log_dt_b, b_re_b, b_im_b, c_re_b, c_im_b,
        ssm_d, w_glu, b_glu, w_branch_attn, w_branch_ssm, w_out, norm_mix_post,
        norm_ffn_pre, w_ffn_gate, w_ffn_up, conv_w, conv_b, w_ffn_down, norm_ffn_post)))
    for layer in range(w_in.shape[0]):
        x = _layer(x, {name: value[layer] for name, value in stacked.items()})
    return x
```

```python
import functools
import math

import numpy as np
import jax
import jax.numpy as jnp
from jax import lax
from jax.experimental import pallas as pl
from jax.experimental.pallas import tpu as pltpu

F32 = jnp.float32
BF16 = jnp.bfloat16

N_HEADS = 8
N_KV_HEADS = 2
HEAD_DIM = 64
GROUP = N_HEADS // N_KV_HEADS
WINDOW = 128
BLOCK = 128
SSM_CH = 16
SSM_STATE = 64
CHUNK = 16
PAIR = 2
N_BRANCH = 2
EPS = 1e-6
NEG_INF = -1e30
LOG2E = math.log2(math.e)

LANES = 128
SUBLANES = 8
HALO = 16
FF_TILE = 256
VMEM_LIMIT = 56 * 1024 * 1024


def _gelu_tanh(x):
    return 0.5 * x * (1.0 + jnp.tanh(math.sqrt(2.0 / math.pi) * (x + 0.044715 * (x * x * x))))


def _rms(x, gain):
    return x * lax.rsqrt(jnp.mean(x * x, axis=-1, keepdims=True) + EPS) * gain


def _params(semantics):
    return pltpu.CompilerParams(dimension_semantics=semantics, vmem_limit_bytes=VMEM_LIMIT)


def _const_spec(shape):
    nd = len(shape)
    return pl.BlockSpec(shape, lambda *_: (0,) * nd)


def _inproj_body(splits, x_ref, g_ref, w_ref, q_ref, k_ref, v_ref, u_ref, gate_ref):
    h = _rms(x_ref[...], g_ref[...]).astype(BF16)

    def proj(i):
        return jnp.dot(h, w_ref[:, splits[i]:splits[i + 1]], preferred_element_type=F32)

    q_ref[...] = (proj(0) * (HEAD_DIM ** -0.5 * LOG2E)).astype(BF16)
    k_ref[...] = proj(1).astype(BF16)
    v_ref[...] = proj(2).astype(BF16)
    u_ref[...] = proj(3).astype(BF16).reshape(u_ref.shape)
    gate_ref[...] = jax.nn.sigmoid(proj(4)).astype(BF16)


def _chunk_rows_spec(tm, tiles_per_seq, width):
    return pl.BlockSpec((tm // CHUNK, None, CHUNK, width),
                        lambda i: (i % tiles_per_seq, i // tiles_per_seq, 0, 0))


def _inproj(x2, gain, w_in, bsz, seq, tm):
    T, D = x2.shape
    attn_w = N_HEADS * HEAD_DIM
    kv_w = N_KV_HEADS * HEAD_DIM
    widths = [attn_w, kv_w, kv_w, D // 2, N_BRANCH * D]
    splits = [0] + np.cumsum(widths).tolist()
    row = lambda w: pl.BlockSpec((tm, w), lambda i: (i, 0))
    out_specs = [row(w) for w in widths]
    out_shape = [jax.ShapeDtypeStruct((T, w), BF16) for w in widths]
    out_specs[3] = _chunk_rows_spec(tm, seq // tm, widths[3])
    out_shape[3] = jax.ShapeDtypeStruct((seq // CHUNK, bsz, CHUNK, widths[3]), BF16)
    return pl.pallas_call(
        functools.partial(_inproj_body, tuple(splits)),
        grid=(T // tm,),
        in_specs=[row(D), _const_spec((1, D)), _const_spec(w_in.shape)],
        out_specs=out_specs,
        out_shape=out_shape,
        compiler_params=_params(("parallel",)),
        name="inproj",
    )(x2, gain, w_in)


def _attn_body(q_ref, kp_ref, kc_ref, kn_ref, vp_ref, vc_ref, vn_ref, bias0_ref, bias1_ref, sink_ref, o_ref):
    lane = lax.broadcasted_iota(jnp.int32, (1, LANES), 1)
    left = lane < HEAD_DIM
    top = lax.broadcasted_iota(jnp.int32, (LANES, 1), 0) < HEAD_DIM
    zero = jnp.zeros((), BF16)
    kall = jnp.concatenate([kp_ref[...], kc_ref[...], kn_ref[...]], axis=0)
    vall = jnp.concatenate([vp_ref[...], vc_ref[...], vn_ref[...]], axis=0)
    for b, bias_ref in enumerate((bias0_ref, bias1_ref)):
        kcat = kall[BLOCK * b:BLOCK * (b + 3)]
        vcat = vall[BLOCK * b:BLOCK * (b + 3)]
        rows = slice(BLOCK * b, BLOCK * (b + 1))
        for i in range(GROUP):
            qi = q_ref[rows, LANES * i:LANES * (i + 1)]
            halves = []
            for g in range(N_KV_HEADS):
                h = g * GROUP + i
                qm = jnp.where(left if g == 0 else jnp.logical_not(left), qi, zero)
                st = lax.dot_general(kcat, qm, (((1,), (1,)), ((), ())), preferred_element_type=F32)
                st = st + bias_ref[h]
                sink = sink_ref[h] * LOG2E
                m = jnp.maximum(jnp.max(st, axis=0, keepdims=True), sink)
                p = jnp.exp2(st - m)
                denom = jnp.sum(p, axis=0, keepdims=True) + jnp.exp2(sink - m)
                ot = lax.dot_general(vcat, p.astype(BF16), (((0,), (0,)), ((), ())),
                                     preferred_element_type=F32)
                halves.append(ot / denom)
            o_ref[rows, LANES * i:LANES * (i + 1)] = jnp.where(top, halves[0], halves[1]).T.astype(BF16)


def _attn_bias():
    t = np.arange(BLOCK)[None, :]
    s = np.arange(3 * BLOCK)[:, None] - BLOCK
    dist = np.abs(t - s).astype(np.float64)
    valid = dist <= WINDOW
    slopes = 2.0 ** (-8.0 * (np.arange(N_HEADS) + 1.0) / N_HEADS)
    out = np.zeros((3, N_HEADS, 3 * BLOCK, BLOCK), np.float32)
    for variant in range(3):
        ok = valid.copy()
        if variant == 0:
            ok[:BLOCK] = False
        if variant == 2:
            ok[2 * BLOCK:] = False
        for h in range(N_HEADS):
            out[variant, h] = np.where(ok, -slopes[h] * dist * LOG2E, NEG_INF)
    return out


def _head_perm():
    cols = []
    for i in range(GROUP):
        for g in range(N_KV_HEADS):
            h = g * GROUP + i
            cols.extend(range(h * HEAD_DIM, (h + 1) * HEAD_DIM))
    return np.asarray(cols, np.int32)


def _attention(q, k, v, bias, sink, bsz, seq):
    nb = seq // BLOCK
    assert nb % 2 == 0
    ns = nb // 2
    kvw = k.shape[1]
    aw = q.shape[1]

    def edge_spec(off):
        return pl.BlockSpec((BLOCK, kvw), lambda b, n: (b * nb + jnp.clip(2 * n + off, 0, nb - 1), 0))

    centre = pl.BlockSpec((2 * BLOCK, kvw), lambda b, n: (b * ns + n, 0))
    bias_spec = lambda pick: pl.BlockSpec((None,) + bias.shape[1:], lambda b, n: (pick(n), 0, 0, 0))
    first = bias_spec(lambda n: jnp.where(n == 0, 0, 1))
    last = bias_spec(lambda n: jnp.where(n == ns - 1, 2, 1))
    qspec = pl.BlockSpec((2 * BLOCK, aw), lambda b, n: (b * ns + n, 0))
    return pl.pallas_call(
        _attn_body,
        grid=(bsz, ns),
        in_specs=[qspec, edge_spec(-1), centre, edge_spec(2), edge_spec(-1), centre, edge_spec(2),
                  first, last,
                  pl.BlockSpec(memory_space=pltpu.SMEM)],
        out_specs=qspec,
        out_shape=jax.ShapeDtypeStruct(q.shape, BF16),
        compiler_params=_params(("parallel", "arbitrary")),
        name="attn",
    )(q, k, k, k, v, v, v, bias, bias, sink)


def _s5prep_body(lam_ref, ldt_ref, bt_ref, c_ref, wb_ref, wct_ref, t_ref, a_ref):
    width = CHUNK * SSM_CH
    lane = lax.broadcasted_iota(jnp.int32, (1, LANES), 1)
    in_group = [lane < SSM_STATE, lane >= SSM_STATE]
    lane_t = lax.broadcasted_iota(jnp.int32, (SSM_CH, width), 1)
    contract = (((1,), (1,)), ((), ()))
    hi = lax.Precision.HIGHEST
    toeplitz = [[None] * CHUNK for _ in range(PAIR)]
    for d in range(2):
        lre, lim = lam_ref[d, 0], lam_ref[d, 1]
        dt = jnp.exp(ldt_ref[d])
        mag = jnp.exp(lre * dt)
        are, aim = mag * jnp.cos(lim * dt), mag * jnp.sin(lim * dt)
        den = lre * lre + lim * lim
        cfr = ((are - 1.0) * lre + aim * lim) / den
        cfi = (aim * lre - (are - 1.0) * lim) / den
        btr, bti = bt_ref[d, 0], bt_ref[d, 1]
        bbr = btr * cfr - bti * cfi
        bbi = btr * cfi + bti * cfr
        cre, cim = c_ref[d, 0], c_ref[d, 1]

        pwr, pwi = jnp.ones_like(are), jnp.zeros_like(are)
        e_r, e_i = [], []
        for p in range(CHUNK + 1):
            er = cre * pwr - cim * pwi
            ei = cre * pwi + cim * pwr
            if p < CHUNK:
                e_r.append(er)
                e_i.append(ei)
                k = CHUNK - 1 - p if d == 0 else p
                for ri, val in enumerate((bbr * pwr - bbi * pwi, bbr * pwi + bbi * pwr)):
                    for g in range(PAIR):
                        wb_ref[width * g + SSM_CH * k:width * g + SSM_CH * (k + 1),
                               LANES * (2 * d + ri):LANES * (2 * d + ri + 1)] = jnp.where(
                                   in_group[g], val, 0.0).astype(BF16)
            if p >= 1:
                r = p - 1 if d == 0 else CHUNK - p
                for ri, val in enumerate((er, -ei)):
                    for g in range(PAIR):
                        wct_ref[width * g + SSM_CH * r:width * g + SSM_CH * (r + 1),
                                LANES * (2 * d + ri):LANES * (2 * d + ri + 1)] = jnp.where(
                                    in_group[g], val, 0.0).astype(BF16)
            if p == CHUNK:
                a_ref[2 * d] = jnp.broadcast_to(pwr, a_ref.shape[1:])
                a_ref[2 * d + 1] = jnp.broadcast_to(pwi, a_ref.shape[1:])
            pwr, pwi = pwr * are - pwi * aim, pwr * aim + pwi * are

        if d == 1:
            e_r, e_i = e_r[::-1], e_i[::-1]
        ecat_r, ecat_i = jnp.concatenate(e_r, axis=0), jnp.concatenate(e_i, axis=0)
        for g in range(PAIR):
            gr, gi = jnp.where(in_group[g], bbr, 0.0), jnp.where(in_group[g], bbi, 0.0)
            krow = (lax.dot_general(gr, ecat_r, contract, precision=hi, preferred_element_type=F32)
                    - lax.dot_general(gi, ecat_i, contract, precision=hi, preferred_element_type=F32))
            for k in range(CHUNK):
                if d == 0:
                    shift, keep = SSM_CH * k, lane_t >= SSM_CH * k
                else:
                    shift, keep = (SSM_CH * (k + 1)) % width, lane_t < SSM_CH * (k + 1)
                blk = jnp.where(keep, pltpu.roll(krow, shift, 1) if shift else krow, 0.0)
                toeplitz[g][k] = blk if d == 0 else toeplitz[g][k] + blk
    for g in range(PAIR):
        t_ref[g] = jnp.concatenate(toeplitz[g], axis=0).astype(BF16)


def _s5prep(lam, ldt, bt, c):
    npair = lam.shape[1]
    width = CHUNK * SSM_CH
    pw = PAIR * width
    return pl.pallas_call(
        _s5prep_body,
        grid=(npair,),
        in_specs=[pl.BlockSpec((2, None, 2, 1, LANES), lambda i: (0, i, 0, 0, 0)),
                  pl.BlockSpec((2, None, 1, LANES), lambda i: (0, i, 0, 0)),
                  pl.BlockSpec((2, None, 2, SSM_CH, LANES), lambda i: (0, i, 0, 0, 0)),
                  pl.BlockSpec((2, None, 2, SSM_CH, LANES), lambda i: (0, i, 0, 0, 0))],
        out_specs=[pl.BlockSpec((None, pw, 4 * LANES), lambda i: (i, 0, 0)),
                   pl.BlockSpec((None, pw, 4 * LANES), lambda i: (i, 0, 0)),
                   pl.BlockSpec((PAIR, width, width), lambda i: (i, 0, 0)),
                   pl.BlockSpec((None, 4, SUBLANES, LANES), lambda i: (i, 0, 0, 0))],
        out_shape=[jax.ShapeDtypeStruct((npair, pw, 4 * LANES), BF16),
                   jax.ShapeDtypeStruct((npair, pw, 4 * LANES), BF16),
                   jax.ShapeDtypeStruct((npair * PAIR, width, width), BF16),
                   jax.ShapeDtypeStruct((npair, 4, SUBLANES, LANES), F32)],
        compiler_params=_params(("parallel",)),
        name="s5prep",
    )(lam, ldt, bt, c)


def _block_transpose(xs):
    n = len(xs)
    assert n * SSM_CH == LANES and n & (n - 1) == 0
    blk = lax.broadcasted_iota(jnp.int32, xs[0].shape, 1) // SSM_CH
    d = n // 2
    while d:
        upper = (blk & d) != 0
        ys = list(xs)
        for a in range(n):
            if a & d:
                continue
            b = a + d
            ys[a] = jnp.where(upper, pltpu.roll(xs[b], SSM_CH * d, 1), xs[a])
            ys[b] = jnp.where(upper, xs[b], pltpu.roll(xs[a], LANES - SSM_CH * d, 1))
        xs = ys
        d //= 2
    return xs


def _s5_scan(a_ref, w_scr, s_scr):
    rows = w_scr.shape[0]
    nsteps = rows // SUBLANES
    nstream = w_scr.shape[1] // (2 * LANES)
    half = SUBLANES // 2
    lo = lax.broadcasted_iota(jnp.int32, (SUBLANES, LANES), 0) < half
    first = [lo, jnp.logical_not(lo)]
    coef = [(a_ref[s // 2, 2 * (s % 2)], a_ref[s // 2, 2 * (s % 2) + 1]) for s in range(nstream)]
    col = lambda i: slice(LANES * i, LANES * (i + 1))

    def advance(a, xr, xi, wr, wi):
        return a[0] * xr - a[1] * xi + wr, a[0] * xi + a[1] * xr + wi

    def swap(x):
        return pltpu.roll(x, half, 0)

    def step(i, carry):
        out = []
        for s in range(nstream):
            d = s % 2
            cr, ci = carry[2 * s], carry[2 * s + 1]
            vreg = i if d == 0 else nsteps - 1 - i
            rs = pl.ds(pl.multiple_of(vreg * SUBLANES, SUBLANES), SUBLANES)
            wr, wi = w_scr[rs, col(2 * s)], w_scr[rs, col(2 * s + 1)]
            r0r, r0i = swap(cr), swap(ci)
            t1r, t1i = advance(coef[s], r0r, r0i, wr, wi)
            r1r, r1i = swap(t1r), swap(t1i)
            t2r, t2i = advance(coef[s], r1r, r1i, wr, wi)
            s_scr[rs, col(2 * s)] = jnp.where(first[d], r0r, r1r)
            s_scr[rs, col(2 * s + 1)] = jnp.where(first[d], r0i, r1i)
            out += [jnp.where(first[d], t1r, t2r), jnp.where(first[d], t1i, t2i)]
        return tuple(out)

    z = jnp.zeros((SUBLANES, LANES), F32)
    lax.fori_loop(0, nsteps, step, (z,) * (2 * nstream))


S5_SLAB_GROUPS = LANES // SSM_CH
S5_ROW_CHUNK = 256
S5_SUB_ROWS = 32
S5_SCAN_PAIRS = 2


def _s5_body(u_ref, wb_ref, t_ref, wct_ref, a_ref, y_ref, ur_scr, w_scr, s_scr, sall_scr, yr_scr, tok_scr):
    rows = ur_scr.shape[0]
    width = CHUNK * SSM_CH
    pw = PAIR * width
    npair = S5_SLAB_GROUPS // PAIR
    half_steps = CHUNK // 2
    tok_rows = S5_ROW_CHUNK * CHUNK

    nchunks = rows // S5_ROW_CHUNK
    nsub = S5_ROW_CHUNK // S5_SUB_ROWS
    sw = 4 * LANES
    row_chunk = lambda c: slice(S5_ROW_CHUNK * c, S5_ROW_CHUNK * (c + 1))
    tok_chunk = lambda c: slice(tok_rows * c, tok_rows * (c + 1))
    step_rows = lambda i, h, kk: pl.ds(S5_SUB_ROWS * i * CHUNK + half_steps * h + kk, S5_SUB_ROWS,
                                       stride=CHUNK)
    lane_blk = lambda g, h: slice(width * g + LANES * h, width * g + LANES * (h + 1))

    def relayout_in(c):
        tok = tok_scr.at[c % 2]
        tok[...] = u_ref[tok_chunk(c), :].astype(F32)
        for i in range(nsub):
            for h in range(2):
                xs = [tok[step_rows(i, h, kk), :] for kk in range(half_steps)]
                ys = _block_transpose(xs)
                for g in range(S5_SLAB_GROUPS):
                    r0 = S5_ROW_CHUNK * c + S5_SUB_ROWS * i
                    ur_scr[r0:r0 + S5_SUB_ROWS, lane_blk(g, h)] = ys[g].astype(BF16)

    def state_in(c, p0):
        for q in range(S5_SCAN_PAIRS):
            pp = p0 + q
            w_scr[row_chunk(c), sw * q:sw * (q + 1)] = jnp.dot(
                ur_scr[row_chunk(c), pw * pp:pw * (pp + 1)], wb_ref[pp], preferred_element_type=F32)

    for p0 in range(0, npair, S5_SCAN_PAIRS):
        for c in range(nchunks):
            if p0 == 0:
                relayout_in(c)
            state_in(c, p0)
        _s5_scan(a_ref.at[p0:p0 + S5_SCAN_PAIRS], w_scr, s_scr)
        sall_scr[:, sw * p0:sw * (p0 + S5_SCAN_PAIRS)] = s_scr[...].astype(BF16)

    def outputs(c):
        yr, tok = yr_scr.at[c % 2], tok_scr.at[c % 2]
        for pp in range(npair):
            y = lax.dot_general(sall_scr[row_chunk(c), sw * pp:sw * (pp + 1)], wct_ref[pp],
                                (((1,), (1,)), ((), ())), preferred_element_type=F32)
            for g in range(PAIR):
                lo = pw * pp + width * g
                yr[:, lo:lo + width] = y[:, width * g:width * (g + 1)] + jnp.dot(
                    ur_scr[row_chunk(c), lo:lo + width], t_ref[PAIR * pp + g], preferred_element_type=F32)
        for i in range(nsub):
            for h in range(2):
                xs = [yr[S5_SUB_ROWS * i:S5_SUB_ROWS * (i + 1), lane_blk(g, h)].astype(BF16)
                      for g in range(S5_SLAB_GROUPS)]
                ys = _block_transpose(xs)
                for kk in range(half_steps):
                    tok[step_rows(i, h, kk), :] = ys[kk].astype(F32)
        y_ref[tok_chunk(c), :] = tok[...].astype(BF16)

    for c in range(nchunks):
        outputs(c)


def _s5(u_tok, wb, tsum, wct, a16):
    T, ssm_w = u_tok.shape
    rows = T // CHUNK
    assert rows % S5_ROW_CHUNK == 0 and S5_SLAB_GROUPS == CHUNK // 2
    width = CHUNK * SSM_CH
    pw = PAIR * width
    npair = S5_SLAB_GROUPS // PAIR
    slab = pl.BlockSpec((T, LANES), lambda s: (0, s))
    return pl.pallas_call(
        _s5_body,
        grid=(ssm_w // LANES,),
        in_specs=[slab,
                  pl.BlockSpec((npair, pw, 4 * LANES), lambda s: (s, 0, 0)),
                  pl.BlockSpec((S5_SLAB_GROUPS, width, width), lambda s: (s, 0, 0)),
                  pl.BlockSpec((npair, pw, 4 * LANES), lambda s: (s, 0, 0)),
                  pl.BlockSpec((npair, 4, SUBLANES, LANES), lambda s: (s, 0, 0, 0))],
        out_specs=slab,
        out_shape=jax.ShapeDtypeStruct((T, ssm_w), BF16),
        scratch_shapes=[pltpu.VMEM((rows, S5_SLAB_GROUPS * width), BF16),
                        pltpu.VMEM((rows, S5_SCAN_PAIRS * 4 * LANES), F32),
                        pltpu.VMEM((rows, S5_SCAN_PAIRS * 4 * LANES), F32),
                        pltpu.VMEM((rows, npair * 4 * LANES), BF16),
                        pltpu.VMEM((2, S5_ROW_CHUNK, S5_SLAB_GROUPS * width), F32),
                        pltpu.VMEM((2, S5_ROW_CHUNK * CHUNK, LANES), F32)],
        compiler_params=_params(("parallel",)),
        name="s5",
    )(u_tok, wb, tsum, wct, a16)


def _merge_body(attn_ref, ys_ref, u_ref, g_ref, x_ref, wba_ref, wglu_ref, bglu_ref, d_ref, wbs_ref,
                wout_ref, gpost_ref, gpre_ref, x1_ref, h2_ref):
    tm, d = x_ref.shape
    u = u_ref[...].reshape(tm, -1).astype(F32)
    y = ys_ref[...].reshape(tm, -1).astype(F32) + d_ref[...] * u
    y = _gelu_tanh(y)
    gl = jnp.dot(y.astype(BF16), wglu_ref[...], preferred_element_type=F32) + bglu_ref[...]
    s = y * jax.nn.sigmoid(gl)
    y_ssm = jnp.dot(s.astype(BF16), wbs_ref[...], preferred_element_type=F32)
    y_attn = jnp.dot(attn_ref[...], wba_ref[...], preferred_element_type=F32)
    merged = g_ref[:, :d].astype(F32) * y_attn + g_ref[:, d:].astype(F32) * y_ssm
    o = jnp.dot(merged.astype(BF16), wout_ref[...], preferred_element_type=F32)
    x1 = x_ref[...] + _rms(o, gpost_ref[...])
    x1_ref[...] = x1
    h2_ref[...] = _rms(x1, gpre_ref[...]).astype(BF16)


def _merge(attn, ys, u, gates, x2, wba, wglu, bglu, ssm_d, wbs, wout, gpost, gpre, seq, tm):
    T, D = x2.shape
    row = lambda w: pl.BlockSpec((tm, w), lambda i: (i, 0))
    chunked = _chunk_rows_spec(tm, seq // tm, u.shape[-1])
    consts = [wba, wglu, bglu, ssm_d, wbs, wout, gpost, gpre]
    return pl.pallas_call(
        _merge_body,
        grid=(T // tm,),
        in_specs=[row(attn.shape[1]), chunked, chunked, row(gates.shape[1]), row(D)]
                 + [_const_spec(c.shape) for c in consts],
        out_specs=[row(D), row(D)],
        out_shape=[jax.ShapeDtypeStruct((T, D), F32), jax.ShapeDtypeStruct((T, D), BF16)],
        compiler_params=_params(("parallel",)),
        name="merge",
    )(attn, ys, u, gates, x2, *consts)


def _ffn_body(tiles_per_seq, hp_ref, hc_ref, hn_ref, x1_ref, wg_ref, wu_ref, cw_ref, cb_ref, wd_ref,
              gn_ref, o_ref, hext, fscr):
    tm = hc_ref.shape[0]
    i = pl.program_id(0) % tiles_per_seq
    zero = jnp.zeros((), BF16)
    hext[0:HALO] = jnp.where(i == 0, zero, hp_ref[...])
    hext[HALO:HALO + tm] = hc_ref[...]
    hext[HALO + tm:] = jnp.where(i == tiles_per_seq - 1, zero, hn_ref[...])
    ext = tm + 2 * HALO
    for c in range(wg_ref.shape[1] // FF_TILE):
        sl = slice(FF_TILE * c, FF_TILE * (c + 1))
        g = jnp.dot(hext[...], wg_ref[:, sl], preferred_element_type=F32)
        up = jnp.dot(hc_ref[...], wu_ref[:, sl], preferred_element_type=F32)
        g_prev = pltpu.roll(g, 1, 0)[HALO:HALO + tm]
        g_next = pltpu.roll(g, ext - 1, 0)[HALO:HALO + tm]
        conv = (g_prev * cw_ref[0:1, sl] + g[HALO:HALO + tm] * cw_ref[1:2, sl]
                + g_next * cw_ref[2:3, sl] + cb_ref[:, sl])
        fscr[:, sl] = (_gelu_tanh(conv) * up).astype(BF16)
    o = jnp.dot(fscr[...], wd_ref[...], preferred_element_type=F32)
    o_ref[...] = x1_ref[...] + _rms(o, gn_ref[...])


def _ffn(h2, x1, wg, wu, cw, cb, wd, gn, seq, tm):
    T, D = x1.shape
    dff = wg.shape[1]
    assert dff % FF_TILE == 0 and tm % HALO == 0 and seq % tm == 0
    per = tm // HALO
    nh = T // HALO
    row = lambda w: pl.BlockSpec((tm, w), lambda i: (i, 0))
    once = lambda shape: pl.BlockSpec(shape, lambda i: (0,) * len(shape), pipeline_mode=pl.Buffered(1))
    return pl.pallas_call(
        functools.partial(_ffn_body, seq // tm),
        grid=(T // tm,),
        in_specs=[pl.BlockSpec((HALO, D), lambda i: (jnp.maximum(i * per - 1, 0), 0)),
                  row(D),
                  pl.BlockSpec((HALO, D), lambda i: (jnp.minimum((i + 1) * per, nh - 1), 0)),
                  row(D),
                  once(wg.shape), once(wu.shape), _const_spec(cw.shape), _const_spec(cb.shape),
                  once(wd.shape), _const_spec(gn.shape)],
        out_specs=row(D),
        out_shape=jax.ShapeDtypeStruct((T, D), F32),
        scratch_shapes=[pltpu.VMEM((tm + 2 * HALO, D), BF16), pltpu.VMEM((tm, dff), BF16)],
        compiler_params=_params(("parallel",)),
        name="ffn",
    )(h2, h2, h2, x1, wg, wu, cw, cb, wd, gn)


def _row_tile(seq):
    return min(1024, seq)


def _layer(x, p):
    bsz, seq, d = x.shape
    T = bsz * seq
    tm = _row_tile(seq)
    ssm_w = d // 2
    ngroups = ssm_w // SSM_CH
    attn_w = N_HEADS * HEAD_DIM
    row2 = lambda v: v.reshape(1, -1).astype(F32)

    perm = _head_perm()
    w_in = jnp.concatenate([p["w_in"][:, :attn_w][:, perm], p["w_in"][:, attn_w:]], axis=1).astype(BF16)
    x2 = x.reshape(T, d)
    q, k, v, u, gates = _inproj(x2, row2(p["norm_mix_pre"]), w_in, bsz, seq, tm)

    attn = _attention(q, k, v, jnp.asarray(_attn_bias()), p["attn_sink"].astype(F32), bsz, seq)

    npair = ngroups // PAIR
    both = lambda re, im: jnp.stack([jnp.stack([p[re + "_f"], p[im + "_f"]]),
                                     jnp.stack([p[re + "_b"], p[im + "_b"]])]).astype(F32)
    lam = both("lam_re", "lam_im").reshape(2, 2, npair, 1, LANES).transpose(0, 2, 1, 3, 4)
    ldt = jnp.repeat(jnp.stack([p["log_dt_f"], p["log_dt_b"]]).astype(F32), SSM_STATE, axis=-1)
    ldt = ldt.reshape(2, npair, 1, LANES)
    bt = both("b_re", "b_im").reshape(2, 2, npair, PAIR, SSM_STATE, SSM_CH)
    bt = bt.transpose(0, 2, 1, 5, 3, 4).reshape(2, npair, 2, SSM_CH, LANES)
    cc = both("c_re", "c_im").reshape(2, 2, npair, PAIR, SSM_CH, SSM_STATE)
    cc = cc.transpose(0, 2, 1, 4, 3, 5).reshape(2, npair, 2, SSM_CH, LANES)
    wb, wct, tsum, a = _s5prep(lam, ldt, bt, cc)

    assert bsz == SUBLANES // 2
    ys = _s5(u.reshape(T, ssm_w), wb, tsum, wct, a).reshape(u.shape)

    x1, h2 = _merge(attn, ys, u, gates, x2,
                    p["w_branch_attn"][perm, :].astype(BF16), p["w_glu"].astype(BF16), row2(p["b_glu"]),
                    row2(p["ssm_d"]), p["w_branch_ssm"].astype(BF16), p["w_out"].astype(BF16),
                    row2(p["norm_mix_post"]), row2(p["norm_ffn_pre"]), seq, tm)

    out = _ffn(h2, x1, p["w_ffn_gate"].astype(BF16), p["w_ffn_up"].astype(BF16), p["conv_w"].astype(F32),
               row2(p["conv_b"]), p["w_ffn_down"].astype(BF16), row2(p["norm_ffn_post"]), seq, tm)
    return out.reshape(bsz, seq, d)


_PARAM_NAMES = (
    "norm_mix_pre", "w_in", "attn_sink",
    "lam_re_f", "lam_im_f", "log_dt_f", "b_re_f", "b_im_f", "c_re_f", "c_im_f",
    "lam_re_b", "lam_im_b", "log_dt_b", "b_re_b", "b_im_b", "c_re_b", "c_im_b",
    "ssm_d", "w_glu", "b_glu", "w_branch_attn", "w_branch_ssm", "w_out", "norm_mix_post",
    "norm_ffn_pre", "w_ffn_gate", "w_ffn_up", "conv_w", "conv_b", "w_ffn_down", "norm_ffn_post")


def kernel(x, norm_mix_pre, w_in, attn_sink, lam_re_f, lam_im_f, log_dt_f, b_re_f, b_im_f, c_re_f, c_im_f, lam_re_b, lam_im_b, log_dt_b, b_re_b, b_im_b, c_re_b, c_im_b, ssm_d, w_glu, b_glu, w_branch_attn, w_branch_ssm, w_out, norm_mix_post, norm_ffn_pre, w_ffn_gate, w_ffn_up, conv_w, conv_b, w_ffn_down, norm_ffn_post):
    stacked = dict(zip(_PARAM_NAMES, (
        norm_mix_pre, w_in, attn_sink,
        lam_re_f, lam_im_f, log_dt_f, b_re_f, b_im_f, c_re_f, c_im_f,
        lam_re_b, lam_im_b, log_dt_b, b_re_b, b_im_b, c_re_b, c_im_b,
        ssm_d, w_glu, b_glu, w_branch_attn, w_branch_ssm, w_out, norm_mix_post,
        norm_ffn_pre, w_ffn_gate, w_ffn_up, conv_w, conv_b, w_ffn_down, norm_ffn_post)))
    for layer in range(w_in.shape[0]):
        x = _layer(x, {name: value[layer] for name, value in stacked.items()})
    return x
```

```python
import functools
import math

import numpy as np
import jax
import jax.numpy as jnp
from jax import lax
from jax.experimental import pallas as pl
from jax.experimental.pallas import tpu as pltpu

F32 = jnp.float32
BF16 = jnp.bfloat16

N_HEADS = 8
N_KV_HEADS = 2
HEAD_DIM = 64
GROUP = N_HEADS // N_KV_HEADS
WINDOW = 128
BLOCK = 128
SSM_CH = 16
SSM_STATE = 64
CHUNK = 16
PAIR = 2
N_BRANCH = 2
EPS = 1e-6
NEG_INF = -1e30
LOG2E = math.log2(math.e)

LANES = 128
SUBLANES = 8
HALO = 16
FF_TILE = 256
VMEM_LIMIT = 56 * 1024 * 1024


def _gelu_tanh(x):
    return 0.5 * x * (1.0 + jnp.tanh(math.sqrt(2.0 / math.pi) * (x + 0.044715 * (x * x * x))))


def _rms(x, gain):
    return x * lax.rsqrt(jnp.mean(x * x, axis=-1, keepdims=True) + EPS) * gain


def _params(semantics):
    return pltpu.CompilerParams(dimension_semantics=semantics, vmem_limit_bytes=VMEM_LIMIT)


def _const_spec(shape):
    nd = len(shape)
    return pl.BlockSpec(shape, lambda *_: (0,) * nd)


def _inproj_body(splits, x_ref, g_ref, w_ref, q_ref, k_ref, v_ref, u_ref, gate_ref):
    h = _rms(x_ref[...], g_ref[...]).astype(BF16)

    def proj(i):
        return jnp.dot(h, w_ref[:, splits[i]:splits[i + 1]], preferred_element_type=F32)

    q_ref[...] = (proj(0) * (HEAD_DIM ** -0.5 * LOG2E)).astype(BF16)
    k_ref[...] = proj(1).astype(BF16)
    v_ref[...] = proj(2).astype(BF16)
    u_ref[...] = proj(3).astype(BF16).reshape(u_ref.shape)
    gate_ref[...] = jax.nn.sigmoid(proj(4)).astype(BF16)


def _chunk_rows_spec(tm, tiles_per_seq, width):
    return pl.BlockSpec((tm // CHUNK, None, CHUNK, width),
                        lambda i: (i % tiles_per_seq, i // tiles_per_seq, 0, 0))


def _inproj(x2, gain, w_in, bsz, seq, tm):
    T, D = x2.shape
    attn_w = N_HEADS * HEAD_DIM
    kv_w = N_KV_HEADS * HEAD_DIM
    widths = [attn_w, kv_w, kv_w, D // 2, N_BRANCH * D]
    splits = [0] + np.cumsum(widths).tolist()
    row = lambda w: pl.BlockSpec((tm, w), lambda i: (i, 0))
    out_specs = [row(w) for w in widths]
    out_shape = [jax.ShapeDtypeStruct((T, w), BF16) for w in widths]
    out_specs[3] = _chunk_rows_spec(tm, seq // tm, widths[3])
    out_shape[3] = jax.ShapeDtypeStruct((seq // CHUNK, bsz, CHUNK, widths[3]), BF16)
    return pl.pallas_call(
        functools.partial(_inproj_body, tuple(splits)),
        grid=(T // tm,),
        in_specs=[row(D), _const_spec((1, D)), _const_spec(w_in.shape)],
        out_specs=out_specs,
        out_shape=out_shape,
        compiler_params=_params(("parallel",)),
        name="inproj",
    )(x2, gain, w_in)


def _attn_body(q_ref, kp_ref, kc_ref, kn_ref, vp_ref, vc_ref, vn_ref, bias0_ref, bias1_ref, sink_ref, o_ref):
    lane = lax.broadcasted_iota(jnp.int32, (1, LANES), 1)
    left = lane < HEAD_DIM
    top = lax.broadcasted_iota(jnp.int32, (LANES, 1), 0) < HEAD_DIM
    zero = jnp.zeros((), BF16)
    kall = jnp.concatenate([kp_ref[...], kc_ref[...], kn_ref[...]], axis=0)
    vall = jnp.concatenate([vp_ref[...], vc_ref[...], vn_ref[...]], axis=0)
    for b, bias_ref in enumerate((bias0_ref, bias1_ref)):
        kcat = kall[BLOCK * b:BLOCK * (b + 3)]
        vcat = vall[BLOCK * b:BLOCK * (b + 3)]
        rows = slice(BLOCK * b, BLOCK * (b + 1))
        for i in range(GROUP):
            qi = q_ref[rows, LANES * i:LANES * (i + 1)]
            halves = []
            for g in range(N_KV_HEADS):
                h = g * GROUP + i
                qm = jnp.where(left if g == 0 else jnp.logical_not(left), qi, zero)
                st = lax.dot_general(kcat, qm, (((1,), (1,)), ((), ())), preferred_element_type=F32)
                st = st + bias_ref[h]
                sink = sink_ref[h] * LOG2E
                m = jnp.maximum(jnp.max(st, axis=0, keepdims=True), sink)
                p = jnp.exp2(st - m)
                denom = jnp.sum(p, axis=0, keepdims=True) + jnp.exp2(sink - m)
                ot = lax.dot_general(vcat, p.astype(BF16), (((0,), (0,)), ((), ())),
                                     preferred_element_type=F32)
                halves.append(ot / denom)
            o_ref[rows, LANES * i:LANES * (i + 1)] = jnp.where(top, halves[0], halves[1]).T.astype(BF16)


def _attn_bias():
    t = np.arange(BLOCK)[None, :]
    s = np.arange(3 * BLOCK)[:, None] - BLOCK
    dist = np.abs(t - s).astype(np.float64)
    valid = dist <= WINDOW
    slopes = 2.0 ** (-8.0 * (np.arange(N_HEADS) + 1.0) / N_HEADS)
    out = np.zeros((3, N_HEADS, 3 * BLOCK, BLOCK), np.float32)
    for variant in range(3):
        ok = valid.copy()
        if variant == 0:
            ok[:BLOCK] = False
        if variant == 2:
            ok[2 * BLOCK:] = False
        for h in range(N_HEADS):
            out[variant, h] = np.where(ok, -slopes[h] * dist * LOG2E, NEG_INF)
    return out


def _head_perm():
    cols = []
    for i in range(GROUP):
        for g in range(N_KV_HEADS):
            h = g * GROUP + i
            cols.extend(range(h * HEAD_DIM, (h + 1) * HEAD_DIM))
    return np.asarray(cols, np.int32)


def _attention(q, k, v, bias, sink, bsz, seq):
    nb = seq // BLOCK
    assert nb % 2 == 0
    ns = nb // 2
    kvw = k.shape[1]
    aw = q.shape[1]

    def edge_spec(off):
        return pl.BlockSpec((BLOCK, kvw), lambda b, n: (b * nb + jnp.clip(2 * n + off, 0, nb - 1), 0))

    centre = pl.BlockSpec((2 * BLOCK, kvw), lambda b, n: (b * ns + n, 0))
    bias_spec = lambda pick: pl.BlockSpec((None,) + bias.shape[1:], lambda b, n: (pick(n), 0, 0, 0))
    first = bias_spec(lambda n: jnp.where(n == 0, 0, 1))
    last = bias_spec(lambda n: jnp.where(n == ns - 1, 2, 1))
    qspec = pl.BlockSpec((2 * BLOCK, aw), lambda b, n: (b * ns + n, 0))
    return pl.pallas_call(
        _attn_body,
        grid=(bsz, ns),
        in_specs=[qspec, edge_spec(-1), centre, edge_spec(2), edge_spec(-1), centre, edge_spec(2),
                  first, last,
                  pl.BlockSpec(memory_space=pltpu.SMEM)],
        out_specs=qspec,
        out_shape=jax.ShapeDtypeStruct(q.shape, BF16),
        compiler_params=_params(("parallel", "arbitrary")),
        name="attn",
    )(q, k, k, k, v, v, v, bias, bias, sink)


def _s5prep_body(lam_ref, ldt_ref, bt_ref, c_ref, wb_ref, wct_ref, t_ref, a_ref):
    width = CHUNK * SSM_CH
    lane = lax.broadcasted_iota(jnp.int32, (1, LANES), 1)
    in_group = [lane < SSM_STATE, lane >= SSM_STATE]
    lane_t = lax.broadcasted_iota(jnp.int32, (SSM_CH, width), 1)
    contract = (((1,), (1,)), ((), ()))
    hi = lax.Precision.HIGHEST
    toeplitz = [[None] * CHUNK for _ in range(PAIR)]
    for d in range(2):
        lre, lim = lam_ref[d, 0], lam_ref[d, 1]
        dt = jnp.exp(ldt_ref[d])
        mag = jnp.exp(lre * dt)
        are, aim = mag * jnp.cos(lim * dt), mag * jnp.sin(lim * dt)
        den = lre * lre + lim * lim
        cfr = ((are - 1.0) * lre + aim * lim) / den
        cfi = (aim * lre - (are - 1.0) * lim) / den
        btr, bti = bt_ref[d, 0], bt_ref[d, 1]
        bbr = btr * cfr - bti * cfi
        bbi = btr * cfi + bti * cfr
        cre, cim = c_ref[d, 0], c_ref[d, 1]

        pwr, pwi = jnp.ones_like(are), jnp.zeros_like(are)
        e_r, e_i = [], []
        for p in range(CHUNK + 1):
            er = cre * pwr - cim * pwi
            ei = cre * pwi + cim * pwr
            if p < CHUNK:
                e_r.append(er)
                e_i.append(ei)
                k = CHUNK - 1 - p if d == 0 else p
                for ri, val in enumerate((bbr * pwr - bbi * pwi, bbr * pwi + bbi * pwr)):
                    for g in range(PAIR):
                        wb_ref[width * g + SSM_CH * k:width * g + SSM_CH * (k + 1),
                               LANES * (2 * d + ri):LANES * (2 * d + ri + 1)] = jnp.where(
                                   in_group[g], val, 0.0).astype(BF16)
            if p >= 1:
                r = p - 1 if d == 0 else CHUNK - p
                for ri, val in enumerate((er, -ei)):
                    for g in range(PAIR):
                        wct_ref[width * g + SSM_CH * r:width * g + SSM_CH * (r + 1),
                                LANES * (2 * d + ri):LANES * (2 * d + ri + 1)] = jnp.where(
                                    in_group[g], val, 0.0).astype(BF16)
            if p == CHUNK:
                a_ref[2 * d] = jnp.broadcast_to(pwr, a_ref.shape[1:])
                a_ref[2 * d + 1] = jnp.broadcast_to(pwi, a_ref.shape[1:])
            pwr, pwi = pwr * are - pwi * aim, pwr * aim + pwi * are

        if d == 1:
            e_r, e_i = e_r[::-1], e_i[::-1]
        ecat_r, ecat_i = jnp.concatenate(e_r, axis=0), jnp.concatenate(e_i, axis=0)
        for g in range(PAIR):
            gr, gi = jnp.where(in_group[g], bbr, 0.0), jnp.where(in_group[g], bbi, 0.0)
            krow = (lax.dot_general(gr, ecat_r, contract, precision=hi, preferred_element_type=F32)
                    - lax.dot_general(gi, ecat_i, contract, precision=hi, preferred_element_type=F32))
            for k in range(CHUNK):
                if d == 0:
                    shift, keep = SSM_CH * k, lane_t >= SSM_CH * k
                else:
                    shift, keep = (SSM_CH * (k + 1)) % width, lane_t < SSM_CH * (k + 1)
                blk = jnp.where(keep, pltpu.roll(krow, shift, 1) if shift else krow, 0.0)
                toeplitz[g][k] = blk if d == 0 else toeplitz[g][k] + blk
    for g in range(PAIR):
        t_ref[g] = jnp.concatenate(toeplitz[g], axis=0).astype(BF16)


def _s5prep(lam, ldt, bt, c):
    npair = lam.shape[1]
    width = CHUNK * SSM_CH
    pw = PAIR * width
    return pl.pallas_call(
        _s5prep_body,
        grid=(npair,),
        in_specs=[pl.BlockSpec((2, None, 2, 1, LANES), lambda i: (0, i, 0, 0, 0)),
                  pl.BlockSpec((2, None, 1, LANES), lambda i: (0, i, 0, 0)),
                  pl.BlockSpec((2, None, 2, SSM_CH, LANES), lambda i: (0, i, 0, 0, 0)),
                  pl.BlockSpec((2, None, 2, SSM_CH, LANES), lambda i: (0, i, 0, 0, 0))],
        out_specs=[pl.BlockSpec((None, pw, 4 * LANES), lambda i: (i, 0, 0)),
                   pl.BlockSpec((None, pw, 4 * LANES), lambda i: (i, 0, 0)),
                   pl.BlockSpec((PAIR, width, width), lambda i: (i, 0, 0)),
                   pl.BlockSpec((None, 4, SUBLANES, LANES), lambda i: (i, 0, 0, 0))],
        out_shape=[jax.ShapeDtypeStruct((npair, pw, 4 * LANES), BF16),
                   jax.ShapeDtypeStruct((npair, pw, 4 * LANES), BF16),
                   jax.ShapeDtypeStruct((npair * PAIR, width, width), BF16),
                   jax.ShapeDtypeStruct((npair, 4, SUBLANES, LANES), F32)],
        compiler_params=_params(("parallel",)),
        name="s5prep",
    )(lam, ldt, bt, c)


def _block_transpose(xs):
    n = len(xs)
    assert n * SSM_CH == LANES and n & (n - 1) == 0
    blk = lax.broadcasted_iota(jnp.int32, xs[0].shape, 1) // SSM_CH
    d = n // 2
    while d:
        upper = (blk & d) != 0
        ys = list(xs)
        for a in range(n):
            if a & d:
                continue
            b = a + d
            ys[a] = jnp.where(upper, pltpu.roll(xs[b], SSM_CH * d, 1), xs[a])
            ys[b] = jnp.where(upper, xs[b], pltpu.roll(xs[a], LANES - SSM_CH * d, 1))
        xs = ys
        d //= 2
    return xs


def _s5_scan(a_ref, w_scr, s_scr):
    rows = w_scr.shape[0]
    nsteps = rows // SUBLANES
    nstream = w_scr.shape[1] // (2 * LANES)
    half = SUBLANES // 2
    lo = lax.broadcasted_iota(jnp.int32, (SUBLANES, LANES), 0) < half
    first = [lo, jnp.logical_not(lo)]
    coef = [(a_ref[s // 2, 2 * (s % 2)], a_ref[s // 2, 2 * (s % 2) + 1]) for s in range(nstream)]
    col = lambda i: slice(LANES * i, LANES * (i + 1))

    def advance(a, xr, xi, wr, wi):
        return a[0] * xr - a[1] * xi + wr, a[0] * xi + a[1] * xr + wi

    def swap(x):
        return pltpu.roll(x, half, 0)

    def step(i, carry):
        out = []
        for s in range(nstream):
            d = s % 2
            cr, ci = carry[2 * s], carry[2 * s + 1]
            vreg = i if d == 0 else nsteps - 1 - i
            rs = pl.ds(pl.multiple_of(vreg * SUBLANES, SUBLANES), SUBLANES)
            wr, wi = w_scr[rs, col(2 * s)], w_scr[rs, col(2 * s + 1)]
            r0r, r0i = swap(cr), swap(ci)
            t1r, t1i = advance(coef[s], r0r, r0i, wr, wi)
            r1r, r1i = swap(t1r), swap(t1i)
            t2r, t2i = advance(coef[s], r1r, r1i, wr, wi)
            s_scr[rs, col(2 * s)] = jnp.where(first[d], r0r, r1r)
            s_scr[rs, col(2 * s + 1)] = jnp.where(first[d], r0i, r1i)
            out += [jnp.where(first[d], t1r, t2r), jnp.where(first[d], t1i, t2i)]
        return tuple(out)

    z = jnp.zeros((SUBLANES, LANES), F32)
    lax.fori_loop(0, nsteps, step, (z,) * (2 * nstream))


S5_SLAB_GROUPS = LANES // SSM_CH
S5_ROW_CHUNK = 256
S5_SUB_ROWS = 32
S5_SCAN_PAIRS = 2


def _s5_body(u_ref, wb_ref, t_ref, wct_ref, a_ref, y_ref, ur_scr, w_scr, s_scr, sall_scr, yr_scr, tok_scr):
    rows = ur_scr.shape[0]
    width = CHUNK * SSM_CH
    pw = PAIR * width
    npair = S5_SLAB_GROUPS // PAIR
    half_steps = CHUNK // 2
    tok_rows = S5_ROW_CHUNK * CHUNK

    nchunks = rows // S5_ROW_CHUNK
    nsub = S5_ROW_CHUNK // S5_SUB_ROWS
    sw = 4 * LANES
    row_chunk = lambda c: slice(S5_ROW_CHUNK * c, S5_ROW_CHUNK * (c + 1))
    tok_chunk = lambda c: slice(tok_rows * c, tok_rows * (c + 1))
    step_rows = lambda i, h, kk: pl.ds(S5_SUB_ROWS * i * CHUNK + half_steps * h + kk, S5_SUB_ROWS,
                                       stride=CHUNK)
    lane_blk = lambda g, h: slice(width * g + LANES * h, width * g + LANES * (h + 1))

    def relayout_in(c):
        tok = tok_scr.at[c % 2]
        tok[...] = u_ref[tok_chunk(c), :].astype(F32)
        for i in range(nsub):
            for h in range(2):
                xs = [tok[step_rows(i, h, kk), :] for kk in range(half_steps)]
                ys = _block_transpose(xs)
                for g in range(S5_SLAB_GROUPS):
                    r0 = S5_ROW_CHUNK * c + S5_SUB_ROWS * i
                    ur_scr[r0:r0 + S5_SUB_ROWS, lane_blk(g, h)] = ys[g].astype(BF16)

    def state_in(c, p0):
        for q in range(S5_SCAN_PAIRS):
            pp = p0 + q
            w_scr[row_chunk(c), sw * q:sw * (q + 1)] = jnp.dot(
                ur_scr[row_chunk(c), pw * pp:pw * (pp + 1)], wb_ref[pp], preferred_element_type=F32)

    for p0 in range(0, npair, S5_SCAN_PAIRS):
        for c in range(nchunks):
            if p0 == 0:
                relayout_in(c)
            state_in(c, p0)
        _s5_scan(a_ref.at[p0:p0 + S5_SCAN_PAIRS], w_scr, s_scr)
        sall_scr[:, sw * p0:sw * (p0 + S5_SCAN_PAIRS)] = s_scr[...].astype(BF16)

    def outputs(c):
        yr, tok = yr_scr.at[c % 2], tok_scr.at[c % 2]
        for pp in range(npair):
            y = lax.dot_general(sall_scr[row_chunk(c), sw * pp:sw * (pp + 1)], wct_ref[pp],
                                (((1,), (1,)), ((), ())), preferred_element_type=F32)
            for g in range(PAIR):
                lo = pw * pp + width * g
                yr[:, lo:lo + width] = y[:, width * g:width * (g + 1)] + jnp.dot(
                    ur_scr[row_chunk(c), lo:lo + width], t_ref[PAIR * pp + g], preferred_element_type=F32)
        for i in range(nsub):
            for h in range(2):
                xs = [yr[S5_SUB_ROWS * i:S5_SUB_ROWS * (i + 1), lane_blk(g, h)].astype(BF16)
                      for g in range(S5_SLAB_GROUPS)]
                ys = _block_transpose(xs)
                for kk in range(half_steps):
                    tok[step_rows(i, h, kk), :] = ys[kk].astype(F32)
        y_ref[tok_chunk(c), :] = tok[...].astype(BF16)

    for c in range(nchunks):
        outputs(c)


def _s5(u_tok, wb, tsum, wct, a16):
    T, ssm_w = u_tok.shape
    rows = T // CHUNK
    assert rows % S5_ROW_CHUNK == 0 and S5_SLAB_GROUPS == CHUNK // 2
    width = CHUNK * SSM_CH
    pw = PAIR * width
    npair = S5_SLAB_GROUPS // PAIR
    slab = pl.BlockSpec((T, LANES), lambda s: (0, s))
    return pl.pallas_call(
        _s5_body,
        grid=(ssm_w // LANES,),
        in_specs=[slab,
                  pl.BlockSpec((npair, pw, 4 * LANES), lambda s: (s, 0, 0)),
                  pl.BlockSpec((S5_SLAB_GROUPS, width, width), lambda s: (s, 0, 0)),
                  pl.BlockSpec((npair, pw, 4 * LANES), lambda s: (s, 0, 0)),
                  pl.BlockSpec((npair, 4, SUBLANES, LANES), lambda s: (s, 0, 0, 0))],
        out_specs=slab,
        out_shape=jax.ShapeDtypeStruct((T, ssm_w), BF16),
        scratch_shapes=[pltpu.VMEM((rows, S5_SLAB_GROUPS * width), BF16),
                        pltpu.VMEM((rows, S5_SCAN_PAIRS * 4 * LANES), F32),
                        pltpu.VMEM((rows, S5_SCAN_PAIRS * 4 * LANES), F32),
                        pltpu.VMEM((rows, npair * 4 * LANES), BF16),
                        pltpu.VMEM((2, S5_ROW_CHUNK, S5_SLAB_GROUPS * width), F32),
                        pltpu.VMEM((2, S5_ROW_CHUNK * CHUNK, LANES), F32)],
        compiler_params=_params(("parallel",)),
        name="s5",
    )(u_tok, wb, tsum, wct, a16)


def _merge_body(attn_ref, ys_ref, u_ref, g_ref, x_ref, wba_ref, wglu_ref, bglu_ref, d_ref, wbs_ref,
                wout_ref, gpost_ref, gpre_ref, x1_ref, h2_ref):
    tm, d = x_ref.shape
    u = u_ref[...].reshape(tm, -1).astype(F32)
    y = ys_ref[...].reshape(tm, -1).astype(F32) + d_ref[...] * u
    y = _gelu_tanh(y)
    gl = jnp.dot(y.astype(BF16), wglu_ref[...], preferred_element_type=F32) + bglu_ref[...]
    s = y * jax.nn.sigmoid(gl)
    y_ssm = jnp.dot(s.astype(BF16), wbs_ref[...], preferred_element_type=F32)
    y_attn = jnp.dot(attn_ref[...], wba_ref[...], preferred_element_type=F32)
    merged = g_ref[:, :d].astype(F32) * y_attn + g_ref[:, d:].astype(F32) * y_ssm
    o = jnp.dot(merged.astype(BF16), wout_ref[...], preferred_element_type=F32)
    x1 = x_ref[...] + _rms(o, gpost_ref[...])
    x1_ref[...] = x1
    h2_ref[...] = _rms(x1, gpre_ref[...]).astype(BF16)


def _merge(attn, ys, u, gates, x2, wba, wglu, bglu, ssm_d, wbs, wout, gpost, gpre, seq, tm):
    T, D = x2.shape
    row = lambda w: pl.BlockSpec((tm, w), lambda i: (i, 0))
    chunked = _chunk_rows_spec(tm, seq // tm, u.shape[-1])
    consts = [wba, wglu, bglu, ssm_d, wbs, wout, gpost, gpre]
    return pl.pallas_call(
        _merge_body,
        grid=(T // tm,),
        in_specs=[row(attn.shape[1]), chunked, chunked, row(gates.shape[1]), row(D)]
                 + [_const_spec(c.shape) for c in consts],
        out_specs=[row(D), row(D)],
        out_shape=[jax.ShapeDtypeStruct((T, D), F32), jax.ShapeDtypeStruct((T, D), BF16)],
        compiler_params=_params(("parallel",)),
        name="merge",
    )(attn, ys, u, gates, x2, *consts)


def _ffn_body(tiles_per_seq, hp_ref, hc_ref, hn_ref, x1_ref, wg_ref, wu_ref, cw_ref, cb_ref, wd_ref,
              gn_ref, o_ref, hext, fscr, gscr):
    tm = hc_ref.shape[0]
    i = pl.program_id(0) % tiles_per_seq
    zero = jnp.zeros((), BF16)
    hext[0:HALO] = jnp.where(i == 0, zero, hp_ref[...])
    hext[HALO:HALO + tm] = hc_ref[...]
    hext[HALO + tm:] = jnp.where(i == tiles_per_seq - 1, zero, hn_ref[...])
    for c in range(wg_ref.shape[1] // FF_TILE):
        sl = slice(FF_TILE * c, FF_TILE * (c + 1))
        g = jnp.dot(hext[...], wg_ref[:, sl], preferred_element_type=F32)
        up = jnp.dot(hc_ref[...], wu_ref[:, sl], preferred_element_type=F32)
        gs = gscr.at[c % 2]
        for j in range(FF_TILE // LANES):
            gs[j] = g[:, LANES * j:LANES * (j + 1)]
        for j in range(FF_TILE // LANES):
            cols = slice(FF_TILE * c + LANES * j, FF_TILE * c + LANES * (j + 1))
            conv = (gs[j, HALO - 1:HALO - 1 + tm, :] * cw_ref[0:1, cols]
                    + gs[j, HALO:HALO + tm, :] * cw_ref[1:2, cols]
                    + gs[j, HALO + 1:HALO + 1 + tm, :] * cw_ref[2:3, cols] + cb_ref[:, cols])
            fscr[:, cols] = (_gelu_tanh(conv) * up[:, LANES * j:LANES * (j + 1)]).astype(BF16)
    o = jnp.dot(fscr[...], wd_ref[...], preferred_element_type=F32)
    o_ref[...] = x1_ref[...] + _rms(o, gn_ref[...])


def _ffn(h2, x1, wg, wu, cw, cb, wd, gn, seq, tm):
    T, D = x1.shape
    dff = wg.shape[1]
    assert dff % FF_TILE == 0 and tm % HALO == 0 and seq % tm == 0
    per = tm // HALO
    nh = T // HALO
    row = lambda w: pl.BlockSpec((tm, w), lambda i: (i, 0))
    once = lambda shape: pl.BlockSpec(shape, lambda i: (0,) * len(shape), pipeline_mode=pl.Buffered(1))
    return pl.pallas_call(
        functools.partial(_ffn_body, seq // tm),
        grid=(T // tm,),
        in_specs=[pl.BlockSpec((HALO, D), lambda i: (jnp.maximum(i * per - 1, 0), 0)),
                  row(D),
                  pl.BlockSpec((HALO, D), lambda i: (jnp.minimum((i + 1) * per, nh - 1), 0)),
                  row(D),
                  once(wg.shape), once(wu.shape), _const_spec(cw.shape), _const_spec(cb.shape),
                  once(wd.shape), _const_spec(gn.shape)],
        out_specs=row(D),
        out_shape=jax.ShapeDtypeStruct((T, D), F32),
        scratch_shapes=[pltpu.VMEM((tm + 2 * HALO, D), BF16), pltpu.VMEM((tm, dff), BF16),
                        pltpu.VMEM((2, FF_TILE // LANES, tm + 2 * HALO, LANES), F32)],
        compiler_params=_params(("parallel",)),
        name="ffn",
    )(h2, h2, h2, x1, wg, wu, cw, cb, wd, gn)


def _row_tile(seq):
    return min(1024, seq)


def _layer(x, p):
    bsz, seq, d = x.shape
    T = bsz * seq
    tm = _row_tile(seq)
    ssm_w = d // 2
    ngroups = ssm_w // SSM_CH
    attn_w = N_HEADS * HEAD_DIM
    row2 = lambda v: v.reshape(1, -1).astype(F32)

    perm = _head_perm()
    w_in = jnp.concatenate([p["w_in"][:, :attn_w][:, perm], p["w_in"][:, attn_w:]], axis=1).astype(BF16)
    x2 = x.reshape(T, d)
    q, k, v, u, gates = _inproj(x2, row2(p["norm_mix_pre"]), w_in, bsz, seq, tm)

    attn = _attention(q, k, v, jnp.asarray(_attn_bias()), p["attn_sink"].astype(F32), bsz, seq)

    npair = ngroups // PAIR
    both = lambda re, im: jnp.stack([jnp.stack([p[re + "_f"], p[im + "_f"]]),
                                     jnp.stack([p[re + "_b"], p[im + "_b"]])]).astype(F32)
    lam = both("lam_re", "lam_im").reshape(2, 2, npair, 1, LANES).transpose(0, 2, 1, 3, 4)
    ldt = jnp.repeat(jnp.stack([p["log_dt_f"], p["log_dt_b"]]).astype(F32), SSM_STATE, axis=-1)
    ldt = ldt.reshape(2, npair, 1, LANES)
    bt = both("b_re", "b_im").reshape(2, 2, npair, PAIR, SSM_STATE, SSM_CH)
    bt = bt.transpose(0, 2, 1, 5, 3, 4).reshape(2, npair, 2, SSM_CH, LANES)
    cc = both("c_re", "c_im").reshape(2, 2, npair, PAIR, SSM_CH, SSM_STATE)
    cc = cc.transpose(0, 2, 1, 4, 3, 5).reshape(2, npair, 2, SSM_CH, LANES)
    wb, wct, tsum, a = _s5prep(lam, ldt, bt, cc)

    assert bsz == SUBLANES // 2
    ys = _s5(u.reshape(T, ssm_w), wb, tsum, wct, a).reshape(u.shape)

    x1, h2 = _merge(attn, ys, u, gates, x2,
                    p["w_branch_attn"][perm, :].astype(BF16), p["w_glu"].astype(BF16), row2(p["b_glu"]),
                    row2(p["ssm_d"]), p["w_branch_ssm"].astype(BF16), p["w_out"].astype(BF16),
                    row2(p["norm_mix_post"]), row2(p["norm_ffn_pre"]), seq, tm)

    out = _ffn(h2, x1, p["w_ffn_gate"].astype(BF16), p["w_ffn_up"].astype(BF16), p["conv_w"].astype(F32),
               row2(p["conv_b"]), p["w_ffn_down"].astype(BF16), row2(p["norm_ffn_post"]), seq, tm)
    return out.reshape(bsz, seq, d)


_PARAM_NAMES = (
    "norm_mix_pre", "w_in", "attn_sink",
    "lam_re_f", "lam_im_f", "log_dt_f", "b_re_f", "b_im_f", "c_re_f", "c_im_f",
    "lam_re_b", "lam_im_b", "log_dt_b", "b_re_b", "b_im_b", "c_re_b", "c_im_b",
    "ssm_d", "w_glu", "b_glu", "w_branch_attn", "w_branch_ssm", "w_out", "norm_mix_post",
    "norm_ffn_pre", "w_ffn_gate", "w_ffn_up", "conv_w", "conv_b", "w_ffn_down", "norm_ffn_post")


def kernel(x, norm_mix_pre, w_in, attn_sink, lam_re_f, lam_im_f, log_dt_f, b_re_f, b_im_f, c_re_f, c_im_f, lam_re_b, lam_im_b, log_dt_b, b_re_b, b_im_b, c_re_b, c_im_b, ssm_d, w_glu, b_glu, w_branch_attn, w_branch_ssm, w_out, norm_mix_post, norm_ffn_pre, w_ffn_gate, w_ffn_up, conv_w, conv_b, w_ffn_down, norm_ffn_post):
    stacked = dict(zip(_PARAM_NAMES, (
        norm_mix_pre, w_in, attn_sink,
        lam_re_f, lam_im_f, log_dt_f, b_re_f, b_im_f, c_re_f, c_im_f,
        lam_re_b, lam_im_b, log_dt_b, b_re_b, b_im_b, c_re_b, c_im_b,
        ssm_d, w_glu, b_glu, w_branch_attn, w_branch_ssm, w_out, norm_mix_post,
        norm_ffn_pre, w_ffn_gate, w_ffn_up, conv_w, conv_b, w_ffn_down, norm_ffn_post)))
    for layer in range(w_in.shape[0]):
        x = _layer(x, {name: value[layer] for name, value in stacked.items()})
    return x
```

```python
import functools
import math

import numpy as np
import jax
import jax.numpy as jnp
from jax import lax
from jax.experimental import pallas as pl
from jax.experimental.pallas import tpu as pltpu

F32 = jnp.float32
BF16 = jnp.bfloat16

N_HEADS = 8
N_KV_HEADS = 2
HEAD_DIM = 64
GROUP = N_HEADS // N_KV_HEADS
WINDOW = 128
BLOCK = 128
SSM_CH = 16
SSM_STATE = 64
CHUNK = 16
PAIR = 2
N_BRANCH = 2
EPS = 1e-6
NEG_INF = -1e30
LOG2E = math.log2(math.e)

LANES = 128
SUBLANES = 8
HALO = 16
FF_TILE = 256
VMEM_LIMIT = 56 * 1024 * 1024


def _gelu_tanh(x):
    return 0.5 * x * (1.0 + jnp.tanh(math.sqrt(2.0 / math.pi) * (x + 0.044715 * (x * x * x))))


def _rms(x, gain):
    return x * lax.rsqrt(jnp.mean(x * x, axis=-1, keepdims=True) + EPS) * gain


def _params(semantics):
    return pltpu.CompilerParams(dimension_semantics=semantics, vmem_limit_bytes=VMEM_LIMIT)


def _const_spec(shape):
    nd = len(shape)
    return pl.BlockSpec(shape, lambda *_: (0,) * nd)


def _inproj_body(splits, x_ref, g_ref, w_ref, q_ref, k_ref, v_ref, u_ref, gate_ref):
    h = _rms(x_ref[...], g_ref[...]).astype(BF16)

    def proj(i):
        return jnp.dot(h, w_ref[:, splits[i]:splits[i + 1]], preferred_element_type=F32)

    q_ref[...] = (proj(0) * (HEAD_DIM ** -0.5 * LOG2E)).astype(BF16)
    k_ref[...] = proj(1).astype(BF16)
    v_ref[...] = proj(2).astype(BF16)
    u_ref[...] = proj(3).astype(BF16).reshape(u_ref.shape)
    gate_ref[...] = jax.nn.sigmoid(proj(4)).astype(BF16)


def _chunk_rows_spec(tm, tiles_per_seq, width):
    return pl.BlockSpec((tm // CHUNK, None, CHUNK, width),
                        lambda i: (i % tiles_per_seq, i // tiles_per_seq, 0, 0))


def _inproj(x2, gain, w_in, bsz, seq, tm):
    T, D = x2.shape
    attn_w = N_HEADS * HEAD_DIM
    kv_w = N_KV_HEADS * HEAD_DIM
    widths = [attn_w, kv_w, kv_w, D // 2, N_BRANCH * D]
    splits = [0] + np.cumsum(widths).tolist()
    row = lambda w: pl.BlockSpec((tm, w), lambda i: (i, 0))
    out_specs = [row(w) for w in widths]
    out_shape = [jax.ShapeDtypeStruct((T, w), BF16) for w in widths]
    out_specs[3] = _chunk_rows_spec(tm, seq // tm, widths[3])
    out_shape[3] = jax.ShapeDtypeStruct((seq // CHUNK, bsz, CHUNK, widths[3]), BF16)
    return pl.pallas_call(
        functools.partial(_inproj_body, tuple(splits)),
        grid=(T // tm,),
        in_specs=[row(D), _const_spec((1, D)), _const_spec(w_in.shape)],
        out_specs=out_specs,
        out_shape=out_shape,
        compiler_params=_params(("parallel",)),
        name="inproj",
    )(x2, gain, w_in)


def _attn_body(q_ref, kp_ref, kc_ref, kn_ref, vp_ref, vc_ref, vn_ref, bias0_ref, bias1_ref, sink_ref, o_ref):
    lane = lax.broadcasted_iota(jnp.int32, (1, LANES), 1)
    left = lane < HEAD_DIM
    top = lax.broadcasted_iota(jnp.int32, (LANES, 1), 0) < HEAD_DIM
    zero = jnp.zeros((), BF16)
    kall = jnp.concatenate([kp_ref[...], kc_ref[...], kn_ref[...]], axis=0)
    vall = jnp.concatenate([vp_ref[...], vc_ref[...], vn_ref[...]], axis=0)
    for b, bias_ref in enumerate((bias0_ref, bias1_ref)):
        kcat = kall[BLOCK * b:BLOCK * (b + 3)]
        vcat = vall[BLOCK * b:BLOCK * (b + 3)]
        rows = slice(BLOCK * b, BLOCK * (b + 1))
        for i in range(GROUP):
            qi = q_ref[rows, LANES * i:LANES * (i + 1)]
            halves = []
            for g in range(N_KV_HEADS):
                h = g * GROUP + i
                qm = jnp.where(left if g == 0 else jnp.logical_not(left), qi, zero)
                st = lax.dot_general(kcat, qm, (((1,), (1,)), ((), ())), preferred_element_type=F32)
                st = st + bias_ref[h]
                sink = sink_ref[h] * LOG2E
                m = jnp.maximum(jnp.max(st, axis=0, keepdims=True), sink)
                p = jnp.exp2(st - m)
                denom = jnp.sum(p, axis=0, keepdims=True) + jnp.exp2(sink - m)
                ot = lax.dot_general(vcat, p.astype(BF16), (((0,), (0,)), ((), ())),
                                     preferred_element_type=F32)
                halves.append(ot / denom)
            o_ref[rows, LANES * i:LANES * (i + 1)] = jnp.where(top, halves[0], halves[1]).T.astype(BF16)


def _attn_bias():
    t = np.arange(BLOCK)[None, :]
    s = np.arange(3 * BLOCK)[:, None] - BLOCK
    dist = np.abs(t - s).astype(np.float64)
    valid = dist <= WINDOW
    slopes = 2.0 ** (-8.0 * (np.arange(N_HEADS) + 1.0) / N_HEADS)
    out = np.zeros((3, N_HEADS, 3 * BLOCK, BLOCK), np.float32)
    for variant in range(3):
        ok = valid.copy()
        if variant == 0:
            ok[:BLOCK] = False
        if variant == 2:
            ok[2 * BLOCK:] = False
        for h in range(N_HEADS):
            out[variant, h] = np.where(ok, -slopes[h] * dist * LOG2E, NEG_INF)
    return out


def _head_perm():
    cols = []
    for i in range(GROUP):
        for g in range(N_KV_HEADS):
            h = g * GROUP + i
            cols.extend(range(h * HEAD_DIM, (h + 1) * HEAD_DIM))
    return np.asarray(cols, np.int32)


def _attention(q, k, v, bias, sink, bsz, seq):
    nb = seq // BLOCK
    assert nb % 2 == 0
    ns = nb // 2
    kvw = k.shape[1]
    aw = q.shape[1]

    def edge_spec(off):
        return pl.BlockSpec((BLOCK, kvw), lambda b, n: (b * nb + jnp.clip(2 * n + off, 0, nb - 1), 0))

    centre = pl.BlockSpec((2 * BLOCK, kvw), lambda b, n: (b * ns + n, 0))
    bias_spec = lambda pick: pl.BlockSpec((None,) + bias.shape[1:], lambda b, n: (pick(n), 0, 0, 0))
    first = bias_spec(lambda n: jnp.where(n == 0, 0, 1))
    last = bias_spec(lambda n: jnp.where(n == ns - 1, 2, 1))
    qspec = pl.BlockSpec((2 * BLOCK, aw), lambda b, n: (b * ns + n, 0))
    return pl.pallas_call(
        _attn_body,
        grid=(bsz, ns),
        in_specs=[qspec, edge_spec(-1), centre, edge_spec(2), edge_spec(-1), centre, edge_spec(2),
                  first, last,
                  pl.BlockSpec(memory_space=pltpu.SMEM)],
        out_specs=qspec,
        out_shape=jax.ShapeDtypeStruct(q.shape, BF16),
        compiler_params=_params(("parallel", "arbitrary")),
        name="attn",
    )(q, k, k, k, v, v, v, bias, bias, sink)


def _s5prep_body(lam_ref, ldt_ref, bt_ref, c_ref, wb_ref, wct_ref, t_ref, a_ref):
    width = CHUNK * SSM_CH
    lane = lax.broadcasted_iota(jnp.int32, (1, LANES), 1)
    in_group = [lane < SSM_STATE, lane >= SSM_STATE]
    lane_t = lax.broadcasted_iota(jnp.int32, (SSM_CH, width), 1)
    contract = (((1,), (1,)), ((), ()))
    hi = lax.Precision.HIGHEST
    toeplitz = [[None] * CHUNK for _ in range(PAIR)]
    for d in range(2):
        lre, lim = lam_ref[d, 0], lam_ref[d, 1]
        dt = jnp.exp(ldt_ref[d])
        mag = jnp.exp(lre * dt)
        are, aim = mag * jnp.cos(lim * dt), mag * jnp.sin(lim * dt)
        den = lre * lre + lim * lim
        cfr = ((are - 1.0) * lre + aim * lim) / den
        cfi = (aim * lre - (are - 1.0) * lim) / den
        btr, bti = bt_ref[d, 0], bt_ref[d, 1]
        bbr = btr * cfr - bti * cfi
        bbi = btr * cfi + bti * cfr
        cre, cim = c_ref[d, 0], c_ref[d, 1]

        pwr, pwi = jnp.ones_like(are), jnp.zeros_like(are)
        e_r, e_i = [], []
        for p in range(CHUNK + 1):
            er = cre * pwr - cim * pwi
            ei = cre * pwi + cim * pwr
            if p < CHUNK:
                e_r.append(er)
                e_i.append(ei)
                k = CHUNK - 1 - p if d == 0 else p
                for ri, val in enumerate((bbr * pwr - bbi * pwi, bbr * pwi + bbi * pwr)):
                    for g in range(PAIR):
                        wb_ref[width * g + SSM_CH * k:width * g + SSM_CH * (k + 1),
                               LANES * (2 * d + ri):LANES * (2 * d + ri + 1)] = jnp.where(
                                   in_group[g], val, 0.0).astype(BF16)
            if p >= 1:
                r = p - 1 if d == 0 else CHUNK - p
                for ri, val in enumerate((er, -ei)):
                    for g in range(PAIR):
                        wct_ref[width * g + SSM_CH * r:width * g + SSM_CH * (r + 1),
                                LANES * (2 * d + ri):LANES * (2 * d + ri + 1)] = jnp.where(
                                    in_group[g], val, 0.0).astype(BF16)
            if p == CHUNK:
                a_ref[2 * d] = jnp.broadcast_to(pwr, a_ref.shape[1:])
                a_ref[2 * d + 1] = jnp.broadcast_to(pwi, a_ref.shape[1:])
            pwr, pwi = pwr * are - pwi * aim, pwr * aim + pwi * are

        if d == 1:
            e_r, e_i = e_r[::-1], e_i[::-1]
        ecat_r, ecat_i = jnp.concatenate(e_r, axis=0), jnp.concatenate(e_i, axis=0)
        for g in range(PAIR):
            gr, gi = jnp.where(in_group[g], bbr, 0.0), jnp.where(in_group[g], bbi, 0.0)
            krow = (lax.dot_general(gr, ecat_r, contract, precision=hi, preferred_element_type=F32)
                    - lax.dot_general(gi, ecat_i, contract, precision=hi, preferred_element_type=F32))
            for k in range(CHUNK):
                if d == 0:
                    shift, keep = SSM_CH * k, lane_t >= SSM_CH * k
                else:
                    shift, keep = (SSM_CH * (k + 1)) % width, lane_t < SSM_CH * (k + 1)
                blk = jnp.where(keep, pltpu.roll(krow, shift, 1) if shift else krow, 0.0)
                toeplitz[g][k] = blk if d == 0 else toeplitz[g][k] + blk
    for g in range(PAIR):
        t_ref[g] = jnp.concatenate(toeplitz[g], axis=0).astype(BF16)


def _s5prep(lam, ldt, bt, c):
    npair = lam.shape[1]
    width = CHUNK * SSM_CH
    pw = PAIR * width
    return pl.pallas_call(
        _s5prep_body,
        grid=(npair,),
        in_specs=[pl.BlockSpec((2, None, 2, 1, LANES), lambda i: (0, i, 0, 0, 0)),
                  pl.BlockSpec((2, None, 1, LANES), lambda i: (0, i, 0, 0)),
                  pl.BlockSpec((2, None, 2, SSM_CH, LANES), lambda i: (0, i, 0, 0, 0)),
                  pl.BlockSpec((2, None, 2, SSM_CH, LANES), lambda i: (0, i, 0, 0, 0))],
        out_specs=[pl.BlockSpec((None, pw, 4 * LANES), lambda i: (i, 0, 0)),
                   pl.BlockSpec((None, pw, 4 * LANES), lambda i: (i, 0, 0)),
                   pl.BlockSpec((PAIR, width, width), lambda i: (i, 0, 0)),
                   pl.BlockSpec((None, 4, SUBLANES, LANES), lambda i: (i, 0, 0, 0))],
        out_shape=[jax.ShapeDtypeStruct((npair, pw, 4 * LANES), BF16),
                   jax.ShapeDtypeStruct((npair, pw, 4 * LANES), BF16),
                   jax.ShapeDtypeStruct((npair * PAIR, width, width), BF16),
                   jax.ShapeDtypeStruct((npair, 4, SUBLANES, LANES), F32)],
        compiler_params=_params(("parallel",)),
        name="s5prep",
    )(lam, ldt, bt, c)


def _block_transpose(xs):
    n = len(xs)
    assert n * SSM_CH == LANES and n & (n - 1) == 0
    blk = lax.broadcasted_iota(jnp.int32, xs[0].shape, 1) // SSM_CH
    d = n // 2
    while d:
        upper = (blk & d) != 0
        ys = list(xs)
        for a in range(n):
            if a & d:
                continue
            b = a + d
            ys[a] = jnp.where(upper, pltpu.roll(xs[b], SSM_CH * d, 1), xs[a])
            ys[b] = jnp.where(upper, xs[b], pltpu.roll(xs[a], LANES - SSM_CH * d, 1))
        xs = ys
        d //= 2
    return xs


def _s5_scan(a_ref, w_scr, s_scr):
    rows = w_scr.shape[0]
    nsteps = rows // SUBLANES
    nstream = w_scr.shape[1] // (2 * LANES)
    half = SUBLANES // 2
    lo = lax.broadcasted_iota(jnp.int32, (SUBLANES, LANES), 0) < half
    first = [lo, jnp.logical_not(lo)]
    coef = [(a_ref[s // 2, 2 * (s % 2)], a_ref[s // 2, 2 * (s % 2) + 1]) for s in range(nstream)]
    col = lambda i: slice(LANES * i, LANES * (i + 1))

    def advance(a, xr, xi, wr, wi):
        return a[0] * xr - a[1] * xi + wr, a[0] * xi + a[1] * xr + wi

    def swap(x):
        return pltpu.roll(x, half, 0)

    def step(i, carry):
        out = []
        for s in range(nstream):
            d = s % 2
            cr, ci = carry[2 * s], carry[2 * s + 1]
            vreg = i if d == 0 else nsteps - 1 - i
            rs = pl.ds(pl.multiple_of(vreg * SUBLANES, SUBLANES), SUBLANES)
            wr, wi = w_scr[rs, col(2 * s)], w_scr[rs, col(2 * s + 1)]
            r0r, r0i = swap(cr), swap(ci)
            t1r, t1i = advance(coef[s], r0r, r0i, wr, wi)
            r1r, r1i = swap(t1r), swap(t1i)
            t2r, t2i = advance(coef[s], r1r, r1i, wr, wi)
            s_scr[rs, col(2 * s)] = jnp.where(first[d], r0r, r1r)
            s_scr[rs, col(2 * s + 1)] = jnp.where(first[d], r0i, r1i)
            out += [jnp.where(first[d], t1r, t2r), jnp.where(first[d], t1i, t2i)]
        return tuple(out)

    z = jnp.zeros((SUBLANES, LANES), F32)
    lax.fori_loop(0, nsteps, step, (z,) * (2 * nstream))


S5_SLAB_GROUPS = LANES // SSM_CH
S5_ROW_CHUNK = 256
S5_SUB_ROWS = 128
S5_SCAN_PAIRS = 2


def _s5_body(u_ref, wb_ref, t_ref, wct_ref, a_ref, y_ref, ur_scr, w_scr, s_scr, sall_scr, yr_scr, tok_scr):
    rows = ur_scr.shape[0]
    width = CHUNK * SSM_CH
    pw = PAIR * width
    npair = S5_SLAB_GROUPS // PAIR
    half_steps = CHUNK // 2
    tok_rows = S5_ROW_CHUNK * CHUNK

    nchunks = rows // S5_ROW_CHUNK
    nsub = S5_ROW_CHUNK // S5_SUB_ROWS
    sw = 4 * LANES
    row_chunk = lambda c: slice(S5_ROW_CHUNK * c, S5_ROW_CHUNK * (c + 1))
    tok_chunk = lambda c: slice(tok_rows * c, tok_rows * (c + 1))
    step_rows = lambda i, h, kk: pl.ds(S5_SUB_ROWS * i * CHUNK + half_steps * h + kk, S5_SUB_ROWS,
                                       stride=CHUNK)
    lane_blk = lambda g, h: slice(width * g + LANES * h, width * g + LANES * (h + 1))

    def relayout_in(c):
        tok = tok_scr.at[c % 2]
        tok[...] = u_ref[tok_chunk(c), :].astype(F32)
        for i in range(nsub):
            for h in range(2):
                xs = [tok[step_rows(i, h, kk), :] for kk in range(half_steps)]
                ys = _block_transpose(xs)
                for g in range(S5_SLAB_GROUPS):
                    r0 = S5_ROW_CHUNK * c + S5_SUB_ROWS * i
                    ur_scr[r0:r0 + S5_SUB_ROWS, lane_blk(g, h)] = ys[g].astype(BF16)

    def state_in(c, p0):
        for q in range(S5_SCAN_PAIRS):
            pp = p0 + q
            w_scr[row_chunk(c), sw * q:sw * (q + 1)] = jnp.dot(
                ur_scr[row_chunk(c), pw * pp:pw * (pp + 1)], wb_ref[pp], preferred_element_type=F32)

    for p0 in range(0, npair, S5_SCAN_PAIRS):
        for c in range(nchunks):
            if p0 == 0:
                relayout_in(c)
            state_in(c, p0)
        _s5_scan(a_ref.at[p0:p0 + S5_SCAN_PAIRS], w_scr, s_scr)
        sall_scr[:, sw * p0:sw * (p0 + S5_SCAN_PAIRS)] = s_scr[...].astype(BF16)

    def outputs(c):
        yr, tok = yr_scr.at[c % 2], tok_scr.at[c % 2]
        for pp in range(npair):
            y = lax.dot_general(sall_scr[row_chunk(c), sw * pp:sw * (pp + 1)], wct_ref[pp],
                                (((1,), (1,)), ((), ())), preferred_element_type=F32)
            for g in range(PAIR):
                lo = pw * pp + width * g
                yr[:, lo:lo + width] = y[:, width * g:width * (g + 1)] + jnp.dot(
                    ur_scr[row_chunk(c), lo:lo + width], t_ref[PAIR * pp + g], preferred_element_type=F32)
        for i in range(nsub):
            for h in range(2):
                xs = [yr[S5_SUB_ROWS * i:S5_SUB_ROWS * (i + 1), lane_blk(g, h)].astype(BF16)
                      for g in range(S5_SLAB_GROUPS)]
                ys = _block_transpose(xs)
                for kk in range(half_steps):
                    tok[step_rows(i, h, kk), :] = ys[kk].astype(F32)
        y_ref[tok_chunk(c), :] = tok[...].astype(BF16)

    for c in range(nchunks):
        outputs(c)


def _s5(u_tok, wb, tsum, wct, a16):
    T, ssm_w = u_tok.shape
    rows = T // CHUNK
    assert rows % S5_ROW_CHUNK == 0 and S5_SLAB_GROUPS == CHUNK // 2
    width = CHUNK * SSM_CH
    pw = PAIR * width
    npair = S5_SLAB_GROUPS // PAIR
    slab = pl.BlockSpec((T, LANES), lambda s: (0, s))
    return pl.pallas_call(
        _s5_body,
        grid=(ssm_w // LANES,),
        in_specs=[slab,
                  pl.BlockSpec((npair, pw, 4 * LANES), lambda s: (s, 0, 0)),
                  pl.BlockSpec((S5_SLAB_GROUPS, width, width), lambda s: (s, 0, 0)),
                  pl.BlockSpec((npair, pw, 4 * LANES), lambda s: (s, 0, 0)),
                  pl.BlockSpec((npair, 4, SUBLANES, LANES), lambda s: (s, 0, 0, 0))],
        out_specs=slab,
        out_shape=jax.ShapeDtypeStruct((T, ssm_w), BF16),
        scratch_shapes=[pltpu.VMEM((rows, S5_SLAB_GROUPS * width), BF16),
                        pltpu.VMEM((rows, S5_SCAN_PAIRS * 4 * LANES), F32),
                        pltpu.VMEM((rows, S5_SCAN_PAIRS * 4 * LANES), F32),
                        pltpu.VMEM((rows, npair * 4 * LANES), BF16),
                        pltpu.VMEM((2, S5_ROW_CHUNK, S5_SLAB_GROUPS * width), F32),
                        pltpu.VMEM((2, S5_ROW_CHUNK * CHUNK, LANES), F32)],
        compiler_params=_params(("parallel",)),
        name="s5",
    )(u_tok, wb, tsum, wct, a16)


def _merge_body(attn_ref, ys_ref, u_ref, g_ref, x_ref, wba_ref, wglu_ref, bglu_ref, d_ref, wbs_ref,
                wout_ref, gpost_ref, gpre_ref, x1_ref, h2_ref):
    tm, d = x_ref.shape
    u = u_ref[...].reshape(tm, -1).astype(F32)
    y = ys_ref[...].reshape(tm, -1).astype(F32) + d_ref[...] * u
    y = _gelu_tanh(y)
    gl = jnp.dot(y.astype(BF16), wglu_ref[...], preferred_element_type=F32) + bglu_ref[...]
    s = y * jax.nn.sigmoid(gl)
    y_ssm = jnp.dot(s.astype(BF16), wbs_ref[...], preferred_element_type=F32)
    y_attn = jnp.dot(attn_ref[...], wba_ref[...], preferred_element_type=F32)
    merged = g_ref[:, :d].astype(F32) * y_attn + g_ref[:, d:].astype(F32) * y_ssm
    o = jnp.dot(merged.astype(BF16), wout_ref[...], preferred_element_type=F32)
    x1 = x_ref[...] + _rms(o, gpost_ref[...])
    x1_ref[...] = x1
    h2_ref[...] = _rms(x1, gpre_ref[...]).astype(BF16)


def _merge(attn, ys, u, gates, x2, wba, wglu, bglu, ssm_d, wbs, wout, gpost, gpre, seq, tm):
    T, D = x2.shape
    row = lambda w: pl.BlockSpec((tm, w), lambda i: (i, 0))
    chunked = _chunk_rows_spec(tm, seq // tm, u.shape[-1])
    consts = [wba, wglu, bglu, ssm_d, wbs, wout, gpost, gpre]
    return pl.pallas_call(
        _merge_body,
        grid=(T // tm,),
        in_specs=[row(attn.shape[1]), chunked, chunked, row(gates.shape[1]), row(D)]
                 + [_const_spec(c.shape) for c in consts],
        out_specs=[row(D), row(D)],
        out_shape=[jax.ShapeDtypeStruct((T, D), F32), jax.ShapeDtypeStruct((T, D), BF16)],
        compiler_params=_params(("parallel",)),
        name="merge",
    )(attn, ys, u, gates, x2, *consts)


def _ffn_body(tiles_per_seq, hp_ref, hc_ref, hn_ref, x1_ref, wg_ref, wu_ref, cw_ref, cb_ref, wd_ref,
              gn_ref, o_ref, hext, fscr):
    tm = hc_ref.shape[0]
    i = pl.program_id(0) % tiles_per_seq
    zero = jnp.zeros((), BF16)
    hext[0:HALO] = jnp.where(i == 0, zero, hp_ref[...])
    hext[HALO:HALO + tm] = hc_ref[...]
    hext[HALO + tm:] = jnp.where(i == tiles_per_seq - 1, zero, hn_ref[...])
    ext = tm + 2 * HALO
    for c in range(wg_ref.shape[1] // FF_TILE):
        sl = slice(FF_TILE * c, FF_TILE * (c + 1))
        g = jnp.dot(hext[...], wg_ref[:, sl], preferred_element_type=F32)
        up = jnp.dot(hc_ref[...], wu_ref[:, sl], preferred_element_type=F32)
        g_prev = pltpu.roll(g, 1, 0)[HALO:HALO + tm]
        g_next = pltpu.roll(g, ext - 1, 0)[HALO:HALO + tm]
        conv = (g_prev * cw_ref[0:1, sl] + g[HALO:HALO + tm] * cw_ref[1:2, sl]
                + g_next * cw_ref[2:3, sl] + cb_ref[:, sl])
        fscr[:, sl] = (_gelu_tanh(conv) * up).astype(BF16)
    o = jnp.dot(fscr[...], wd_ref[...], preferred_element_type=F32)
    o_ref[...] = x1_ref[...] + _rms(o, gn_ref[...])


def _ffn(h2, x1, wg, wu, cw, cb, wd, gn, seq, tm):
    T, D = x1.shape
    dff = wg.shape[1]
    assert dff % FF_TILE == 0 and tm % HALO == 0 and seq % tm == 0
    per = tm // HALO
    nh = T // HALO
    row = lambda w: pl.BlockSpec((tm, w), lambda i: (i, 0))
    once = lambda shape: pl.BlockSpec(shape, lambda i: (0,) * len(shape), pipeline_mode=pl.Buffered(1))
    return pl.pallas_call(
        functools.partial(_ffn_body, seq // tm),
        grid=(T // tm,),
        in_specs=[pl.BlockSpec((HALO, D), lambda i: (jnp.maximum(i * per - 1, 0), 0)),
                  row(D),
                  pl.BlockSpec((HALO, D), lambda i: (jnp.minimum((i + 1) * per, nh - 1), 0)),
                  row(D),
                  once(wg.shape), once(wu.shape), _const_spec(cw.shape), _const_spec(cb.shape),
                  once(wd.shape), _const_spec(gn.shape)],
        out_specs=row(D),
        out_shape=jax.ShapeDtypeStruct((T, D), F32),
        scratch_shapes=[pltpu.VMEM((tm + 2 * HALO, D), BF16), pltpu.VMEM((tm, dff), BF16)],
        compiler_params=_params(("parallel",)),
        name="ffn",
    )(h2, h2, h2, x1, wg, wu, cw, cb, wd, gn)


def _row_tile(seq):
    return min(1024, seq)


def _layer(x, p):
    bsz, seq, d = x.shape
    T = bsz * seq
    tm = _row_tile(seq)
    ssm_w = d // 2
    ngroups = ssm_w // SSM_CH
    attn_w = N_HEADS * HEAD_DIM
    row2 = lambda v: v.reshape(1, -1).astype(F32)

    perm = _head_perm()
    w_in = jnp.concatenate([p["w_in"][:, :attn_w][:, perm], p["w_in"][:, attn_w:]], axis=1).astype(BF16)
    x2 = x.reshape(T, d)
    q, k, v, u, gates = _inproj(x2, row2(p["norm_mix_pre"]), w_in, bsz, seq, tm)

    attn = _attention(q, k, v, jnp.asarray(_attn_bias()), p["attn_sink"].astype(F32), bsz, seq)

    npair = ngroups // PAIR
    both = lambda re, im: jnp.stack([jnp.stack([p[re + "_f"], p[im + "_f"]]),
                                     jnp.stack([p[re + "_b"], p[im + "_b"]])]).astype(F32)
    lam = both("lam_re", "lam_im").reshape(2, 2, npair, 1, LANES).transpose(0, 2, 1, 3, 4)
    ldt = jnp.repeat(jnp.stack([p["log_dt_f"], p["log_dt_b"]]).astype(F32), SSM_STATE, axis=-1)
    ldt = ldt.reshape(2, npair, 1, LANES)
    bt = both("b_re", "b_im").reshape(2, 2, npair, PAIR, SSM_STATE, SSM_CH)
    bt = bt.transpose(0, 2, 1, 5, 3, 4).reshape(2, npair, 2, SSM_CH, LANES)
    cc = both("c_re", "c_im").reshape(2, 2, npair, PAIR, SSM_CH, SSM_STATE)
    cc = cc.transpose(0, 2, 1, 4, 3, 5).reshape(2, npair, 2, SSM_CH, LANES)
    wb, wct, tsum, a = _s5prep(lam, ldt, bt, cc)

    assert bsz == SUBLANES // 2
    ys = _s5(u.reshape(T, ssm_w), wb, tsum, wct, a).reshape(u.shape)

    x1, h2 = _merge(attn, ys, u, gates, x2,
                    p["w_branch_attn"][perm, :].astype(BF16), p["w_glu"].astype(BF16), row2(p["b_glu"]),
                    row2(p["ssm_d"]), p["w_branch_ssm"].astype(BF16), p["w_out"].astype(BF16),
                    row2(p["norm_mix_post"]), row2(p["norm_ffn_pre"]), seq, tm)

    out = _ffn(h2, x1, p["w_ffn_gate"].astype(BF16), p["w_ffn_up"].astype(BF16), p["conv_w"].astype(F32),
               row2(p["conv_b"]), p["w_ffn_down"].astype(BF16), row2(p["norm_ffn_post"]), seq, tm)
    return out.reshape(bsz, seq, d)


_PARAM_NAMES = (
    "norm_mix_pre", "w_in", "attn_sink",
    "lam_re_f", "lam_im_f", "log_dt_f", "b_re_f", "b_im_f", "c_re_f", "c_im_f",
    "lam_re_b", "lam_im_b", "log_dt_b", "b_re_b", "b_im_b", "c_re_b", "c_im_b",
    "ssm_d", "w_glu", "b_glu", "w_branch_attn", "w_branch_ssm", "w_out", "norm_mix_post",
    "norm_ffn_pre", "w_ffn_gate", "w_ffn_up", "conv_w", "conv_b", "w_ffn_down", "norm_ffn_post")


def kernel(x, norm_mix_pre, w_in, attn_sink, lam_re_f, lam_im_f, log_dt_f, b_re_f, b_im_f, c_re_f, c_im_f, lam_re_b, lam_im_b, log_dt_b, b_re_b, b_im_b, c_re_b, c_im_b, ssm_d, w_glu, b_glu, w_branch_attn, w_branch_ssm, w_out, norm_mix_post, norm_ffn_pre, w_ffn_gate, w_ffn_up, conv_w, conv_b, w_ffn_down, norm_ffn_post):
    stacked = dict(zip(_PARAM_NAMES, (
        norm_mix_pre, w_in, attn_sink,
        lam_re_f, lam_im_f, log_dt_f, b_re_f, b_im_f, c_re_f, c_im_f,
        lam_re_b, lam_im_b, log_dt_b, b_re_b, b_im_b, c_re_b, c_im_b,
        ssm_d, w_glu, b_glu, w_branch_attn, w_branch_ssm, w_out, norm_mix_post,
        norm_ffn_pre, w_ffn_gate, w_ffn_up, conv_w, conv_b, w_ffn_down, norm_ffn_post)))
    for layer in range(w_in.shape[0]):
        x = _layer(x, {name: value[layer] for name, value in stacked.items()})
    return x
```

```python
import functools
import math

import numpy as np
import jax
import jax.numpy as jnp
from jax import lax
from jax.experimental import pallas as pl
from jax.experimental.pallas import tpu as pltpu

F32 = jnp.float32
BF16 = jnp.bfloat16

N_HEADS = 8
N_KV_HEADS = 2
HEAD_DIM = 64
GROUP = N_HEADS // N_KV_HEADS
WINDOW = 128
BLOCK = 128
SSM_CH = 16
SSM_STATE = 64
CHUNK = 16
PAIR = 2
N_BRANCH = 2
EPS = 1e-6
NEG_INF = -1e30
LOG2E = math.log2(math.e)

LANES = 128
SUBLANES = 8
HALO = 16
FF_TILE = 256
VMEM_LIMIT = 56 * 1024 * 1024


def _gelu_tanh(x):
    return 0.5 * x * (1.0 + jnp.tanh(math.sqrt(2.0 / math.pi) * (x + 0.044715 * (x * x * x))))


def _rms(x, gain):
    return x * lax.rsqrt(jnp.mean(x * x, axis=-1, keepdims=True) + EPS) * gain


def _params(semantics):
    return pltpu.CompilerParams(dimension_semantics=semantics, vmem_limit_bytes=VMEM_LIMIT)


def _const_spec(shape):
    nd = len(shape)
    return pl.BlockSpec(shape, lambda *_: (0,) * nd)


def _inproj_body(splits, x_ref, g_ref, wq_ref, w_ref, q_ref, k_ref, v_ref, u_ref, gate_ref):
    h = _rms(x_ref[...], g_ref[...]).astype(BF16)

    def proj(i):
        return jnp.dot(h, w_ref[:, splits[i]:splits[i + 1]], preferred_element_type=F32)

    q = jnp.dot(h, wq_ref[...], preferred_element_type=F32)
    q_ref[...] = (q * (HEAD_DIM ** -0.5 * LOG2E)).astype(BF16)
    k_ref[...] = proj(1).astype(BF16)
    v_ref[...] = proj(2).astype(BF16)
    u_ref[...] = proj(3).astype(BF16).reshape(u_ref.shape)
    gate_ref[...] = jax.nn.sigmoid(proj(4)).astype(BF16)


def _chunk_rows_spec(tm, tiles_per_seq, width):
    return pl.BlockSpec((tm // CHUNK, None, CHUNK, width),
                        lambda i: (i % tiles_per_seq, i // tiles_per_seq, 0, 0))


def _inproj(x2, gain, w_q, w_in, bsz, seq, tm):
    T, D = x2.shape
    attn_w = N_HEADS * HEAD_DIM
    kv_w = N_KV_HEADS * HEAD_DIM
    widths = [attn_w, kv_w, kv_w, D // 2, N_BRANCH * D]
    splits = [0] + np.cumsum(widths).tolist()
    row = lambda w: pl.BlockSpec((tm, w), lambda i: (i, 0))
    out_specs = [row(w) for w in widths]
    out_shape = [jax.ShapeDtypeStruct((T, w), BF16) for w in widths]
    out_specs[3] = _chunk_rows_spec(tm, seq // tm, widths[3])
    out_shape[3] = jax.ShapeDtypeStruct((seq // CHUNK, bsz, CHUNK, widths[3]), BF16)
    return pl.pallas_call(
        functools.partial(_inproj_body, tuple(splits)),
        grid=(T // tm,),
        in_specs=[row(D), _const_spec((1, D)), _const_spec(w_q.shape), _const_spec(w_in.shape)],
        out_specs=out_specs,
        out_shape=out_shape,
        compiler_params=_params(("parallel",)),
        name="inproj",
    )(x2, gain, w_q, w_in)


def _cast_view(w, steps):
    per_step = w.size // steps
    assert w.size % steps == 0 and per_step % (HALO * LANES) == 0
    return w.reshape(steps * HALO, per_step // HALO)


def _cast_specs(views, index):
    specs = [pl.BlockSpec((HALO, v.shape[1]), lambda *ids: (index(*ids), 0)) for v in views]
    return specs, specs, [jax.ShapeDtypeStruct(v.shape, BF16) for v in views]


def _attn_body(ncast, q_ref, kp_ref, kc_ref, kn_ref, vp_ref, vc_ref, vn_ref, bias0_ref, bias1_ref, sink_ref,
               *refs):
    o_ref = refs[ncast]
    for src, dst in zip(refs[:ncast], refs[ncast + 1:]):
        dst[...] = src[...].astype(BF16)
    lane = lax.broadcasted_iota(jnp.int32, (1, LANES), 1)
    left = lane < HEAD_DIM
    top = lax.broadcasted_iota(jnp.int32, (LANES, 1), 0) < HEAD_DIM
    zero = jnp.zeros((), BF16)
    kall = jnp.concatenate([kp_ref[...], kc_ref[...], kn_ref[...]], axis=0)
    vall = jnp.concatenate([vp_ref[...], vc_ref[...], vn_ref[...]], axis=0)
    for b, bias_ref in enumerate((bias0_ref, bias1_ref)):
        kcat = kall[BLOCK * b:BLOCK * (b + 3)]
        vcat = vall[BLOCK * b:BLOCK * (b + 3)]
        rows = slice(BLOCK * b, BLOCK * (b + 1))
        for i in range(GROUP):
            qi = q_ref[rows, LANES * i:LANES * (i + 1)]
            halves = []
            for g in range(N_KV_HEADS):
                h = g * GROUP + i
                qm = jnp.where(left if g == 0 else jnp.logical_not(left), qi, zero)
                st = lax.dot_general(kcat, qm, (((1,), (1,)), ((), ())), preferred_element_type=F32)
                st = st + bias_ref[h]
                sink = sink_ref[h] * LOG2E
                m = jnp.maximum(jnp.max(st, axis=0, keepdims=True), sink)
                p = jnp.exp2(st - m)
                denom = jnp.sum(p, axis=0, keepdims=True) + jnp.exp2(sink - m)
                ot = lax.dot_general(vcat, p.astype(BF16), (((0,), (0,)), ((), ())),
                                     preferred_element_type=F32)
                halves.append(ot / denom)
            o_ref[rows, LANES * i:LANES * (i + 1)] = jnp.where(top, halves[0], halves[1]).T.astype(BF16)


def _attn_bias():
    t = np.arange(BLOCK)[None, :]
    s = np.arange(3 * BLOCK)[:, None] - BLOCK
    dist = np.abs(t - s).astype(np.float64)
    valid = dist <= WINDOW
    slopes = 2.0 ** (-8.0 * (np.arange(N_HEADS) + 1.0) / N_HEADS)
    out = np.zeros((3, N_HEADS, 3 * BLOCK, BLOCK), np.float32)
    for variant in range(3):
        ok = valid.copy()
        if variant == 0:
            ok[:BLOCK] = False
        if variant == 2:
            ok[2 * BLOCK:] = False
        for h in range(N_HEADS):
            out[variant, h] = np.where(ok, -slopes[h] * dist * LOG2E, NEG_INF)
    return out


def _head_perm():
    cols = []
    for i in range(GROUP):
        for g in range(N_KV_HEADS):
            h = g * GROUP + i
            cols.extend(range(h * HEAD_DIM, (h + 1) * HEAD_DIM))
    return np.asarray(cols, np.int32)


def _attention(q, k, v, bias, sink, bsz, seq, cast_views):
    nb = seq // BLOCK
    assert nb % 2 == 0
    ns = nb // 2
    kvw = k.shape[1]
    aw = q.shape[1]

    def edge_spec(off):
        return pl.BlockSpec((BLOCK, kvw), lambda b, n: (b * nb + jnp.clip(2 * n + off, 0, nb - 1), 0))

    centre = pl.BlockSpec((2 * BLOCK, kvw), lambda b, n: (b * ns + n, 0))
    bias_spec = lambda pick: pl.BlockSpec((None,) + bias.shape[1:], lambda b, n: (pick(n), 0, 0, 0))
    first = bias_spec(lambda n: jnp.where(n == 0, 0, 1))
    last = bias_spec(lambda n: jnp.where(n == ns - 1, 2, 1))
    qspec = pl.BlockSpec((2 * BLOCK, aw), lambda b, n: (b * ns + n, 0))
    cast_in, cast_out, cast_shape = _cast_specs(cast_views, lambda b, n: b * ns + n)
    return pl.pallas_call(
        functools.partial(_attn_body, len(cast_views)),
        grid=(bsz, ns),
        in_specs=[qspec, edge_spec(-1), centre, edge_spec(2), edge_spec(-1), centre, edge_spec(2),
                  first, last,
                  pl.BlockSpec(memory_space=pltpu.SMEM)] + cast_in,
        out_specs=[qspec] + cast_out,
        out_shape=[jax.ShapeDtypeStruct(q.shape, BF16)] + cast_shape,
        compiler_params=_params(("parallel", "arbitrary")),
        name="attn",
    )(q, k, k, k, v, v, v, bias, bias, sink, *cast_views)


def _s5prep_body(lam_ref, ldt_ref, bt_ref, c_ref, w32_ref, wb_ref, wct_ref, t_ref, a_ref, w16_ref):
    w16_ref[...] = w32_ref[...].astype(BF16)
    width = CHUNK * SSM_CH
    lane = lax.broadcasted_iota(jnp.int32, (1, LANES), 1)
    in_group = [lane < SSM_STATE, lane >= SSM_STATE]
    lane_t = lax.broadcasted_iota(jnp.int32, (SSM_CH, width), 1)
    contract = (((1,), (1,)), ((), ()))
    hi = lax.Precision.HIGHEST
    toeplitz = [[None] * CHUNK for _ in range(PAIR)]
    for d in range(2):
        lre, lim = lam_ref[d, 0], lam_ref[d, 1]
        dt = jnp.exp(ldt_ref[d])
        mag = jnp.exp(lre * dt)
        are, aim = mag * jnp.cos(lim * dt), mag * jnp.sin(lim * dt)
        den = lre * lre + lim * lim
        cfr = ((are - 1.0) * lre + aim * lim) / den
        cfi = (aim * lre - (are - 1.0) * lim) / den
        btr, bti = bt_ref[d, 0], bt_ref[d, 1]
        bbr = btr * cfr - bti * cfi
        bbi = btr * cfi + bti * cfr
        cre, cim = c_ref[d, 0], c_ref[d, 1]

        pwr, pwi = jnp.ones_like(are), jnp.zeros_like(are)
        e_r, e_i = [], []
        for p in range(CHUNK + 1):
            er = cre * pwr - cim * pwi
            ei = cre * pwi + cim * pwr
            if p < CHUNK:
                e_r.append(er)
                e_i.append(ei)
                k = CHUNK - 1 - p if d == 0 else p
                for ri, val in enumerate((bbr * pwr - bbi * pwi, bbr * pwi + bbi * pwr)):
                    for g in range(PAIR):
                        wb_ref[width * g + SSM_CH * k:width * g + SSM_CH * (k + 1),
                               LANES * (2 * d + ri):LANES * (2 * d + ri + 1)] = jnp.where(
                                   in_group[g], val, 0.0).astype(BF16)
            if p >= 1:
                r = p - 1 if d == 0 else CHUNK - p
                for ri, val in enumerate((er, -ei)):
                    for g in range(PAIR):
                        wct_ref[width * g + SSM_CH * r:width * g + SSM_CH * (r + 1),
                                LANES * (2 * d + ri):LANES * (2 * d + ri + 1)] = jnp.where(
                                    in_group[g], val, 0.0).astype(BF16)
            if p == CHUNK:
                a_ref[2 * d] = jnp.broadcast_to(pwr, a_ref.shape[1:])
                a_ref[2 * d + 1] = jnp.broadcast_to(pwi, a_ref.shape[1:])
            pwr, pwi = pwr * are - pwi * aim, pwr * aim + pwi * are

        if d == 1:
            e_r, e_i = e_r[::-1], e_i[::-1]
        ecat_r, ecat_i = jnp.concatenate(e_r, axis=0), jnp.concatenate(e_i, axis=0)
        for g in range(PAIR):
            gr, gi = jnp.where(in_group[g], bbr, 0.0), jnp.where(in_group[g], bbi, 0.0)
            krow = (lax.dot_general(gr, ecat_r, contract, precision=hi, preferred_element_type=F32)
                    - lax.dot_general(gi, ecat_i, contract, precision=hi, preferred_element_type=F32))
            for k in range(CHUNK):
                if d == 0:
                    shift, keep = SSM_CH * k, lane_t >= SSM_CH * k
                else:
                    shift, keep = (SSM_CH * (k + 1)) % width, lane_t < SSM_CH * (k + 1)
                blk = jnp.where(keep, pltpu.roll(krow, shift, 1) if shift else krow, 0.0)
                toeplitz[g][k] = blk if d == 0 else toeplitz[g][k] + blk
    for g in range(PAIR):
        t_ref[g] = jnp.concatenate(toeplitz[g], axis=0).astype(BF16)


def _s5prep(lam, ldt, bt, c, w32):
    npair = lam.shape[1]
    width = CHUNK * SSM_CH
    pw = PAIR * width
    cast_in, cast_out, cast_shape = _cast_specs([w32], lambda i: i)
    return pl.pallas_call(
        _s5prep_body,
        grid=(npair,),
        in_specs=[pl.BlockSpec((2, None, 2, 1, LANES), lambda i: (0, i, 0, 0, 0)),
                  pl.BlockSpec((2, None, 1, LANES), lambda i: (0, i, 0, 0)),
                  pl.BlockSpec((2, None, 2, SSM_CH, LANES), lambda i: (0, i, 0, 0, 0)),
                  pl.BlockSpec((2, None, 2, SSM_CH, LANES), lambda i: (0, i, 0, 0, 0))] + cast_in,
        out_specs=[pl.BlockSpec((None, pw, 4 * LANES), lambda i: (i, 0, 0)),
                   pl.BlockSpec((None, pw, 4 * LANES), lambda i: (i, 0, 0)),
                   pl.BlockSpec((PAIR, width, width), lambda i: (i, 0, 0)),
                   pl.BlockSpec((None, 4, SUBLANES, LANES), lambda i: (i, 0, 0, 0))] + cast_out,
        out_shape=[jax.ShapeDtypeStruct((npair, pw, 4 * LANES), BF16),
                   jax.ShapeDtypeStruct((npair, pw, 4 * LANES), BF16),
                   jax.ShapeDtypeStruct((npair * PAIR, width, width), BF16),
                   jax.ShapeDtypeStruct((npair, 4, SUBLANES, LANES), F32)] + cast_shape,
        compiler_params=_params(("parallel",)),
        name="s5prep",
    )(lam, ldt, bt, c, w32)


def _block_transpose(xs):
    n = len(xs)
    assert n * SSM_CH == LANES and n & (n - 1) == 0
    blk = lax.broadcasted_iota(jnp.int32, xs[0].shape, 1) // SSM_CH
    d = n // 2
    while d:
        upper = (blk & d) != 0
        ys = list(xs)
        for a in range(n):
            if a & d:
                continue
            b = a + d
            ys[a] = jnp.where(upper, pltpu.roll(xs[b], SSM_CH * d, 1), xs[a])
            ys[b] = jnp.where(upper, xs[b], pltpu.roll(xs[a], LANES - SSM_CH * d, 1))
        xs = ys
        d //= 2
    return xs


def _s5_scan(a_ref, w_scr, s_scr):
    rows = w_scr.shape[0]
    nsteps = rows // SUBLANES
    nstream = w_scr.shape[1] // (2 * LANES)
    half = SUBLANES // 2
    lo = lax.broadcasted_iota(jnp.int32, (SUBLANES, LANES), 0) < half
    first = [lo, jnp.logical_not(lo)]
    coef = [(a_ref[s // 2, 2 * (s % 2)], a_ref[s // 2, 2 * (s % 2) + 1]) for s in range(nstream)]
    col = lambda i: slice(LANES * i, LANES * (i + 1))

    def advance(a, xr, xi, wr, wi):
        return a[0] * xr - a[1] * xi + wr, a[0] * xi + a[1] * xr + wi

    def swap(x):
        return pltpu.roll(x, half, 0)

    def step(i, carry):
        out = []
        for s in range(nstream):
            d = s % 2
            cr, ci = carry[2 * s], carry[2 * s + 1]
            vreg = i if d == 0 else nsteps - 1 - i
            rs = pl.ds(pl.multiple_of(vreg * SUBLANES, SUBLANES), SUBLANES)
            wr, wi = w_scr[rs, col(2 * s)], w_scr[rs, col(2 * s + 1)]
            r0r, r0i = swap(cr), swap(ci)
            t1r, t1i = advance(coef[s], r0r, r0i, wr, wi)
            r1r, r1i = swap(t1r), swap(t1i)
            t2r, t2i = advance(coef[s], r1r, r1i, wr, wi)
            s_scr[rs, col(2 * s)] = jnp.where(first[d], r0r, r1r)
            s_scr[rs, col(2 * s + 1)] = jnp.where(first[d], r0i, r1i)
            out += [jnp.where(first[d], t1r, t2r), jnp.where(first[d], t1i, t2i)]
        return tuple(out)

    z = jnp.zeros((SUBLANES, LANES), F32)
    lax.fori_loop(0, nsteps, step, (z,) * (2 * nstream))


S5_SLAB_GROUPS = LANES // SSM_CH
S5_ROW_CHUNK = 256
S5_SUB_ROWS = 128
S5_SCAN_PAIRS = 2


def _s5_body(u_ref, wb_ref, t_ref, wct_ref, a_ref, y_ref, ur_scr, w_scr, s_scr, sall_scr, yr_scr, tok_scr):
    rows = ur_scr.shape[0]
    width = CHUNK * SSM_CH
    pw = PAIR * width
    npair = S5_SLAB_GROUPS // PAIR
    half_steps = CHUNK // 2
    tok_rows = S5_ROW_CHUNK * CHUNK

    nchunks = rows // S5_ROW_CHUNK
    nsub = S5_ROW_CHUNK // S5_SUB_ROWS
    sw = 4 * LANES
    row_chunk = lambda c: slice(S5_ROW_CHUNK * c, S5_ROW_CHUNK * (c + 1))
    tok_chunk = lambda c: slice(tok_rows * c, tok_rows * (c + 1))
    step_rows = lambda i, h, kk: pl.ds(S5_SUB_ROWS * i * CHUNK + half_steps * h + kk, S5_SUB_ROWS,
                                       stride=CHUNK)
    lane_blk = lambda g, h: slice(width * g + LANES * h, width * g + LANES * (h + 1))

    def relayout_in(c):
        tok = tok_scr.at[c % 2]
        tok[...] = u_ref[tok_chunk(c), :].astype(F32)
        for i in range(nsub):
            for h in range(2):
                xs = [tok[step_rows(i, h, kk), :] for kk in range(half_steps)]
                ys = _block_transpose(xs)
                for g in range(S5_SLAB_GROUPS):
                    r0 = S5_ROW_CHUNK * c + S5_SUB_ROWS * i
                    ur_scr[r0:r0 + S5_SUB_ROWS, lane_blk(g, h)] = ys[g].astype(BF16)

    def state_in(c, p0):
        for q in range(S5_SCAN_PAIRS):
            pp = p0 + q
            w_scr[row_chunk(c), sw * q:sw * (q + 1)] = jnp.dot(
                ur_scr[row_chunk(c), pw * pp:pw * (pp + 1)], wb_ref[pp], preferred_element_type=F32)

    for p0 in range(0, npair, S5_SCAN_PAIRS):
        for c in range(nchunks):
            if p0 == 0:
                relayout_in(c)
            state_in(c, p0)
        _s5_scan(a_ref.at[p0:p0 + S5_SCAN_PAIRS], w_scr, s_scr)
        sall_scr[:, sw * p0:sw * (p0 + S5_SCAN_PAIRS)] = s_scr[...].astype(BF16)

    def outputs(c):
        yr, tok = yr_scr.at[c % 2], tok_scr.at[c % 2]
        for pp in range(npair):
            y = lax.dot_general(sall_scr[row_chunk(c), sw * pp:sw * (pp + 1)], wct_ref[pp],
                                (((1,), (1,)), ((), ())), preferred_element_type=F32)
            for g in range(PAIR):
                lo = pw * pp + width * g
                yr[:, lo:lo + width] = y[:, width * g:width * (g + 1)] + jnp.dot(
                    ur_scr[row_chunk(c), lo:lo + width], t_ref[PAIR * pp + g], preferred_element_type=F32)
        for i in range(nsub):
            for h in range(2):
                xs = [yr[S5_SUB_ROWS * i:S5_SUB_ROWS * (i + 1), lane_blk(g, h)].astype(BF16)
                      for g in range(S5_SLAB_GROUPS)]
                ys = _block_transpose(xs)
                for kk in range(half_steps):
                    tok[step_rows(i, h, kk), :] = ys[kk].astype(F32)
        y_ref[tok_chunk(c), :] = tok[...].astype(BF16)

    for c in range(nchunks):
        outputs(c)


def _s5(u_tok, wb, tsum, wct, a16):
    T, ssm_w = u_tok.shape
    rows = T // CHUNK
    assert rows % S5_ROW_CHUNK == 0 and S5_SLAB_GROUPS == CHUNK // 2
    width = CHUNK * SSM_CH
    pw = PAIR * width
    npair = S5_SLAB_GROUPS // PAIR
    slab = pl.BlockSpec((T, LANES), lambda s: (0, s))
    return pl.pallas_call(
        _s5_body,
        grid=(ssm_w // LANES,),
        in_specs=[slab,
                  pl.BlockSpec((npair, pw, 4 * LANES), lambda s: (s, 0, 0)),
                  pl.BlockSpec((S5_SLAB_GROUPS, width, width), lambda s: (s, 0, 0)),
                  pl.BlockSpec((npair, pw, 4 * LANES), lambda s: (s, 0, 0)),
                  pl.BlockSpec((npair, 4, SUBLANES, LANES), lambda s: (s, 0, 0, 0))],
        out_specs=slab,
        out_shape=jax.ShapeDtypeStruct((T, ssm_w), BF16),
        scratch_shapes=[pltpu.VMEM((rows, S5_SLAB_GROUPS * width), BF16),
                        pltpu.VMEM((rows, S5_SCAN_PAIRS * 4 * LANES), F32),
                        pltpu.VMEM((rows, S5_SCAN_PAIRS * 4 * LANES), F32),
                        pltpu.VMEM((rows, npair * 4 * LANES), BF16),
                        pltpu.VMEM((2, S5_ROW_CHUNK, S5_SLAB_GROUPS * width), F32),
                        pltpu.VMEM((2, S5_ROW_CHUNK * CHUNK, LANES), F32)],
        compiler_params=_params(("parallel",)),
        name="s5",
    )(u_tok, wb, tsum, wct, a16)


def _merge_body(attn_ref, ys_ref, u_ref, g_ref, x_ref, wba_ref, wglu_ref, bglu_ref, d_ref, wbs_ref,
                wout_ref, gpost_ref, gpre_ref, x1_ref, h2_ref):
    tm, d = x_ref.shape
    u = u_ref[...].reshape(tm, -1).astype(F32)
    y = ys_ref[...].reshape(tm, -1).astype(F32) + d_ref[...] * u
    y = _gelu_tanh(y)
    gl = jnp.dot(y.astype(BF16), wglu_ref[...], preferred_element_type=F32) + bglu_ref[...]
    s = y * jax.nn.sigmoid(gl)
    y_ssm = jnp.dot(s.astype(BF16), wbs_ref[...], preferred_element_type=F32)
    y_attn = jnp.dot(attn_ref[...], wba_ref[...], preferred_element_type=F32)
    merged = g_ref[:, :d].astype(F32) * y_attn + g_ref[:, d:].astype(F32) * y_ssm
    o = jnp.dot(merged.astype(BF16), wout_ref[...], preferred_element_type=F32)
    x1 = x_ref[...] + _rms(o, gpost_ref[...])
    x1_ref[...] = x1
    h2_ref[...] = _rms(x1, gpre_ref[...]).astype(BF16)


def _merge(attn, ys, u, gates, x2, wba, wglu, bglu, ssm_d, wbs, wout, gpost, gpre, seq, tm):
    T, D = x2.shape
    row = lambda w: pl.BlockSpec((tm, w), lambda i: (i, 0))
    chunked = _chunk_rows_spec(tm, seq // tm, u.shape[-1])
    consts = [wba, wglu, bglu, ssm_d, wbs, wout, gpost, gpre]
    return pl.pallas_call(
        _merge_body,
        grid=(T // tm,),
        in_specs=[row(attn.shape[1]), chunked, chunked, row(gates.shape[1]), row(D)]
                 + [_const_spec(c.shape) for c in consts],
        out_specs=[row(D), row(D)],
        out_shape=[jax.ShapeDtypeStruct((T, D), F32), jax.ShapeDtypeStruct((T, D), BF16)],
        compiler_params=_params(("parallel",)),
        name="merge",
    )(attn, ys, u, gates, x2, *consts)


def _ffn_body(tiles_per_seq, hp_ref, hc_ref, hn_ref, x1_ref, wg_ref, wu_ref, cw_ref, cb_ref, wd_ref,
              gn_ref, o_ref, hext, fscr):
    tm = hc_ref.shape[0]
    i = pl.program_id(0) % tiles_per_seq
    zero = jnp.zeros((), BF16)
    hext[0:HALO] = jnp.where(i == 0, zero, hp_ref[...])
    hext[HALO:HALO + tm] = hc_ref[...]
    hext[HALO + tm:] = jnp.where(i == tiles_per_seq - 1, zero, hn_ref[...])
    ext = tm + 2 * HALO
    for c in range(wg_ref.shape[1] // FF_TILE):
        sl = slice(FF_TILE * c, FF_TILE * (c + 1))
        g = jnp.dot(hext[...], wg_ref[:, sl], preferred_element_type=F32)
        up = jnp.dot(hc_ref[...], wu_ref[:, sl], preferred_element_type=F32)
        g_prev = pltpu.roll(g, 1, 0)[HALO:HALO + tm]
        g_next = pltpu.roll(g, ext - 1, 0)[HALO:HALO + tm]
        conv = (g_prev * cw_ref[0:1, sl] + g[HALO:HALO + tm] * cw_ref[1:2, sl]
                + g_next * cw_ref[2:3, sl] + cb_ref[:, sl])
        fscr[:, sl] = (_gelu_tanh(conv) * up).astype(BF16)
    o = jnp.dot(fscr[...], wd_ref[...], preferred_element_type=F32)
    o_ref[...] = x1_ref[...] + _rms(o, gn_ref[...])


def _ffn(h2, x1, wg, wu, cw, cb, wd, gn, seq, tm):
    T, D = x1.shape
    dff = wg.shape[1]
    assert dff % FF_TILE == 0 and tm % HALO == 0 and seq % tm == 0
    per = tm // HALO
    nh = T // HALO
    row = lambda w: pl.BlockSpec((tm, w), lambda i: (i, 0))
    once = lambda shape: pl.BlockSpec(shape, lambda i: (0,) * len(shape), pipeline_mode=pl.Buffered(1))
    return pl.pallas_call(
        functools.partial(_ffn_body, seq // tm),
        grid=(T // tm,),
        in_specs=[pl.BlockSpec((HALO, D), lambda i: (jnp.maximum(i * per - 1, 0), 0)),
                  row(D),
                  pl.BlockSpec((HALO, D), lambda i: (jnp.minimum((i + 1) * per, nh - 1), 0)),
                  row(D),
                  once(wg.shape), once(wu.shape), _const_spec(cw.shape), _const_spec(cb.shape),
                  once(wd.shape), _const_spec(gn.shape)],
        out_specs=row(D),
        out_shape=jax.ShapeDtypeStruct((T, D), F32),
        scratch_shapes=[pltpu.VMEM((tm + 2 * HALO, D), BF16), pltpu.VMEM((tm, dff), BF16)],
        compiler_params=_params(("parallel",)),
        name="ffn",
    )(h2, h2, h2, x1, wg, wu, cw, cb, wd, gn)


def _row_tile(seq):
    return min(1024, seq)


def _layer(x, p):
    bsz, seq, d = x.shape
    T = bsz * seq
    tm = _row_tile(seq)
    ssm_w = d // 2
    ngroups = ssm_w // SSM_CH
    attn_w = N_HEADS * HEAD_DIM
    row2 = lambda v: v.reshape(1, -1).astype(F32)

    npair = ngroups // PAIR
    both = lambda re, im: jnp.stack([jnp.stack([p[re + "_f"], p[im + "_f"]]),
                                     jnp.stack([p[re + "_b"], p[im + "_b"]])]).astype(F32)
    lam = both("lam_re", "lam_im").reshape(2, 2, npair, 1, LANES).transpose(0, 2, 1, 3, 4)
    ldt = jnp.repeat(jnp.stack([p["log_dt_f"], p["log_dt_b"]]).astype(F32), SSM_STATE, axis=-1)
    ldt = ldt.reshape(2, npair, 1, LANES)
    bt = both("b_re", "b_im").reshape(2, 2, npair, PAIR, SSM_STATE, SSM_CH)
    bt = bt.transpose(0, 2, 1, 5, 3, 4).reshape(2, npair, 2, SSM_CH, LANES)
    cc = both("c_re", "c_im").reshape(2, 2, npair, PAIR, SSM_CH, SSM_STATE)
    cc = cc.transpose(0, 2, 1, 4, 3, 5).reshape(2, npair, 2, SSM_CH, LANES)
    wb, wct, tsum, a, w_in = _s5prep(lam, ldt, bt, cc, _cast_view(p["w_in"], npair))
    w_in = w_in.reshape(p["w_in"].shape)

    perm = _head_perm()
    x2 = x.reshape(T, d)
    q, k, v, u, gates = _inproj(x2, row2(p["norm_mix_pre"]), p["w_in"][:, perm].astype(BF16), w_in, bsz, seq, tm)

    later = ("w_ffn_gate", "w_ffn_up", "w_ffn_down", "w_out", "w_branch_ssm", "w_glu")
    steps = bsz * (seq // (2 * BLOCK))
    attn, *rounded = _attention(q, k, v, jnp.asarray(_attn_bias()), p["attn_sink"].astype(F32), bsz, seq,
                                [_cast_view(p[name], steps) for name in later])
    w16 = {name: r.reshape(p[name].shape) for name, r in zip(later, rounded)}

    assert bsz == SUBLANES // 2
    ys = _s5(u.reshape(T, ssm_w), wb, tsum, wct, a).reshape(u.shape)

    x1, h2 = _merge(attn, ys, u, gates, x2,
                    p["w_branch_attn"][perm, :].astype(BF16), w16["w_glu"], row2(p["b_glu"]),
                    row2(p["ssm_d"]), w16["w_branch_ssm"], w16["w_out"],
                    row2(p["norm_mix_post"]), row2(p["norm_ffn_pre"]), seq, tm)

    out = _ffn(h2, x1, w16["w_ffn_gate"], w16["w_ffn_up"], p["conv_w"].astype(F32),
               row2(p["conv_b"]), w16["w_ffn_down"], row2(p["norm_ffn_post"]), seq, tm)
    return out.reshape(bsz, seq, d)


_PARAM_NAMES = (
    "norm_mix_pre", "w_in", "attn_sink",
    "lam_re_f", "lam_im_f", "log_dt_f", "b_re_f", "b_im_f", "c_re_f", "c_im_f",
    "lam_re_b", "lam_im_b", "log_dt_b", "b_re_b", "b_im_b", "c_re_b", "c_im_b",
    "ssm_d", "w_glu", "b_glu", "w_branch_attn", "w_branch_ssm", "w_out", "norm_mix_post",
    "norm_ffn_pre", "w_ffn_gate", "w_ffn_up", "conv_w", "conv_b", "w_ffn_down", "norm_ffn_post")


def kernel(x, norm_mix_pre, w_in, attn_sink, lam_re_f, lam_im_f, log_dt_f, b_re_f, b_im_f, c_re_f, c_im_f, lam_re_b, lam_im_b, log_dt_b, b_re_b, b_im_b, c_re_b, c_im_b, ssm_d, w_glu, b_glu, w_branch_attn, w_branch_ssm, w_out, norm_mix_post, norm_ffn_pre, w_ffn_gate, w_ffn_up, conv_w, conv_b, w_ffn_down, norm_ffn_post):
    stacked = dict(zip(_PARAM_NAMES, (
        norm_mix_pre, w_in, attn_sink,
        lam_re_f, lam_im_f, log_dt_f, b_re_f, b_im_f, c_re_f, c_im_f,
        lam_re_b, lam_im_b, log_dt_b, b_re_b, b_im_b, c_re_b, c_im_b,
        ssm_d, w_glu, b_glu, w_branch_attn, w_branch_ssm, w_out, norm_mix_post,
        norm_ffn_pre, w_ffn_gate, w_ffn_up, conv_w, conv_b, w_ffn_down, norm_ffn_post)))
    for layer in range(w_in.shape[0]):
        x = _layer(x, {name: value[layer] for name, value in stacked.items()})
    return x
```

```python
import functools
import math

import numpy as np
import jax
import jax.numpy as jnp
from jax import lax
from jax.experimental import pallas as pl
from jax.experimental.pallas import tpu as pltpu

F32 = jnp.float32
BF16 = jnp.bfloat16

N_HEADS = 8
N_KV_HEADS = 2
HEAD_DIM = 64
GROUP = N_HEADS // N_KV_HEADS
WINDOW = 128
BLOCK = 128
SSM_CH = 16
SSM_STATE = 64
CHUNK = 16
PAIR = 2
N_BRANCH = 2
EPS = 1e-6
NEG_INF = -1e30
LOG2E = math.log2(math.e)

LANES = 128
SUBLANES = 8
HALO = 16
FF_TILE = 256
VMEM_LIMIT = 56 * 1024 * 1024


def _gelu_tanh(x):
    return 0.5 * x * (1.0 + jnp.tanh(math.sqrt(2.0 / math.pi) * (x + 0.044715 * (x * x * x))))


def _rms(x, gain):
    return x * lax.rsqrt(jnp.mean(x * x, axis=-1, keepdims=True) + EPS) * gain


def _params(semantics):
    return pltpu.CompilerParams(dimension_semantics=semantics, vmem_limit_bytes=VMEM_LIMIT)


def _const_spec(shape):
    nd = len(shape)
    return pl.BlockSpec(shape, lambda *_: (0,) * nd)


def _inproj_body(splits, x_ref, g_ref, wq_ref, w_ref, q_ref, k_ref, v_ref, u_ref, gate_ref):
    h = _rms(x_ref[...], g_ref[...]).astype(BF16)

    def proj(i):
        return jnp.dot(h, w_ref[:, splits[i]:splits[i + 1]], preferred_element_type=F32)

    q = jnp.dot(h, wq_ref[...], preferred_element_type=F32)
    q_ref[...] = (q * (HEAD_DIM ** -0.5 * LOG2E)).astype(BF16)
    k_ref[...] = proj(1).astype(BF16)
    v_ref[...] = proj(2).astype(BF16)
    u_ref[...] = proj(3).astype(BF16).reshape(u_ref.shape)
    gate_ref[...] = jax.nn.sigmoid(proj(4)).astype(BF16)


def _chunk_rows_spec(tm, tiles_per_seq, width):
    return pl.BlockSpec((tm // CHUNK, None, CHUNK, width),
                        lambda i: (i % tiles_per_seq, i // tiles_per_seq, 0, 0))


def _inproj(x2, gain, w_q, w_in, bsz, seq, tm):
    T, D = x2.shape
    attn_w = N_HEADS * HEAD_DIM
    kv_w = N_KV_HEADS * HEAD_DIM
    widths = [attn_w, kv_w, kv_w, D // 2, N_BRANCH * D]
    splits = [0] + np.cumsum(widths).tolist()
    row = lambda w: pl.BlockSpec((tm, w), lambda i: (i, 0))
    out_specs = [row(w) for w in widths]
    out_shape = [jax.ShapeDtypeStruct((T, w), BF16) for w in widths]
    out_specs[3] = _chunk_rows_spec(tm, seq // tm, widths[3])
    out_shape[3] = jax.ShapeDtypeStruct((seq // CHUNK, bsz, CHUNK, widths[3]), BF16)
    return pl.pallas_call(
        functools.partial(_inproj_body, tuple(splits)),
        grid=(T // tm,),
        in_specs=[row(D), _const_spec((1, D)), _const_spec(w_q.shape), _const_spec(w_in.shape)],
        out_specs=out_specs,
        out_shape=out_shape,
        compiler_params=_params(("parallel",)),
        name="inproj",
    )(x2, gain, w_q, w_in)


def _cast_specs(weights, steps, step_of):
    specs = []
    for w in weights:
        nblocks = max(n for n in range(1, steps + 1)
                      if steps % n == 0 and w.shape[0] % (n * HALO) == 0)
        hold = steps // nblocks
        specs.append(pl.BlockSpec((w.shape[0] // nblocks, w.shape[1]),
                                  lambda *ids, hold=hold: (step_of(*ids) // hold, 0)))
    return specs, specs, [jax.ShapeDtypeStruct(w.shape, BF16) for w in weights]


def _attn_body(ncast, q_ref, kp_ref, kc_ref, kn_ref, vp_ref, vc_ref, vn_ref, bias0_ref, bias1_ref, sink_ref,
               *refs):
    o_ref = refs[ncast]
    for src, dst in zip(refs[:ncast], refs[ncast + 1:]):
        dst[...] = src[...].astype(BF16)
    lane = lax.broadcasted_iota(jnp.int32, (1, LANES), 1)
    left = lane < HEAD_DIM
    top = lax.broadcasted_iota(jnp.int32, (LANES, 1), 0) < HEAD_DIM
    zero = jnp.zeros((), BF16)
    kall = jnp.concatenate([kp_ref[...], kc_ref[...], kn_ref[...]], axis=0)
    vall = jnp.concatenate([vp_ref[...], vc_ref[...], vn_ref[...]], axis=0)
    for b, bias_ref in enumerate((bias0_ref, bias1_ref)):
        kcat = kall[BLOCK * b:BLOCK * (b + 3)]
        vcat = vall[BLOCK * b:BLOCK * (b + 3)]
        rows = slice(BLOCK * b, BLOCK * (b + 1))
        for i in range(GROUP):
            qi = q_ref[rows, LANES * i:LANES * (i + 1)]
            halves = []
            for g in range(N_KV_HEADS):
                h = g * GROUP + i
                qm = jnp.where(left if g == 0 else jnp.logical_not(left), qi, zero)
                st = lax.dot_general(kcat, qm, (((1,), (1,)), ((), ())), preferred_element_type=F32)
                st = st + bias_ref[h]
                sink = sink_ref[h] * LOG2E
                m = jnp.maximum(jnp.max(st, axis=0, keepdims=True), sink)
                p = jnp.exp2(st - m)
                denom = jnp.sum(p, axis=0, keepdims=True) + jnp.exp2(sink - m)
                ot = lax.dot_general(vcat, p.astype(BF16), (((0,), (0,)), ((), ())),
                                     preferred_element_type=F32)
                halves.append(ot / denom)
            o_ref[rows, LANES * i:LANES * (i + 1)] = jnp.where(top, halves[0], halves[1]).T.astype(BF16)


def _attn_bias():
    t = np.arange(BLOCK)[None, :]
    s = np.arange(3 * BLOCK)[:, None] - BLOCK
    dist = np.abs(t - s).astype(np.float64)
    valid = dist <= WINDOW
    slopes = 2.0 ** (-8.0 * (np.arange(N_HEADS) + 1.0) / N_HEADS)
    out = np.zeros((3, N_HEADS, 3 * BLOCK, BLOCK), np.float32)
    for variant in range(3):
        ok = valid.copy()
        if variant == 0:
            ok[:BLOCK] = False
        if variant == 2:
            ok[2 * BLOCK:] = False
        for h in range(N_HEADS):
            out[variant, h] = np.where(ok, -slopes[h] * dist * LOG2E, NEG_INF)
    return out


def _head_perm():
    cols = []
    for i in range(GROUP):
        for g in range(N_KV_HEADS):
            h = g * GROUP + i
            cols.extend(range(h * HEAD_DIM, (h + 1) * HEAD_DIM))
    return np.asarray(cols, np.int32)


def _attention(q, k, v, bias, sink, bsz, seq, cast_views):
    nb = seq // BLOCK
    assert nb % 2 == 0
    ns = nb // 2
    kvw = k.shape[1]
    aw = q.shape[1]

    def edge_spec(off):
        return pl.BlockSpec((BLOCK, kvw), lambda b, n: (b * nb + jnp.clip(2 * n + off, 0, nb - 1), 0))

    centre = pl.BlockSpec((2 * BLOCK, kvw), lambda b, n: (b * ns + n, 0))
    bias_spec = lambda pick: pl.BlockSpec((None,) + bias.shape[1:], lambda b, n: (pick(n), 0, 0, 0))
    first = bias_spec(lambda n: jnp.where(n == 0, 0, 1))
    last = bias_spec(lambda n: jnp.where(n == ns - 1, 2, 1))
    qspec = pl.BlockSpec((2 * BLOCK, aw), lambda b, n: (b * ns + n, 0))
    cast_in, cast_out, cast_shape = _cast_specs(cast_views, bsz * ns, lambda b, n: b * ns + n)
    return pl.pallas_call(
        functools.partial(_attn_body, len(cast_views)),
        grid=(bsz, ns),
        in_specs=[qspec, edge_spec(-1), centre, edge_spec(2), edge_spec(-1), centre, edge_spec(2),
                  first, last,
                  pl.BlockSpec(memory_space=pltpu.SMEM)] + cast_in,
        out_specs=[qspec] + cast_out,
        out_shape=[jax.ShapeDtypeStruct(q.shape, BF16)] + cast_shape,
        compiler_params=_params(("parallel", "arbitrary")),
        name="attn",
    )(q, k, k, k, v, v, v, bias, bias, sink, *cast_views)


def _s5prep_body(lam_ref, ldt_ref, bt_ref, c_ref, w32_ref, wb_ref, wct_ref, t_ref, a_ref, w16_ref):
    w16_ref[...] = w32_ref[...].astype(BF16)
    width = CHUNK * SSM_CH
    lane = lax.broadcasted_iota(jnp.int32, (1, LANES), 1)
    in_group = [lane < SSM_STATE, lane >= SSM_STATE]
    lane_t = lax.broadcasted_iota(jnp.int32, (SSM_CH, width), 1)
    contract = (((1,), (1,)), ((), ()))
    hi = lax.Precision.HIGHEST
    toeplitz = [[None] * CHUNK for _ in range(PAIR)]
    for d in range(2):
        lre, lim = lam_ref[d, 0], lam_ref[d, 1]
        dt = jnp.exp(ldt_ref[d])
        mag = jnp.exp(lre * dt)
        are, aim = mag * jnp.cos(lim * dt), mag * jnp.sin(lim * dt)
        den = lre * lre + lim * lim
        cfr = ((are - 1.0) * lre + aim * lim) / den
        cfi = (aim * lre - (are - 1.0) * lim) / den
        btr, bti = bt_ref[d, 0], bt_ref[d, 1]
        bbr = btr * cfr - bti * cfi
        bbi = btr * cfi + bti * cfr
        cre, cim = c_ref[d, 0], c_ref[d, 1]

        pwr, pwi = jnp.ones_like(are), jnp.zeros_like(are)
        e_r, e_i = [], []
        for p in range(CHUNK + 1):
            er = cre * pwr - cim * pwi
            ei = cre * pwi + cim * pwr
            if p < CHUNK:
                e_r.append(er)
                e_i.append(ei)
                k = CHUNK - 1 - p if d == 0 else p
                for ri, val in enumerate((bbr * pwr - bbi * pwi, bbr * pwi + bbi * pwr)):
                    for g in range(PAIR):
                        wb_ref[width * g + SSM_CH * k:width * g + SSM_CH * (k + 1),
                               LANES * (2 * d + ri):LANES * (2 * d + ri + 1)] = jnp.where(
                                   in_group[g], val, 0.0).astype(BF16)
            if p >= 1:
                r = p - 1 if d == 0 else CHUNK - p
                for ri, val in enumerate((er, -ei)):
                    for g in range(PAIR):
                        wct_ref[width * g + SSM_CH * r:width * g + SSM_CH * (r + 1),
                                LANES * (2 * d + ri):LANES * (2 * d + ri + 1)] = jnp.where(
                                    in_group[g], val, 0.0).astype(BF16)
            if p == CHUNK:
                a_ref[2 * d] = jnp.broadcast_to(pwr, a_ref.shape[1:])
                a_ref[2 * d + 1] = jnp.broadcast_to(pwi, a_ref.shape[1:])
            pwr, pwi = pwr * are - pwi * aim, pwr * aim + pwi * are

        if d == 1:
            e_r, e_i = e_r[::-1], e_i[::-1]
        ecat_r, ecat_i = jnp.concatenate(e_r, axis=0), jnp.concatenate(e_i, axis=0)
        for g in range(PAIR):
            gr, gi = jnp.where(in_group[g], bbr, 0.0), jnp.where(in_group[g], bbi, 0.0)
            krow = (lax.dot_general(gr, ecat_r, contract, precision=hi, preferred_element_type=F32)
                    - lax.dot_general(gi, ecat_i, contract, precision=hi, preferred_element_type=F32))
            for k in range(CHUNK):
                if d == 0:
                    shift, keep = SSM_CH * k, lane_t >= SSM_CH * k
                else:
                    shift, keep = (SSM_CH * (k + 1)) % width, lane_t < SSM_CH * (k + 1)
                blk = jnp.where(keep, pltpu.roll(krow, shift, 1) if shift else krow, 0.0)
                toeplitz[g][k] = blk if d == 0 else toeplitz[g][k] + blk
    for g in range(PAIR):
        t_ref[g] = jnp.concatenate(toeplitz[g], axis=0).astype(BF16)


def _s5prep(lam, ldt, bt, c, w32):
    npair = lam.shape[1]
    width = CHUNK * SSM_CH
    pw = PAIR * width
    cast_in, cast_out, cast_shape = _cast_specs([w32], npair, lambda i: i)
    return pl.pallas_call(
        _s5prep_body,
        grid=(npair,),
        in_specs=[pl.BlockSpec((2, None, 2, 1, LANES), lambda i: (0, i, 0, 0, 0)),
                  pl.BlockSpec((2, None, 1, LANES), lambda i: (0, i, 0, 0)),
                  pl.BlockSpec((2, None, 2, SSM_CH, LANES), lambda i: (0, i, 0, 0, 0)),
                  pl.BlockSpec((2, None, 2, SSM_CH, LANES), lambda i: (0, i, 0, 0, 0))] + cast_in,
        out_specs=[pl.BlockSpec((None, pw, 4 * LANES), lambda i: (i, 0, 0)),
                   pl.BlockSpec((None, pw, 4 * LANES), lambda i: (i, 0, 0)),
                   pl.BlockSpec((PAIR, width, width), lambda i: (i, 0, 0)),
                   pl.BlockSpec((None, 4, SUBLANES, LANES), lambda i: (i, 0, 0, 0))] + cast_out,
        out_shape=[jax.ShapeDtypeStruct((npair, pw, 4 * LANES), BF16),
                   jax.ShapeDtypeStruct((npair, pw, 4 * LANES), BF16),
                   jax.ShapeDtypeStruct((npair * PAIR, width, width), BF16),
                   jax.ShapeDtypeStruct((npair, 4, SUBLANES, LANES), F32)] + cast_shape,
        compiler_params=_params(("parallel",)),
        name="s5prep",
    )(lam, ldt, bt, c, w32)


def _block_transpose(xs):
    n = len(xs)
    assert n * SSM_CH == LANES and n & (n - 1) == 0
    blk = lax.broadcasted_iota(jnp.int32, xs[0].shape, 1) // SSM_CH
    d = n // 2
    while d:
        upper = (blk & d) != 0
        ys = list(xs)
        for a in range(n):
            if a & d:
                continue
            b = a + d
            ys[a] = jnp.where(upper, pltpu.roll(xs[b], SSM_CH * d, 1), xs[a])
            ys[b] = jnp.where(upper, xs[b], pltpu.roll(xs[a], LANES - SSM_CH * d, 1))
        xs = ys
        d //= 2
    return xs


def _s5_scan(a_ref, w_scr, s_scr):
    rows = w_scr.shape[0]
    nsteps = rows // SUBLANES
    nstream = w_scr.shape[1] // (2 * LANES)
    half = SUBLANES // 2
    lo = lax.broadcasted_iota(jnp.int32, (SUBLANES, LANES), 0) < half
    first = [lo, jnp.logical_not(lo)]
    coef = [(a_ref[s // 2, 2 * (s % 2)], a_ref[s // 2, 2 * (s % 2) + 1]) for s in range(nstream)]
    col = lambda i: slice(LANES * i, LANES * (i + 1))

    def advance(a, xr, xi, wr, wi):
        return a[0] * xr - a[1] * xi + wr, a[0] * xi + a[1] * xr + wi

    def swap(x):
        return pltpu.roll(x, half, 0)

    def step(i, carry):
        out = []
        for s in range(nstream):
            d = s % 2
            cr, ci = carry[2 * s], carry[2 * s + 1]
            vreg = i if d == 0 else nsteps - 1 - i
            rs = pl.ds(pl.multiple_of(vreg * SUBLANES, SUBLANES), SUBLANES)
            wr, wi = w_scr[rs, col(2 * s)], w_scr[rs, col(2 * s + 1)]
            r0r, r0i = swap(cr), swap(ci)
            t1r, t1i = advance(coef[s], r0r, r0i, wr, wi)
            r1r, r1i = swap(t1r), swap(t1i)
            t2r, t2i = advance(coef[s], r1r, r1i, wr, wi)
            s_scr[rs, col(2 * s)] = jnp.where(first[d], r0r, r1r)
            s_scr[rs, col(2 * s + 1)] = jnp.where(first[d], r0i, r1i)
            out += [jnp.where(first[d], t1r, t2r), jnp.where(first[d], t1i, t2i)]
        return tuple(out)

    z = jnp.zeros((SUBLANES, LANES), F32)
    lax.fori_loop(0, nsteps, step, (z,) * (2 * nstream))


S5_SLAB_GROUPS = LANES // SSM_CH
S5_ROW_CHUNK = 256
S5_SUB_ROWS = 128
S5_SCAN_PAIRS = 2


def _s5_body(u_ref, wb_ref, t_ref, wct_ref, a_ref, y_ref, ur_scr, w_scr, s_scr, sall_scr, yr_scr, tok_scr):
    rows = ur_scr.shape[0]
    width = CHUNK * SSM_CH
    pw = PAIR * width
    npair = S5_SLAB_GROUPS // PAIR
    half_steps = CHUNK // 2
    tok_rows = S5_ROW_CHUNK * CHUNK

    nchunks = rows // S5_ROW_CHUNK
    nsub = S5_ROW_CHUNK // S5_SUB_ROWS
    sw = 4 * LANES
    row_chunk = lambda c: slice(S5_ROW_CHUNK * c, S5_ROW_CHUNK * (c + 1))
    tok_chunk = lambda c: slice(tok_rows * c, tok_rows * (c + 1))
    step_rows = lambda i, h, kk: pl.ds(S5_SUB_ROWS * i * CHUNK + half_steps * h + kk, S5_SUB_ROWS,
                                       stride=CHUNK)
    lane_blk = lambda g, h: slice(width * g + LANES * h, width * g + LANES * (h + 1))

    def relayout_in(c):
        tok = tok_scr.at[c % 2]
        tok[...] = u_ref[tok_chunk(c), :].astype(F32)
        for i in range(nsub):
            for h in range(2):
                xs = [tok[step_rows(i, h, kk), :] for kk in range(half_steps)]
                ys = _block_transpose(xs)
                for g in range(S5_SLAB_GROUPS):
                    r0 = S5_ROW_CHUNK * c + S5_SUB_ROWS * i
                    ur_scr[r0:r0 + S5_SUB_ROWS, lane_blk(g, h)] = ys[g].astype(BF16)

    def state_in(c, p0):
        for q in range(S5_SCAN_PAIRS):
            pp = p0 + q
            w_scr[row_chunk(c), sw * q:sw * (q + 1)] = jnp.dot(
                ur_scr[row_chunk(c), pw * pp:pw * (pp + 1)], wb_ref[pp], preferred_element_type=F32)

    for p0 in range(0, npair, S5_SCAN_PAIRS):
        for c in range(nchunks):
            if p0 == 0:
                relayout_in(c)
            state_in(c, p0)
        _s5_scan(a_ref.at[p0:p0 + S5_SCAN_PAIRS], w_scr, s_scr)
        sall_scr[:, sw * p0:sw * (p0 + S5_SCAN_PAIRS)] = s_scr[...].astype(BF16)

    def outputs(c):
        yr, tok = yr_scr.at[c % 2], tok_scr.at[c % 2]
        for pp in range(npair):
            y = lax.dot_general(sall_scr[row_chunk(c), sw * pp:sw * (pp + 1)], wct_ref[pp],
                                (((1,), (1,)), ((), ())), preferred_element_type=F32)
            for g in range(PAIR):
                lo = pw * pp + width * g
                yr[:, lo:lo + width] = y[:, width * g:width * (g + 1)] + jnp.dot(
                    ur_scr[row_chunk(c), lo:lo + width], t_ref[PAIR * pp + g], preferred_element_type=F32)
        for i in range(nsub):
            for h in range(2):
                xs = [yr[S5_SUB_ROWS * i:S5_SUB_ROWS * (i + 1), lane_blk(g, h)].astype(BF16)
                      for g in range(S5_SLAB_GROUPS)]
                ys = _block_transpose(xs)
                for kk in range(half_steps):
                    tok[step_rows(i, h, kk), :] = ys[kk].astype(F32)
        y_ref[tok_chunk(c), :] = tok[...].astype(BF16)

    for c in range(nchunks):
        outputs(c)


def _s5(u_tok, wb, tsum, wct, a16):
    T, ssm_w = u_tok.shape
    rows = T // CHUNK
    assert rows % S5_ROW_CHUNK == 0 and S5_SLAB_GROUPS == CHUNK // 2
    width = CHUNK * SSM_CH
    pw = PAIR * width
    npair = S5_SLAB_GROUPS // PAIR
    slab = pl.BlockSpec((T, LANES), lambda s: (0, s))
    return pl.pallas_call(
        _s5_body,
        grid=(ssm_w // LANES,),
        in_specs=[slab,
                  pl.BlockSpec((npair, pw, 4 * LANES), lambda s: (s, 0, 0)),
                  pl.BlockSpec((S5_SLAB_GROUPS, width, width), lambda s: (s, 0, 0)),
                  pl.BlockSpec((npair, pw, 4 * LANES), lambda s: (s, 0, 0)),
                  pl.BlockSpec((npair, 4, SUBLANES, LANES), lambda s: (s, 0, 0, 0))],
        out_specs=slab,
        out_shape=jax.ShapeDtypeStruct((T, ssm_w), BF16),
        scratch_shapes=[pltpu.VMEM((rows, S5_SLAB_GROUPS * width), BF16),
                        pltpu.VMEM((rows, S5_SCAN_PAIRS * 4 * LANES), F32),
                        pltpu.VMEM((rows, S5_SCAN_PAIRS * 4 * LANES), F32),
                        pltpu.VMEM((rows, npair * 4 * LANES), BF16),
                        pltpu.VMEM((2, S5_ROW_CHUNK, S5_SLAB_GROUPS * width), F32),
                        pltpu.VMEM((2, S5_ROW_CHUNK * CHUNK, LANES), F32)],
        compiler_params=_params(("parallel",)),
        name="s5",
    )(u_tok, wb, tsum, wct, a16)


def _merge_body(attn_ref, ys_ref, u_ref, g_ref, x_ref, wba_ref, wglu_ref, bglu_ref, d_ref, wbs_ref,
                wout_ref, gpost_ref, gpre_ref, x1_ref, h2_ref):
    tm, d = x_ref.shape
    u = u_ref[...].reshape(tm, -1).astype(F32)
    y = ys_ref[...].reshape(tm, -1).astype(F32) + d_ref[...] * u
    y = _gelu_tanh(y)
    gl = jnp.dot(y.astype(BF16), wglu_ref[...], preferred_element_type=F32) + bglu_ref[...]
    s = y * jax.nn.sigmoid(gl)
    y_ssm = jnp.dot(s.astype(BF16), wbs_ref[...], preferred_element_type=F32)
    y_attn = jnp.dot(attn_ref[...], wba_ref[...], preferred_element_type=F32)
    merged = g_ref[:, :d].astype(F32) * y_attn + g_ref[:, d:].astype(F32) * y_ssm
    o = jnp.dot(merged.astype(BF16), wout_ref[...], preferred_element_type=F32)
    x1 = x_ref[...] + _rms(o, gpost_ref[...])
    x1_ref[...] = x1
    h2_ref[...] = _rms(x1, gpre_ref[...]).astype(BF16)


def _merge(attn, ys, u, gates, x2, wba, wglu, bglu, ssm_d, wbs, wout, gpost, gpre, seq, tm):
    T, D = x2.shape
    row = lambda w: pl.BlockSpec((tm, w), lambda i: (i, 0))
    chunked = _chunk_rows_spec(tm, seq // tm, u.shape[-1])
    consts = [wba, wglu, bglu, ssm_d, wbs, wout, gpost, gpre]
    return pl.pallas_call(
        _merge_body,
        grid=(T // tm,),
        in_specs=[row(attn.shape[1]), chunked, chunked, row(gates.shape[1]), row(D)]
                 + [_const_spec(c.shape) for c in consts],
        out_specs=[row(D), row(D)],
        out_shape=[jax.ShapeDtypeStruct((T, D), F32), jax.ShapeDtypeStruct((T, D), BF16)],
        compiler_params=_params(("parallel",)),
        name="merge",
    )(attn, ys, u, gates, x2, *consts)


def _ffn_body(tiles_per_seq, hp_ref, hc_ref, hn_ref, x1_ref, wg_ref, wu_ref, cw_ref, cb_ref, wd_ref,
              gn_ref, o_ref, hext, fscr):
    tm = hc_ref.shape[0]
    i = pl.program_id(0) % tiles_per_seq
    zero = jnp.zeros((), BF16)
    hext[0:HALO] = jnp.where(i == 0, zero, hp_ref[...])
    hext[HALO:HALO + tm] = hc_ref[...]
    hext[HALO + tm:] = jnp.where(i == tiles_per_seq - 1, zero, hn_ref[...])
    ext = tm + 2 * HALO
    for c in range(wg_ref.shape[1] // FF_TILE):
        sl = slice(FF_TILE * c, FF_TILE * (c + 1))
        g = jnp.dot(hext[...], wg_ref[:, sl], preferred_element_type=F32)
        up = jnp.dot(hc_ref[...], wu_ref[:, sl], preferred_element_type=F32)
        g_prev = pltpu.roll(g, 1, 0)[HALO:HALO + tm]
        g_next = pltpu.roll(g, ext - 1, 0)[HALO:HALO + tm]
        conv = (g_prev * cw_ref[0:1, sl] + g[HALO:HALO + tm] * cw_ref[1:2, sl]
                + g_next * cw_ref[2:3, sl] + cb_ref[:, sl])
        fscr[:, sl] = (_gelu_tanh(conv) * up).astype(BF16)
    o = jnp.dot(fscr[...], wd_ref[...], preferred_element_type=F32)
    o_ref[...] = x1_ref[...] + _rms(o, gn_ref[...])


def _ffn(h2, x1, wg, wu, cw, cb, wd, gn, seq, tm):
    T, D = x1.shape
    dff = wg.shape[1]
    assert dff % FF_TILE == 0 and tm % HALO == 0 and seq % tm == 0
    per = tm // HALO
    nh = T // HALO
    row = lambda w: pl.BlockSpec((tm, w), lambda i: (i, 0))
    once = lambda shape: pl.BlockSpec(shape, lambda i: (0,) * len(shape), pipeline_mode=pl.Buffered(1))
    return pl.pallas_call(
        functools.partial(_ffn_body, seq // tm),
        grid=(T // tm,),
        in_specs=[pl.BlockSpec((HALO, D), lambda i: (jnp.maximum(i * per - 1, 0), 0)),
                  row(D),
                  pl.BlockSpec((HALO, D), lambda i: (jnp.minimum((i + 1) * per, nh - 1), 0)),
                  row(D),
                  once(wg.shape), once(wu.shape), _const_spec(cw.shape), _const_spec(cb.shape),
                  once(wd.shape), _const_spec(gn.shape)],
        out_specs=row(D),
        out_shape=jax.ShapeDtypeStruct((T, D), F32),
        scratch_shapes=[pltpu.VMEM((tm + 2 * HALO, D), BF16), pltpu.VMEM((tm, dff), BF16)],
        compiler_params=_params(("parallel",)),
        name="ffn",
    )(h2, h2, h2, x1, wg, wu, cw, cb, wd, gn)


def _row_tile(seq):
    return min(1024, seq)


def _layer(x, p):
    bsz, seq, d = x.shape
    T = bsz * seq
    tm = _row_tile(seq)
    ssm_w = d // 2
    ngroups = ssm_w // SSM_CH
    attn_w = N_HEADS * HEAD_DIM
    row2 = lambda v: v.reshape(1, -1).astype(F32)

    npair = ngroups // PAIR
    both = lambda re, im: jnp.stack([jnp.stack([p[re + "_f"], p[im + "_f"]]),
                                     jnp.stack([p[re + "_b"], p[im + "_b"]])]).astype(F32)
    lam = both("lam_re", "lam_im").reshape(2, 2, npair, 1, LANES).transpose(0, 2, 1, 3, 4)
    ldt = jnp.repeat(jnp.stack([p["log_dt_f"], p["log_dt_b"]]).astype(F32), SSM_STATE, axis=-1)
    ldt = ldt.reshape(2, npair, 1, LANES)
    bt = both("b_re", "b_im").reshape(2, 2, npair, PAIR, SSM_STATE, SSM_CH)
    bt = bt.transpose(0, 2, 1, 5, 3, 4).reshape(2, npair, 2, SSM_CH, LANES)
    cc = both("c_re", "c_im").reshape(2, 2, npair, PAIR, SSM_CH, SSM_STATE)
    cc = cc.transpose(0, 2, 1, 4, 3, 5).reshape(2, npair, 2, SSM_CH, LANES)
    wb, wct, tsum, a, w_in = _s5prep(lam, ldt, bt, cc, p["w_in"])

    perm = _head_perm()
    x2 = x.reshape(T, d)
    q, k, v, u, gates = _inproj(x2, row2(p["norm_mix_pre"]), p["w_in"][:, perm].astype(BF16), w_in, bsz, seq, tm)

    later = ("w_ffn_gate", "w_ffn_up", "w_ffn_down", "w_out", "w_branch_ssm", "w_glu")
    attn, *rounded = _attention(q, k, v, jnp.asarray(_attn_bias()), p["attn_sink"].astype(F32), bsz, seq,
                                [p[name] for name in later])
    w16 = dict(zip(later, rounded))

    assert bsz == SUBLANES // 2
    ys = _s5(u.reshape(T, ssm_w), wb, tsum, wct, a).reshape(u.shape)

    x1, h2 = _merge(attn, ys, u, gates, x2,
                    p["w_branch_attn"][perm, :].astype(BF16), w16["w_glu"], row2(p["b_glu"]),
                    row2(p["ssm_d"]), w16["w_branch_ssm"], w16["w_out"],
                    row2(p["norm_mix_post"]), row2(p["norm_ffn_pre"]), seq, tm)

    out = _ffn(h2, x1, w16["w_ffn_gate"], w16["w_ffn_up"], p["conv_w"].astype(F32),
               row2(p["conv_b"]), w16["w_ffn_down"], row2(p["norm_ffn_post"]), seq, tm)
    return out.reshape(bsz, seq, d)


_PARAM_NAMES = (
    "norm_mix_pre", "w_in", "attn_sink",
    "lam_re_f", "lam_im_f", "log_dt_f", "b_re_f", "b_im_f", "c_re_f", "c_im_f",
    "lam_re_b", "lam_im_b", "log_dt_b", "b_re_b", "b_im_b", "c_re_b", "c_im_b",
    "ssm_d", "w_glu", "b_glu", "w_branch_attn", "w_branch_ssm", "w_out", "norm_mix_post",
    "norm_ffn_pre", "w_ffn_gate", "w_ffn_up", "conv_w", "conv_b", "w_ffn_down", "norm_ffn_post")


def kernel(x, norm_mix_pre, w_in, attn_sink, lam_re_f, lam_im_f, log_dt_f, b_re_f, b_im_f, c_re_f, c_im_f, lam_re_b, lam_im_b, log_dt_b, b_re_b, b_im_b, c_re_b, c_im_b, ssm_d, w_glu, b_glu, w_branch_attn, w_branch_ssm, w_out, norm_mix_post, norm_ffn_pre, w_ffn_gate, w_ffn_up, conv_w, conv_b, w_ffn_down, norm_ffn_post):
    stacked = dict(zip(_PARAM_NAMES, (
        norm_mix_pre, w_in, attn_sink,
        lam_re_f, lam_im_f, log_dt_f, b_re_f, b_im_f, c_re_f, c_im_f,
        lam_re_b, lam_im_b, log_dt_b, b_re_b, b_im_b, c_re_b, c_im_b,
        ssm_d, w_glu, b_glu, w_branch_attn, w_branch_ssm, w_out, norm_mix_post,
        norm_ffn_pre, w_ffn_gate, w_ffn_up, conv_w, conv_b, w_ffn_down, norm_ffn_post)))
    for layer in range(w_in.shape[0]):
        x = _layer(x, {name: value[layer] for name, value in stacked.items()})
    return x
```

```python
import functools
import math

import numpy as np
import jax
import jax.numpy as jnp
from jax import lax
from jax.experimental import pallas as pl
from jax.experimental.pallas import tpu as pltpu

F32 = jnp.float32
BF16 = jnp.bfloat16

N_HEADS = 8
N_KV_HEADS = 2
HEAD_DIM = 64
GROUP = N_HEADS // N_KV_HEADS
WINDOW = 128
BLOCK = 128
SSM_CH = 16
SSM_STATE = 64
CHUNK = 16
PAIR = 2
N_BRANCH = 2
EPS = 1e-6
NEG_INF = -1e30
LOG2E = math.log2(math.e)

LANES = 128
SUBLANES = 8
HALO = 16
FF_TILE = 256
VMEM_LIMIT = 56 * 1024 * 1024


def _gelu_tanh(x):
    return 0.5 * x * (1.0 + jnp.tanh(math.sqrt(2.0 / math.pi) * (x + 0.044715 * (x * x * x))))


def _rms(x, gain):
    return x * lax.rsqrt(jnp.mean(x * x, axis=-1, keepdims=True) + EPS) * gain


def _params(semantics):
    return pltpu.CompilerParams(dimension_semantics=semantics, vmem_limit_bytes=VMEM_LIMIT)


def _const_spec(shape):
    nd = len(shape)
    return pl.BlockSpec(shape, lambda *_: (0,) * nd)


def _inproj_body(splits, x_ref, g_ref, w_ref, q_ref, k_ref, ks_ref, v_ref, u_ref, gate_ref):
    h = _rms(x_ref[...], g_ref[...]).astype(BF16)

    def proj(i):
        return jnp.dot(h, w_ref[:, splits[i]:splits[i + 1]], preferred_element_type=F32)

    q_ref[...] = (proj(0) * (HEAD_DIM ** -0.5 * LOG2E)).astype(BF16)
    k = proj(1).astype(BF16)
    k_ref[...] = k
    ks_ref[...] = pltpu.roll(k, HEAD_DIM, 1)
    v_ref[...] = proj(2).astype(BF16)
    u_ref[...] = proj(3).astype(BF16).reshape(u_ref.shape)
    gate_ref[...] = jax.nn.sigmoid(proj(4)).astype(BF16)


def _chunk_rows_spec(tm, tiles_per_seq, width):
    return pl.BlockSpec((tm // CHUNK, None, CHUNK, width),
                        lambda i: (i % tiles_per_seq, i // tiles_per_seq, 0, 0))


def _inproj(x2, gain, w_in, bsz, seq, tm):
    T, D = x2.shape
    attn_w = N_HEADS * HEAD_DIM
    kv_w = N_KV_HEADS * HEAD_DIM
    widths = [attn_w, kv_w, kv_w, D // 2, N_BRANCH * D]
    splits = [0] + np.cumsum(widths).tolist()
    row = lambda w: pl.BlockSpec((tm, w), lambda i: (i, 0))
    out_widths = [attn_w, kv_w, kv_w, kv_w, D // 2, N_BRANCH * D]
    out_specs = [row(w) for w in out_widths]
    out_shape = [jax.ShapeDtypeStruct((T, w), BF16) for w in out_widths]
    out_specs[4] = _chunk_rows_spec(tm, seq // tm, out_widths[4])
    out_shape[4] = jax.ShapeDtypeStruct((seq // CHUNK, bsz, CHUNK, out_widths[4]), BF16)
    return pl.pallas_call(
        functools.partial(_inproj_body, tuple(splits)),
        grid=(T // tm,),
        in_specs=[row(D), _const_spec((1, D)), _const_spec(w_in.shape)],
        out_specs=out_specs,
        out_shape=out_shape,
        compiler_params=_params(("parallel",)),
        name="inproj",
    )(x2, gain, w_in)


def _cast_specs(weights, steps, step_of):
    specs = []
    for w in weights:
        nblocks = max(n for n in range(1, steps + 1)
                      if steps % n == 0 and w.shape[0] % (n * HALO) == 0)
        hold = steps // nblocks
        specs.append(pl.BlockSpec((w.shape[0] // nblocks, w.shape[1]),
                                  lambda *ids, hold=hold: (step_of(*ids) // hold, 0)))
    return specs, specs, [jax.ShapeDtypeStruct(w.shape, BF16) for w in weights]


def _attn_body(ncast, q_ref, kp_ref, kc_ref, kn_ref, sp_ref, sc_ref, sn_ref, vp_ref, vc_ref, vn_ref,
               bias0_ref, bias1_ref, sink_ref, *refs):
    o_ref = refs[ncast]
    for src, dst in zip(refs[:ncast], refs[ncast + 1:]):
        dst[...] = src[...].astype(BF16)
    lane = lax.broadcasted_iota(jnp.int32, (1, LANES), 1)
    half_mask = [lane < HEAD_DIM, lane >= HEAD_DIM]
    zero = jnp.zeros((), BF16)
    kall = [jnp.concatenate([kp_ref[...], kc_ref[...], kn_ref[...]], axis=0),
            jnp.concatenate([sp_ref[...], sc_ref[...], sn_ref[...]], axis=0)]
    vall = jnp.concatenate([vp_ref[...], vc_ref[...], vn_ref[...]], axis=0)
    for b, bias_ref in enumerate((bias0_ref, bias1_ref)):
        keys = slice(BLOCK * b, BLOCK * (b + 3))
        vcat = vall[keys]
        rows = slice(BLOCK * b, BLOCK * (b + 1))
        for j in range(N_HEADS // 2):
            qj = q_ref[rows, LANES * j:LANES * (j + 1)]
            g = (2 * j) // GROUP
            halves = []
            for e in range(2):
                h = 2 * j + e
                qm = jnp.where(half_mask[e], qj, zero)
                kcat = kall[0 if e == g else 1][keys]
                st = lax.dot_general(kcat, qm, (((1,), (1,)), ((), ())), preferred_element_type=F32)
                st = st + bias_ref[h]
                sink = sink_ref[h] * LOG2E
                m = jnp.maximum(jnp.max(st, axis=0, keepdims=True), sink)
                p = jnp.exp2(st - m)
                denom = jnp.sum(p, axis=0, keepdims=True) + jnp.exp2(sink - m)
                ot = lax.dot_general(vcat, p.astype(BF16), (((0,), (0,)), ((), ())),
                                     preferred_element_type=F32)
                halves.append(ot[HEAD_DIM * g:HEAD_DIM * (g + 1)] / denom)
            o_ref[rows, LANES * j:LANES * (j + 1)] = jnp.concatenate(halves, axis=0).T.astype(BF16)


def _attn_bias():
    t = np.arange(BLOCK)[None, :]
    s = np.arange(3 * BLOCK)[:, None] - BLOCK
    dist = np.abs(t - s).astype(np.float64)
    valid = dist <= WINDOW
    slopes = 2.0 ** (-8.0 * (np.arange(N_HEADS) + 1.0) / N_HEADS)
    out = np.zeros((3, N_HEADS, 3 * BLOCK, BLOCK), np.float32)
    for variant in range(3):
        ok = valid.copy()
        if variant == 0:
            ok[:BLOCK] = False
        if variant == 2:
            ok[2 * BLOCK:] = False
        for h in range(N_HEADS):
            out[variant, h] = np.where(ok, -slopes[h] * dist * LOG2E, NEG_INF)
    return out


def _attention(q, k, ks, v, bias, sink, bsz, seq, cast_views):
    nb = seq // BLOCK
    assert nb % 2 == 0
    ns = nb // 2
    kvw = k.shape[1]
    aw = q.shape[1]

    def edge_spec(off):
        return pl.BlockSpec((BLOCK, kvw), lambda b, n: (b * nb + jnp.clip(2 * n + off, 0, nb - 1), 0))

    centre = pl.BlockSpec((2 * BLOCK, kvw), lambda b, n: (b * ns + n, 0))
    bias_spec = lambda pick: pl.BlockSpec((None,) + bias.shape[1:], lambda b, n: (pick(n), 0, 0, 0))
    first = bias_spec(lambda n: jnp.where(n == 0, 0, 1))
    last = bias_spec(lambda n: jnp.where(n == ns - 1, 2, 1))
    qspec = pl.BlockSpec((2 * BLOCK, aw), lambda b, n: (b * ns + n, 0))
    cast_in, cast_out, cast_shape = _cast_specs(cast_views, bsz * ns, lambda b, n: b * ns + n)
    return pl.pallas_call(
        functools.partial(_attn_body, len(cast_views)),
        grid=(bsz, ns),
        in_specs=[qspec] + [edge_spec(-1), centre, edge_spec(2)] * 3 + [
                  first, last,
                  pl.BlockSpec(memory_space=pltpu.SMEM)] + cast_in,
        out_specs=[qspec] + cast_out,
        out_shape=[jax.ShapeDtypeStruct(q.shape, BF16)] + cast_shape,
        compiler_params=_params(("parallel", "arbitrary")),
        name="attn",
    )(q, k, k, k, ks, ks, ks, v, v, v, bias, bias, sink, *cast_views)


def _s5prep_body(lam_ref, ldt_ref, bt_ref, c_ref, w32_ref, wb_ref, wct_ref, t_ref, a_ref, w16_ref):
    w16_ref[...] = w32_ref[...].astype(BF16)
    width = CHUNK * SSM_CH
    lane = lax.broadcasted_iota(jnp.int32, (1, LANES), 1)
    in_group = [lane < SSM_STATE, lane >= SSM_STATE]
    lane_t = lax.broadcasted_iota(jnp.int32, (SSM_CH, width), 1)
    contract = (((1,), (1,)), ((), ()))
    hi = lax.Precision.HIGHEST
    toeplitz = [[None] * CHUNK for _ in range(PAIR)]
    for d in range(2):
        lre, lim = lam_ref[d, 0], lam_ref[d, 1]
        dt = jnp.exp(ldt_ref[d])
        mag = jnp.exp(lre * dt)
        are, aim = mag * jnp.cos(lim * dt), mag * jnp.sin(lim * dt)
        den = lre * lre + lim * lim
        cfr = ((are - 1.0) * lre + aim * lim) / den
        cfi = (aim * lre - (are - 1.0) * lim) / den
        btr, bti = bt_ref[d, 0], bt_ref[d, 1]
        bbr = btr * cfr - bti * cfi
        bbi = btr * cfi + bti * cfr
        cre, cim = c_ref[d, 0], c_ref[d, 1]

        pwr, pwi = jnp.ones_like(are), jnp.zeros_like(are)
        e_r, e_i = [], []
        for p in range(CHUNK + 1):
            er = cre * pwr - cim * pwi
            ei = cre * pwi + cim * pwr
            if p < CHUNK:
                e_r.append(er)
                e_i.append(ei)
                k = CHUNK - 1 - p if d == 0 else p
                for ri, val in enumerate((bbr * pwr - bbi * pwi, bbr * pwi + bbi * pwr)):
                    for g in range(PAIR):
                        wb_ref[width * g + SSM_CH * k:width * g + SSM_CH * (k + 1),
                               LANES * (2 * d + ri):LANES * (2 * d + ri + 1)] = jnp.where(
                                   in_group[g], val, 0.0).astype(BF16)
            if p >= 1:
                r = p - 1 if d == 0 else CHUNK - p
                for ri, val in enumerate((er, -ei)):
                    for g in range(PAIR):
                        wct_ref[width * g + SSM_CH * r:width * g + SSM_CH * (r + 1),
                                LANES * (2 * d + ri):LANES * (2 * d + ri + 1)] = jnp.where(
                                    in_group[g], val, 0.0).astype(BF16)
            if p == CHUNK:
                a_ref[2 * d] = jnp.broadcast_to(pwr, a_ref.shape[1:])
                a_ref[2 * d + 1] = jnp.broadcast_to(pwi, a_ref.shape[1:])
            pwr, pwi = pwr * are - pwi * aim, pwr * aim + pwi * are

        if d == 1:
            e_r, e_i = e_r[::-1], e_i[::-1]
        ecat_r, ecat_i = jnp.concatenate(e_r, axis=0), jnp.concatenate(e_i, axis=0)
        for g in range(PAIR):
            gr, gi = jnp.where(in_group[g], bbr, 0.0), jnp.where(in_group[g], bbi, 0.0)
            krow = (lax.dot_general(gr, ecat_r, contract, precision=hi, preferred_element_type=F32)
                    - lax.dot_general(gi, ecat_i, contract, precision=hi, preferred_element_type=F32))
            for k in range(CHUNK):
                if d == 0:
                    shift, keep = SSM_CH * k, lane_t >= SSM_CH * k
                else:
                    shift, keep = (SSM_CH * (k + 1)) % width, lane_t < SSM_CH * (k + 1)
                blk = jnp.where(keep, pltpu.roll(krow, shift, 1) if shift else krow, 0.0)
                toeplitz[g][k] = blk if d == 0 else toeplitz[g][k] + blk
    for g in range(PAIR):
        t_ref[g] = jnp.concatenate(toeplitz[g], axis=0).astype(BF16)


def _s5prep(lam, ldt, bt, c, w32):
    npair = lam.shape[1]
    width = CHUNK * SSM_CH
    pw = PAIR * width
    cast_in, cast_out, cast_shape = _cast_specs([w32], npair, lambda i: i)
    return pl.pallas_call(
        _s5prep_body,
        grid=(npair,),
        in_specs=[pl.BlockSpec((2, None, 2, 1, LANES), lambda i: (0, i, 0, 0, 0)),
                  pl.BlockSpec((2, None, 1, LANES), lambda i: (0, i, 0, 0)),
                  pl.BlockSpec((2, None, 2, SSM_CH, LANES), lambda i: (0, i, 0, 0, 0)),
                  pl.BlockSpec((2, None, 2, SSM_CH, LANES), lambda i: (0, i, 0, 0, 0))] + cast_in,
        out_specs=[pl.BlockSpec((None, pw, 4 * LANES), lambda i: (i, 0, 0)),
                   pl.BlockSpec((None, pw, 4 * LANES), lambda i: (i, 0, 0)),
                   pl.BlockSpec((PAIR, width, width), lambda i: (i, 0, 0)),
                   pl.BlockSpec((None, 4, SUBLANES, LANES), lambda i: (i, 0, 0, 0))] + cast_out,
        out_shape=[jax.ShapeDtypeStruct((npair, pw, 4 * LANES), BF16),
                   jax.ShapeDtypeStruct((npair, pw, 4 * LANES), BF16),
                   jax.ShapeDtypeStruct((npair * PAIR, width, width), BF16),
                   jax.ShapeDtypeStruct((npair, 4, SUBLANES, LANES), F32)] + cast_shape,
        compiler_params=_params(("parallel",)),
        name="s5prep",
    )(lam, ldt, bt, c, w32)


def _block_transpose(xs):
    n = len(xs)
    assert n * SSM_CH == LANES and n & (n - 1) == 0
    blk = lax.broadcasted_iota(jnp.int32, xs[0].shape, 1) // SSM_CH
    d = n // 2
    while d:
        upper = (blk & d) != 0
        ys = list(xs)
        for a in range(n):
            if a & d:
                continue
            b = a + d
            ys[a] = jnp.where(upper, pltpu.roll(xs[b], SSM_CH * d, 1), xs[a])
            ys[b] = jnp.where(upper, xs[b], pltpu.roll(xs[a], LANES - SSM_CH * d, 1))
        xs = ys
        d //= 2
    return xs


def _s5_scan(a_ref, w_scr, s_scr):
    rows = w_scr.shape[0]
    nsteps = rows // SUBLANES
    nstream = w_scr.shape[1] // (2 * LANES)
    half = SUBLANES // 2
    lo = lax.broadcasted_iota(jnp.int32, (SUBLANES, LANES), 0) < half
    first = [lo, jnp.logical_not(lo)]
    coef = [(a_ref[s // 2, 2 * (s % 2)], a_ref[s // 2, 2 * (s % 2) + 1]) for s in range(nstream)]
    col = lambda i: slice(LANES * i, LANES * (i + 1))

    def advance(a, xr, xi, wr, wi):
        return a[0] * xr - a[1] * xi + wr, a[0] * xi + a[1] * xr + wi

    def swap(x):
        return pltpu.roll(x, half, 0)

    def step(i, carry):
        out = []
        for s in range(nstream):
            d = s % 2
            cr, ci = carry[2 * s], carry[2 * s + 1]
            vreg = i if d == 0 else nsteps - 1 - i
            rs = pl.ds(pl.multiple_of(vreg * SUBLANES, SUBLANES), SUBLANES)
            wr, wi = w_scr[rs, col(2 * s)], w_scr[rs, col(2 * s + 1)]
            r0r, r0i = swap(cr), swap(ci)
            t1r, t1i = advance(coef[s], r0r, r0i, wr, wi)
            r1r, r1i = swap(t1r), swap(t1i)
            t2r, t2i = advance(coef[s], r1r, r1i, wr, wi)
            s_scr[rs, col(2 * s)] = jnp.where(first[d], r0r, r1r)
            s_scr[rs, col(2 * s + 1)] = jnp.where(first[d], r0i, r1i)
            out += [jnp.where(first[d], t1r, t2r), jnp.where(first[d], t1i, t2i)]
        return tuple(out)

    z = jnp.zeros((SUBLANES, LANES), F32)
    lax.fori_loop(0, nsteps, step, (z,) * (2 * nstream))


S5_SLAB_GROUPS = LANES // SSM_CH
S5_ROW_CHUNK = 256
S5_SUB_ROWS = 128
S5_SCAN_PAIRS = 2


def _s5_body(u_ref, wb_ref, t_ref, wct_ref, a_ref, y_ref, ur_scr, w_scr, s_scr, sall_scr, yr_scr, tok_scr):
    rows = ur_scr.shape[0]
    width = CHUNK * SSM_CH
    pw = PAIR * width
    npair = S5_SLAB_GROUPS // PAIR
    half_steps = CHUNK // 2
    tok_rows = S5_ROW_CHUNK * CHUNK

    nchunks = rows // S5_ROW_CHUNK
    nsub = S5_ROW_CHUNK // S5_SUB_ROWS
    sw = 4 * LANES
    row_chunk = lambda c: slice(S5_ROW_CHUNK * c, S5_ROW_CHUNK * (c + 1))
    tok_chunk = lambda c: slice(tok_rows * c, tok_rows * (c + 1))
    step_rows = lambda i, h, kk: pl.ds(S5_SUB_ROWS * i * CHUNK + half_steps * h + kk, S5_SUB_ROWS,
                                       stride=CHUNK)
    lane_blk = lambda g, h: slice(width * g + LANES * h, width * g + LANES * (h + 1))

    def relayout_in(c):
        tok = tok_scr.at[c % 2]
        tok[...] = u_ref[tok_chunk(c), :].astype(F32)
        for i in range(nsub):
            for h in range(2):
                xs = [tok[step_rows(i, h, kk), :] for kk in range(half_steps)]
                ys = _block_transpose(xs)
                for g in range(S5_SLAB_GROUPS):
                    r0 = S5_ROW_CHUNK * c + S5_SUB_ROWS * i
                    ur_scr[r0:r0 + S5_SUB_ROWS, lane_blk(g, h)] = ys[g].astype(BF16)

    def state_in(c, p0):
        for q in range(S5_SCAN_PAIRS):
            pp = p0 + q
            w_scr[row_chunk(c), sw * q:sw * (q + 1)] = jnp.dot(
                ur_scr[row_chunk(c), pw * pp:pw * (pp + 1)], wb_ref[pp], preferred_element_type=F32)

    for p0 in range(0, npair, S5_SCAN_PAIRS):
        for c in range(nchunks):
            if p0 == 0:
                relayout_in(c)
            state_in(c, p0)
        _s5_scan(a_ref.at[p0:p0 + S5_SCAN_PAIRS], w_scr, s_scr)
        sall_scr[:, sw * p0:sw * (p0 + S5_SCAN_PAIRS)] = s_scr[...].astype(BF16)

    def outputs(c):
        yr, tok = yr_scr.at[c % 2], tok_scr.at[c % 2]
        for pp in range(npair):
            y = lax.dot_general(sall_scr[row_chunk(c), sw * pp:sw * (pp + 1)], wct_ref[pp],
                                (((1,), (1,)), ((), ())), preferred_element_type=F32)
            for g in range(PAIR):
                lo = pw * pp + width * g
                yr[:, lo:lo + width] = y[:, width * g:width * (g + 1)] + jnp.dot(
                    ur_scr[row_chunk(c), lo:lo + width], t_ref[PAIR * pp + g], preferred_element_type=F32)
        for i in range(nsub):
            for h in range(2):
                xs = [yr[S5_SUB_ROWS * i:S5_SUB_ROWS * (i + 1), lane_blk(g, h)].astype(BF16)
                      for g in range(S5_SLAB_GROUPS)]
                ys = _block_transpose(xs)
                for kk in range(half_steps):
                    tok[step_rows(i, h, kk), :] = ys[kk].astype(F32)
        y_ref[tok_chunk(c), :] = tok[...].astype(BF16)

    for c in range(nchunks):
        outputs(c)


def _s5(u_tok, wb, tsum, wct, a16):
    T, ssm_w = u_tok.shape
    rows = T // CHUNK
    assert rows % S5_ROW_CHUNK == 0 and S5_SLAB_GROUPS == CHUNK // 2
    width = CHUNK * SSM_CH
    pw = PAIR * width
    npair = S5_SLAB_GROUPS // PAIR
    slab = pl.BlockSpec((T, LANES), lambda s: (0, s))
    return pl.pallas_call(
        _s5_body,
        grid=(ssm_w // LANES,),
        in_specs=[slab,
                  pl.BlockSpec((npair, pw, 4 * LANES), lambda s: (s, 0, 0)),
                  pl.BlockSpec((S5_SLAB_GROUPS, width, width), lambda s: (s, 0, 0)),
                  pl.BlockSpec((npair, pw, 4 * LANES), lambda s: (s, 0, 0)),
                  pl.BlockSpec((npair, 4, SUBLANES, LANES), lambda s: (s, 0, 0, 0))],
        out_specs=slab,
        out_shape=jax.ShapeDtypeStruct((T, ssm_w), BF16),
        scratch_shapes=[pltpu.VMEM((rows, S5_SLAB_GROUPS * width), BF16),
                        pltpu.VMEM((rows, S5_SCAN_PAIRS * 4 * LANES), F32),
                        pltpu.VMEM((rows, S5_SCAN_PAIRS * 4 * LANES), F32),
                        pltpu.VMEM((rows, npair * 4 * LANES), BF16),
                        pltpu.VMEM((2, S5_ROW_CHUNK, S5_SLAB_GROUPS * width), F32),
                        pltpu.VMEM((2, S5_ROW_CHUNK * CHUNK, LANES), F32)],
        compiler_params=_params(("parallel",)),
        name="s5",
    )(u_tok, wb, tsum, wct, a16)


def _merge_body(attn_ref, ys_ref, u_ref, g_ref, x_ref, wba_ref, wglu_ref, bglu_ref, d_ref, wbs_ref,
                wout_ref, gpost_ref, gpre_ref, x1_ref, h2_ref):
    tm, d = x_ref.shape
    u = u_ref[...].reshape(tm, -1).astype(F32)
    y = ys_ref[...].reshape(tm, -1).astype(F32) + d_ref[...] * u
    y = _gelu_tanh(y)
    gl = jnp.dot(y.astype(BF16), wglu_ref[...], preferred_element_type=F32) + bglu_ref[...]
    s = y * jax.nn.sigmoid(gl)
    y_ssm = jnp.dot(s.astype(BF16), wbs_ref[...], preferred_element_type=F32)
    y_attn = jnp.dot(attn_ref[...], wba_ref[...], preferred_element_type=F32)
    merged = g_ref[:, :d].astype(F32) * y_attn + g_ref[:, d:].astype(F32) * y_ssm
    o = jnp.dot(merged.astype(BF16), wout_ref[...], preferred_element_type=F32)
    x1 = x_ref[...] + _rms(o, gpost_ref[...])
    x1_ref[...] = x1
    h2_ref[...] = _rms(x1, gpre_ref[...]).astype(BF16)


def _merge(attn, ys, u, gates, x2, wba, wglu, bglu, ssm_d, wbs, wout, gpost, gpre, seq, tm):
    T, D = x2.shape
    row = lambda w: pl.BlockSpec((tm, w), lambda i: (i, 0))
    chunked = _chunk_rows_spec(tm, seq // tm, u.shape[-1])
    consts = [wba, wglu, bglu, ssm_d, wbs, wout, gpost, gpre]
    return pl.pallas_call(
        _merge_body,
        grid=(T // tm,),
        in_specs=[row(attn.shape[1]), chunked, chunked, row(gates.shape[1]), row(D)]
                 + [_const_spec(c.shape) for c in consts],
        out_specs=[row(D), row(D)],
        out_shape=[jax.ShapeDtypeStruct((T, D), F32), jax.ShapeDtypeStruct((T, D), BF16)],
        compiler_params=_params(("parallel",)),
        name="merge",
    )(attn, ys, u, gates, x2, *consts)


def _ffn_body(tiles_per_seq, hp_ref, hc_ref, hn_ref, x1_ref, wg_ref, wu_ref, cw_ref, cb_ref, wd_ref,
              gn_ref, o_ref, hext, fscr):
    tm = hc_ref.shape[0]
    i = pl.program_id(0) % tiles_per_seq
    zero = jnp.zeros((), BF16)
    hext[0:HALO] = jnp.where(i == 0, zero, hp_ref[...])
    hext[HALO:HALO + tm] = hc_ref[...]
    hext[HALO + tm:] = jnp.where(i == tiles_per_seq - 1, zero, hn_ref[...])
    ext = tm + 2 * HALO
    for c in range(wg_ref.shape[1] // FF_TILE):
        sl = slice(FF_TILE * c, FF_TILE * (c + 1))
        g = jnp.dot(hext[...], wg_ref[:, sl], preferred_element_type=F32)
        up = jnp.dot(hc_ref[...], wu_ref[:, sl], preferred_element_type=F32)
        g_prev = pltpu.roll(g, 1, 0)[HALO:HALO + tm]
        g_next = pltpu.roll(g, ext - 1, 0)[HALO:HALO + tm]
        conv = (g_prev * cw_ref[0:1, sl] + g[HALO:HALO + tm] * cw_ref[1:2, sl]
                + g_next * cw_ref[2:3, sl] + cb_ref[:, sl])
        fscr[:, sl] = (_gelu_tanh(conv) * up).astype(BF16)
    o = jnp.dot(fscr[...], wd_ref[...], preferred_element_type=F32)
    o_ref[...] = x1_ref[...] + _rms(o, gn_ref[...])


def _ffn(h2, x1, wg, wu, cw, cb, wd, gn, seq, tm):
    T, D = x1.shape
    dff = wg.shape[1]
    assert dff % FF_TILE == 0 and tm % HALO == 0 and seq % tm == 0
    per = tm // HALO
    nh = T // HALO
    row = lambda w: pl.BlockSpec((tm, w), lambda i: (i, 0))
    once = lambda shape: pl.BlockSpec(shape, lambda i: (0,) * len(shape), pipeline_mode=pl.Buffered(1))
    return pl.pallas_call(
        functools.partial(_ffn_body, seq // tm),
        grid=(T // tm,),
        in_specs=[pl.BlockSpec((HALO, D), lambda i: (jnp.maximum(i * per - 1, 0), 0)),
                  row(D),
                  pl.BlockSpec((HALO, D), lambda i: (jnp.minimum((i + 1) * per, nh - 1), 0)),
                  row(D),
                  once(wg.shape), once(wu.shape), _const_spec(cw.shape), _const_spec(cb.shape),
                  once(wd.shape), _const_spec(gn.shape)],
        out_specs=row(D),
        out_shape=jax.ShapeDtypeStruct((T, D), F32),
        scratch_shapes=[pltpu.VMEM((tm + 2 * HALO, D), BF16), pltpu.VMEM((tm, dff), BF16)],
        compiler_params=_params(("parallel",)),
        name="ffn",
    )(h2, h2, h2, x1, wg, wu, cw, cb, wd, gn)


def _row_tile(seq):
    return min(1024, seq)


def _layer(x, p):
    bsz, seq, d = x.shape
    T = bsz * seq
    tm = _row_tile(seq)
    ssm_w = d // 2
    ngroups = ssm_w // SSM_CH
    attn_w = N_HEADS * HEAD_DIM
    row2 = lambda v: v.reshape(1, -1).astype(F32)

    npair = ngroups // PAIR
    both = lambda re, im: jnp.stack([jnp.stack([p[re + "_f"], p[im + "_f"]]),
                                     jnp.stack([p[re + "_b"], p[im + "_b"]])]).astype(F32)
    lam = both("lam_re", "lam_im").reshape(2, 2, npair, 1, LANES).transpose(0, 2, 1, 3, 4)
    ldt = jnp.repeat(jnp.stack([p["log_dt_f"], p["log_dt_b"]]).astype(F32), SSM_STATE, axis=-1)
    ldt = ldt.reshape(2, npair, 1, LANES)
    bt = both("b_re", "b_im").reshape(2, 2, npair, PAIR, SSM_STATE, SSM_CH)
    bt = bt.transpose(0, 2, 1, 5, 3, 4).reshape(2, npair, 2, SSM_CH, LANES)
    cc = both("c_re", "c_im").reshape(2, 2, npair, PAIR, SSM_CH, SSM_STATE)
    cc = cc.transpose(0, 2, 1, 4, 3, 5).reshape(2, npair, 2, SSM_CH, LANES)
    wb, wct, tsum, a, w_in = _s5prep(lam, ldt, bt, cc, p["w_in"])

    x2 = x.reshape(T, d)
    q, k, ks, v, u, gates = _inproj(x2, row2(p["norm_mix_pre"]), w_in, bsz, seq, tm)

    later = ("w_ffn_gate", "w_ffn_up", "w_ffn_down", "w_out", "w_branch_attn", "w_branch_ssm", "w_glu")
    attn, *rounded = _attention(q, k, ks, v, jnp.asarray(_attn_bias()), p["attn_sink"].astype(F32), bsz, seq,
                                [p[name] for name in later])
    w16 = dict(zip(later, rounded))

    assert bsz == SUBLANES // 2
    ys = _s5(u.reshape(T, ssm_w), wb, tsum, wct, a).reshape(u.shape)

    x1, h2 = _merge(attn, ys, u, gates, x2,
                    w16["w_branch_attn"], w16["w_glu"], row2(p["b_glu"]),
                    row2(p["ssm_d"]), w16["w_branch_ssm"], w16["w_out"],
                    row2(p["norm_mix_post"]), row2(p["norm_ffn_pre"]), seq, tm)

    out = _ffn(h2, x1, w16["w_ffn_gate"], w16["w_ffn_up"], p["conv_w"].astype(F32),
               row2(p["conv_b"]), w16["w_ffn_down"], row2(p["norm_ffn_post"]), seq, tm)
    return out.reshape(bsz, seq, d)


_PARAM_NAMES = (
    "norm_mix_pre", "w_in", "attn_sink",
    "lam_re_f", "lam_im_f", "log_dt_f", "b_re_f", "b_im_f", "c_re_f", "c_im_f",
    "lam_re_b", "lam_im_b", "log_dt_b", "b_re_b", "b_im_b", "c_re_b", "c_im_b",
    "ssm_d", "w_glu", "b_glu", "w_branch_attn", "w_branch_ssm", "w_out", "norm_mix_post",
    "norm_ffn_pre", "w_ffn_gate", "w_ffn_up", "conv_w", "conv_b", "w_ffn_down", "norm_ffn_post")


def kernel(x, norm_mix_pre, w_in, attn_sink, lam_re_f, lam_im_f, log_dt_f, b_re_f, b_im_f, c_re_f, c_im_f, lam_re_b, lam_im_b, log_dt_b, b_re_b, b_im_b, c_re_b, c_im_b, ssm_d, w_glu, b_glu, w_branch_attn, w_branch_ssm, w_out, norm_mix_post, norm_ffn_pre, w_ffn_gate, w_ffn_up, conv_w, conv_b, w_ffn_down, norm_ffn_post):
    stacked = dict(zip(_PARAM_NAMES, (
        norm_mix_pre, w_in, attn_sink,
        lam_re_f, lam_im_f, log_dt_f, b_re_f, b_im_f, c_re_f, c_im_f,
        lam_re_b, lam_im_b, log_dt_b, b_re_b, b_im_b, c_re_b, c_im_b,
        ssm_d, w_glu, b_glu, w_branch_attn, w_branch_ssm, w_out, norm_mix_post,
        norm_ffn_pre, w_ffn_gate, w_ffn_up, conv_w, conv_b, w_ffn_down, norm_ffn_post)))
    for layer in range(w_in.shape[0]):
        x = _layer(x, {name: value[layer] for name, value in stacked.items()})
    return x
```

```python
import functools
import math

import numpy as np
import jax
import jax.numpy as jnp
from jax import lax
from jax.experimental import pallas as pl
from jax.experimental.pallas import tpu as pltpu

F32 = jnp.float32
BF16 = jnp.bfloat16

N_HEADS = 8
N_KV_HEADS = 2
HEAD_DIM = 64
GROUP = N_HEADS // N_KV_HEADS
WINDOW = 128
BLOCK = 128
ATTN_SUB_BLOCKS = 4
SSM_CH = 16
SSM_STATE = 64
CHUNK = 16
PAIR = 2
N_BRANCH = 2
EPS = 1e-6
NEG_INF = -1e30
LOG2E = math.log2(math.e)

LANES = 128
SUBLANES = 8
HALO = 16
FF_TILE = 256
VMEM_LIMIT = 56 * 1024 * 1024


def _gelu_tanh(x):
    return 0.5 * x * (1.0 + jnp.tanh(math.sqrt(2.0 / math.pi) * (x + 0.044715 * (x * x * x))))


def _rms(x, gain):
    return x * lax.rsqrt(jnp.mean(x * x, axis=-1, keepdims=True) + EPS) * gain


def _params(semantics):
    return pltpu.CompilerParams(dimension_semantics=semantics, vmem_limit_bytes=VMEM_LIMIT)


def _const_spec(shape):
    nd = len(shape)
    return pl.BlockSpec(shape, lambda *_: (0,) * nd)


def _inproj_body(splits, x_ref, g_ref, w_ref, q_ref, k_ref, ks_ref, v_ref, u_ref, gate_ref):
    h = _rms(x_ref[...], g_ref[...]).astype(BF16)

    def proj(i):
        return jnp.dot(h, w_ref[:, splits[i]:splits[i + 1]], preferred_element_type=F32)

    q_ref[...] = (proj(0) * (HEAD_DIM ** -0.5 * LOG2E)).astype(BF16)
    k = proj(1).astype(BF16)
    k_ref[...] = k
    ks_ref[...] = pltpu.roll(k, HEAD_DIM, 1)
    v_ref[...] = proj(2).astype(BF16)
    u_ref[...] = proj(3).astype(BF16).reshape(u_ref.shape)
    gate_ref[...] = jax.nn.sigmoid(proj(4)).astype(BF16)


def _chunk_rows_spec(tm, tiles_per_seq, width):
    return pl.BlockSpec((tm // CHUNK, None, CHUNK, width),
                        lambda i: (i % tiles_per_seq, i // tiles_per_seq, 0, 0))


def _inproj(x2, gain, w_in, bsz, seq, tm):
    T, D = x2.shape
    attn_w = N_HEADS * HEAD_DIM
    kv_w = N_KV_HEADS * HEAD_DIM
    widths = [attn_w, kv_w, kv_w, D // 2, N_BRANCH * D]
    splits = [0] + np.cumsum(widths).tolist()
    row = lambda w: pl.BlockSpec((tm, w), lambda i: (i, 0))
    out_widths = [attn_w, kv_w, kv_w, kv_w, D // 2, N_BRANCH * D]
    out_specs = [row(w) for w in out_widths]
    out_shape = [jax.ShapeDtypeStruct((T, w), BF16) for w in out_widths]
    out_specs[4] = _chunk_rows_spec(tm, seq // tm, out_widths[4])
    out_shape[4] = jax.ShapeDtypeStruct((seq // CHUNK, bsz, CHUNK, out_widths[4]), BF16)
    return pl.pallas_call(
        functools.partial(_inproj_body, tuple(splits)),
        grid=(T // tm,),
        in_specs=[row(D), _const_spec((1, D)), _const_spec(w_in.shape)],
        out_specs=out_specs,
        out_shape=out_shape,
        compiler_params=_params(("parallel",)),
        name="inproj",
    )(x2, gain, w_in)


def _cast_specs(weights, steps, step_of):
    specs = []
    for w in weights:
        nblocks = max(n for n in range(1, steps + 1)
                      if steps % n == 0 and w.shape[0] % (n * HALO) == 0)
        hold = steps // nblocks
        specs.append(pl.BlockSpec((w.shape[0] // nblocks, w.shape[1]),
                                  lambda *ids, hold=hold: (step_of(*ids) // hold, 0)))
    return specs, specs, [jax.ShapeDtypeStruct(w.shape, BF16) for w in weights]


def _attn_body(ncast, q_ref, kp_ref, kc_ref, kn_ref, sp_ref, sc_ref, sn_ref, vp_ref, vc_ref, vn_ref,
               bias_first_ref, bias_mid_ref, bias_last_ref, sink_ref, *refs):
    o_ref = refs[ncast]
    for src, dst in zip(refs[:ncast], refs[ncast + 1:]):
        dst[...] = src[...].astype(BF16)
    bias_refs = [bias_first_ref] + [bias_mid_ref] * (ATTN_SUB_BLOCKS - 2) + [bias_last_ref]
    lane = lax.broadcasted_iota(jnp.int32, (1, LANES), 1)
    half_mask = [lane < HEAD_DIM, lane >= HEAD_DIM]
    zero = jnp.zeros((), BF16)
    kall = [jnp.concatenate([kp_ref[...], kc_ref[...], kn_ref[...]], axis=0),
            jnp.concatenate([sp_ref[...], sc_ref[...], sn_ref[...]], axis=0)]
    vall = jnp.concatenate([vp_ref[...], vc_ref[...], vn_ref[...]], axis=0)
    for b, bias_ref in enumerate(bias_refs):
        keys = slice(BLOCK * b, BLOCK * (b + 3))
        vcat = vall[keys]
        rows = slice(BLOCK * b, BLOCK * (b + 1))
        for j in range(N_HEADS // 2):
            qj = q_ref[rows, LANES * j:LANES * (j + 1)]
            g = (2 * j) // GROUP
            halves = []
            for e in range(2):
                h = 2 * j + e
                qm = jnp.where(half_mask[e], qj, zero)
                kcat = kall[0 if e == g else 1][keys]
                st = lax.dot_general(kcat, qm, (((1,), (1,)), ((), ())), preferred_element_type=F32)
                st = st + bias_ref[h]
                sink = sink_ref[h] * LOG2E
                m = jnp.maximum(jnp.max(st, axis=0, keepdims=True), sink)
                p = jnp.exp2(st - m)
                denom = jnp.sum(p, axis=0, keepdims=True) + jnp.exp2(sink - m)
                ot = lax.dot_general(vcat, p.astype(BF16), (((0,), (0,)), ((), ())),
                                     preferred_element_type=F32)
                halves.append(ot[HEAD_DIM * g:HEAD_DIM * (g + 1)] / denom)
            o_ref[rows, LANES * j:LANES * (j + 1)] = jnp.concatenate(halves, axis=0).T.astype(BF16)


def _attn_bias():
    t = np.arange(BLOCK)[None, :]
    s = np.arange(3 * BLOCK)[:, None] - BLOCK
    dist = np.abs(t - s).astype(np.float64)
    valid = dist <= WINDOW
    slopes = 2.0 ** (-8.0 * (np.arange(N_HEADS) + 1.0) / N_HEADS)
    out = np.zeros((3, N_HEADS, 3 * BLOCK, BLOCK), np.float32)
    for variant in range(3):
        ok = valid.copy()
        if variant == 0:
            ok[:BLOCK] = False
        if variant == 2:
            ok[2 * BLOCK:] = False
        for h in range(N_HEADS):
            out[variant, h] = np.where(ok, -slopes[h] * dist * LOG2E, NEG_INF)
    return out


def _attention(q, k, ks, v, bias, sink, bsz, seq, cast_views):
    nb = seq // BLOCK
    sub = ATTN_SUB_BLOCKS
    assert nb % sub == 0
    ns = nb // sub
    kvw = k.shape[1]
    aw = q.shape[1]

    def edge_spec(off):
        return pl.BlockSpec((BLOCK, kvw), lambda b, n: (b * nb + jnp.clip(sub * n + off, 0, nb - 1), 0))

    centre = pl.BlockSpec((sub * BLOCK, kvw), lambda b, n: (b * ns + n, 0))
    bias_spec = lambda pick: pl.BlockSpec((None,) + bias.shape[1:], lambda b, n: (pick(n), 0, 0, 0))
    first = bias_spec(lambda n: jnp.where(n == 0, 0, 1))
    middle = bias_spec(lambda n: 1)
    last = bias_spec(lambda n: jnp.where(n == ns - 1, 2, 1))
    qspec = pl.BlockSpec((sub * BLOCK, aw), lambda b, n: (b * ns + n, 0))
    cast_in, cast_out, cast_shape = _cast_specs(cast_views, bsz * ns, lambda b, n: b * ns + n)
    return pl.pallas_call(
        functools.partial(_attn_body, len(cast_views)),
        grid=(bsz, ns),
        in_specs=[qspec] + [edge_spec(-1), centre, edge_spec(sub)] * 3 + [
                  first, middle, last,
                  pl.BlockSpec(memory_space=pltpu.SMEM)] + cast_in,
        out_specs=[qspec] + cast_out,
        out_shape=[jax.ShapeDtypeStruct(q.shape, BF16)] + cast_shape,
        compiler_params=_params(("parallel", "arbitrary")),
        name="attn",
    )(q, k, k, k, ks, ks, ks, v, v, v, bias, bias, bias, sink, *cast_views)


def _s5prep_body(lam_ref, ldt_ref, bt_ref, c_ref, w32_ref, wb_ref, wct_ref, t_ref, a_ref, w16_ref):
    w16_ref[...] = w32_ref[...].astype(BF16)
    width = CHUNK * SSM_CH
    lane = lax.broadcasted_iota(jnp.int32, (1, LANES), 1)
    in_group = [lane < SSM_STATE, lane >= SSM_STATE]
    lane_t = lax.broadcasted_iota(jnp.int32, (SSM_CH, width), 1)
    contract = (((1,), (1,)), ((), ()))
    hi = lax.Precision.HIGHEST
    toeplitz = [[None] * CHUNK for _ in range(PAIR)]
    for d in range(2):
        lre, lim = lam_ref[d, 0], lam_ref[d, 1]
        dt = jnp.exp(ldt_ref[d])
        mag = jnp.exp(lre * dt)
        are, aim = mag * jnp.cos(lim * dt), mag * jnp.sin(lim * dt)
        den = lre * lre + lim * lim
        cfr = ((are - 1.0) * lre + aim * lim) / den
        cfi = (aim * lre - (are - 1.0) * lim) / den
        btr, bti = bt_ref[d, 0], bt_ref[d, 1]
        bbr = btr * cfr - bti * cfi
        bbi = btr * cfi + bti * cfr
        cre, cim = c_ref[d, 0], c_ref[d, 1]

        pwr, pwi = jnp.ones_like(are), jnp.zeros_like(are)
        e_r, e_i = [], []
        for p in range(CHUNK + 1):
            er = cre * pwr - cim * pwi
            ei = cre * pwi + cim * pwr
            if p < CHUNK:
                e_r.append(er)
                e_i.append(ei)
                k = CHUNK - 1 - p if d == 0 else p
                for ri, val in enumerate((bbr * pwr - bbi * pwi, bbr * pwi + bbi * pwr)):
                    for g in range(PAIR):
                        wb_ref[width * g + SSM_CH * k:width * g + SSM_CH * (k + 1),
                               LANES * (2 * d + ri):LANES * (2 * d + ri + 1)] = jnp.where(
                                   in_group[g], val, 0.0).astype(BF16)
            if p >= 1:
                r = p - 1 if d == 0 else CHUNK - p
                for ri, val in enumerate((er, -ei)):
                    for g in range(PAIR):
                        wct_ref[width * g + SSM_CH * r:width * g + SSM_CH * (r + 1),
                                LANES * (2 * d + ri):LANES * (2 * d + ri + 1)] = jnp.where(
                                    in_group[g], val, 0.0).astype(BF16)
            if p == CHUNK:
                a_ref[2 * d] = jnp.broadcast_to(pwr, a_ref.shape[1:])
                a_ref[2 * d + 1] = jnp.broadcast_to(pwi, a_ref.shape[1:])
            pwr, pwi = pwr * are - pwi * aim, pwr * aim + pwi * are

        if d == 1:
            e_r, e_i = e_r[::-1], e_i[::-1]
        ecat_r, ecat_i = jnp.concatenate(e_r, axis=0), jnp.concatenate(e_i, axis=0)
        for g in range(PAIR):
            gr, gi = jnp.where(in_group[g], bbr, 0.0), jnp.where(in_group[g], bbi, 0.0)
            krow = (lax.dot_general(gr, ecat_r, contract, precision=hi, preferred_element_type=F32)
                    - lax.dot_general(gi, ecat_i, contract, precision=hi, preferred_element_type=F32))
            for k in range(CHUNK):
                if d == 0:
                    shift, keep = SSM_CH * k, lane_t >= SSM_CH * k
                else:
                    shift, keep = (SSM_CH * (k + 1)) % width, lane_t < SSM_CH * (k + 1)
                blk = jnp.where(keep, pltpu.roll(krow, shift, 1) if shift else krow, 0.0)
                toeplitz[g][k] = blk if d == 0 else toeplitz[g][k] + blk
    for g in range(PAIR):
        t_ref[g] = jnp.concatenate(toeplitz[g], axis=0).astype(BF16)


def _s5prep(lam, ldt, bt, c, w32):
    npair = lam.shape[1]
    width = CHUNK * SSM_CH
    pw = PAIR * width
    cast_in, cast_out, cast_shape = _cast_specs([w32], npair, lambda i: i)
    return pl.pallas_call(
        _s5prep_body,
        grid=(npair,),
        in_specs=[pl.BlockSpec((2, None, 2, 1, LANES), lambda i: (0, i, 0, 0, 0)),
                  pl.BlockSpec((2, None, 1, LANES), lambda i: (0, i, 0, 0)),
                  pl.BlockSpec((2, None, 2, SSM_CH, LANES), lambda i: (0, i, 0, 0, 0)),
                  pl.BlockSpec((2, None, 2, SSM_CH, LANES), lambda i: (0, i, 0, 0, 0))] + cast_in,
        out_specs=[pl.BlockSpec((None, pw, 4 * LANES), lambda i: (i, 0, 0)),
                   pl.BlockSpec((None, pw, 4 * LANES), lambda i: (i, 0, 0)),
                   pl.BlockSpec((PAIR, width, width), lambda i: (i, 0, 0)),
                   pl.BlockSpec((None, 4, SUBLANES, LANES), lambda i: (i, 0, 0, 0))] + cast_out,
        out_shape=[jax.ShapeDtypeStruct((npair, pw, 4 * LANES), BF16),
                   jax.ShapeDtypeStruct((npair, pw, 4 * LANES), BF16),
                   jax.ShapeDtypeStruct((npair * PAIR, width, width), BF16),
                   jax.ShapeDtypeStruct((npair, 4, SUBLANES, LANES), F32)] + cast_shape,
        compiler_params=_params(("parallel",)),
        name="s5prep",
    )(lam, ldt, bt, c, w32)


def _block_transpose(xs):
    n = len(xs)
    assert n * SSM_CH == LANES and n & (n - 1) == 0
    blk = lax.broadcasted_iota(jnp.int32, xs[0].shape, 1) // SSM_CH
    d = n // 2
    while d:
        upper = (blk & d) != 0
        ys = list(xs)
        for a in range(n):
            if a & d:
                continue
            b = a + d
            ys[a] = jnp.where(upper, pltpu.roll(xs[b], SSM_CH * d, 1), xs[a])
            ys[b] = jnp.where(upper, xs[b], pltpu.roll(xs[a], LANES - SSM_CH * d, 1))
        xs = ys
        d //= 2
    return xs


def _s5_scan(a_ref, w_scr, s_scr):
    rows = w_scr.shape[0]
    nsteps = rows // SUBLANES
    nstream = w_scr.shape[1] // (2 * LANES)
    half = SUBLANES // 2
    lo = lax.broadcasted_iota(jnp.int32, (SUBLANES, LANES), 0) < half
    first = [lo, jnp.logical_not(lo)]
    coef = [(a_ref[s // 2, 2 * (s % 2)], a_ref[s // 2, 2 * (s % 2) + 1]) for s in range(nstream)]
    col = lambda i: slice(LANES * i, LANES * (i + 1))

    def advance(a, xr, xi, wr, wi):
        return a[0] * xr - a[1] * xi + wr, a[0] * xi + a[1] * xr + wi

    def swap(x):
        return pltpu.roll(x, half, 0)

    def step(i, carry):
        out = []
        for s in range(nstream):
            d = s % 2
            cr, ci = carry[2 * s], carry[2 * s + 1]
            vreg = i if d == 0 else nsteps - 1 - i
            rs = pl.ds(pl.multiple_of(vreg * SUBLANES, SUBLANES), SUBLANES)
            wr, wi = w_scr[rs, col(2 * s)], w_scr[rs, col(2 * s + 1)]
            r0r, r0i = swap(cr), swap(ci)
            t1r, t1i = advance(coef[s], r0r, r0i, wr, wi)
            r1r, r1i = swap(t1r), swap(t1i)
            t2r, t2i = advance(coef[s], r1r, r1i, wr, wi)
            s_scr[rs, col(2 * s)] = jnp.where(first[d], r0r, r1r)
            s_scr[rs, col(2 * s + 1)] = jnp.where(first[d], r0i, r1i)
            out += [jnp.where(first[d], t1r, t2r), jnp.where(first[d], t1i, t2i)]
        return tuple(out)

    z = jnp.zeros((SUBLANES, LANES), F32)
    lax.fori_loop(0, nsteps, step, (z,) * (2 * nstream))


S5_SLAB_GROUPS = LANES // SSM_CH
S5_ROW_CHUNK = 256
S5_SUB_ROWS = 128
S5_SCAN_PAIRS = 2


def _s5_body(u_ref, wb_ref, t_ref, wct_ref, a_ref, y_ref, ur_scr, w_scr, s_scr, sall_scr, yr_scr, tok_scr):
    rows = ur_scr.shape[0]
    width = CHUNK * SSM_CH
    pw = PAIR * width
    npair = S5_SLAB_GROUPS // PAIR
    half_steps = CHUNK // 2
    tok_rows = S5_ROW_CHUNK * CHUNK

    nchunks = rows // S5_ROW_CHUNK
    nsub = S5_ROW_CHUNK // S5_SUB_ROWS
    sw = 4 * LANES
    row_chunk = lambda c: slice(S5_ROW_CHUNK * c, S5_ROW_CHUNK * (c + 1))
    tok_chunk = lambda c: slice(tok_rows * c, tok_rows * (c + 1))
    step_rows = lambda i, h, kk: pl.ds(S5_SUB_ROWS * i * CHUNK + half_steps * h + kk, S5_SUB_ROWS,
                                       stride=CHUNK)
    lane_blk = lambda g, h: slice(width * g + LANES * h, width * g + LANES * (h + 1))

    def relayout_in(c):
        tok = tok_scr.at[c % 2]
        tok[...] = u_ref[tok_chunk(c), :].astype(F32)
        for i in range(nsub):
            for h in range(2):
                xs = [tok[step_rows(i, h, kk), :] for kk in range(half_steps)]
                ys = _block_transpose(xs)
                for g in range(S5_SLAB_GROUPS):
                    r0 = S5_ROW_CHUNK * c + S5_SUB_ROWS * i
                    ur_scr[r0:r0 + S5_SUB_ROWS, lane_blk(g, h)] = ys[g].astype(BF16)

    def state_in(c, p0):
        for q in range(S5_SCAN_PAIRS):
            pp = p0 + q
            w_scr[row_chunk(c), sw * q:sw * (q + 1)] = jnp.dot(
                ur_scr[row_chunk(c), pw * pp:pw * (pp + 1)], wb_ref[pp], preferred_element_type=F32)

    for p0 in range(0, npair, S5_SCAN_PAIRS):
        for c in range(nchunks):
            if p0 == 0:
                relayout_in(c)
            state_in(c, p0)
        _s5_scan(a_ref.at[p0:p0 + S5_SCAN_PAIRS], w_scr, s_scr)
        sall_scr[:, sw * p0:sw * (p0 + S5_SCAN_PAIRS)] = s_scr[...].astype(BF16)

    def outputs(c):
        yr, tok = yr_scr.at[c % 2], tok_scr.at[c % 2]
        for pp in range(npair):
            y = lax.dot_general(sall_scr[row_chunk(c), sw * pp:sw * (pp + 1)], wct_ref[pp],
                                (((1,), (1,)), ((), ())), preferred_element_type=F32)
            for g in range(PAIR):
                lo = pw * pp + width * g
                yr[:, lo:lo + width] = y[:, width * g:width * (g + 1)] + jnp.dot(
                    ur_scr[row_chunk(c), lo:lo + width], t_ref[PAIR * pp + g], preferred_element_type=F32)
        for i in range(nsub):
            for h in range(2):
                xs = [yr[S5_SUB_ROWS * i:S5_SUB_ROWS * (i + 1), lane_blk(g, h)].astype(BF16)
                      for g in range(S5_SLAB_GROUPS)]
                ys = _block_transpose(xs)
                for kk in range(half_steps):
                    tok[step_rows(i, h, kk), :] = ys[kk].astype(F32)
        y_ref[tok_chunk(c), :] = tok[...].astype(BF16)

    for c in range(nchunks):
        outputs(c)


def _s5(u_tok, wb, tsum, wct, a16):
    T, ssm_w = u_tok.shape
    rows = T // CHUNK
    assert rows % S5_ROW_CHUNK == 0 and S5_SLAB_GROUPS == CHUNK // 2
    width = CHUNK * SSM_CH
    pw = PAIR * width
    npair = S5_SLAB_GROUPS // PAIR
    slab = pl.BlockSpec((T, LANES), lambda s: (0, s))
    return pl.pallas_call(
        _s5_body,
        grid=(ssm_w // LANES,),
        in_specs=[slab,
                  pl.BlockSpec((npair, pw, 4 * LANES), lambda s: (s, 0, 0)),
                  pl.BlockSpec((S5_SLAB_GROUPS, width, width), lambda s: (s, 0, 0)),
                  pl.BlockSpec((npair, pw, 4 * LANES), lambda s: (s, 0, 0)),
                  pl.BlockSpec((npair, 4, SUBLANES, LANES), lambda s: (s, 0, 0, 0))],
        out_specs=slab,
        out_shape=jax.ShapeDtypeStruct((T, ssm_w), BF16),
        scratch_shapes=[pltpu.VMEM((rows, S5_SLAB_GROUPS * width), BF16),
                        pltpu.VMEM((rows, S5_SCAN_PAIRS * 4 * LANES), F32),
                        pltpu.VMEM((rows, S5_SCAN_PAIRS * 4 * LANES), F32),
                        pltpu.VMEM((rows, npair * 4 * LANES), BF16),
                        pltpu.VMEM((2, S5_ROW_CHUNK, S5_SLAB_GROUPS * width), F32),
                        pltpu.VMEM((2, S5_ROW_CHUNK * CHUNK, LANES), F32)],
        compiler_params=_params(("parallel",)),
        name="s5",
    )(u_tok, wb, tsum, wct, a16)


def _merge_body(attn_ref, ys_ref, u_ref, g_ref, x_ref, wba_ref, wglu_ref, bglu_ref, d_ref, wbs_ref,
                wout_ref, gpost_ref, gpre_ref, x1_ref, h2_ref):
    tm, d = x_ref.shape
    u = u_ref[...].reshape(tm, -1).astype(F32)
    y = ys_ref[...].reshape(tm, -1).astype(F32) + d_ref[...] * u
    y = _gelu_tanh(y)
    gl = jnp.dot(y.astype(BF16), wglu_ref[...], preferred_element_type=F32) + bglu_ref[...]
    s = y * jax.nn.sigmoid(gl)
    y_ssm = jnp.dot(s.astype(BF16), wbs_ref[...], preferred_element_type=F32)
    y_attn = jnp.dot(attn_ref[...], wba_ref[...], preferred_element_type=F32)
    merged = g_ref[:, :d].astype(F32) * y_attn + g_ref[:, d:].astype(F32) * y_ssm
    o = jnp.dot(merged.astype(BF16), wout_ref[...], preferred_element_type=F32)
    x1 = x_ref[...] + _rms(o, gpost_ref[...])
    x1_ref[...] = x1
    h2_ref[...] = _rms(x1, gpre_ref[...]).astype(BF16)


def _merge(attn, ys, u, gates, x2, wba, wglu, bglu, ssm_d, wbs, wout, gpost, gpre, seq, tm):
    T, D = x2.shape
    row = lambda w: pl.BlockSpec((tm, w), lambda i: (i, 0))
    chunked = _chunk_rows_spec(tm, seq // tm, u.shape[-1])
    consts = [wba, wglu, bglu, ssm_d, wbs, wout, gpost, gpre]
    return pl.pallas_call(
        _merge_body,
        grid=(T // tm,),
        in_specs=[row(attn.shape[1]), chunked, chunked, row(gates.shape[1]), row(D)]
                 + [_const_spec(c.shape) for c in consts],
        out_specs=[row(D), row(D)],
        out_shape=[jax.ShapeDtypeStruct((T, D), F32), jax.ShapeDtypeStruct((T, D), BF16)],
        compiler_params=_params(("parallel",)),
        name="merge",
    )(attn, ys, u, gates, x2, *consts)


def _ffn_body(tiles_per_seq, hp_ref, hc_ref, hn_ref, x1_ref, wg_ref, wu_ref, cw_ref, cb_ref, wd_ref,
              gn_ref, o_ref, hext, fscr):
    tm = hc_ref.shape[0]
    i = pl.program_id(0) % tiles_per_seq
    zero = jnp.zeros((), BF16)
    hext[0:HALO] = jnp.where(i == 0, zero, hp_ref[...])
    hext[HALO:HALO + tm] = hc_ref[...]
    hext[HALO + tm:] = jnp.where(i == tiles_per_seq - 1, zero, hn_ref[...])
    ext = tm + 2 * HALO
    for c in range(wg_ref.shape[1] // FF_TILE):
        sl = slice(FF_TILE * c, FF_TILE * (c + 1))
        g = jnp.dot(hext[...], wg_ref[:, sl], preferred_element_type=F32)
        up = jnp.dot(hc_ref[...], wu_ref[:, sl], preferred_element_type=F32)
        g_prev = pltpu.roll(g, 1, 0)[HALO:HALO + tm]
        g_next = pltpu.roll(g, ext - 1, 0)[HALO:HALO + tm]
        conv = (g_prev * cw_ref[0:1, sl] + g[HALO:HALO + tm] * cw_ref[1:2, sl]
                + g_next * cw_ref[2:3, sl] + cb_ref[:, sl])
        fscr[:, sl] = (_gelu_tanh(conv) * up).astype(BF16)
    o = jnp.dot(fscr[...], wd_ref[...], preferred_element_type=F32)
    o_ref[...] = x1_ref[...] + _rms(o, gn_ref[...])


def _ffn(h2, x1, wg, wu, cw, cb, wd, gn, seq, tm):
    T, D = x1.shape
    dff = wg.shape[1]
    assert dff % FF_TILE == 0 and tm % HALO == 0 and seq % tm == 0
    per = tm // HALO
    nh = T // HALO
    row = lambda w: pl.BlockSpec((tm, w), lambda i: (i, 0))
    once = lambda shape: pl.BlockSpec(shape, lambda i: (0,) * len(shape), pipeline_mode=pl.Buffered(1))
    return pl.pallas_call(
        functools.partial(_ffn_body, seq // tm),
        grid=(T // tm,),
        in_specs=[pl.BlockSpec((HALO, D), lambda i: (jnp.maximum(i * per - 1, 0), 0)),
                  row(D),
                  pl.BlockSpec((HALO, D), lambda i: (jnp.minimum((i + 1) * per, nh - 1), 0)),
                  row(D),
                  once(wg.shape), once(wu.shape), _const_spec(cw.shape), _const_spec(cb.shape),
                  once(wd.shape), _const_spec(gn.shape)],
        out_specs=row(D),
        out_shape=jax.ShapeDtypeStruct((T, D), F32),
        scratch_shapes=[pltpu.VMEM((tm + 2 * HALO, D), BF16), pltpu.VMEM((tm, dff), BF16)],
        compiler_params=_params(("parallel",)),
        name="ffn",
    )(h2, h2, h2, x1, wg, wu, cw, cb, wd, gn)


def _row_tile(seq):
    return min(1024, seq)


def _layer(x, p):
    bsz, seq, d = x.shape
    T = bsz * seq
    tm = _row_tile(seq)
    ssm_w = d // 2
    ngroups = ssm_w // SSM_CH
    attn_w = N_HEADS * HEAD_DIM
    row2 = lambda v: v.reshape(1, -1).astype(F32)

    npair = ngroups // PAIR
    both = lambda re, im: jnp.stack([jnp.stack([p[re + "_f"], p[im + "_f"]]),
                                     jnp.stack([p[re + "_b"], p[im + "_b"]])]).astype(F32)
    lam = both("lam_re", "lam_im").reshape(2, 2, npair, 1, LANES).transpose(0, 2, 1, 3, 4)
    ldt = jnp.repeat(jnp.stack([p["log_dt_f"], p["log_dt_b"]]).astype(F32), SSM_STATE, axis=-1)
    ldt = ldt.reshape(2, npair, 1, LANES)
    bt = both("b_re", "b_im").reshape(2, 2, npair, PAIR, SSM_STATE, SSM_CH)
    bt = bt.transpose(0, 2, 1, 5, 3, 4).reshape(2, npair, 2, SSM_CH, LANES)
    cc = both("c_re", "c_im").reshape(2, 2, npair, PAIR, SSM_CH, SSM_STATE)
    cc = cc.transpose(0, 2, 1, 4, 3, 5).reshape(2, npair, 2, SSM_CH, LANES)
    wb, wct, tsum, a, w_in = _s5prep(lam, ldt, bt, cc, p["w_in"])

    x2 = x.reshape(T, d)
    q, k, ks, v, u, gates = _inproj(x2, row2(p["norm_mix_pre"]), w_in, bsz, seq, tm)

    later = ("w_ffn_gate", "w_ffn_up", "w_ffn_down", "w_out", "w_branch_attn", "w_branch_ssm", "w_glu")
    attn, *rounded = _attention(q, k, ks, v, jnp.asarray(_attn_bias()), p["attn_sink"].astype(F32), bsz, seq,
                                [p[name] for name in later])
    w16 = dict(zip(later, rounded))

    assert bsz == SUBLANES // 2
    ys = _s5(u.reshape(T, ssm_w), wb, tsum, wct, a).reshape(u.shape)

    x1, h2 = _merge(attn, ys, u, gates, x2,
                    w16["w_branch_attn"], w16["w_glu"], row2(p["b_glu"]),
                    row2(p["ssm_d"]), w16["w_branch_ssm"], w16["w_out"],
                    row2(p["norm_mix_post"]), row2(p["norm_ffn_pre"]), seq, tm)

    out = _ffn(h2, x1, w16["w_ffn_gate"], w16["w_ffn_up"], p["conv_w"].astype(F32),
               row2(p["conv_b"]), w16["w_ffn_down"], row2(p["norm_ffn_post"]), seq, tm)
    return out.reshape(bsz, seq, d)


_PARAM_NAMES = (
    "norm_mix_pre", "w_in", "attn_sink",
    "lam_re_f", "lam_im_f", "log_dt_f", "b_re_f", "b_im_f", "c_re_f", "c_im_f",
    "lam_re_b", "lam_im_b", "log_dt_b", "b_re_b", "b_im_b", "c_re_b", "c_im_b",
    "ssm_d", "w_glu", "b_glu", "w_branch_attn", "w_branch_ssm", "w_out", "norm_mix_post",
    "norm_ffn_pre", "w_ffn_gate", "w_ffn_up", "conv_w", "conv_b", "w_ffn_down", "norm_ffn_post")


def kernel(x, norm_mix_pre, w_in, attn_sink, lam_re_f, lam_im_f, log_dt_f, b_re_f, b_im_f, c_re_f, c_im_f, lam_re_b, lam_im_b, log_dt_b, b_re_b, b_im_b, c_re_b, c_im_b, ssm_d, w_glu, b_glu, w_branch_attn, w_branch_ssm, w_out, norm_mix_post, norm_ffn_pre, w_ffn_gate, w_ffn_up, conv_w, conv_b, w_ffn_down, norm_ffn_post):
    stacked = dict(zip(_PARAM_NAMES, (
        norm_mix_pre, w_in, attn_sink,
        lam_re_f, lam_im_f, log_dt_f, b_re_f, b_im_f, c_re_f, c_im_f,
        lam_re_b, lam_im_b, log_dt_b, b_re_b, b_im_b, c_re_b, c_im_b,
        ssm_d, w_glu, b_glu, w_branch_attn, w_branch_ssm, w_out, norm_mix_post,
        norm_ffn_pre, w_ffn_gate, w_ffn_up, conv_w, conv_b, w_ffn_down, norm_ffn_post)))
    for layer in range(w_in.shape[0]):
        x = _layer(x, {name: value[layer] for name, value in stacked.items()})
    return x
```

```python
import functools
import math

import numpy as np
import jax
import jax.numpy as jnp
from jax import lax
from jax.experimental import pallas as pl
from jax.experimental.pallas import tpu as pltpu

F32 = jnp.float32
BF16 = jnp.bfloat16

N_HEADS = 8
N_KV_HEADS = 2
HEAD_DIM = 64
GROUP = N_HEADS // N_KV_HEADS
WINDOW = 128
BLOCK = 128
ATTN_SUB_BLOCKS = 8
SSM_CH = 16
SSM_STATE = 64
CHUNK = 16
PAIR = 2
N_BRANCH = 2
EPS = 1e-6
NEG_INF = -1e30
LOG2E = math.log2(math.e)

LANES = 128
SUBLANES = 8
HALO = 16
FF_TILE = 256
VMEM_LIMIT = 56 * 1024 * 1024


def _gelu_tanh(x):
    return 0.5 * x * (1.0 + jnp.tanh(math.sqrt(2.0 / math.pi) * (x + 0.044715 * (x * x * x))))


def _rms(x, gain):
    return x * lax.rsqrt(jnp.mean(x * x, axis=-1, keepdims=True) + EPS) * gain


def _params(semantics):
    return pltpu.CompilerParams(dimension_semantics=semantics, vmem_limit_bytes=VMEM_LIMIT)


def _const_spec(shape):
    nd = len(shape)
    return pl.BlockSpec(shape, lambda *_: (0,) * nd)


def _inproj_body(splits, x_ref, g_ref, w_ref, q_ref, k_ref, ks_ref, v_ref, u_ref, gate_ref):
    h = _rms(x_ref[...], g_ref[...]).astype(BF16)

    def proj(i):
        return jnp.dot(h, w_ref[:, splits[i]:splits[i + 1]], preferred_element_type=F32)

    q_ref[...] = (proj(0) * (HEAD_DIM ** -0.5 * LOG2E)).astype(BF16)
    k = proj(1).astype(BF16)
    k_ref[...] = k
    ks_ref[...] = pltpu.roll(k, HEAD_DIM, 1)
    v_ref[...] = proj(2).astype(BF16)
    u_ref[...] = proj(3).astype(BF16).reshape(u_ref.shape)
    gate_ref[...] = jax.nn.sigmoid(proj(4)).astype(BF16)


def _chunk_rows_spec(tm, tiles_per_seq, width):
    return pl.BlockSpec((tm // CHUNK, None, CHUNK, width),
                        lambda i: (i % tiles_per_seq, i // tiles_per_seq, 0, 0))


def _inproj(x2, gain, w_in, bsz, seq, tm):
    T, D = x2.shape
    attn_w = N_HEADS * HEAD_DIM
    kv_w = N_KV_HEADS * HEAD_DIM
    widths = [attn_w, kv_w, kv_w, D // 2, N_BRANCH * D]
    splits = [0] + np.cumsum(widths).tolist()
    row = lambda w: pl.BlockSpec((tm, w), lambda i: (i, 0))
    out_widths = [attn_w, kv_w, kv_w, kv_w, D // 2, N_BRANCH * D]
    out_specs = [row(w) for w in out_widths]
    out_shape = [jax.ShapeDtypeStruct((T, w), BF16) for w in out_widths]
    out_specs[4] = _chunk_rows_spec(tm, seq // tm, out_widths[4])
    out_shape[4] = jax.ShapeDtypeStruct((seq // CHUNK, bsz, CHUNK, out_widths[4]), BF16)
    return pl.pallas_call(
        functools.partial(_inproj_body, tuple(splits)),
        grid=(T // tm,),
        in_specs=[row(D), _const_spec((1, D)), _const_spec(w_in.shape)],
        out_specs=out_specs,
        out_shape=out_shape,
        compiler_params=_params(("parallel",)),
        name="inproj",
    )(x2, gain, w_in)


def _cast_specs(weights, steps, step_of):
    specs = []
    for w in weights:
        nblocks = max(n for n in range(1, steps + 1)
                      if steps % n == 0 and w.shape[0] % (n * HALO) == 0)
        hold = steps // nblocks
        specs.append(pl.BlockSpec((w.shape[0] // nblocks, w.shape[1]),
                                  lambda *ids, hold=hold: (step_of(*ids) // hold, 0)))
    return specs, specs, [jax.ShapeDtypeStruct(w.shape, BF16) for w in weights]


def _attn_body(ncast, q_ref, kp_ref, kc_ref, kn_ref, sp_ref, sc_ref, sn_ref, vp_ref, vc_ref, vn_ref,
               bias_first_ref, bias_mid_ref, bias_last_ref, sink_ref, *refs):
    o_ref = refs[ncast]
    for src, dst in zip(refs[:ncast], refs[ncast + 1:]):
        dst[...] = src[...].astype(BF16)
    bias_refs = [bias_first_ref] + [bias_mid_ref] * (ATTN_SUB_BLOCKS - 2) + [bias_last_ref]
    lane = lax.broadcasted_iota(jnp.int32, (1, LANES), 1)
    half_mask = [lane < HEAD_DIM, lane >= HEAD_DIM]
    zero = jnp.zeros((), BF16)
    kall = [jnp.concatenate([kp_ref[...], kc_ref[...], kn_ref[...]], axis=0),
            jnp.concatenate([sp_ref[...], sc_ref[...], sn_ref[...]], axis=0)]
    vall = jnp.concatenate([vp_ref[...], vc_ref[...], vn_ref[...]], axis=0)
    for b, bias_ref in enumerate(bias_refs):
        keys = slice(BLOCK * b, BLOCK * (b + 3))
        vcat = vall[keys]
        rows = slice(BLOCK * b, BLOCK * (b + 1))
        for j in range(N_HEADS // 2):
            qj = q_ref[rows, LANES * j:LANES * (j + 1)]
            g = (2 * j) // GROUP
            halves = []
            for e in range(2):
                h = 2 * j + e
                qm = jnp.where(half_mask[e], qj, zero)
                kcat = kall[0 if e == g else 1][keys]
                st = lax.dot_general(kcat, qm, (((1,), (1,)), ((), ())), preferred_element_type=F32)
                st = st + bias_ref[h]
                sink = sink_ref[h] * LOG2E
                m = jnp.maximum(jnp.max(st, axis=0, keepdims=True), sink)
                p = jnp.exp2(st - m)
                denom = jnp.sum(p, axis=0, keepdims=True) + jnp.exp2(sink - m)
                ot = lax.dot_general(vcat, p.astype(BF16), (((0,), (0,)), ((), ())),
                                     preferred_element_type=F32)
                halves.append(ot[HEAD_DIM * g:HEAD_DIM * (g + 1)] / denom)
            o_ref[rows, LANES * j:LANES * (j + 1)] = jnp.concatenate(halves, axis=0).T.astype(BF16)


def _attn_bias():
    t = np.arange(BLOCK)[None, :]
    s = np.arange(3 * BLOCK)[:, None] - BLOCK
    dist = np.abs(t - s).astype(np.float64)
    valid = dist <= WINDOW
    slopes = 2.0 ** (-8.0 * (np.arange(N_HEADS) + 1.0) / N_HEADS)
    out = np.zeros((3, N_HEADS, 3 * BLOCK, BLOCK), np.float32)
    for variant in range(3):
        ok = valid.copy()
        if variant == 0:
            ok[:BLOCK] = False
        if variant == 2:
            ok[2 * BLOCK:] = False
        for h in range(N_HEADS):
            out[variant, h] = np.where(ok, -slopes[h] * dist * LOG2E, NEG_INF)
    return out


def _attention(q, k, ks, v, bias, sink, bsz, seq, cast_views):
    nb = seq // BLOCK
    sub = ATTN_SUB_BLOCKS
    assert nb % sub == 0
    ns = nb // sub
    kvw = k.shape[1]
    aw = q.shape[1]

    def edge_spec(off):
        return pl.BlockSpec((BLOCK, kvw), lambda b, n: (b * nb + jnp.clip(sub * n + off, 0, nb - 1), 0))

    centre = pl.BlockSpec((sub * BLOCK, kvw), lambda b, n: (b * ns + n, 0))
    bias_spec = lambda pick: pl.BlockSpec((None,) + bias.shape[1:], lambda b, n: (pick(n), 0, 0, 0))
    first = bias_spec(lambda n: jnp.where(n == 0, 0, 1))
    middle = bias_spec(lambda n: 1)
    last = bias_spec(lambda n: jnp.where(n == ns - 1, 2, 1))
    qspec = pl.BlockSpec((sub * BLOCK, aw), lambda b, n: (b * ns + n, 0))
    cast_in, cast_out, cast_shape = _cast_specs(cast_views, bsz * ns, lambda b, n: b * ns + n)
    return pl.pallas_call(
        functools.partial(_attn_body, len(cast_views)),
        grid=(bsz, ns),
        in_specs=[qspec] + [edge_spec(-1), centre, edge_spec(sub)] * 3 + [
                  first, middle, last,
                  pl.BlockSpec(memory_space=pltpu.SMEM)] + cast_in,
        out_specs=[qspec] + cast_out,
        out_shape=[jax.ShapeDtypeStruct(q.shape, BF16)] + cast_shape,
        compiler_params=_params(("parallel", "arbitrary")),
        name="attn",
    )(q, k, k, k, ks, ks, ks, v, v, v, bias, bias, bias, sink, *cast_views)


def _s5prep_body(lam_ref, ldt_ref, bt_ref, c_ref, w32_ref, wb_ref, wct_ref, t_ref, a_ref, w16_ref):
    w16_ref[...] = w32_ref[...].astype(BF16)
    width = CHUNK * SSM_CH
    lane = lax.broadcasted_iota(jnp.int32, (1, LANES), 1)
    in_group = [lane < SSM_STATE, lane >= SSM_STATE]
    lane_t = lax.broadcasted_iota(jnp.int32, (SSM_CH, width), 1)
    contract = (((1,), (1,)), ((), ()))
    hi = lax.Precision.HIGHEST
    toeplitz = [[None] * CHUNK for _ in range(PAIR)]
    for d in range(2):
        lre, lim = lam_ref[d, 0], lam_ref[d, 1]
        dt = jnp.exp(ldt_ref[d])
        mag = jnp.exp(lre * dt)
        are, aim = mag * jnp.cos(lim * dt), mag * jnp.sin(lim * dt)
        den = lre * lre + lim * lim
        cfr = ((are - 1.0) * lre + aim * lim) / den
        cfi = (aim * lre - (are - 1.0) * lim) / den
        btr, bti = bt_ref[d, 0], bt_ref[d, 1]
        bbr = btr * cfr - bti * cfi
        bbi = btr * cfi + bti * cfr
        cre, cim = c_ref[d, 0], c_ref[d, 1]

        pwr, pwi = jnp.ones_like(are), jnp.zeros_like(are)
        e_r, e_i = [], []
        for p in range(CHUNK + 1):
            er = cre * pwr - cim * pwi
            ei = cre * pwi + cim * pwr
            if p < CHUNK:
                e_r.append(er)
                e_i.append(ei)
                k = CHUNK - 1 - p if d == 0 else p
                for ri, val in enumerate((bbr * pwr - bbi * pwi, bbr * pwi + bbi * pwr)):
                    for g in range(PAIR):
                        wb_ref[width * g + SSM_CH * k:width * g + SSM_CH * (k + 1),
                               LANES * (2 * d + ri):LANES * (2 * d + ri + 1)] = jnp.where(
                                   in_group[g], val, 0.0).astype(BF16)
            if p >= 1:
                r = p - 1 if d == 0 else CHUNK - p
                for ri, val in enumerate((er, -ei)):
                    for g in range(PAIR):
                        wct_ref[width * g + SSM_CH * r:width * g + SSM_CH * (r + 1),
                                LANES * (2 * d + ri):LANES * (2 * d + ri + 1)] = jnp.where(
                                    in_group[g], val, 0.0).astype(BF16)
            if p == CHUNK:
                a_ref[2 * d] = jnp.broadcast_to(pwr, a_ref.shape[1:])
                a_ref[2 * d + 1] = jnp.broadcast_to(pwi, a_ref.shape[1:])
            pwr, pwi = pwr * are - pwi * aim, pwr * aim + pwi * are

        if d == 1:
            e_r, e_i = e_r[::-1], e_i[::-1]
        ecat_r, ecat_i = jnp.concatenate(e_r, axis=0), jnp.concatenate(e_i, axis=0)
        for g in range(PAIR):
            gr, gi = jnp.where(in_group[g], bbr, 0.0), jnp.where(in_group[g], bbi, 0.0)
            krow = (lax.dot_general(gr, ecat_r, contract, precision=hi, preferred_element_type=F32)
                    - lax.dot_general(gi, ecat_i, contract, precision=hi, preferred_element_type=F32))
            for k in range(CHUNK):
                if d == 0:
                    shift, keep = SSM_CH * k, lane_t >= SSM_CH * k
                else:
                    shift, keep = (SSM_CH * (k + 1)) % width, lane_t < SSM_CH * (k + 1)
                blk = jnp.where(keep, pltpu.roll(krow, shift, 1) if shift else krow, 0.0)
                toeplitz[g][k] = blk if d == 0 else toeplitz[g][k] + blk
    for g in range(PAIR):
        t_ref[g] = jnp.concatenate(toeplitz[g], axis=0).astype(BF16)


def _s5prep(lam, ldt, bt, c, w32):
    npair = lam.shape[1]
    width = CHUNK * SSM_CH
    pw = PAIR * width
    cast_in, cast_out, cast_shape = _cast_specs([w32], npair, lambda i: i)
    return pl.pallas_call(
        _s5prep_body,
        grid=(npair,),
        in_specs=[pl.BlockSpec((2, None, 2, 1, LANES), lambda i: (0, i, 0, 0, 0)),
                  pl.BlockSpec((2, None, 1, LANES), lambda i: (0, i, 0, 0)),
                  pl.BlockSpec((2, None, 2, SSM_CH, LANES), lambda i: (0, i, 0, 0, 0)),
                  pl.BlockSpec((2, None, 2, SSM_CH, LANES), lambda i: (0, i, 0, 0, 0))] + cast_in,
        out_specs=[pl.BlockSpec((None, pw, 4 * LANES), lambda i: (i, 0, 0)),
                   pl.BlockSpec((None, pw, 4 * LANES), lambda i: (i, 0, 0)),
                   pl.BlockSpec((PAIR, width, width), lambda i: (i, 0, 0)),
                   pl.BlockSpec((None, 4, SUBLANES, LANES), lambda i: (i, 0, 0, 0))] + cast_out,
        out_shape=[jax.ShapeDtypeStruct((npair, pw, 4 * LANES), BF16),
                   jax.ShapeDtypeStruct((npair, pw, 4 * LANES), BF16),
                   jax.ShapeDtypeStruct((npair * PAIR, width, width), BF16),
                   jax.ShapeDtypeStruct((npair, 4, SUBLANES, LANES), F32)] + cast_shape,
        compiler_params=_params(("parallel",)),
        name="s5prep",
    )(lam, ldt, bt, c, w32)


def _block_transpose(xs):
    n = len(xs)
    assert n * SSM_CH == LANES and n & (n - 1) == 0
    blk = lax.broadcasted_iota(jnp.int32, xs[0].shape, 1) // SSM_CH
    d = n // 2
    while d:
        upper = (blk & d) != 0
        ys = list(xs)
        for a in range(n):
            if a & d:
                continue
            b = a + d
            ys[a] = jnp.where(upper, pltpu.roll(xs[b], SSM_CH * d, 1), xs[a])
            ys[b] = jnp.where(upper, xs[b], pltpu.roll(xs[a], LANES - SSM_CH * d, 1))
        xs = ys
        d //= 2
    return xs


def _s5_scan(a_ref, w_scr, s_scr):
    rows = w_scr.shape[0]
    nsteps = rows // SUBLANES
    nstream = w_scr.shape[1] // (2 * LANES)
    half = SUBLANES // 2
    lo = lax.broadcasted_iota(jnp.int32, (SUBLANES, LANES), 0) < half
    first = [lo, jnp.logical_not(lo)]
    coef = [(a_ref[s // 2, 2 * (s % 2)], a_ref[s // 2, 2 * (s % 2) + 1]) for s in range(nstream)]
    col = lambda i: slice(LANES * i, LANES * (i + 1))

    def advance(a, xr, xi, wr, wi):
        return a[0] * xr - a[1] * xi + wr, a[0] * xi + a[1] * xr + wi

    def swap(x):
        return pltpu.roll(x, half, 0)

    def step(i, carry):
        out = []
        for s in range(nstream):
            d = s % 2
            cr, ci = carry[2 * s], carry[2 * s + 1]
            vreg = i if d == 0 else nsteps - 1 - i
            rs = pl.ds(pl.multiple_of(vreg * SUBLANES, SUBLANES), SUBLANES)
            wr, wi = w_scr[rs, col(2 * s)], w_scr[rs, col(2 * s + 1)]
            r0r, r0i = swap(cr), swap(ci)
            t1r, t1i = advance(coef[s], r0r, r0i, wr, wi)
            r1r, r1i = swap(t1r), swap(t1i)
            t2r, t2i = advance(coef[s], r1r, r1i, wr, wi)
            s_scr[rs, col(2 * s)] = jnp.where(first[d], r0r, r1r)
            s_scr[rs, col(2 * s + 1)] = jnp.where(first[d], r0i, r1i)
            out += [jnp.where(first[d], t1r, t2r), jnp.where(first[d], t1i, t2i)]
        return tuple(out)

    z = jnp.zeros((SUBLANES, LANES), F32)
    lax.fori_loop(0, nsteps, step, (z,) * (2 * nstream))


S5_SLAB_GROUPS = LANES // SSM_CH
S5_ROW_CHUNK = 256
S5_SUB_ROWS = 128
S5_SCAN_PAIRS = 2


def _s5_body(u_ref, wb_ref, t_ref, wct_ref, a_ref, y_ref, ur_scr, w_scr, s_scr, sall_scr, yr_scr, tok_scr):
    rows = ur_scr.shape[0]
    width = CHUNK * SSM_CH
    pw = PAIR * width
    npair = S5_SLAB_GROUPS // PAIR
    half_steps = CHUNK // 2
    tok_rows = S5_ROW_CHUNK * CHUNK

    nchunks = rows // S5_ROW_CHUNK
    nsub = S5_ROW_CHUNK // S5_SUB_ROWS
    sw = 4 * LANES
    row_chunk = lambda c: slice(S5_ROW_CHUNK * c, S5_ROW_CHUNK * (c + 1))
    tok_chunk = lambda c: slice(tok_rows * c, tok_rows * (c + 1))
    step_rows = lambda i, h, kk: pl.ds(S5_SUB_ROWS * i * CHUNK + half_steps * h + kk, S5_SUB_ROWS,
                                       stride=CHUNK)
    lane_blk = lambda g, h: slice(width * g + LANES * h, width * g + LANES * (h + 1))

    def relayout_in(c):
        tok = tok_scr.at[c % 2]
        tok[...] = u_ref[tok_chunk(c), :].astype(F32)
        for i in range(nsub):
            for h in range(2):
                xs = [tok[step_rows(i, h, kk), :] for kk in range(half_steps)]
                ys = _block_transpose(xs)
                for g in range(S5_SLAB_GROUPS):
                    r0 = S5_ROW_CHUNK * c + S5_SUB_ROWS * i
                    ur_scr[r0:r0 + S5_SUB_ROWS, lane_blk(g, h)] = ys[g].astype(BF16)

    def state_in(c, p0):
        for q in range(S5_SCAN_PAIRS):
            pp = p0 + q
            w_scr[row_chunk(c), sw * q:sw * (q + 1)] = jnp.dot(
                ur_scr[row_chunk(c), pw * pp:pw * (pp + 1)], wb_ref[pp], preferred_element_type=F32)

    for p0 in range(0, npair, S5_SCAN_PAIRS):
        for c in range(nchunks):
            if p0 == 0:
                relayout_in(c)
            state_in(c, p0)
        _s5_scan(a_ref.at[p0:p0 + S5_SCAN_PAIRS], w_scr, s_scr)
        sall_scr[:, sw * p0:sw * (p0 + S5_SCAN_PAIRS)] = s_scr[...].astype(BF16)

    def outputs(c):
        yr, tok = yr_scr.at[c % 2], tok_scr.at[c % 2]
        for pp in range(npair):
            y = lax.dot_general(sall_scr[row_chunk(c), sw * pp:sw * (pp + 1)], wct_ref[pp],
                                (((1,), (1,)), ((), ())), preferred_element_type=F32)
            for g in range(PAIR):
                lo = pw * pp + width * g
                yr[:, lo:lo + width] = y[:, width * g:width * (g + 1)] + jnp.dot(
                    ur_scr[row_chunk(c), lo:lo + width], t_ref[PAIR * pp + g], preferred_element_type=F32)
        for i in range(nsub):
            for h in range(2):
                xs = [yr[S5_SUB_ROWS * i:S5_SUB_ROWS * (i + 1), lane_blk(g, h)].astype(BF16)
                      for g in range(S5_SLAB_GROUPS)]
                ys = _block_transpose(xs)
                for kk in range(half_steps):
                    tok[step_rows(i, h, kk), :] = ys[kk].astype(F32)
        y_ref[tok_chunk(c), :] = tok[...].astype(BF16)

    for c in range(nchunks):
        outputs(c)


def _s5(u_tok, wb, tsum, wct, a16):
    T, ssm_w = u_tok.shape
    rows = T // CHUNK
    assert rows % S5_ROW_CHUNK == 0 and S5_SLAB_GROUPS == CHUNK // 2
    width = CHUNK * SSM_CH
    pw = PAIR * width
    npair = S5_SLAB_GROUPS // PAIR
    slab = pl.BlockSpec((T, LANES), lambda s: (0, s))
    return pl.pallas_call(
        _s5_body,
        grid=(ssm_w // LANES,),
        in_specs=[slab,
                  pl.BlockSpec((npair, pw, 4 * LANES), lambda s: (s, 0, 0)),
                  pl.BlockSpec((S5_SLAB_GROUPS, width, width), lambda s: (s, 0, 0)),
                  pl.BlockSpec((npair, pw, 4 * LANES), lambda s: (s, 0, 0)),
                  pl.BlockSpec((npair, 4, SUBLANES, LANES), lambda s: (s, 0, 0, 0))],
        out_specs=slab,
        out_shape=jax.ShapeDtypeStruct((T, ssm_w), BF16),
        scratch_shapes=[pltpu.VMEM((rows, S5_SLAB_GROUPS * width), BF16),
                        pltpu.VMEM((rows, S5_SCAN_PAIRS * 4 * LANES), F32),
                        pltpu.VMEM((rows, S5_SCAN_PAIRS * 4 * LANES), F32),
                        pltpu.VMEM((rows, npair * 4 * LANES), BF16),
                        pltpu.VMEM((2, S5_ROW_CHUNK, S5_SLAB_GROUPS * width), F32),
                        pltpu.VMEM((2, S5_ROW_CHUNK * CHUNK, LANES), F32)],
        compiler_params=_params(("parallel",)),
        name="s5",
    )(u_tok, wb, tsum, wct, a16)


def _merge_body(attn_ref, ys_ref, u_ref, g_ref, x_ref, wba_ref, wglu_ref, bglu_ref, d_ref, wbs_ref,
                wout_ref, gpost_ref, gpre_ref, x1_ref, h2_ref):
    tm, d = x_ref.shape
    u = u_ref[...].reshape(tm, -1).astype(F32)
    y = ys_ref[...].reshape(tm, -1).astype(F32) + d_ref[...] * u
    y = _gelu_tanh(y)
    gl = jnp.dot(y.astype(BF16), wglu_ref[...], preferred_element_type=F32) + bglu_ref[...]
    s = y * jax.nn.sigmoid(gl)
    y_ssm = jnp.dot(s.astype(BF16), wbs_ref[...], preferred_element_type=F32)
    y_attn = jnp.dot(attn_ref[...], wba_ref[...], preferred_element_type=F32)
    merged = g_ref[:, :d].astype(F32) * y_attn + g_ref[:, d:].astype(F32) * y_ssm
    o = jnp.dot(merged.astype(BF16), wout_ref[...], preferred_element_type=F32)
    x1 = x_ref[...] + _rms(o, gpost_ref[...])
    x1_ref[...] = x1
    h2_ref[...] = _rms(x1, gpre_ref[...]).astype(BF16)


def _merge(attn, ys, u, gates, x2, wba, wglu, bglu, ssm_d, wbs, wout, gpost, gpre, seq, tm):
    T, D = x2.shape
    row = lambda w: pl.BlockSpec((tm, w), lambda i: (i, 0))
    chunked = _chunk_rows_spec(tm, seq // tm, u.shape[-1])
    consts = [wba, wglu, bglu, ssm_d, wbs, wout, gpost, gpre]
    return pl.pallas_call(
        _merge_body,
        grid=(T // tm,),
        in_specs=[row(attn.shape[1]), chunked, chunked, row(gates.shape[1]), row(D)]
                 + [_const_spec(c.shape) for c in consts],
        out_specs=[row(D), row(D)],
        out_shape=[jax.ShapeDtypeStruct((T, D), F32), jax.ShapeDtypeStruct((T, D), BF16)],
        compiler_params=_params(("parallel",)),
        name="merge",
    )(attn, ys, u, gates, x2, *consts)


def _ffn_body(tiles_per_seq, hp_ref, hc_ref, hn_ref, x1_ref, wg_ref, wu_ref, cw_ref, cb_ref, wd_ref,
              gn_ref, o_ref, hext, fscr):
    tm = hc_ref.shape[0]
    i = pl.program_id(0) % tiles_per_seq
    zero = jnp.zeros((), BF16)
    hext[0:HALO] = jnp.where(i == 0, zero, hp_ref[...])
    hext[HALO:HALO + tm] = hc_ref[...]
    hext[HALO + tm:] = jnp.where(i == tiles_per_seq - 1, zero, hn_ref[...])
    ext = tm + 2 * HALO
    for c in range(wg_ref.shape[1] // FF_TILE):
        sl = slice(FF_TILE * c, FF_TILE * (c + 1))
        g = jnp.dot(hext[...], wg_ref[:, sl], preferred_element_type=F32)
        up = jnp.dot(hc_ref[...], wu_ref[:, sl], preferred_element_type=F32)
        g_prev = pltpu.roll(g, 1, 0)[HALO:HALO + tm]
        g_next = pltpu.roll(g, ext - 1, 0)[HALO:HALO + tm]
        conv = (g_prev * cw_ref[0:1, sl] + g[HALO:HALO + tm] * cw_ref[1:2, sl]
                + g_next * cw_ref[2:3, sl] + cb_ref[:, sl])
        fscr[:, sl] = (_gelu_tanh(conv) * up).astype(BF16)
    o = jnp.dot(fscr[...], wd_ref[...], preferred_element_type=F32)
    o_ref[...] = x1_ref[...] + _rms(o, gn_ref[...])


def _ffn(h2, x1, wg, wu, cw, cb, wd, gn, seq, tm):
    T, D = x1.shape
    dff = wg.shape[1]
    assert dff % FF_TILE == 0 and tm % HALO == 0 and seq % tm == 0
    per = tm // HALO
    nh = T // HALO
    row = lambda w: pl.BlockSpec((tm, w), lambda i: (i, 0))
    once = lambda shape: pl.BlockSpec(shape, lambda i: (0,) * len(shape), pipeline_mode=pl.Buffered(1))
    return pl.pallas_call(
        functools.partial(_ffn_body, seq // tm),
        grid=(T // tm,),
        in_specs=[pl.BlockSpec((HALO, D), lambda i: (jnp.maximum(i * per - 1, 0), 0)),
                  row(D),
                  pl.BlockSpec((HALO, D), lambda i: (jnp.minimum((i + 1) * per, nh - 1), 0)),
                  row(D),
                  once(wg.shape), once(wu.shape), _const_spec(cw.shape), _const_spec(cb.shape),
                  once(wd.shape), _const_spec(gn.shape)],
        out_specs=row(D),
        out_shape=jax.ShapeDtypeStruct((T, D), F32),
        scratch_shapes=[pltpu.VMEM((tm + 2 * HALO, D), BF16), pltpu.VMEM((tm, dff), BF16)],
        compiler_params=_params(("parallel",)),
        name="ffn",
    )(h2, h2, h2, x1, wg, wu, cw, cb, wd, gn)


def _row_tile(seq):
    return min(1024, seq)


def _layer(x, p):
    bsz, seq, d = x.shape
    T = bsz * seq
    tm = _row_tile(seq)
    ssm_w = d // 2
    ngroups = ssm_w // SSM_CH
    attn_w = N_HEADS * HEAD_DIM
    row2 = lambda v: v.reshape(1, -1).astype(F32)

    npair = ngroups // PAIR
    both = lambda re, im: jnp.stack([jnp.stack([p[re + "_f"], p[im + "_f"]]),
                                     jnp.stack([p[re + "_b"], p[im + "_b"]])]).astype(F32)
    lam = both("lam_re", "lam_im").reshape(2, 2, npair, 1, LANES).transpose(0, 2, 1, 3, 4)
    ldt = jnp.repeat(jnp.stack([p["log_dt_f"], p["log_dt_b"]]).astype(F32), SSM_STATE, axis=-1)
    ldt = ldt.reshape(2, npair, 1, LANES)
    bt = both("b_re", "b_im").reshape(2, 2, npair, PAIR, SSM_STATE, SSM_CH)
    bt = bt.transpose(0, 2, 1, 5, 3, 4).reshape(2, npair, 2, SSM_CH, LANES)
    cc = both("c_re", "c_im").reshape(2, 2, npair, PAIR, SSM_CH, SSM_STATE)
    cc = cc.transpose(0, 2, 1, 4, 3, 5).reshape(2, npair, 2, SSM_CH, LANES)
    wb, wct, tsum, a, w_in = _s5prep(lam, ldt, bt, cc, p["w_in"])

    x2 = x.reshape(T, d)
    q, k, ks, v, u, gates = _inproj(x2, row2(p["norm_mix_pre"]), w_in, bsz, seq, tm)

    later = ("w_ffn_gate", "w_ffn_up", "w_ffn_down", "w_out", "w_branch_attn", "w_branch_ssm", "w_glu")
    attn, *rounded = _attention(q, k, ks, v, jnp.asarray(_attn_bias()), p["attn_sink"].astype(F32), bsz, seq,
                                [p[name] for name in later])
    w16 = dict(zip(later, rounded))

    assert bsz == SUBLANES // 2
    ys = _s5(u.reshape(T, ssm_w), wb, tsum, wct, a).reshape(u.shape)

    x1, h2 = _merge(attn, ys, u, gates, x2,
                    w16["w_branch_attn"], w16["w_glu"], row2(p["b_glu"]),
                    row2(p["ssm_d"]), w16["w_branch_ssm"], w16["w_out"],
                    row2(p["norm_mix_post"]), row2(p["norm_ffn_pre"]), seq, tm)

    out = _ffn(h2, x1, w16["w_ffn_gate"], w16["w_ffn_up"], p["conv_w"].astype(F32),
               row2(p["conv_b"]), w16["w_ffn_down"], row2(p["norm_ffn_post"]), seq, tm)
    return out.reshape(bsz, seq, d)


_PARAM_NAMES = (
    "norm_mix_pre", "w_in", "attn_sink",
    "lam_re_f", "lam_im_f", "log_dt_f", "b_re_f", "b_im_f", "c_re_f", "c_im_f",
    "lam_re_b", "lam_im_b", "log_dt_b", "b_re_b", "b_im_b", "c_re_b", "c_im_b",
    "ssm_d", "w_glu", "b_glu", "w_branch_attn", "w_branch_ssm", "w_out", "norm_mix_post",
    "norm_ffn_pre", "w_ffn_gate", "w_ffn_up", "conv_w", "conv_b", "w_ffn_down", "norm_ffn_post")


def kernel(x, norm_mix_pre, w_in, attn_sink, lam_re_f, lam_im_f, log_dt_f, b_re_f, b_im_f, c_re_f, c_im_f, lam_re_b, lam_im_b, log_dt_b, b_re_b, b_im_b, c_re_b, c_im_b, ssm_d, w_glu, b_glu, w_branch_attn, w_branch_ssm, w_out, norm_mix_post, norm_ffn_pre, w_ffn_gate, w_ffn_up, conv_w, conv_b, w_ffn_down, norm_ffn_post):
    stacked = dict(zip(_PARAM_NAMES, (
        norm_mix_pre, w_in, attn_sink,
        lam_re_f, lam_im_f, log_dt_f, b_re_f, b_im_f, c_re_f, c_im_f,
        lam_re_b, lam_im_b, log_dt_b, b_re_b, b_im_b, c_re_b, c_im_b,
        ssm_d, w_glu, b_glu, w_branch_attn, w_branch_ssm, w_out, norm_mix_post,
        norm_ffn_pre, w_ffn_gate, w_ffn_up, conv_w, conv_b, w_ffn_down, norm_ffn_post)))
    for layer in range(w_in.shape[0]):
        x = _layer(x, {name: value[layer] for name, value in stacked.items()})
    return x
```

```python
import functools
import math

import numpy as np
import jax
import jax.numpy as jnp
from jax import lax
from jax.experimental import pallas as pl
from jax.experimental.pallas import tpu as pltpu

F32 = jnp.float32
BF16 = jnp.bfloat16

N_HEADS = 8
N_KV_HEADS = 2
HEAD_DIM = 64
GROUP = N_HEADS // N_KV_HEADS
WINDOW = 128
BLOCK = 128
ATTN_SUB_BLOCKS = 8
SSM_CH = 16
SSM_STATE = 64
CHUNK = 16
PAIR = 2
N_BRANCH = 2
EPS = 1e-6
NEG_INF = -1e30
LOG2E = math.log2(math.e)

LANES = 128
SUBLANES = 8
HALO = 16
FF_TILE = 256
VMEM_LIMIT = 56 * 1024 * 1024


def _gelu_tanh(x):
    return 0.5 * x * (1.0 + jnp.tanh(math.sqrt(2.0 / math.pi) * (x + 0.044715 * (x * x * x))))


def _rms(x, gain):
    return x * lax.rsqrt(jnp.mean(x * x, axis=-1, keepdims=True) + EPS) * gain


def _params(semantics):
    return pltpu.CompilerParams(dimension_semantics=semantics, vmem_limit_bytes=VMEM_LIMIT)


def _const_spec(shape):
    nd = len(shape)
    return pl.BlockSpec(shape, lambda *_: (0,) * nd)


def _inproj_body(splits, x_ref, g_ref, w_ref, q_ref, k_ref, ks_ref, v_ref, u_ref, gate_ref):
    h = _rms(x_ref[...], g_ref[...]).astype(BF16)

    def proj(i):
        return jnp.dot(h, w_ref[:, splits[i]:splits[i + 1]], preferred_element_type=F32)

    q_ref[...] = (proj(0) * (HEAD_DIM ** -0.5 * LOG2E)).astype(BF16)
    k = proj(1).astype(BF16)
    k_ref[...] = k
    ks_ref[...] = pltpu.roll(k, HEAD_DIM, 1)
    v_ref[...] = proj(2).astype(BF16)
    u_ref[...] = proj(3).astype(BF16).reshape(u_ref.shape)
    gate_ref[...] = jax.nn.sigmoid(proj(4)).astype(BF16)


def _chunk_rows_spec(tm, tiles_per_seq, width):
    return pl.BlockSpec((tm // CHUNK, None, CHUNK, width),
                        lambda i: (i % tiles_per_seq, i // tiles_per_seq, 0, 0))


def _inproj(x2, gain, w_in, bsz, seq, tm):
    T, D = x2.shape
    attn_w = N_HEADS * HEAD_DIM
    kv_w = N_KV_HEADS * HEAD_DIM
    widths = [attn_w, kv_w, kv_w, D // 2, N_BRANCH * D]
    splits = [0] + np.cumsum(widths).tolist()
    row = lambda w: pl.BlockSpec((tm, w), lambda i: (i, 0))
    out_widths = [attn_w, kv_w, kv_w, kv_w, D // 2, N_BRANCH * D]
    out_specs = [row(w) for w in out_widths]
    out_shape = [jax.ShapeDtypeStruct((T, w), BF16) for w in out_widths]
    out_specs[4] = _chunk_rows_spec(tm, seq // tm, out_widths[4])
    out_shape[4] = jax.ShapeDtypeStruct((seq // CHUNK, bsz, CHUNK, out_widths[4]), BF16)
    return pl.pallas_call(
        functools.partial(_inproj_body, tuple(splits)),
        grid=(T // tm,),
        in_specs=[row(D), _const_spec((1, D)), _const_spec(w_in.shape)],
        out_specs=out_specs,
        out_shape=out_shape,
        compiler_params=_params(("parallel",)),
        name="inproj",
    )(x2, gain, w_in)


def _cast_specs(weights, steps, step_of):
    specs = []
    for w in weights:
        nblocks = max(n for n in range(1, steps + 1)
                      if steps % n == 0 and w.shape[0] % (n * HALO) == 0)
        hold = steps // nblocks
        specs.append(pl.BlockSpec((w.shape[0] // nblocks, w.shape[1]),
                                  lambda *ids, hold=hold: (step_of(*ids) // hold, 0)))
    return specs, specs, [jax.ShapeDtypeStruct(w.shape, BF16) for w in weights]


def _attn_body(ncast, q_ref, kp_ref, kc_ref, kn_ref, sp_ref, sc_ref, sn_ref, vp_ref, vc_ref, vn_ref,
               bias_first_ref, bias_mid_ref, bias_last_ref, sink_ref, *refs):
    o_ref = refs[ncast]
    for src, dst in zip(refs[:ncast], refs[ncast + 1:]):
        dst[...] = src[...].astype(BF16)
    bias_refs = [bias_first_ref] + [bias_mid_ref] * (ATTN_SUB_BLOCKS - 2) + [bias_last_ref]
    lane = lax.broadcasted_iota(jnp.int32, (1, LANES), 1)
    half_mask = [lane < HEAD_DIM, lane >= HEAD_DIM]
    zero = jnp.zeros((), BF16)
    kall = [jnp.concatenate([kp_ref[...], kc_ref[...], kn_ref[...]], axis=0),
            jnp.concatenate([sp_ref[...], sc_ref[...], sn_ref[...]], axis=0)]
    vall = jnp.concatenate([vp_ref[...], vc_ref[...], vn_ref[...]], axis=0)
    for b, bias_ref in enumerate(bias_refs):
        keys = slice(BLOCK * b, BLOCK * (b + 3))
        vcat = vall[keys]
        rows = slice(BLOCK * b, BLOCK * (b + 1))
        for j in range(N_HEADS // 2):
            qj = q_ref[rows, LANES * j:LANES * (j + 1)]
            g = (2 * j) // GROUP
            halves = []
            for e in range(2):
                h = 2 * j + e
                qm = jnp.where(half_mask[e], qj, zero)
                kcat = kall[0 if e == g else 1][keys]
                st = lax.dot_general(kcat, qm, (((1,), (1,)), ((), ())), preferred_element_type=F32)
                st = st + bias_ref[h]
                sink = sink_ref[h] * LOG2E
                m = jnp.maximum(jnp.max(st, axis=0, keepdims=True), sink)
                p = jnp.exp2(st - m)
                denom = jnp.sum(p, axis=0, keepdims=True) + jnp.exp2(sink - m)
                ot = lax.dot_general(vcat, p.astype(BF16), (((0,), (0,)), ((), ())),
                                     preferred_element_type=F32)
                halves.append(ot[HEAD_DIM * g:HEAD_DIM * (g + 1)] / denom)
            o_ref[rows, LANES * j:LANES * (j + 1)] = jnp.concatenate(halves, axis=0).T.astype(BF16)


def _attn_bias():
    t = np.arange(BLOCK)[None, :]
    s = np.arange(3 * BLOCK)[:, None] - BLOCK
    dist = np.abs(t - s).astype(np.float64)
    valid = dist <= WINDOW
    slopes = 2.0 ** (-8.0 * (np.arange(N_HEADS) + 1.0) / N_HEADS)
    out = np.zeros((3, N_HEADS, 3 * BLOCK, BLOCK), np.float32)
    for variant in range(3):
        ok = valid.copy()
        if variant == 0:
            ok[:BLOCK] = False
        if variant == 2:
            ok[2 * BLOCK:] = False
        for h in range(N_HEADS):
            out[variant, h] = np.where(ok, -slopes[h] * dist * LOG2E, NEG_INF)
    return out


def _attention(q, k, ks, v, bias, sink, bsz, seq, cast_views):
    nb = seq // BLOCK
    sub = ATTN_SUB_BLOCKS
    assert nb % sub == 0
    ns = nb // sub
    kvw = k.shape[1]
    aw = q.shape[1]

    def edge_spec(off):
        return pl.BlockSpec((BLOCK, kvw), lambda b, n: (b * nb + jnp.clip(sub * n + off, 0, nb - 1), 0))

    centre = pl.BlockSpec((sub * BLOCK, kvw), lambda b, n: (b * ns + n, 0))
    bias_spec = lambda pick: pl.BlockSpec((None,) + bias.shape[1:], lambda b, n: (pick(n), 0, 0, 0))
    first = bias_spec(lambda n: jnp.where(n == 0, 0, 1))
    middle = bias_spec(lambda n: 1)
    last = bias_spec(lambda n: jnp.where(n == ns - 1, 2, 1))
    qspec = pl.BlockSpec((sub * BLOCK, aw), lambda b, n: (b * ns + n, 0))
    cast_in, cast_out, cast_shape = _cast_specs(cast_views, bsz * ns, lambda b, n: b * ns + n)
    return pl.pallas_call(
        functools.partial(_attn_body, len(cast_views)),
        grid=(bsz, ns),
        in_specs=[qspec] + [edge_spec(-1), centre, edge_spec(sub)] * 3 + [
                  first, middle, last,
                  pl.BlockSpec(memory_space=pltpu.SMEM)] + cast_in,
        out_specs=[qspec] + cast_out,
        out_shape=[jax.ShapeDtypeStruct(q.shape, BF16)] + cast_shape,
        compiler_params=_params(("parallel", "arbitrary")),
        name="attn",
    )(q, k, k, k, ks, ks, ks, v, v, v, bias, bias, bias, sink, *cast_views)


def _s5prep_body(lam_ref, ldt_ref, bt_ref, c_ref, w32_ref, wb_ref, wct_ref, t_ref, a_ref, w16_ref):
    w16_ref[...] = w32_ref[...].astype(BF16)
    width = CHUNK * SSM_CH
    lane = lax.broadcasted_iota(jnp.int32, (1, LANES), 1)
    in_group = [lane < SSM_STATE, lane >= SSM_STATE]
    lane_t = lax.broadcasted_iota(jnp.int32, (SSM_CH, width), 1)
    contract = (((1,), (1,)), ((), ()))
    hi = lax.Precision.HIGHEST
    toeplitz = [[None] * CHUNK for _ in range(PAIR)]
    for d in range(2):
        lre, lim = lam_ref[d, 0], lam_ref[d, 1]
        dt = jnp.exp(ldt_ref[d])
        mag = jnp.exp(lre * dt)
        are, aim = mag * jnp.cos(lim * dt), mag * jnp.sin(lim * dt)
        den = lre * lre + lim * lim
        cfr = ((are - 1.0) * lre + aim * lim) / den
        cfi = (aim * lre - (are - 1.0) * lim) / den
        btr, bti = bt_ref[d, 0], bt_ref[d, 1]
        bbr = btr * cfr - bti * cfi
        bbi = btr * cfi + bti * cfr
        cre, cim = c_ref[d, 0], c_ref[d, 1]

        pwr, pwi = jnp.ones_like(are), jnp.zeros_like(are)
        e_r, e_i = [], []
        for p in range(CHUNK + 1):
            er = cre * pwr - cim * pwi
            ei = cre * pwi + cim * pwr
            if p < CHUNK:
                e_r.append(er)
                e_i.append(ei)
                k = CHUNK - 1 - p if d == 0 else p
                for ri, val in enumerate((bbr * pwr - bbi * pwi, bbr * pwi + bbi * pwr)):
                    for g in range(PAIR):
                        wb_ref[width * g + SSM_CH * k:width * g + SSM_CH * (k + 1),
                               LANES * (2 * d + ri):LANES * (2 * d + ri + 1)] = jnp.where(
                                   in_group[g], val, 0.0).astype(BF16)
            if p >= 1:
                r = p - 1 if d == 0 else CHUNK - p
                for ri, val in enumerate((er, -ei)):
                    for g in range(PAIR):
                        wct_ref[width * g + SSM_CH * r:width * g + SSM_CH * (r + 1),
                                LANES * (2 * d + ri):LANES * (2 * d + ri + 1)] = jnp.where(
                                    in_group[g], val, 0.0).astype(BF16)
            if p == CHUNK:
                a_ref[2 * d] = jnp.broadcast_to(pwr, a_ref.shape[1:])
                a_ref[2 * d + 1] = jnp.broadcast_to(pwi, a_ref.shape[1:])
            pwr, pwi = pwr * are - pwi * aim, pwr * aim + pwi * are

        if d == 1:
            e_r, e_i = e_r[::-1], e_i[::-1]
        ecat_r, ecat_i = jnp.concatenate(e_r, axis=0), jnp.concatenate(e_i, axis=0)
        for g in range(PAIR):
            gr, gi = jnp.where(in_group[g], bbr, 0.0), jnp.where(in_group[g], bbi, 0.0)
            krow = (lax.dot_general(gr, ecat_r, contract, precision=hi, preferred_element_type=F32)
                    - lax.dot_general(gi, ecat_i, contract, precision=hi, preferred_element_type=F32))
            for k in range(CHUNK):
                if d == 0:
                    shift, keep = SSM_CH * k, lane_t >= SSM_CH * k
                else:
                    shift, keep = (SSM_CH * (k + 1)) % width, lane_t < SSM_CH * (k + 1)
                blk = jnp.where(keep, pltpu.roll(krow, shift, 1) if shift else krow, 0.0)
                toeplitz[g][k] = blk if d == 0 else toeplitz[g][k] + blk
    for g in range(PAIR):
        t_ref[g] = jnp.concatenate(toeplitz[g], axis=0).astype(BF16)


def _s5prep(lam, ldt, bt, c, w32):
    npair = lam.shape[1]
    width = CHUNK * SSM_CH
    pw = PAIR * width
    cast_in, cast_out, cast_shape = _cast_specs([w32], npair, lambda i: i)
    return pl.pallas_call(
        _s5prep_body,
        grid=(npair,),
        in_specs=[pl.BlockSpec((2, None, 2, 1, LANES), lambda i: (0, i, 0, 0, 0)),
                  pl.BlockSpec((2, None, 1, LANES), lambda i: (0, i, 0, 0)),
                  pl.BlockSpec((2, None, 2, SSM_CH, LANES), lambda i: (0, i, 0, 0, 0)),
                  pl.BlockSpec((2, None, 2, SSM_CH, LANES), lambda i: (0, i, 0, 0, 0))] + cast_in,
        out_specs=[pl.BlockSpec((None, pw, 4 * LANES), lambda i: (i, 0, 0)),
                   pl.BlockSpec((None, pw, 4 * LANES), lambda i: (i, 0, 0)),
                   pl.BlockSpec((PAIR, width, width), lambda i: (i, 0, 0)),
                   pl.BlockSpec((None, 4, SUBLANES, LANES), lambda i: (i, 0, 0, 0))] + cast_out,
        out_shape=[jax.ShapeDtypeStruct((npair, pw, 4 * LANES), BF16),
                   jax.ShapeDtypeStruct((npair, pw, 4 * LANES), BF16),
                   jax.ShapeDtypeStruct((npair * PAIR, width, width), BF16),
                   jax.ShapeDtypeStruct((npair, 4, SUBLANES, LANES), F32)] + cast_shape,
        compiler_params=_params(("parallel",)),
        name="s5prep",
    )(lam, ldt, bt, c, w32)


def _block_transpose(xs):
    n = len(xs)
    assert n * SSM_CH == LANES and n & (n - 1) == 0
    blk = lax.broadcasted_iota(jnp.int32, xs[0].shape, 1) // SSM_CH
    d = n // 2
    while d:
        upper = (blk & d) != 0
        ys = list(xs)
        for a in range(n):
            if a & d:
                continue
            b = a + d
            ys[a] = jnp.where(upper, pltpu.roll(xs[b], SSM_CH * d, 1), xs[a])
            ys[b] = jnp.where(upper, xs[b], pltpu.roll(xs[a], LANES - SSM_CH * d, 1))
        xs = ys
        d //= 2
    return xs


def _s5_scan(a_ref, w_scr, s_scr):
    rows = w_scr.shape[0]
    nsteps = rows // SUBLANES
    nstream = w_scr.shape[1] // (2 * LANES)
    half = SUBLANES // 2
    lo = lax.broadcasted_iota(jnp.int32, (SUBLANES, LANES), 0) < half
    first = [lo, jnp.logical_not(lo)]
    coef = [(a_ref[s // 2, 2 * (s % 2)], a_ref[s // 2, 2 * (s % 2) + 1]) for s in range(nstream)]
    col = lambda i: slice(LANES * i, LANES * (i + 1))

    def advance(a, xr, xi, wr, wi):
        return a[0] * xr - a[1] * xi + wr, a[0] * xi + a[1] * xr + wi

    def swap(x):
        return pltpu.roll(x, half, 0)

    def step(i, carry):
        out = []
        for s in range(nstream):
            d = s % 2
            cr, ci = carry[2 * s], carry[2 * s + 1]
            vreg = i if d == 0 else nsteps - 1 - i
            rs = pl.ds(pl.multiple_of(vreg * SUBLANES, SUBLANES), SUBLANES)
            wr, wi = w_scr[rs, col(2 * s)], w_scr[rs, col(2 * s + 1)]
            r0r, r0i = swap(cr), swap(ci)
            t1r, t1i = advance(coef[s], r0r, r0i, wr, wi)
            r1r, r1i = swap(t1r), swap(t1i)
            t2r, t2i = advance(coef[s], r1r, r1i, wr, wi)
            s_scr[rs, col(2 * s)] = jnp.where(first[d], r0r, r1r)
            s_scr[rs, col(2 * s + 1)] = jnp.where(first[d], r0i, r1i)
            out += [jnp.where(first[d], t1r, t2r), jnp.where(first[d], t1i, t2i)]
        return tuple(out)

    z = jnp.zeros((SUBLANES, LANES), F32)
    lax.fori_loop(0, nsteps, step, (z,) * (2 * nstream))


S5_SLAB_GROUPS = LANES // SSM_CH
S5_ROW_CHUNK = 256
S5_SUB_ROWS = 128
S5_SCAN_PAIRS = 2
S5_TOK_PITCH = CHUNK + SUBLANES


def _s5_body(u_ref, wb_ref, t_ref, wct_ref, a_ref, y_ref, ur_scr, w_scr, s_scr, sall_scr, yr_scr, tok_scr):
    rows = ur_scr.shape[0]
    width = CHUNK * SSM_CH
    pw = PAIR * width
    npair = S5_SLAB_GROUPS // PAIR
    half_steps = CHUNK // 2
    tok_rows = S5_ROW_CHUNK * CHUNK

    nchunks = rows // S5_ROW_CHUNK
    nsub = S5_ROW_CHUNK // S5_SUB_ROWS
    sw = 4 * LANES
    row_chunk = lambda c: slice(S5_ROW_CHUNK * c, S5_ROW_CHUNK * (c + 1))
    tok_chunk = lambda c: slice(tok_rows * c, tok_rows * (c + 1))
    step_rows = lambda i, step: pl.ds(S5_TOK_PITCH * S5_SUB_ROWS * i + step, S5_SUB_ROWS, stride=S5_TOK_PITCH)

    def copy_chunk_rows(c, tok, to_tok):
        for r in range(S5_ROW_CHUNK):
            ext = slice(tok_rows * c + r * CHUNK, tok_rows * c + (r + 1) * CHUNK)
            pad = slice(r * S5_TOK_PITCH, r * S5_TOK_PITCH + CHUNK)
            if to_tok:
                tok[pad, :] = u_ref[ext, :].astype(F32)
            else:
                y_ref[ext, :] = tok[pad, :].astype(BF16)
    lane_blk = lambda g, h: slice(width * g + LANES * h, width * g + LANES * (h + 1))

    def relayout_in(c):
        tok = tok_scr.at[c % 2]
        copy_chunk_rows(c, tok, True)
        for i in range(nsub):
            for h in range(2):
                xs = [tok[step_rows(i, half_steps * h + kk), :] for kk in range(half_steps)]
                ys = _block_transpose(xs)
                for g in range(S5_SLAB_GROUPS):
                    r0 = S5_ROW_CHUNK * c + S5_SUB_ROWS * i
                    ur_scr[r0:r0 + S5_SUB_ROWS, lane_blk(g, h)] = ys[g].astype(BF16)

    def state_in(c, p0):
        for q in range(S5_SCAN_PAIRS):
            pp = p0 + q
            w_scr[row_chunk(c), sw * q:sw * (q + 1)] = jnp.dot(
                ur_scr[row_chunk(c), pw * pp:pw * (pp + 1)], wb_ref[pp], preferred_element_type=F32)

    for p0 in range(0, npair, S5_SCAN_PAIRS):
        for c in range(nchunks):
            if p0 == 0:
                relayout_in(c)
            state_in(c, p0)
        _s5_scan(a_ref.at[p0:p0 + S5_SCAN_PAIRS], w_scr, s_scr)
        sall_scr[:, sw * p0:sw * (p0 + S5_SCAN_PAIRS)] = s_scr[...].astype(BF16)

    def outputs(c):
        yr, tok = yr_scr.at[c % 2], tok_scr.at[c % 2]
        for pp in range(npair):
            y = lax.dot_general(sall_scr[row_chunk(c), sw * pp:sw * (pp + 1)], wct_ref[pp],
                                (((1,), (1,)), ((), ())), preferred_element_type=F32)
            for g in range(PAIR):
                lo = pw * pp + width * g
                yr[:, lo:lo + width] = y[:, width * g:width * (g + 1)] + jnp.dot(
                    ur_scr[row_chunk(c), lo:lo + width], t_ref[PAIR * pp + g], preferred_element_type=F32)
        for i in range(nsub):
            for h in range(2):
                xs = [yr[S5_SUB_ROWS * i:S5_SUB_ROWS * (i + 1), lane_blk(g, h)].astype(BF16)
                      for g in range(S5_SLAB_GROUPS)]
                ys = _block_transpose(xs)
                for kk in range(half_steps):
                    tok[step_rows(i, half_steps * h + kk), :] = ys[kk].astype(F32)
        copy_chunk_rows(c, tok, False)

    for c in range(nchunks):
        outputs(c)


def _s5(u_tok, wb, tsum, wct, a16):
    T, ssm_w = u_tok.shape
    rows = T // CHUNK
    assert rows % S5_ROW_CHUNK == 0 and S5_SLAB_GROUPS == CHUNK // 2
    width = CHUNK * SSM_CH
    pw = PAIR * width
    npair = S5_SLAB_GROUPS // PAIR
    slab = pl.BlockSpec((T, LANES), lambda s: (0, s))
    return pl.pallas_call(
        _s5_body,
        grid=(ssm_w // LANES,),
        in_specs=[slab,
                  pl.BlockSpec((npair, pw, 4 * LANES), lambda s: (s, 0, 0)),
                  pl.BlockSpec((S5_SLAB_GROUPS, width, width), lambda s: (s, 0, 0)),
                  pl.BlockSpec((npair, pw, 4 * LANES), lambda s: (s, 0, 0)),
                  pl.BlockSpec((npair, 4, SUBLANES, LANES), lambda s: (s, 0, 0, 0))],
        out_specs=slab,
        out_shape=jax.ShapeDtypeStruct((T, ssm_w), BF16),
        scratch_shapes=[pltpu.VMEM((rows, S5_SLAB_GROUPS * width), BF16),
                        pltpu.VMEM((rows, S5_SCAN_PAIRS * 4 * LANES), F32),
                        pltpu.VMEM((rows, S5_SCAN_PAIRS * 4 * LANES), F32),
                        pltpu.VMEM((rows, npair * 4 * LANES), BF16),
                        pltpu.VMEM((2, S5_ROW_CHUNK, S5_SLAB_GROUPS * width), F32),
                        pltpu.VMEM((2, S5_ROW_CHUNK * S5_TOK_PITCH, LANES), F32)],
        compiler_params=_params(("parallel",)),
        name="s5",
    )(u_tok, wb, tsum, wct, a16)


def _merge_body(attn_ref, ys_ref, u_ref, g_ref, x_ref, wba_ref, wglu_ref, bglu_ref, d_ref, wbs_ref,
                wout_ref, gpost_ref, gpre_ref, x1_ref, h2_ref):
    tm, d = x_ref.shape
    u = u_ref[...].reshape(tm, -1).astype(F32)
    y = ys_ref[...].reshape(tm, -1).astype(F32) + d_ref[...] * u
    y = _gelu_tanh(y)
    gl = jnp.dot(y.astype(BF16), wglu_ref[...], preferred_element_type=F32) + bglu_ref[...]
    s = y * jax.nn.sigmoid(gl)
    y_ssm = jnp.dot(s.astype(BF16), wbs_ref[...], preferred_element_type=F32)
    y_attn = jnp.dot(attn_ref[...], wba_ref[...], preferred_element_type=F32)
    merged = g_ref[:, :d].astype(F32) * y_attn + g_ref[:, d:].astype(F32) * y_ssm
    o = jnp.dot(merged.astype(BF16), wout_ref[...], preferred_element_type=F32)
    x1 = x_ref[...] + _rms(o, gpost_ref[...])
    x1_ref[...] = x1
    h2_ref[...] = _rms(x1, gpre_ref[...]).astype(BF16)


def _merge(attn, ys, u, gates, x2, wba, wglu, bglu, ssm_d, wbs, wout, gpost, gpre, seq, tm):
    T, D = x2.shape
    row = lambda w: pl.BlockSpec((tm, w), lambda i: (i, 0))
    chunked = _chunk_rows_spec(tm, seq // tm, u.shape[-1])
    consts = [wba, wglu, bglu, ssm_d, wbs, wout, gpost, gpre]
    return pl.pallas_call(
        _merge_body,
        grid=(T // tm,),
        in_specs=[row(attn.shape[1]), chunked, chunked, row(gates.shape[1]), row(D)]
                 + [_const_spec(c.shape) for c in consts],
        out_specs=[row(D), row(D)],
        out_shape=[jax.ShapeDtypeStruct((T, D), F32), jax.ShapeDtypeStruct((T, D), BF16)],
        compiler_params=_params(("parallel",)),
        name="merge",
    )(attn, ys, u, gates, x2, *consts)


def _ffn_body(tiles_per_seq, hp_ref, hc_ref, hn_ref, x1_ref, wg_ref, wu_ref, cw_ref, cb_ref, wd_ref,
              gn_ref, o_ref, hext, fscr):
    tm = hc_ref.shape[0]
    i = pl.program_id(0) % tiles_per_seq
    zero = jnp.zeros((), BF16)
    hext[0:HALO] = jnp.where(i == 0, zero, hp_ref[...])
    hext[HALO:HALO + tm] = hc_ref[...]
    hext[HALO + tm:] = jnp.where(i == tiles_per_seq - 1, zero, hn_ref[...])
    ext = tm + 2 * HALO
    for c in range(wg_ref.shape[1] // FF_TILE):
        sl = slice(FF_TILE * c, FF_TILE * (c + 1))
        g = jnp.dot(hext[...], wg_ref[:, sl], preferred_element_type=F32)
        up = jnp.dot(hc_ref[...], wu_ref[:, sl], preferred_element_type=F32)
        g_prev = pltpu.roll(g, 1, 0)[HALO:HALO + tm]
        g_next = pltpu.roll(g, ext - 1, 0)[HALO:HALO + tm]
        conv = (g_prev * cw_ref[0:1, sl] + g[HALO:HALO + tm] * cw_ref[1:2, sl]
                + g_next * cw_ref[2:3, sl] + cb_ref[:, sl])
        fscr[:, sl] = (_gelu_tanh(conv) * up).astype(BF16)
    o = jnp.dot(fscr[...], wd_ref[...], preferred_element_type=F32)
    o_ref[...] = x1_ref[...] + _rms(o, gn_ref[...])


def _ffn(h2, x1, wg, wu, cw, cb, wd, gn, seq, tm):
    T, D = x1.shape
    dff = wg.shape[1]
    assert dff % FF_TILE == 0 and tm % HALO == 0 and seq % tm == 0
    per = tm // HALO
    nh = T // HALO
    row = lambda w: pl.BlockSpec((tm, w), lambda i: (i, 0))
    once = lambda shape: pl.BlockSpec(shape, lambda i: (0,) * len(shape), pipeline_mode=pl.Buffered(1))
    return pl.pallas_call(
        functools.partial(_ffn_body, seq // tm),
        grid=(T // tm,),
        in_specs=[pl.BlockSpec((HALO, D), lambda i: (jnp.maximum(i * per - 1, 0), 0)),
                  row(D),
                  pl.BlockSpec((HALO, D), lambda i: (jnp.minimum((i + 1) * per, nh - 1), 0)),
                  row(D),
                  once(wg.shape), once(wu.shape), _const_spec(cw.shape), _const_spec(cb.shape),
                  once(wd.shape), _const_spec(gn.shape)],
        out_specs=row(D),
        out_shape=jax.ShapeDtypeStruct((T, D), F32),
        scratch_shapes=[pltpu.VMEM((tm + 2 * HALO, D), BF16), pltpu.VMEM((tm, dff), BF16)],
        compiler_params=_params(("parallel",)),
        name="ffn",
    )(h2, h2, h2, x1, wg, wu, cw, cb, wd, gn)


def _row_tile(seq):
    return min(1024, seq)


def _layer(x, p):
    bsz, seq, d = x.shape
    T = bsz * seq
    tm = _row_tile(seq)
    ssm_w = d // 2
    ngroups = ssm_w // SSM_CH
    attn_w = N_HEADS * HEAD_DIM
    row2 = lambda v: v.reshape(1, -1).astype(F32)

    npair = ngroups // PAIR
    both = lambda re, im: jnp.stack([jnp.stack([p[re + "_f"], p[im + "_f"]]),
                                     jnp.stack([p[re + "_b"], p[im + "_b"]])]).astype(F32)
    lam = both("lam_re", "lam_im").reshape(2, 2, npair, 1, LANES).transpose(0, 2, 1, 3, 4)
    ldt = jnp.repeat(jnp.stack([p["log_dt_f"], p["log_dt_b"]]).astype(F32), SSM_STATE, axis=-1)
    ldt = ldt.reshape(2, npair, 1, LANES)
    bt = both("b_re", "b_im").reshape(2, 2, npair, PAIR, SSM_STATE, SSM_CH)
    bt = bt.transpose(0, 2, 1, 5, 3, 4).reshape(2, npair, 2, SSM_CH, LANES)
    cc = both("c_re", "c_im").reshape(2, 2, npair, PAIR, SSM_CH, SSM_STATE)
    cc = cc.transpose(0, 2, 1, 4, 3, 5).reshape(2, npair, 2, SSM_CH, LANES)
    wb, wct, tsum, a, w_in = _s5prep(lam, ldt, bt, cc, p["w_in"])

    x2 = x.reshape(T, d)
    q, k, ks, v, u, gates = _inproj(x2, row2(p["norm_mix_pre"]), w_in, bsz, seq, tm)

    later = ("w_ffn_gate", "w_ffn_up", "w_ffn_down", "w_out", "w_branch_attn", "w_branch_ssm", "w_glu")
    attn, *rounded = _attention(q, k, ks, v, jnp.asarray(_attn_bias()), p["attn_sink"].astype(F32), bsz, seq,
                                [p[name] for name in later])
    w16 = dict(zip(later, rounded))

    assert bsz == SUBLANES // 2
    ys = _s5(u.reshape(T, ssm_w), wb, tsum, wct, a).reshape(u.shape)

    x1, h2 = _merge(attn, ys, u, gates, x2,
                    w16["w_branch_attn"], w16["w_glu"], row2(p["b_glu"]),
                    row2(p["ssm_d"]), w16["w_branch_ssm"], w16["w_out"],
                    row2(p["norm_mix_post"]), row2(p["norm_ffn_pre"]), seq, tm)

    out = _ffn(h2, x1, w16["w_ffn_gate"], w16["w_ffn_up"], p["conv_w"].astype(F32),
               row2(p["conv_b"]), w16["w_ffn_down"], row2(p["norm_ffn_post"]), seq, tm)
    return out.reshape(bsz, seq, d)


_PARAM_NAMES = (
    "norm_mix_pre", "w_in", "attn_sink",
    "lam_re_f", "lam_im_f", "log_dt_f", "b_re_f", "b_im_f", "c_re_f", "c_im_f",
    "lam_re_b", "lam_im_b", "log_dt_b", "b_re_b", "b_im_b", "c_re_b", "c_im_b",
    "ssm_d", "w_glu", "b_glu", "w_branch_attn", "w_branch_ssm", "w_out", "norm_mix_post",
    "norm_ffn_pre", "w_ffn_gate", "w_ffn_up", "conv_w", "conv_b", "w_ffn_down", "norm_ffn_post")


def kernel(x, norm_mix_pre, w_in, attn_sink, lam_re_f, lam_im_f, log_dt_f, b_re_f, b_im_f, c_re_f, c_im_f, lam_re_b, lam_im_b, log_dt_b, b_re_b, b_im_b, c_re_b, c_im_b, ssm_d, w_glu, b_glu, w_branch_attn, w_branch_ssm, w_out, norm_mix_post, norm_ffn_pre, w_ffn_gate, w_ffn_up, conv_w, conv_b, w_ffn_down, norm_ffn_post):
    stacked = dict(zip(_PARAM_NAMES, (
        norm_mix_pre, w_in, attn_sink,
        lam_re_f, lam_im_f, log_dt_f, b_re_f, b_im_f, c_re_f, c_im_f,
        lam_re_b, lam_im_b, log_dt_b, b_re_b, b_im_b, c_re_b, c_im_b,
        ssm_d, w_glu, b_glu, w_branch_attn, w_branch_ssm, w_out, norm_mix_post,
        norm_ffn_pre, w_ffn_gate, w_ffn_up, conv_w, conv_b, w_ffn_down, norm_ffn_post)))
    for layer in range(w_in.shape[0]):
        x = _layer(x, {name: value[layer] for name, value in stacked.items()})
    return x
```

```python
import functools
import math

import numpy as np
import jax
import jax.numpy as jnp
from jax import lax
from jax.experimental import pallas as pl
from jax.experimental.pallas import tpu as pltpu

F32 = jnp.float32
BF16 = jnp.bfloat16

N_HEADS = 8
N_KV_HEADS = 2
HEAD_DIM = 64
GROUP = N_HEADS // N_KV_HEADS
WINDOW = 128
BLOCK = 128
ATTN_SUB_BLOCKS = 16
SSM_CH = 16
SSM_STATE = 64
CHUNK = 16
PAIR = 2
N_BRANCH = 2
EPS = 1e-6
NEG_INF = -1e30
LOG2E = math.log2(math.e)

LANES = 128
SUBLANES = 8
BF16_SUBLANES = 2 * SUBLANES
HALO = BF16_SUBLANES
FF_TILE = 256
V7X_VMEM_BYTES = 64 * 1024 * 1024
VMEM_LIMIT = V7X_VMEM_BYTES * 7 // 8


def _gelu_tanh(x):
    return 0.5 * x * (1.0 + jnp.tanh(math.sqrt(2.0 / math.pi) * (x + 0.044715 * (x * x * x))))


def _rms(x, gain):
    return x * lax.rsqrt(jnp.mean(x * x, axis=-1, keepdims=True) + EPS) * gain


def _params(semantics):
    return pltpu.CompilerParams(dimension_semantics=semantics, vmem_limit_bytes=VMEM_LIMIT)


def _const_spec(shape):
    nd = len(shape)
    return pl.BlockSpec(shape, lambda *_: (0,) * nd)


def _inproj_body(splits, x_ref, g_ref, w_ref, q_ref, k_ref, ks_ref, v_ref, u_ref, gate_ref):
    h = _rms(x_ref[...], g_ref[...]).astype(BF16)

    def proj(i):
        return jnp.dot(h, w_ref[:, splits[i]:splits[i + 1]], preferred_element_type=F32)

    q_ref[...] = (proj(0) * (HEAD_DIM ** -0.5 * LOG2E)).astype(BF16)
    k = proj(1).astype(BF16)
    k_ref[...] = k
    ks_ref[...] = pltpu.roll(k, HEAD_DIM, 1)
    v_ref[...] = proj(2).astype(BF16)
    u_ref[...] = proj(3).astype(BF16).reshape(u_ref.shape)
    gate_ref[...] = jax.nn.sigmoid(proj(4)).astype(BF16)


def _chunk_rows_spec(tm, tiles_per_seq, width):
    return pl.BlockSpec((tm // CHUNK, None, CHUNK, width),
                        lambda i: (i % tiles_per_seq, i // tiles_per_seq, 0, 0))


def _inproj(x2, gain, w_in, bsz, seq, tm):
    T, D = x2.shape
    attn_w = N_HEADS * HEAD_DIM
    kv_w = N_KV_HEADS * HEAD_DIM
    widths = [attn_w, kv_w, kv_w, D // 2, N_BRANCH * D]
    splits = [0] + np.cumsum(widths).tolist()
    row = lambda w: pl.BlockSpec((tm, w), lambda i: (i, 0))
    out_widths = [attn_w, kv_w, kv_w, kv_w, D // 2, N_BRANCH * D]
    out_specs = [row(w) for w in out_widths]
    out_shape = [jax.ShapeDtypeStruct((T, w), BF16) for w in out_widths]
    out_specs[4] = _chunk_rows_spec(tm, seq // tm, out_widths[4])
    out_shape[4] = jax.ShapeDtypeStruct((seq // CHUNK, bsz, CHUNK, out_widths[4]), BF16)
    return pl.pallas_call(
        functools.partial(_inproj_body, tuple(splits)),
        grid=(T // tm,),
        in_specs=[row(D), _const_spec((1, D)), _const_spec(w_in.shape)],
        out_specs=out_specs,
        out_shape=out_shape,
        compiler_params=_params(("parallel",)),
        name="inproj",
    )(x2, gain, w_in)


def _cast_specs(weights, steps, step_of):
    specs = []
    for w in weights:
        nblocks = max(n for n in range(1, steps + 1)
                      if steps % n == 0 and w.shape[0] % (n * BF16_SUBLANES) == 0)
        hold = steps // nblocks
        specs.append(pl.BlockSpec((w.shape[0] // nblocks, w.shape[1]),
                                  lambda *ids, hold=hold: (step_of(*ids) // hold, 0)))
    return specs, specs, [jax.ShapeDtypeStruct(w.shape, BF16) for w in weights]


def _attn_body(ncast, q_ref, kp_ref, kc_ref, kn_ref, sp_ref, sc_ref, sn_ref, vp_ref, vc_ref, vn_ref,
               bias_first_ref, bias_mid_ref, bias_last_ref, sink_ref, *refs):
    o_ref = refs[ncast]
    for src, dst in zip(refs[:ncast], refs[ncast + 1:]):
        dst[...] = src[...].astype(BF16)
    bias_refs = [bias_first_ref] + [bias_mid_ref] * (ATTN_SUB_BLOCKS - 2) + [bias_last_ref]
    lane = lax.broadcasted_iota(jnp.int32, (1, LANES), 1)
    half_mask = [lane < HEAD_DIM, lane >= HEAD_DIM]
    zero = jnp.zeros((), BF16)
    kall = [jnp.concatenate([kp_ref[...], kc_ref[...], kn_ref[...]], axis=0),
            jnp.concatenate([sp_ref[...], sc_ref[...], sn_ref[...]], axis=0)]
    vall = jnp.concatenate([vp_ref[...], vc_ref[...], vn_ref[...]], axis=0)
    for b, bias_ref in enumerate(bias_refs):
        keys = slice(BLOCK * b, BLOCK * (b + 3))
        vcat = vall[keys]
        rows = slice(BLOCK * b, BLOCK * (b + 1))
        for j in range(N_HEADS // 2):
            qj = q_ref[rows, LANES * j:LANES * (j + 1)]
            g = (2 * j) // GROUP
            halves = []
            for e in range(2):
                h = 2 * j + e
                qm = jnp.where(half_mask[e], qj, zero)
                kcat = kall[0 if e == g else 1][keys]
                st = lax.dot_general(kcat, qm, (((1,), (1,)), ((), ())), preferred_element_type=F32)
                st = st + bias_ref[h]
                sink = sink_ref[h] * LOG2E
                m = jnp.maximum(jnp.max(st, axis=0, keepdims=True), sink)
                p = jnp.exp2(st - m)
                denom = jnp.sum(p, axis=0, keepdims=True) + jnp.exp2(sink - m)
                ot = lax.dot_general(vcat, p.astype(BF16), (((0,), (0,)), ((), ())),
                                     preferred_element_type=F32)
                halves.append(ot[HEAD_DIM * g:HEAD_DIM * (g + 1)] / denom)
            o_ref[rows, LANES * j:LANES * (j + 1)] = jnp.concatenate(halves, axis=0).T.astype(BF16)


def _attn_bias():
    t = np.arange(BLOCK)[None, :]
    s = np.arange(3 * BLOCK)[:, None] - BLOCK
    dist = np.abs(t - s).astype(np.float64)
    valid = dist <= WINDOW
    slopes = 2.0 ** (-8.0 * (np.arange(N_HEADS) + 1.0) / N_HEADS)
    out = np.zeros((3, N_HEADS, 3 * BLOCK, BLOCK), np.float32)
    for variant in range(3):
        ok = valid.copy()
        if variant == 0:
            ok[:BLOCK] = False
        if variant == 2:
            ok[2 * BLOCK:] = False
        for h in range(N_HEADS):
            out[variant, h] = np.where(ok, -slopes[h] * dist * LOG2E, NEG_INF)
    return out


def _attention(q, k, ks, v, bias, sink, bsz, seq, cast_views):
    nb = seq // BLOCK
    sub = ATTN_SUB_BLOCKS
    assert nb % sub == 0
    ns = nb // sub
    kvw = k.shape[1]
    aw = q.shape[1]

    def edge_spec(off):
        return pl.BlockSpec((BLOCK, kvw), lambda b, n: (b * nb + jnp.clip(sub * n + off, 0, nb - 1), 0))

    centre = pl.BlockSpec((sub * BLOCK, kvw), lambda b, n: (b * ns + n, 0))
    bias_spec = lambda pick: pl.BlockSpec((None,) + bias.shape[1:], lambda b, n: (pick(n), 0, 0, 0))
    first = bias_spec(lambda n: jnp.where(n == 0, 0, 1))
    middle = bias_spec(lambda n: 1)
    last = bias_spec(lambda n: jnp.where(n == ns - 1, 2, 1))
    qspec = pl.BlockSpec((sub * BLOCK, aw), lambda b, n: (b * ns + n, 0))
    cast_in, cast_out, cast_shape = _cast_specs(cast_views, bsz * ns, lambda b, n: b * ns + n)
    return pl.pallas_call(
        functools.partial(_attn_body, len(cast_views)),
        grid=(bsz, ns),
        in_specs=[qspec] + [edge_spec(-1), centre, edge_spec(sub)] * 3 + [
                  first, middle, last,
                  pl.BlockSpec(memory_space=pltpu.SMEM)] + cast_in,
        out_specs=[qspec] + cast_out,
        out_shape=[jax.ShapeDtypeStruct(q.shape, BF16)] + cast_shape,
        compiler_params=_params(("parallel", "arbitrary")),
        name="attn",
    )(q, k, k, k, ks, ks, ks, v, v, v, bias, bias, bias, sink, *cast_views)


def _s5prep_body(lam_ref, ldt_ref, bt_ref, c_ref, w32_ref, wb_ref, wct_ref, t_ref, a_ref, w16_ref):
    w16_ref[...] = w32_ref[...].astype(BF16)
    width = CHUNK * SSM_CH
    lane = lax.broadcasted_iota(jnp.int32, (1, LANES), 1)
    in_group = [lane < SSM_STATE, lane >= SSM_STATE]
    lane_t = lax.broadcasted_iota(jnp.int32, (SSM_CH, width), 1)
    contract = (((1,), (1,)), ((), ()))
    hi = lax.Precision.HIGHEST
    toeplitz = [[None] * CHUNK for _ in range(PAIR)]
    for d in range(2):
        lre, lim = lam_ref[d, 0], lam_ref[d, 1]
        dt = jnp.exp(ldt_ref[d])
        mag = jnp.exp(lre * dt)
        are, aim = mag * jnp.cos(lim * dt), mag * jnp.sin(lim * dt)
        den = lre * lre + lim * lim
        cfr = ((are - 1.0) * lre + aim * lim) / den
        cfi = (aim * lre - (are - 1.0) * lim) / den
        btr, bti = bt_ref[d, 0], bt_ref[d, 1]
        bbr = btr * cfr - bti * cfi
        bbi = btr * cfi + bti * cfr
        cre, cim = c_ref[d, 0], c_ref[d, 1]

        pwr, pwi = jnp.ones_like(are), jnp.zeros_like(are)
        e_r, e_i = [], []
        for p in range(CHUNK + 1):
            er = cre * pwr - cim * pwi
            ei = cre * pwi + cim * pwr
            if p < CHUNK:
                e_r.append(er)
                e_i.append(ei)
                k = CHUNK - 1 - p if d == 0 else p
                for ri, val in enumerate((bbr * pwr - bbi * pwi, bbr * pwi + bbi * pwr)):
                    for g in range(PAIR):
                        wb_ref[width * g + SSM_CH * k:width * g + SSM_CH * (k + 1),
                               LANES * (2 * d + ri):LANES * (2 * d + ri + 1)] = jnp.where(
                                   in_group[g], val, 0.0).astype(BF16)
            if p >= 1:
                r = p - 1 if d == 0 else CHUNK - p
                for ri, val in enumerate((er, -ei)):
                    for g in range(PAIR):
                        wct_ref[width * g + SSM_CH * r:width * g + SSM_CH * (r + 1),
                                LANES * (2 * d + ri):LANES * (2 * d + ri + 1)] = jnp.where(
                                    in_group[g], val, 0.0).astype(BF16)
            if p == CHUNK:
                a_ref[2 * d] = jnp.broadcast_to(pwr, a_ref.shape[1:])
                a_ref[2 * d + 1] = jnp.broadcast_to(pwi, a_ref.shape[1:])
            pwr, pwi = pwr * are - pwi * aim, pwr * aim + pwi * are

        if d == 1:
            e_r, e_i = e_r[::-1], e_i[::-1]
        ecat_r, ecat_i = jnp.concatenate(e_r, axis=0), jnp.concatenate(e_i, axis=0)
        for g in range(PAIR):
            gr, gi = jnp.where(in_group[g], bbr, 0.0), jnp.where(in_group[g], bbi, 0.0)
            krow = (lax.dot_general(gr, ecat_r, contract, precision=hi, preferred_element_type=F32)
                    - lax.dot_general(gi, ecat_i, contract, precision=hi, preferred_element_type=F32))
            for k in range(CHUNK):
                if d == 0:
                    shift, keep = SSM_CH * k, lane_t >= SSM_CH * k
                else:
                    shift, keep = (SSM_CH * (k + 1)) % width, lane_t < SSM_CH * (k + 1)
                blk = jnp.where(keep, pltpu.roll(krow, shift, 1) if shift else krow, 0.0)
                toeplitz[g][k] = blk if d == 0 else toeplitz[g][k] + blk
    for g in range(PAIR):
        t_ref[g] = jnp.concatenate(toeplitz[g], axis=0).astype(BF16)


def _s5prep(lam, ldt, bt, c, w32):
    npair = lam.shape[1]
    width = CHUNK * SSM_CH
    pw = PAIR * width
    cast_in, cast_out, cast_shape = _cast_specs([w32], npair, lambda i: i)
    return pl.pallas_call(
        _s5prep_body,
        grid=(npair,),
        in_specs=[pl.BlockSpec((2, None, 2, 1, LANES), lambda i: (0, i, 0, 0, 0)),
                  pl.BlockSpec((2, None, 1, LANES), lambda i: (0, i, 0, 0)),
                  pl.BlockSpec((2, None, 2, SSM_CH, LANES), lambda i: (0, i, 0, 0, 0)),
                  pl.BlockSpec((2, None, 2, SSM_CH, LANES), lambda i: (0, i, 0, 0, 0))] + cast_in,
        out_specs=[pl.BlockSpec((None, pw, 4 * LANES), lambda i: (i, 0, 0)),
                   pl.BlockSpec((None, pw, 4 * LANES), lambda i: (i, 0, 0)),
                   pl.BlockSpec((PAIR, width, width), lambda i: (i, 0, 0)),
                   pl.BlockSpec((None, 4, SUBLANES, LANES), lambda i: (i, 0, 0, 0))] + cast_out,
        out_shape=[jax.ShapeDtypeStruct((npair, pw, 4 * LANES), BF16),
                   jax.ShapeDtypeStruct((npair, pw, 4 * LANES), BF16),
                   jax.ShapeDtypeStruct((npair * PAIR, width, width), BF16),
                   jax.ShapeDtypeStruct((npair, 4, SUBLANES, LANES), F32)] + cast_shape,
        compiler_params=_params(("parallel",)),
        name="s5prep",
    )(lam, ldt, bt, c, w32)


def _block_transpose(xs):
    n = len(xs)
    assert n * SSM_CH == LANES and n & (n - 1) == 0
    blk = lax.broadcasted_iota(jnp.int32, xs[0].shape, 1) // SSM_CH
    d = n // 2
    while d:
        upper = (blk & d) != 0
        ys = list(xs)
        for a in range(n):
            if a & d:
                continue
            b = a + d
            ys[a] = jnp.where(upper, pltpu.roll(xs[b], SSM_CH * d, 1), xs[a])
            ys[b] = jnp.where(upper, xs[b], pltpu.roll(xs[a], LANES - SSM_CH * d, 1))
        xs = ys
        d //= 2
    return xs


def _s5_scan(a_ref, w_scr, s_scr):
    rows = w_scr.shape[0]
    nsteps = rows // SUBLANES
    nstream = w_scr.shape[1] // (2 * LANES)
    half = SUBLANES // 2
    lo = lax.broadcasted_iota(jnp.int32, (SUBLANES, LANES), 0) < half
    first = [lo, jnp.logical_not(lo)]
    coef = [(a_ref[s // 2, 2 * (s % 2)], a_ref[s // 2, 2 * (s % 2) + 1]) for s in range(nstream)]
    col = lambda i: slice(LANES * i, LANES * (i + 1))

    def advance(a, xr, xi, wr, wi):
        return a[0] * xr - a[1] * xi + wr, a[0] * xi + a[1] * xr + wi

    def swap(x):
        return pltpu.roll(x, half, 0)

    def step(i, carry):
        out = []
        for s in range(nstream):
            d = s % 2
            cr, ci = carry[2 * s], carry[2 * s + 1]
            vreg = i if d == 0 else nsteps - 1 - i
            rs = pl.ds(pl.multiple_of(vreg * SUBLANES, SUBLANES), SUBLANES)
            wr, wi = w_scr[rs, col(2 * s)], w_scr[rs, col(2 * s + 1)]
            r0r, r0i = swap(cr), swap(ci)
            t1r, t1i = advance(coef[s], r0r, r0i, wr, wi)
            r1r, r1i = swap(t1r), swap(t1i)
            t2r, t2i = advance(coef[s], r1r, r1i, wr, wi)
            s_scr[rs, col(2 * s)] = jnp.where(first[d], r0r, r1r)
            s_scr[rs, col(2 * s + 1)] = jnp.where(first[d], r0i, r1i)
            out += [jnp.where(first[d], t1r, t2r), jnp.where(first[d], t1i, t2i)]
        return tuple(out)

    z = jnp.zeros((SUBLANES, LANES), F32)
    lax.fori_loop(0, nsteps, step, (z,) * (2 * nstream))


S5_SLAB_GROUPS = LANES // SSM_CH
S5_ROW_CHUNK = 256
S5_SUB_ROWS = 128
S5_SCAN_PAIRS = 2
S5_TOK_PITCH = CHUNK + SUBLANES


def _s5_body(u_ref, wb_ref, t_ref, wct_ref, a_ref, y_ref, ur_scr, w_scr, s_scr, sall_scr, yr_scr, tok_scr):
    rows = ur_scr.shape[0]
    width = CHUNK * SSM_CH
    pw = PAIR * width
    npair = S5_SLAB_GROUPS // PAIR
    half_steps = CHUNK // 2
    tok_rows = S5_ROW_CHUNK * CHUNK

    nchunks = rows // S5_ROW_CHUNK
    nsub = S5_ROW_CHUNK // S5_SUB_ROWS
    sw = 4 * LANES
    row_chunk = lambda c: slice(S5_ROW_CHUNK * c, S5_ROW_CHUNK * (c + 1))
    tok_chunk = lambda c: slice(tok_rows * c, tok_rows * (c + 1))
    step_rows = lambda i, step: pl.ds(S5_TOK_PITCH * S5_SUB_ROWS * i + step, S5_SUB_ROWS, stride=S5_TOK_PITCH)

    def copy_chunk_rows(c, tok, to_tok):
        for r in range(S5_ROW_CHUNK):
            ext = slice(tok_rows * c + r * CHUNK, tok_rows * c + (r + 1) * CHUNK)
            pad = slice(r * S5_TOK_PITCH, r * S5_TOK_PITCH + CHUNK)
            if to_tok:
                tok[pad, :] = u_ref[ext, :].astype(F32)
            else:
                y_ref[ext, :] = tok[pad, :].astype(BF16)
    lane_blk = lambda g, h: slice(width * g + LANES * h, width * g + LANES * (h + 1))

    def relayout_in(c):
        tok = tok_scr.at[c % 2]
        copy_chunk_rows(c, tok, True)
        for i in range(nsub):
            for h in range(2):
                xs = [tok[step_rows(i, half_steps * h + kk), :] for kk in range(half_steps)]
                ys = _block_transpose(xs)
                for g in range(S5_SLAB_GROUPS):
                    r0 = S5_ROW_CHUNK * c + S5_SUB_ROWS * i
                    ur_scr[r0:r0 + S5_SUB_ROWS, lane_blk(g, h)] = ys[g].astype(BF16)

    def state_in(c, p0):
        for q in range(S5_SCAN_PAIRS):
            pp = p0 + q
            w_scr[row_chunk(c), sw * q:sw * (q + 1)] = jnp.dot(
                ur_scr[row_chunk(c), pw * pp:pw * (pp + 1)], wb_ref[pp], preferred_element_type=F32)

    for p0 in range(0, npair, S5_SCAN_PAIRS):
        for c in range(nchunks):
            if p0 == 0:
                relayout_in(c)
            state_in(c, p0)
        _s5_scan(a_ref.at[p0:p0 + S5_SCAN_PAIRS], w_scr, s_scr)
        sall_scr[:, sw * p0:sw * (p0 + S5_SCAN_PAIRS)] = s_scr[...].astype(BF16)

    def outputs(c):
        yr, tok = yr_scr.at[c % 2], tok_scr.at[c % 2]
        for pp in range(npair):
            y = lax.dot_general(sall_scr[row_chunk(c), sw * pp:sw * (pp + 1)], wct_ref[pp],
                                (((1,), (1,)), ((), ())), preferred_element_type=F32)
            for g in range(PAIR):
                lo = pw * pp + width * g
                yr[:, lo:lo + width] = y[:, width * g:width * (g + 1)] + jnp.dot(
                    ur_scr[row_chunk(c), lo:lo + width], t_ref[PAIR * pp + g], preferred_element_type=F32)
        for i in range(nsub):
            for h in range(2):
                xs = [yr[S5_SUB_ROWS * i:S5_SUB_ROWS * (i + 1), lane_blk(g, h)].astype(BF16)
                      for g in range(S5_SLAB_GROUPS)]
                ys = _block_transpose(xs)
                for kk in range(half_steps):
                    tok[step_rows(i, half_steps * h + kk), :] = ys[kk].astype(F32)
        copy_chunk_rows(c, tok, False)

    for c in range(nchunks):
        outputs(c)


def _s5(u_tok, wb, tsum, wct, a16):
    T, ssm_w = u_tok.shape
    rows = T // CHUNK
    assert rows % S5_ROW_CHUNK == 0 and S5_SLAB_GROUPS == CHUNK // 2
    width = CHUNK * SSM_CH
    pw = PAIR * width
    npair = S5_SLAB_GROUPS // PAIR
    slab = pl.BlockSpec((T, LANES), lambda s: (0, s))
    return pl.pallas_call(
        _s5_body,
        grid=(ssm_w // LANES,),
        in_specs=[slab,
                  pl.BlockSpec((npair, pw, 4 * LANES), lambda s: (s, 0, 0)),
                  pl.BlockSpec((S5_SLAB_GROUPS, width, width), lambda s: (s, 0, 0)),
                  pl.BlockSpec((npair, pw, 4 * LANES), lambda s: (s, 0, 0)),
                  pl.BlockSpec((npair, 4, SUBLANES, LANES), lambda s: (s, 0, 0, 0))],
        out_specs=slab,
        out_shape=jax.ShapeDtypeStruct((T, ssm_w), BF16),
        scratch_shapes=[pltpu.VMEM((rows, S5_SLAB_GROUPS * width), BF16),
                        pltpu.VMEM((rows, S5_SCAN_PAIRS * 4 * LANES), F32),
                        pltpu.VMEM((rows, S5_SCAN_PAIRS * 4 * LANES), F32),
                        pltpu.VMEM((rows, npair * 4 * LANES), BF16),
                        pltpu.VMEM((2, S5_ROW_CHUNK, S5_SLAB_GROUPS * width), F32),
                        pltpu.VMEM((2, S5_ROW_CHUNK * S5_TOK_PITCH, LANES), F32)],
        compiler_params=_params(("parallel",)),
        name="s5",
    )(u_tok, wb, tsum, wct, a16)


def _merge_body(attn_ref, ys_ref, u_ref, g_ref, x_ref, wba_ref, wglu_ref, bglu_ref, d_ref, wbs_ref,
                wout_ref, gpost_ref, gpre_ref, x1_ref, h2_ref):
    tm, d = x_ref.shape
    u = u_ref[...].reshape(tm, -1).astype(F32)
    y = ys_ref[...].reshape(tm, -1).astype(F32) + d_ref[...] * u
    y = _gelu_tanh(y)
    gl = jnp.dot(y.astype(BF16), wglu_ref[...], preferred_element_type=F32) + bglu_ref[...]
    s = y * jax.nn.sigmoid(gl)
    y_ssm = jnp.dot(s.astype(BF16), wbs_ref[...], preferred_element_type=F32)
    y_attn = jnp.dot(attn_ref[...], wba_ref[...], preferred_element_type=F32)
    merged = g_ref[:, :d].astype(F32) * y_attn + g_ref[:, d:].astype(F32) * y_ssm
    o = jnp.dot(merged.astype(BF16), wout_ref[...], preferred_element_type=F32)
    x1 = x_ref[...] + _rms(o, gpost_ref[...])
    x1_ref[...] = x1
    h2_ref[...] = _rms(x1, gpre_ref[...]).astype(BF16)


def _merge(attn, ys, u, gates, x2, wba, wglu, bglu, ssm_d, wbs, wout, gpost, gpre, seq, tm):
    T, D = x2.shape
    row = lambda w: pl.BlockSpec((tm, w), lambda i: (i, 0))
    chunked = _chunk_rows_spec(tm, seq // tm, u.shape[-1])
    consts = [wba, wglu, bglu, ssm_d, wbs, wout, gpost, gpre]
    return pl.pallas_call(
        _merge_body,
        grid=(T // tm,),
        in_specs=[row(attn.shape[1]), chunked, chunked, row(gates.shape[1]), row(D)]
                 + [_const_spec(c.shape) for c in consts],
        out_specs=[row(D), row(D)],
        out_shape=[jax.ShapeDtypeStruct((T, D), F32), jax.ShapeDtypeStruct((T, D), BF16)],
        compiler_params=_params(("parallel",)),
        name="merge",
    )(attn, ys, u, gates, x2, *consts)


def _ffn_body(tiles_per_seq, hp_ref, hc_ref, hn_ref, x1_ref, wg_ref, wu_ref, cw_ref, cb_ref, wd_ref,
              gn_ref, o_ref, hext, fscr):
    tm = hc_ref.shape[0]
    i = pl.program_id(0) % tiles_per_seq
    zero = jnp.zeros((), BF16)
    hext[0:HALO] = jnp.where(i == 0, zero, hp_ref[...])
    hext[HALO:HALO + tm] = hc_ref[...]
    hext[HALO + tm:] = jnp.where(i == tiles_per_seq - 1, zero, hn_ref[...])
    ext = tm + 2 * HALO
    for c in range(wg_ref.shape[1] // FF_TILE):
        sl = slice(FF_TILE * c, FF_TILE * (c + 1))
        g = jnp.dot(hext[...], wg_ref[:, sl], preferred_element_type=F32)
        up = jnp.dot(hc_ref[...], wu_ref[:, sl], preferred_element_type=F32)
        g_prev = pltpu.roll(g, 1, 0)[HALO:HALO + tm]
        g_next = pltpu.roll(g, ext - 1, 0)[HALO:HALO + tm]
        conv = (g_prev * cw_ref[0:1, sl] + g[HALO:HALO + tm] * cw_ref[1:2, sl]
                + g_next * cw_ref[2:3, sl] + cb_ref[:, sl])
        fscr[:, sl] = (_gelu_tanh(conv) * up).astype(BF16)
    o = jnp.dot(fscr[...], wd_ref[...], preferred_element_type=F32)
    o_ref[...] = x1_ref[...] + _rms(o, gn_ref[...])


def _ffn(h2, x1, wg, wu, cw, cb, wd, gn, seq, tm):
    T, D = x1.shape
    dff = wg.shape[1]
    assert dff % FF_TILE == 0 and tm % HALO == 0 and seq % tm == 0
    per = tm // HALO
    nh = T // HALO
    row = lambda w: pl.BlockSpec((tm, w), lambda i: (i, 0))
    once = lambda shape: pl.BlockSpec(shape, lambda i: (0,) * len(shape), pipeline_mode=pl.Buffered(1))
    return pl.pallas_call(
        functools.partial(_ffn_body, seq // tm),
        grid=(T // tm,),
        in_specs=[pl.BlockSpec((HALO, D), lambda i: (jnp.maximum(i * per - 1, 0), 0)),
                  row(D),
                  pl.BlockSpec((HALO, D), lambda i: (jnp.minimum((i + 1) * per, nh - 1), 0)),
                  row(D),
                  once(wg.shape), once(wu.shape), _const_spec(cw.shape), _const_spec(cb.shape),
                  once(wd.shape), _const_spec(gn.shape)],
        out_specs=row(D),
        out_shape=jax.ShapeDtypeStruct((T, D), F32),
        scratch_shapes=[pltpu.VMEM((tm + 2 * HALO, D), BF16), pltpu.VMEM((tm, dff), BF16)],
        compiler_params=_params(("parallel",)),
        name="ffn",
    )(h2, h2, h2, x1, wg, wu, cw, cb, wd, gn)


def _row_tile(seq):
    return min(1024, seq)


def _layer(x, p):
    bsz, seq, d = x.shape
    T = bsz * seq
    tm = _row_tile(seq)
    ssm_w = d // 2
    ngroups = ssm_w // SSM_CH
    row2 = lambda v: v.reshape(1, -1).astype(F32)

    npair = ngroups // PAIR
    both = lambda re, im: jnp.stack([jnp.stack([p[re + "_f"], p[im + "_f"]]),
                                     jnp.stack([p[re + "_b"], p[im + "_b"]])]).astype(F32)
    lam = both("lam_re", "lam_im").reshape(2, 2, npair, 1, LANES).transpose(0, 2, 1, 3, 4)
    ldt = jnp.repeat(jnp.stack([p["log_dt_f"], p["log_dt_b"]]).astype(F32), SSM_STATE, axis=-1)
    ldt = ldt.reshape(2, npair, 1, LANES)
    bt = both("b_re", "b_im").reshape(2, 2, npair, PAIR, SSM_STATE, SSM_CH)
    bt = bt.transpose(0, 2, 1, 5, 3, 4).reshape(2, npair, 2, SSM_CH, LANES)
    cc = both("c_re", "c_im").reshape(2, 2, npair, PAIR, SSM_CH, SSM_STATE)
    cc = cc.transpose(0, 2, 1, 4, 3, 5).reshape(2, npair, 2, SSM_CH, LANES)
    wb, wct, tsum, a, w_in = _s5prep(lam, ldt, bt, cc, p["w_in"])

    x2 = x.reshape(T, d)
    q, k, ks, v, u, gates = _inproj(x2, row2(p["norm_mix_pre"]), w_in, bsz, seq, tm)

    later = ("w_ffn_gate", "w_ffn_up", "w_ffn_down", "w_out", "w_branch_attn", "w_branch_ssm", "w_glu")
    attn, *rounded = _attention(q, k, ks, v, jnp.asarray(_attn_bias()), p["attn_sink"].astype(F32), bsz, seq,
                                [p[name] for name in later])
    w16 = dict(zip(later, rounded))

    assert bsz == SUBLANES // 2
    ys = _s5(u.reshape(T, ssm_w), wb, tsum, wct, a).reshape(u.shape)

    x1, h2 = _merge(attn, ys, u, gates, x2,
                    w16["w_branch_attn"], w16["w_glu"], row2(p["b_glu"]),
                    row2(p["ssm_d"]), w16["w_branch_ssm"], w16["w_out"],
                    row2(p["norm_mix_post"]), row2(p["norm_ffn_pre"]), seq, tm)

    out = _ffn(h2, x1, w16["w_ffn_gate"], w16["w_ffn_up"], p["conv_w"].astype(F32),
               row2(p["conv_b"]), w16["w_ffn_down"], row2(p["norm_ffn_post"]), seq, tm)
    return out.reshape(bsz, seq, d)


_PARAM_NAMES = (
    "norm_mix_pre", "w_in", "attn_sink",
    "lam_re_f", "lam_im_f", "log_dt_f", "b_re_f", "b_im_f", "c_re_f", "c_im_f",
    "lam_re_b", "lam_im_b", "log_dt_b", "b_re_b", "b_im_b", "c_re_b", "c_im_b",
    "ssm_d", "w_glu", "b_glu", "w_branch_attn", "w_branch_ssm", "w_out", "norm_mix_post",
    "norm_ffn_pre", "w_ffn_gate", "w_ffn_up", "conv_w", "conv_b", "w_ffn_down", "norm_ffn_post")


def kernel(x, norm_mix_pre, w_in, attn_sink, lam_re_f, lam_im_f, log_dt_f, b_re_f, b_im_f, c_re_f, c_im_f, lam_re_b, lam_im_b, log_dt_b, b_re_b, b_im_b, c_re_b, c_im_b, ssm_d, w_glu, b_glu, w_branch_attn, w_branch_ssm, w_out, norm_mix_post, norm_ffn_pre, w_ffn_gate, w_ffn_up, conv_w, conv_b, w_ffn_down, norm_ffn_post):
    stacked = dict(zip(_PARAM_NAMES, (
        norm_mix_pre, w_in, attn_sink,
        lam_re_f, lam_im_f, log_dt_f, b_re_f, b_im_f, c_re_f, c_im_f,
        lam_re_b, lam_im_b, log_dt_b, b_re_b, b_im_b, c_re_b, c_im_b,
        ssm_d, w_glu, b_glu, w_branch_attn, w_branch_ssm, w_out, norm_mix_post,
        norm_ffn_pre, w_ffn_gate, w_ffn_up, conv_w, conv_b, w_ffn_down, norm_ffn_post)))
    for layer in range(w_in.shape[0]):
        x = _layer(x, {name: value[layer] for name, value in stacked.items()})
    return x
```

```python
import functools
import math

import numpy as np
import jax
import jax.numpy as jnp
from jax import lax
from jax.experimental import pallas as pl
from jax.experimental.pallas import tpu as pltpu

F32 = jnp.float32
BF16 = jnp.bfloat16

N_HEADS = 8
N_KV_HEADS = 2
HEAD_DIM = 64
GROUP = N_HEADS // N_KV_HEADS
WINDOW = 128
BLOCK = 128
ATTN_SUB_BLOCKS = 8
SSM_CH = 16
SSM_STATE = 64
CHUNK = 16
PAIR = 2
N_BRANCH = 2
EPS = 1e-6
NEG_INF = -1e30
LOG2E = math.log2(math.e)

LANES = 128
SUBLANES = 8
BF16_SUBLANES = 2 * SUBLANES
HALO = BF16_SUBLANES
FF_TILE = 256
V7X_VMEM_BYTES = 64 * 1024 * 1024
VMEM_LIMIT = V7X_VMEM_BYTES * 7 // 8


def _gelu_tanh(x):
    return 0.5 * x * (1.0 + jnp.tanh(math.sqrt(2.0 / math.pi) * (x + 0.044715 * (x * x * x))))


def _rms(x, gain):
    return x * lax.rsqrt(jnp.mean(x * x, axis=-1, keepdims=True) + EPS) * gain


def _params(semantics):
    return pltpu.CompilerParams(dimension_semantics=semantics, vmem_limit_bytes=VMEM_LIMIT)


def _const_spec(shape):
    nd = len(shape)
    return pl.BlockSpec(shape, lambda *_: (0,) * nd)


def _inproj_body(splits, x_ref, g_ref, w_ref, q_ref, k_ref, ks_ref, v_ref, u_ref, gate_ref):
    h = _rms(x_ref[...], g_ref[...]).astype(BF16)

    def proj(i):
        return jnp.dot(h, w_ref[:, splits[i]:splits[i + 1]], preferred_element_type=F32)

    q_ref[...] = (proj(0) * (HEAD_DIM ** -0.5 * LOG2E)).astype(BF16)
    k = proj(1).astype(BF16)
    k_ref[...] = k
    ks_ref[...] = pltpu.roll(k, HEAD_DIM, 1)
    v_ref[...] = proj(2).astype(BF16)
    u_ref[...] = proj(3).astype(BF16).reshape(u_ref.shape)
    gate_ref[...] = jax.nn.sigmoid(proj(4)).astype(BF16)


def _chunk_rows_spec(tm, tiles_per_seq, width):
    return pl.BlockSpec((tm // CHUNK, None, CHUNK, width),
                        lambda i: (i % tiles_per_seq, i // tiles_per_seq, 0, 0))


def _inproj(x2, gain, w_in, bsz, seq, tm):
    T, D = x2.shape
    attn_w = N_HEADS * HEAD_DIM
    kv_w = N_KV_HEADS * HEAD_DIM
    widths = [attn_w, kv_w, kv_w, D // 2, N_BRANCH * D]
    splits = [0] + np.cumsum(widths).tolist()
    row = lambda w: pl.BlockSpec((tm, w), lambda i: (i, 0))
    out_widths = [attn_w, kv_w, kv_w, kv_w, D // 2, N_BRANCH * D]
    out_specs = [row(w) for w in out_widths]
    out_shape = [jax.ShapeDtypeStruct((T, w), BF16) for w in out_widths]
    out_specs[4] = _chunk_rows_spec(tm, seq // tm, out_widths[4])
    out_shape[4] = jax.ShapeDtypeStruct((seq // CHUNK, bsz, CHUNK, out_widths[4]), BF16)
    return pl.pallas_call(
        functools.partial(_inproj_body, tuple(splits)),
        grid=(T // tm,),
        in_specs=[row(D), _const_spec((1, D)), _const_spec(w_in.shape)],
        out_specs=out_specs,
        out_shape=out_shape,
        compiler_params=_params(("parallel",)),
        name="inproj",
    )(x2, gain, w_in)


def _cast_specs(weights, steps, step_of):
    specs = []
    for w in weights:
        nblocks = max(n for n in range(1, steps + 1)
                      if steps % n == 0 and w.shape[0] % (n * BF16_SUBLANES) == 0)
        hold = steps // nblocks
        specs.append(pl.BlockSpec((w.shape[0] // nblocks, w.shape[1]),
                                  lambda *ids, hold=hold: (step_of(*ids) // hold, 0)))
    return specs, specs, [jax.ShapeDtypeStruct(w.shape, BF16) for w in weights]


def _attn_body(ncast, q_ref, kp_ref, kc_ref, kn_ref, sp_ref, sc_ref, sn_ref, vp_ref, vc_ref, vn_ref,
               bias_first_ref, bias_mid_ref, bias_last_ref, sink_ref, *refs):
    o_ref = refs[ncast]
    for src, dst in zip(refs[:ncast], refs[ncast + 1:]):
        dst[...] = src[...].astype(BF16)
    bias_refs = [bias_first_ref] + [bias_mid_ref] * (ATTN_SUB_BLOCKS - 2) + [bias_last_ref]
    lane = lax.broadcasted_iota(jnp.int32, (1, LANES), 1)
    half_mask = [lane < HEAD_DIM, lane >= HEAD_DIM]
    zero = jnp.zeros((), BF16)
    kall = [jnp.concatenate([kp_ref[...], kc_ref[...], kn_ref[...]], axis=0),
            jnp.concatenate([sp_ref[...], sc_ref[...], sn_ref[...]], axis=0)]
    vall = jnp.concatenate([vp_ref[...], vc_ref[...], vn_ref[...]], axis=0)
    for b, bias_ref in enumerate(bias_refs):
        keys = slice(BLOCK * b, BLOCK * (b + 3))
        vcat = vall[keys]
        rows = slice(BLOCK * b, BLOCK * (b + 1))
        for j in range(N_HEADS // 2):
            qj = q_ref[rows, LANES * j:LANES * (j + 1)]
            g = (2 * j) // GROUP
            halves = []
            for e in range(2):
                h = 2 * j + e
                qm = jnp.where(half_mask[e], qj, zero)
                kcat = kall[0 if e == g else 1][keys]
                st = lax.dot_general(kcat, qm, (((1,), (1,)), ((), ())), preferred_element_type=F32)
                st = st + bias_ref[h]
                sink = sink_ref[h] * LOG2E
                m = jnp.maximum(jnp.max(st, axis=0, keepdims=True), sink)
                p = jnp.exp2(st - m)
                denom = jnp.sum(p, axis=0, keepdims=True) + jnp.exp2(sink - m)
                ot = lax.dot_general(vcat, p.astype(BF16), (((0,), (0,)), ((), ())),
                                     preferred_element_type=F32)
                halves.append(ot[HEAD_DIM * g:HEAD_DIM * (g + 1)] / denom)
            o_ref[rows, LANES * j:LANES * (j + 1)] = jnp.concatenate(halves, axis=0).T.astype(BF16)


def _attn_bias():
    t = np.arange(BLOCK)[None, :]
    s = np.arange(3 * BLOCK)[:, None] - BLOCK
    dist = np.abs(t - s).astype(np.float64)
    valid = dist <= WINDOW
    slopes = 2.0 ** (-8.0 * (np.arange(N_HEADS) + 1.0) / N_HEADS)
    out = np.zeros((3, N_HEADS, 3 * BLOCK, BLOCK), np.float32)
    for variant in range(3):
        ok = valid.copy()
        if variant == 0:
            ok[:BLOCK] = False
        if variant == 2:
            ok[2 * BLOCK:] = False
        for h in range(N_HEADS):
            out[variant, h] = np.where(ok, -slopes[h] * dist * LOG2E, NEG_INF)
    return out


def _attention(q, k, ks, v, bias, sink, bsz, seq, cast_views):
    nb = seq // BLOCK
    sub = ATTN_SUB_BLOCKS
    assert nb % sub == 0
    ns = nb // sub
    kvw = k.shape[1]
    aw = q.shape[1]

    def edge_spec(off):
        return pl.BlockSpec((BLOCK, kvw), lambda b, n: (b * nb + jnp.clip(sub * n + off, 0, nb - 1), 0))

    centre = pl.BlockSpec((sub * BLOCK, kvw), lambda b, n: (b * ns + n, 0))
    bias_spec = lambda pick: pl.BlockSpec((None,) + bias.shape[1:], lambda b, n: (pick(n), 0, 0, 0))
    first = bias_spec(lambda n: jnp.where(n == 0, 0, 1))
    middle = bias_spec(lambda n: 1)
    last = bias_spec(lambda n: jnp.where(n == ns - 1, 2, 1))
    qspec = pl.BlockSpec((sub * BLOCK, aw), lambda b, n: (b * ns + n, 0))
    cast_in, cast_out, cast_shape = _cast_specs(cast_views, bsz * ns, lambda b, n: b * ns + n)
    return pl.pallas_call(
        functools.partial(_attn_body, len(cast_views)),
        grid=(bsz, ns),
        in_specs=[qspec] + [edge_spec(-1), centre, edge_spec(sub)] * 3 + [
                  first, middle, last,
                  pl.BlockSpec(memory_space=pltpu.SMEM)] + cast_in,
        out_specs=[qspec] + cast_out,
        out_shape=[jax.ShapeDtypeStruct(q.shape, BF16)] + cast_shape,
        compiler_params=_params(("parallel", "arbitrary")),
        name="attn",
    )(q, k, k, k, ks, ks, ks, v, v, v, bias, bias, bias, sink, *cast_views)


def _s5prep_body(lam_ref, ldt_ref, bt_ref, c_ref, w32_ref, wb_ref, wct_ref, t_ref, a_ref, w16_ref):
    w16_ref[...] = w32_ref[...].astype(BF16)
    width = CHUNK * SSM_CH
    lane = lax.broadcasted_iota(jnp.int32, (1, LANES), 1)
    in_group = [lane < SSM_STATE, lane >= SSM_STATE]
    lane_t = lax.broadcasted_iota(jnp.int32, (SSM_CH, width), 1)
    contract = (((1,), (1,)), ((), ()))
    hi = lax.Precision.HIGHEST
    toeplitz = [[None] * CHUNK for _ in range(PAIR)]
    for d in range(2):
        lre, lim = lam_ref[d, 0], lam_ref[d, 1]
        dt = jnp.exp(ldt_ref[d])
        mag = jnp.exp(lre * dt)
        are, aim = mag * jnp.cos(lim * dt), mag * jnp.sin(lim * dt)
        den = lre * lre + lim * lim
        cfr = ((are - 1.0) * lre + aim * lim) / den
        cfi = (aim * lre - (are - 1.0) * lim) / den
        btr, bti = bt_ref[d, 0], bt_ref[d, 1]
        bbr = btr * cfr - bti * cfi
        bbi = btr * cfi + bti * cfr
        cre, cim = c_ref[d, 0], c_ref[d, 1]

        pwr, pwi = jnp.ones_like(are), jnp.zeros_like(are)
        e_r, e_i = [], []
        for p in range(CHUNK + 1):
            er = cre * pwr - cim * pwi
            ei = cre * pwi + cim * pwr
            if p < CHUNK:
                e_r.append(er)
                e_i.append(ei)
                k = CHUNK - 1 - p if d == 0 else p
                for ri, val in enumerate((bbr * pwr - bbi * pwi, bbr * pwi + bbi * pwr)):
                    for g in range(PAIR):
                        wb_ref[width * g + SSM_CH * k:width * g + SSM_CH * (k + 1),
                               LANES * (2 * d + ri):LANES * (2 * d + ri + 1)] = jnp.where(
                                   in_group[g], val, 0.0).astype(BF16)
            if p >= 1:
                r = p - 1 if d == 0 else CHUNK - p
                for ri, val in enumerate((er, -ei)):
                    for g in range(PAIR):
                        wct_ref[width * g + SSM_CH * r:width * g + SSM_CH * (r + 1),
                                LANES * (2 * d + ri):LANES * (2 * d + ri + 1)] = jnp.where(
                                    in_group[g], val, 0.0).astype(BF16)
            if p == CHUNK:
                a_ref[2 * d] = jnp.broadcast_to(pwr, a_ref.shape[1:])
                a_ref[2 * d + 1] = jnp.broadcast_to(pwi, a_ref.shape[1:])
            pwr, pwi = pwr * are - pwi * aim, pwr * aim + pwi * are

        if d == 1:
            e_r, e_i = e_r[::-1], e_i[::-1]
        ecat_r, ecat_i = jnp.concatenate(e_r, axis=0), jnp.concatenate(e_i, axis=0)
        for g in range(PAIR):
            gr, gi = jnp.where(in_group[g], bbr, 0.0), jnp.where(in_group[g], bbi, 0.0)
            krow = (lax.dot_general(gr, ecat_r, contract, precision=hi, preferred_element_type=F32)
                    - lax.dot_general(gi, ecat_i, contract, precision=hi, preferred_element_type=F32))
            for k in range(CHUNK):
                if d == 0:
                    shift, keep = SSM_CH * k, lane_t >= SSM_CH * k
                else:
                    shift, keep = (SSM_CH * (k + 1)) % width, lane_t < SSM_CH * (k + 1)
                blk = jnp.where(keep, pltpu.roll(krow, shift, 1) if shift else krow, 0.0)
                toeplitz[g][k] = blk if d == 0 else toeplitz[g][k] + blk
    for g in range(PAIR):
        t_ref[g] = jnp.concatenate(toeplitz[g], axis=0).astype(BF16)


def _s5prep(lam, ldt, bt, c, w32):
    npair = lam.shape[1]
    width = CHUNK * SSM_CH
    pw = PAIR * width
    cast_in, cast_out, cast_shape = _cast_specs([w32], npair, lambda i: i)
    return pl.pallas_call(
        _s5prep_body,
        grid=(npair,),
        in_specs=[pl.BlockSpec((2, None, 2, 1, LANES), lambda i: (0, i, 0, 0, 0)),
                  pl.BlockSpec((2, None, 1, LANES), lambda i: (0, i, 0, 0)),
                  pl.BlockSpec((2, None, 2, SSM_CH, LANES), lambda i: (0, i, 0, 0, 0)),
                  pl.BlockSpec((2, None, 2, SSM_CH, LANES), lambda i: (0, i, 0, 0, 0))] + cast_in,
        out_specs=[pl.BlockSpec((None, pw, 4 * LANES), lambda i: (i, 0, 0)),
                   pl.BlockSpec((None, pw, 4 * LANES), lambda i: (i, 0, 0)),
                   pl.BlockSpec((PAIR, width, width), lambda i: (i, 0, 0)),
                   pl.BlockSpec((None, 4, SUBLANES, LANES), lambda i: (i, 0, 0, 0))] + cast_out,
        out_shape=[jax.ShapeDtypeStruct((npair, pw, 4 * LANES), BF16),
                   jax.ShapeDtypeStruct((npair, pw, 4 * LANES), BF16),
                   jax.ShapeDtypeStruct((npair * PAIR, width, width), BF16),
                   jax.ShapeDtypeStruct((npair, 4, SUBLANES, LANES), F32)] + cast_shape,
        compiler_params=_params(("parallel",)),
        name="s5prep",
    )(lam, ldt, bt, c, w32)


def _block_transpose(xs):
    n = len(xs)
    assert n * SSM_CH == LANES and n & (n - 1) == 0
    blk = lax.broadcasted_iota(jnp.int32, xs[0].shape, 1) // SSM_CH
    d = n // 2
    while d:
        upper = (blk & d) != 0
        ys = list(xs)
        for a in range(n):
            if a & d:
                continue
            b = a + d
            ys[a] = jnp.where(upper, pltpu.roll(xs[b], SSM_CH * d, 1), xs[a])
            ys[b] = jnp.where(upper, xs[b], pltpu.roll(xs[a], LANES - SSM_CH * d, 1))
        xs = ys
        d //= 2
    return xs


def _s5_scan(a_ref, w_scr, s_scr):
    rows = w_scr.shape[0]
    nsteps = rows // SUBLANES
    nstream = w_scr.shape[1] // (2 * LANES)
    half = SUBLANES // 2
    lo = lax.broadcasted_iota(jnp.int32, (SUBLANES, LANES), 0) < half
    first = [lo, jnp.logical_not(lo)]
    coef = [(a_ref[s // 2, 2 * (s % 2)], a_ref[s // 2, 2 * (s % 2) + 1]) for s in range(nstream)]
    col = lambda i: slice(LANES * i, LANES * (i + 1))

    def advance(a, xr, xi, wr, wi):
        return a[0] * xr - a[1] * xi + wr, a[0] * xi + a[1] * xr + wi

    def swap(x):
        return pltpu.roll(x, half, 0)

    def step(i, carry):
        out = []
        for s in range(nstream):
            d = s % 2
            cr, ci = carry[2 * s], carry[2 * s + 1]
            vreg = i if d == 0 else nsteps - 1 - i
            rs = pl.ds(pl.multiple_of(vreg * SUBLANES, SUBLANES), SUBLANES)
            wr, wi = w_scr[rs, col(2 * s)], w_scr[rs, col(2 * s + 1)]
            r0r, r0i = swap(cr), swap(ci)
            t1r, t1i = advance(coef[s], r0r, r0i, wr, wi)
            r1r, r1i = swap(t1r), swap(t1i)
            t2r, t2i = advance(coef[s], r1r, r1i, wr, wi)
            s_scr[rs, col(2 * s)] = jnp.where(first[d], r0r, r1r)
            s_scr[rs, col(2 * s + 1)] = jnp.where(first[d], r0i, r1i)
            out += [jnp.where(first[d], t1r, t2r), jnp.where(first[d], t1i, t2i)]
        return tuple(out)

    z = jnp.zeros((SUBLANES, LANES), F32)
    lax.fori_loop(0, nsteps, step, (z,) * (2 * nstream))


S5_SLAB_GROUPS = LANES // SSM_CH
S5_ROW_CHUNK = 256
S5_SUB_ROWS = 128
S5_SCAN_PAIRS = 2
S5_TOK_PITCH = CHUNK + SUBLANES


def _s5_body(u_ref, wb_ref, t_ref, wct_ref, a_ref, y_ref, ur_scr, w_scr, s_scr, sall_scr, yr_scr, tok_scr):
    rows = ur_scr.shape[0]
    width = CHUNK * SSM_CH
    pw = PAIR * width
    npair = S5_SLAB_GROUPS // PAIR
    half_steps = CHUNK // 2
    tok_rows = S5_ROW_CHUNK * CHUNK

    nchunks = rows // S5_ROW_CHUNK
    nsub = S5_ROW_CHUNK // S5_SUB_ROWS
    sw = 4 * LANES
    row_chunk = lambda c: slice(S5_ROW_CHUNK * c, S5_ROW_CHUNK * (c + 1))
    tok_chunk = lambda c: slice(tok_rows * c, tok_rows * (c + 1))
    step_rows = lambda i, step: pl.ds(S5_TOK_PITCH * S5_SUB_ROWS * i + step, S5_SUB_ROWS, stride=S5_TOK_PITCH)

    def copy_chunk_rows(c, tok, to_tok):
        for r in range(S5_ROW_CHUNK):
            ext = slice(tok_rows * c + r * CHUNK, tok_rows * c + (r + 1) * CHUNK)
            pad = slice(r * S5_TOK_PITCH, r * S5_TOK_PITCH + CHUNK)
            if to_tok:
                tok[pad, :] = u_ref[ext, :].astype(F32)
            else:
                y_ref[ext, :] = tok[pad, :].astype(BF16)
    lane_blk = lambda g, h: slice(width * g + LANES * h, width * g + LANES * (h + 1))

    def relayout_in(c):
        tok = tok_scr.at[c % 2]
        copy_chunk_rows(c, tok, True)
        for i in range(nsub):
            for h in range(2):
                xs = [tok[step_rows(i, half_steps * h + kk), :] for kk in range(half_steps)]
                ys = _block_transpose(xs)
                for g in range(S5_SLAB_GROUPS):
                    r0 = S5_ROW_CHUNK * c + S5_SUB_ROWS * i
                    ur_scr[r0:r0 + S5_SUB_ROWS, lane_blk(g, h)] = ys[g].astype(BF16)

    def state_in(c, p0):
        for q in range(S5_SCAN_PAIRS):
            pp = p0 + q
            w_scr[row_chunk(c), sw * q:sw * (q + 1)] = jnp.dot(
                ur_scr[row_chunk(c), pw * pp:pw * (pp + 1)], wb_ref[pp], preferred_element_type=F32)

    for p0 in range(0, npair, S5_SCAN_PAIRS):
        for c in range(nchunks):
            if p0 == 0:
                relayout_in(c)
            state_in(c, p0)
        _s5_scan(a_ref.at[p0:p0 + S5_SCAN_PAIRS], w_scr, s_scr)
        sall_scr[:, sw * p0:sw * (p0 + S5_SCAN_PAIRS)] = s_scr[...].astype(BF16)

    def outputs(c):
        yr, tok = yr_scr.at[c % 2], tok_scr.at[c % 2]
        for pp in range(npair):
            y = lax.dot_general(sall_scr[row_chunk(c), sw * pp:sw * (pp + 1)], wct_ref[pp],
                                (((1,), (1,)), ((), ())), preferred_element_type=F32)
            for g in range(PAIR):
                lo = pw * pp + width * g
                yr[:, lo:lo + width] = y[:, width * g:width * (g + 1)] + jnp.dot(
                    ur_scr[row_chunk(c), lo:lo + width], t_ref[PAIR * pp + g], preferred_element_type=F32)
        for i in range(nsub):
            for h in range(2):
                xs = [yr[S5_SUB_ROWS * i:S5_SUB_ROWS * (i + 1), lane_blk(g, h)].astype(BF16)
                      for g in range(S5_SLAB_GROUPS)]
                ys = _block_transpose(xs)
                for kk in range(half_steps):
                    tok[step_rows(i, half_steps * h + kk), :] = ys[kk].astype(F32)
        copy_chunk_rows(c, tok, False)

    for c in range(nchunks):
        outputs(c)


def _s5(u_tok, wb, tsum, wct, a16):
    T, ssm_w = u_tok.shape
    rows = T // CHUNK
    assert rows % S5_ROW_CHUNK == 0 and S5_SLAB_GROUPS == CHUNK // 2
    width = CHUNK * SSM_CH
    pw = PAIR * width
    npair = S5_SLAB_GROUPS // PAIR
    slab = pl.BlockSpec((T, LANES), lambda s: (0, s))
    return pl.pallas_call(
        _s5_body,
        grid=(ssm_w // LANES,),
        in_specs=[slab,
                  pl.BlockSpec((npair, pw, 4 * LANES), lambda s: (s, 0, 0)),
                  pl.BlockSpec((S5_SLAB_GROUPS, width, width), lambda s: (s, 0, 0)),
                  pl.BlockSpec((npair, pw, 4 * LANES), lambda s: (s, 0, 0)),
                  pl.BlockSpec((npair, 4, SUBLANES, LANES), lambda s: (s, 0, 0, 0))],
        out_specs=slab,
        out_shape=jax.ShapeDtypeStruct((T, ssm_w), BF16),
        scratch_shapes=[pltpu.VMEM((rows, S5_SLAB_GROUPS * width), BF16),
                        pltpu.VMEM((rows, S5_SCAN_PAIRS * 4 * LANES), F32),
                        pltpu.VMEM((rows, S5_SCAN_PAIRS * 4 * LANES), F32),
                        pltpu.VMEM((rows, npair * 4 * LANES), BF16),
                        pltpu.VMEM((2, S5_ROW_CHUNK, S5_SLAB_GROUPS * width), F32),
                        pltpu.VMEM((2, S5_ROW_CHUNK * S5_TOK_PITCH, LANES), F32)],
        compiler_params=_params(("parallel",)),
        name="s5",
    )(u_tok, wb, tsum, wct, a16)


def _merge_body(attn_ref, ys_ref, u_ref, g_ref, x_ref, wba_ref, wglu_ref, bglu_ref, d_ref, wbs_ref,
                wout_ref, gpost_ref, gpre_ref, x1_ref, h2_ref):
    tm, d = x_ref.shape
    u = u_ref[...].reshape(tm, -1).astype(F32)
    y = ys_ref[...].reshape(tm, -1).astype(F32) + d_ref[...] * u
    y = _gelu_tanh(y)
    gl = jnp.dot(y.astype(BF16), wglu_ref[...], preferred_element_type=F32) + bglu_ref[...]
    s = y * jax.nn.sigmoid(gl)
    y_ssm = jnp.dot(s.astype(BF16), wbs_ref[...], preferred_element_type=F32)
    y_attn = jnp.dot(attn_ref[...], wba_ref[...], preferred_element_type=F32)
    merged = g_ref[:, :d].astype(F32) * y_attn + g_ref[:, d:].astype(F32) * y_ssm
    o = jnp.dot(merged.astype(BF16), wout_ref[...], preferred_element_type=F32)
    x1 = x_ref[...] + _rms(o, gpost_ref[...])
    x1_ref[...] = x1
    h2_ref[...] = _rms(x1, gpre_ref[...]).astype(BF16)


def _merge(attn, ys, u, gates, x2, wba, wglu, bglu, ssm_d, wbs, wout, gpost, gpre, seq, tm):
    T, D = x2.shape
    row = lambda w: pl.BlockSpec((tm, w), lambda i: (i, 0))
    chunked = _chunk_rows_spec(tm, seq // tm, u.shape[-1])
    consts = [wba, wglu, bglu, ssm_d, wbs, wout, gpost, gpre]
    return pl.pallas_call(
        _merge_body,
        grid=(T // tm,),
        in_specs=[row(attn.shape[1]), chunked, chunked, row(gates.shape[1]), row(D)]
                 + [_const_spec(c.shape) for c in consts],
        out_specs=[row(D), row(D)],
        out_shape=[jax.ShapeDtypeStruct((T, D), F32), jax.ShapeDtypeStruct((T, D), BF16)],
        compiler_params=_params(("parallel",)),
        name="merge",
    )(attn, ys, u, gates, x2, *consts)


def _ffn_body(tiles_per_seq, hp_ref, hc_ref, hn_ref, x1_ref, wg_ref, wu_ref, cw_ref, cb_ref, wd_ref,
              gn_ref, o_ref, hext, fscr):
    tm = hc_ref.shape[0]
    i = pl.program_id(0) % tiles_per_seq
    zero = jnp.zeros((), BF16)
    hext[0:HALO] = jnp.where(i == 0, zero, hp_ref[...])
    hext[HALO:HALO + tm] = hc_ref[...]
    hext[HALO + tm:] = jnp.where(i == tiles_per_seq - 1, zero, hn_ref[...])
    ext = tm + 2 * HALO
    for c in range(wg_ref.shape[1] // FF_TILE):
        sl = slice(FF_TILE * c, FF_TILE * (c + 1))
        g = jnp.dot(hext[...], wg_ref[:, sl], preferred_element_type=F32)
        up = jnp.dot(hc_ref[...], wu_ref[:, sl], preferred_element_type=F32)
        g_prev = pltpu.roll(g, 1, 0)[HALO:HALO + tm]
        g_next = pltpu.roll(g, ext - 1, 0)[HALO:HALO + tm]
        conv = (g_prev * cw_ref[0:1, sl] + g[HALO:HALO + tm] * cw_ref[1:2, sl]
                + g_next * cw_ref[2:3, sl] + cb_ref[:, sl])
        fscr[:, sl] = (_gelu_tanh(conv) * up).astype(BF16)
    o = jnp.dot(fscr[...], wd_ref[...], preferred_element_type=F32)
    o_ref[...] = x1_ref[...] + _rms(o, gn_ref[...])


def _ffn(h2, x1, wg, wu, cw, cb, wd, gn, seq, tm):
    T, D = x1.shape
    dff = wg.shape[1]
    assert dff % FF_TILE == 0 and tm % HALO == 0 and seq % tm == 0
    per = tm // HALO
    nh = T // HALO
    row = lambda w: pl.BlockSpec((tm, w), lambda i: (i, 0))
    once = lambda shape: pl.BlockSpec(shape, lambda i: (0,) * len(shape), pipeline_mode=pl.Buffered(1))
    return pl.pallas_call(
        functools.partial(_ffn_body, seq // tm),
        grid=(T // tm,),
        in_specs=[pl.BlockSpec((HALO, D), lambda i: (jnp.maximum(i * per - 1, 0), 0)),
                  row(D),
                  pl.BlockSpec((HALO, D), lambda i: (jnp.minimum((i + 1) * per, nh - 1), 0)),
                  row(D),
                  once(wg.shape), once(wu.shape), _const_spec(cw.shape), _const_spec(cb.shape),
                  once(wd.shape), _const_spec(gn.shape)],
        out_specs=row(D),
        out_shape=jax.ShapeDtypeStruct((T, D), F32),
        scratch_shapes=[pltpu.VMEM((tm + 2 * HALO, D), BF16), pltpu.VMEM((tm, dff), BF16)],
        compiler_params=_params(("parallel",)),
        name="ffn",
    )(h2, h2, h2, x1, wg, wu, cw, cb, wd, gn)


def _row_tile(seq):
    return min(1024, seq)


def _layer(x, p):
    bsz, seq, d = x.shape
    T = bsz * seq
    tm = _row_tile(seq)
    ssm_w = d // 2
    ngroups = ssm_w // SSM_CH
    row2 = lambda v: v.reshape(1, -1).astype(F32)

    npair = ngroups // PAIR
    both = lambda re, im: jnp.stack([jnp.stack([p[re + "_f"], p[im + "_f"]]),
                                     jnp.stack([p[re + "_b"], p[im + "_b"]])]).astype(F32)
    lam = both("lam_re", "lam_im").reshape(2, 2, npair, 1, LANES).transpose(0, 2, 1, 3, 4)
    ldt = jnp.repeat(jnp.stack([p["log_dt_f"], p["log_dt_b"]]).astype(F32), SSM_STATE, axis=-1)
    ldt = ldt.reshape(2, npair, 1, LANES)
    bt = both("b_re", "b_im").reshape(2, 2, npair, PAIR, SSM_STATE, SSM_CH)
    bt = bt.transpose(0, 2, 1, 5, 3, 4).reshape(2, npair, 2, SSM_CH, LANES)
    cc = both("c_re", "c_im").reshape(2, 2, npair, PAIR, SSM_CH, SSM_STATE)
    cc = cc.transpose(0, 2, 1, 4, 3, 5).reshape(2, npair, 2, SSM_CH, LANES)
    wb, wct, tsum, a, w_in = _s5prep(lam, ldt, bt, cc, p["w_in"])

    x2 = x.reshape(T, d)
    q, k, ks, v, u, gates = _inproj(x2, row2(p["norm_mix_pre"]), w_in, bsz, seq, tm)

    later = ("w_ffn_gate", "w_ffn_up", "w_ffn_down", "w_out", "w_branch_attn", "w_branch_ssm", "w_glu")
    attn, *rounded = _attention(q, k, ks, v, jnp.asarray(_attn_bias()), p["attn_sink"].astype(F32), bsz, seq,
                                [p[name] for name in later])
    w16 = dict(zip(later, rounded))

    assert bsz == SUBLANES // 2
    ys = _s5(u.reshape(T, ssm_w), wb, tsum, wct, a).reshape(u.shape)

    x1, h2 = _merge(attn, ys, u, gates, x2,
                    w16["w_branch_attn"], w16["w_glu"], row2(p["b_glu"]),
                    row2(p["ssm_d"]), w16["w_branch_ssm"], w16["w_out"],
                    row2(p["norm_mix_post"]), row2(p["norm_ffn_pre"]), seq, tm)

    out = _ffn(h2, x1, w16["w_ffn_gate"], w16["w_ffn_up"], p["conv_w"].astype(F32),
               row2(p["conv_b"]), w16["w_ffn_down"], row2(p["norm_ffn_post"]), seq, tm)
    return out.reshape(bsz, seq, d)


_PARAM_NAMES = (
    "norm_mix_pre", "w_in", "attn_sink",
    "lam_re_f", "lam_im_f", "log_dt_f", "b_re_f", "b_im_f", "c_re_f", "c_im_f",
    "lam_re_b", "lam_im_b", "log_dt_b", "b_re_b", "b_im_b", "c_re_b", "c_im_b",
    "ssm_d", "w_glu", "b_glu", "w_branch_attn", "w_branch_ssm", "w_out", "norm_mix_post",
    "norm_ffn_pre", "w_ffn_gate", "w_ffn_up", "conv_w", "conv_b", "w_ffn_down", "norm_ffn_post")


def kernel(x, norm_mix_pre, w_in, attn_sink, lam_re_f, lam_im_f, log_dt_f, b_re_f, b_im_f, c_re_f, c_im_f, lam_re_b, lam_im_b, log_dt_b, b_re_b, b_im_b, c_re_b, c_im_b, ssm_d, w_glu, b_glu, w_branch_attn, w_branch_ssm, w_out, norm_mix_post, norm_ffn_pre, w_ffn_gate, w_ffn_up, conv_w, conv_b, w_ffn_down, norm_ffn_post):
    stacked = dict(zip(_PARAM_NAMES, (
        norm_mix_pre, w_in, attn_sink,
        lam_re_f, lam_im_f, log_dt_f, b_re_f, b_im_f, c_re_f, c_im_f,
        lam_re_b, lam_im_b, log_dt_b, b_re_b, b_im_b, c_re_b, c_im_b,
        ssm_d, w_glu, b_glu, w_branch_attn, w_branch_ssm, w_out, norm_mix_post,
        norm_ffn_pre, w_ffn_gate, w_ffn_up, conv_w, conv_b, w_ffn_down, norm_ffn_post)))
    for layer in range(w_in.shape[0]):
        x = _layer(x, {name: value[layer] for name, value in stacked.items()})
    return x
```

```python
import functools
import math

import numpy as np
import jax
import jax.numpy as jnp
from jax import lax
from jax.experimental import pallas as pl
from jax.experimental.pallas import tpu as pltpu

F32 = jnp.float32
BF16 = jnp.bfloat16

N_HEADS = 8
N_KV_HEADS = 2
HEAD_DIM = 64
GROUP = N_HEADS // N_KV_HEADS
WINDOW = 128
BLOCK = 128
ATTN_SUB_BLOCKS = 8
SSM_CH = 16
SSM_STATE = 64
CHUNK = 16
PAIR = 2
N_BRANCH = 2
EPS = 1e-6
NEG_INF = -1e30
LOG2E = math.log2(math.e)

LANES = 128
SUBLANES = 8
BF16_SUBLANES = 2 * SUBLANES
HALO = BF16_SUBLANES
FF_TILE = 256
V7X_VMEM_BYTES = 64 * 1024 * 1024
VMEM_LIMIT = V7X_VMEM_BYTES * 7 // 8


def _gelu_tanh(x):
    return 0.5 * x * (1.0 + jnp.tanh(math.sqrt(2.0 / math.pi) * (x + 0.044715 * (x * x * x))))


def _rms(x, gain):
    return x * lax.rsqrt(jnp.mean(x * x, axis=-1, keepdims=True) + EPS) * gain


def _params(semantics):
    return pltpu.CompilerParams(dimension_semantics=semantics, vmem_limit_bytes=VMEM_LIMIT)


def _const_spec(shape):
    nd = len(shape)
    return pl.BlockSpec(shape, lambda *_: (0,) * nd)


def _inproj_body(splits, ncast, x_ref, g_ref, w_ref, *refs):
    q_ref, k_ref, ks_ref, v_ref, u_ref, gate_ref = refs[ncast:ncast + 6]
    for src, dst in zip(refs[:ncast], refs[ncast + 6:]):
        dst[...] = src[...].astype(BF16)
    h = _rms(x_ref[...], g_ref[...]).astype(BF16)

    def proj(i):
        return jnp.dot(h, w_ref[:, splits[i]:splits[i + 1]], preferred_element_type=F32)

    q_ref[...] = (proj(0) * (HEAD_DIM ** -0.5 * LOG2E)).astype(BF16)
    k = proj(1).astype(BF16)
    k_ref[...] = k
    ks_ref[...] = pltpu.roll(k, HEAD_DIM, 1)
    v_ref[...] = proj(2).astype(BF16)
    u_ref[...] = proj(3).astype(BF16).reshape(u_ref.shape)
    gate_ref[...] = jax.nn.sigmoid(proj(4)).astype(BF16)


def _chunk_rows_spec(tm, tiles_per_seq, width):
    return pl.BlockSpec((tm // CHUNK, None, CHUNK, width),
                        lambda i: (i % tiles_per_seq, i // tiles_per_seq, 0, 0))


def _inproj(x2, gain, w_in, bsz, seq, tm, cast_views):
    T, D = x2.shape
    attn_w = N_HEADS * HEAD_DIM
    kv_w = N_KV_HEADS * HEAD_DIM
    widths = [attn_w, kv_w, kv_w, D // 2, N_BRANCH * D]
    splits = [0] + np.cumsum(widths).tolist()
    row = lambda w: pl.BlockSpec((tm, w), lambda i: (i, 0))
    out_widths = [attn_w, kv_w, kv_w, kv_w, D // 2, N_BRANCH * D]
    out_specs = [row(w) for w in out_widths]
    out_shape = [jax.ShapeDtypeStruct((T, w), BF16) for w in out_widths]
    out_specs[4] = _chunk_rows_spec(tm, seq // tm, out_widths[4])
    out_shape[4] = jax.ShapeDtypeStruct((seq // CHUNK, bsz, CHUNK, out_widths[4]), BF16)
    cast_in, cast_out, cast_shape = _cast_specs(cast_views, T // tm, lambda i: i)
    return pl.pallas_call(
        functools.partial(_inproj_body, tuple(splits), len(cast_views)),
        grid=(T // tm,),
        in_specs=[row(D), _const_spec((1, D)), _const_spec(w_in.shape)] + cast_in,
        out_specs=out_specs + cast_out,
        out_shape=out_shape + cast_shape,
        compiler_params=_params(("arbitrary",)),
        name="inproj",
    )(x2, gain, w_in, *cast_views)


def _cast_specs(weights, steps, step_of):
    specs = []
    for w in weights:
        nblocks = max(n for n in range(1, steps + 1)
                      if steps % n == 0 and w.shape[0] % (n * BF16_SUBLANES) == 0)
        hold = steps // nblocks
        specs.append(pl.BlockSpec((w.shape[0] // nblocks, w.shape[1]),
                                  lambda *ids, hold=hold: (step_of(*ids) // hold, 0)))
    return specs, specs, [jax.ShapeDtypeStruct(w.shape, BF16) for w in weights]


def _attn_body(ncast, q_ref, kp_ref, kc_ref, kn_ref, sp_ref, sc_ref, sn_ref, vp_ref, vc_ref, vn_ref,
               bias_first_ref, bias_mid_ref, bias_last_ref, sink_ref, *refs):
    o_ref = refs[ncast]
    for src, dst in zip(refs[:ncast], refs[ncast + 1:]):
        dst[...] = src[...].astype(BF16)
    bias_refs = [bias_first_ref] + [bias_mid_ref] * (ATTN_SUB_BLOCKS - 2) + [bias_last_ref]
    lane = lax.broadcasted_iota(jnp.int32, (1, LANES), 1)
    half_mask = [lane < HEAD_DIM, lane >= HEAD_DIM]
    zero = jnp.zeros((), BF16)
    kall = [jnp.concatenate([kp_ref[...], kc_ref[...], kn_ref[...]], axis=0),
            jnp.concatenate([sp_ref[...], sc_ref[...], sn_ref[...]], axis=0)]
    vall = jnp.concatenate([vp_ref[...], vc_ref[...], vn_ref[...]], axis=0)
    for b, bias_ref in enumerate(bias_refs):
        keys = slice(BLOCK * b, BLOCK * (b + 3))
        vcat = vall[keys]
        rows = slice(BLOCK * b, BLOCK * (b + 1))
        for j in range(N_HEADS // 2):
            qj = q_ref[rows, LANES * j:LANES * (j + 1)]
            g = (2 * j) // GROUP
            halves = []
            for e in range(2):
                h = 2 * j + e
                qm = jnp.where(half_mask[e], qj, zero)
                kcat = kall[0 if e == g else 1][keys]
                st = lax.dot_general(kcat, qm, (((1,), (1,)), ((), ())), preferred_element_type=F32)
                st = st + bias_ref[h]
                sink = sink_ref[h] * LOG2E
                m = jnp.maximum(jnp.max(st, axis=0, keepdims=True), sink)
                p = jnp.exp2(st - m)
                denom = jnp.sum(p, axis=0, keepdims=True) + jnp.exp2(sink - m)
                ot = lax.dot_general(vcat, p.astype(BF16), (((0,), (0,)), ((), ())),
                                     preferred_element_type=F32)
                halves.append(ot[HEAD_DIM * g:HEAD_DIM * (g + 1)] / denom)
            o_ref[rows, LANES * j:LANES * (j + 1)] = jnp.concatenate(halves, axis=0).T.astype(BF16)


def _attn_bias():
    t = np.arange(BLOCK)[None, :]
    s = np.arange(3 * BLOCK)[:, None] - BLOCK
    dist = np.abs(t - s).astype(np.float64)
    valid = dist <= WINDOW
    slopes = 2.0 ** (-8.0 * (np.arange(N_HEADS) + 1.0) / N_HEADS)
    out = np.zeros((3, N_HEADS, 3 * BLOCK, BLOCK), np.float32)
    for variant in range(3):
        ok = valid.copy()
        if variant == 0:
            ok[:BLOCK] = False
        if variant == 2:
            ok[2 * BLOCK:] = False
        for h in range(N_HEADS):
            out[variant, h] = np.where(ok, -slopes[h] * dist * LOG2E, NEG_INF)
    return out


def _attention(q, k, ks, v, bias, sink, bsz, seq, cast_views):
    nb = seq // BLOCK
    sub = ATTN_SUB_BLOCKS
    assert nb % sub == 0
    ns = nb // sub
    kvw = k.shape[1]
    aw = q.shape[1]

    def edge_spec(off):
        return pl.BlockSpec((BLOCK, kvw), lambda b, n: (b * nb + jnp.clip(sub * n + off, 0, nb - 1), 0))

    centre = pl.BlockSpec((sub * BLOCK, kvw), lambda b, n: (b * ns + n, 0))
    bias_spec = lambda pick: pl.BlockSpec((None,) + bias.shape[1:], lambda b, n: (pick(n), 0, 0, 0))
    first = bias_spec(lambda n: jnp.where(n == 0, 0, 1))
    middle = bias_spec(lambda n: 1)
    last = bias_spec(lambda n: jnp.where(n == ns - 1, 2, 1))
    qspec = pl.BlockSpec((sub * BLOCK, aw), lambda b, n: (b * ns + n, 0))
    cast_in, cast_out, cast_shape = _cast_specs(cast_views, bsz * ns, lambda b, n: b * ns + n)
    return pl.pallas_call(
        functools.partial(_attn_body, len(cast_views)),
        grid=(bsz, ns),
        in_specs=[qspec] + [edge_spec(-1), centre, edge_spec(sub)] * 3 + [
                  first, middle, last,
                  pl.BlockSpec(memory_space=pltpu.SMEM)] + cast_in,
        out_specs=[qspec] + cast_out,
        out_shape=[jax.ShapeDtypeStruct(q.shape, BF16)] + cast_shape,
        compiler_params=_params(("parallel", "arbitrary")),
        name="attn",
    )(q, k, k, k, ks, ks, ks, v, v, v, bias, bias, bias, sink, *cast_views)


def _s5prep_body(lam_ref, ldt_ref, bt_ref, c_ref, w32_ref, wb_ref, wct_ref, t_ref, a_ref, w16_ref):
    w16_ref[...] = w32_ref[...].astype(BF16)
    width = CHUNK * SSM_CH
    lane = lax.broadcasted_iota(jnp.int32, (1, LANES), 1)
    in_group = [lane < SSM_STATE, lane >= SSM_STATE]
    lane_t = lax.broadcasted_iota(jnp.int32, (SSM_CH, width), 1)
    contract = (((1,), (1,)), ((), ()))
    hi = lax.Precision.HIGHEST
    toeplitz = [[None] * CHUNK for _ in range(PAIR)]
    for d in range(2):
        lre, lim = lam_ref[d, 0], lam_ref[d, 1]
        dt = jnp.exp(ldt_ref[d])
        mag = jnp.exp(lre * dt)
        are, aim = mag * jnp.cos(lim * dt), mag * jnp.sin(lim * dt)
        den = lre * lre + lim * lim
        cfr = ((are - 1.0) * lre + aim * lim) / den
        cfi = (aim * lre - (are - 1.0) * lim) / den
        btr, bti = bt_ref[d, 0], bt_ref[d, 1]
        bbr = btr * cfr - bti * cfi
        bbi = btr * cfi + bti * cfr
        cre, cim = c_ref[d, 0], c_ref[d, 1]

        pwr, pwi = jnp.ones_like(are), jnp.zeros_like(are)
        e_r, e_i = [], []
        for p in range(CHUNK + 1):
            er = cre * pwr - cim * pwi
            ei = cre * pwi + cim * pwr
            if p < CHUNK:
                e_r.append(er)
                e_i.append(ei)
                k = CHUNK - 1 - p if d == 0 else p
                for ri, val in enumerate((bbr * pwr - bbi * pwi, bbr * pwi + bbi * pwr)):
                    for g in range(PAIR):
                        wb_ref[width * g + SSM_CH * k:width * g + SSM_CH * (k + 1),
                               LANES * (2 * d + ri):LANES * (2 * d + ri + 1)] = jnp.where(
                                   in_group[g], val, 0.0).astype(BF16)
            if p >= 1:
                r = p - 1 if d == 0 else CHUNK - p
                for ri, val in enumerate((er, -ei)):
                    for g in range(PAIR):
                        wct_ref[width * g + SSM_CH * r:width * g + SSM_CH * (r + 1),
                                LANES * (2 * d + ri):LANES * (2 * d + ri + 1)] = jnp.where(
                                    in_group[g], val, 0.0).astype(BF16)
            if p == CHUNK:
                a_ref[2 * d] = jnp.broadcast_to(pwr, a_ref.shape[1:])
                a_ref[2 * d + 1] = jnp.broadcast_to(pwi, a_ref.shape[1:])
            pwr, pwi = pwr * are - pwi * aim, pwr * aim + pwi * are

        if d == 1:
            e_r, e_i = e_r[::-1], e_i[::-1]
        ecat_r, ecat_i = jnp.concatenate(e_r, axis=0), jnp.concatenate(e_i, axis=0)
        for g in range(PAIR):
            gr, gi = jnp.where(in_group[g], bbr, 0.0), jnp.where(in_group[g], bbi, 0.0)
            krow = (lax.dot_general(gr, ecat_r, contract, precision=hi, preferred_element_type=F32)
                    - lax.dot_general(gi, ecat_i, contract, precision=hi, preferred_element_type=F32))
            for k in range(CHUNK):
                if d == 0:
                    shift, keep = SSM_CH * k, lane_t >= SSM_CH * k
                else:
                    shift, keep = (SSM_CH * (k + 1)) % width, lane_t < SSM_CH * (k + 1)
                blk = jnp.where(keep, pltpu.roll(krow, shift, 1) if shift else krow, 0.0)
                toeplitz[g][k] = blk if d == 0 else toeplitz[g][k] + blk
    for g in range(PAIR):
        t_ref[g] = jnp.concatenate(toeplitz[g], axis=0).astype(BF16)


def _s5prep(lam, ldt, bt, c, w32):
    npair = lam.shape[1]
    width = CHUNK * SSM_CH
    pw = PAIR * width
    cast_in, cast_out, cast_shape = _cast_specs([w32], npair, lambda i: i)
    return pl.pallas_call(
        _s5prep_body,
        grid=(npair,),
        in_specs=[pl.BlockSpec((2, None, 2, 1, LANES), lambda i: (0, i, 0, 0, 0)),
                  pl.BlockSpec((2, None, 1, LANES), lambda i: (0, i, 0, 0)),
                  pl.BlockSpec((2, None, 2, SSM_CH, LANES), lambda i: (0, i, 0, 0, 0)),
                  pl.BlockSpec((2, None, 2, SSM_CH, LANES), lambda i: (0, i, 0, 0, 0))] + cast_in,
        out_specs=[pl.BlockSpec((None, pw, 4 * LANES), lambda i: (i, 0, 0)),
                   pl.BlockSpec((None, pw, 4 * LANES), lambda i: (i, 0, 0)),
                   pl.BlockSpec((PAIR, width, width), lambda i: (i, 0, 0)),
                   pl.BlockSpec((None, 4, SUBLANES, LANES), lambda i: (i, 0, 0, 0))] + cast_out,
        out_shape=[jax.ShapeDtypeStruct((npair, pw, 4 * LANES), BF16),
                   jax.ShapeDtypeStruct((npair, pw, 4 * LANES), BF16),
                   jax.ShapeDtypeStruct((npair * PAIR, width, width), BF16),
                   jax.ShapeDtypeStruct((npair, 4, SUBLANES, LANES), F32)] + cast_shape,
        compiler_params=_params(("parallel",)),
        name="s5prep",
    )(lam, ldt, bt, c, w32)


def _block_transpose(xs):
    n = len(xs)
    assert n * SSM_CH == LANES and n & (n - 1) == 0
    blk = lax.broadcasted_iota(jnp.int32, xs[0].shape, 1) // SSM_CH
    d = n // 2
    while d:
        upper = (blk & d) != 0
        ys = list(xs)
        for a in range(n):
            if a & d:
                continue
            b = a + d
            ys[a] = jnp.where(upper, pltpu.roll(xs[b], SSM_CH * d, 1), xs[a])
            ys[b] = jnp.where(upper, xs[b], pltpu.roll(xs[a], LANES - SSM_CH * d, 1))
        xs = ys
        d //= 2
    return xs


def _s5_scan(a_ref, w_scr, s_scr):
    rows = w_scr.shape[0]
    nsteps = rows // SUBLANES
    nstream = w_scr.shape[1] // (2 * LANES)
    half = SUBLANES // 2
    lo = lax.broadcasted_iota(jnp.int32, (SUBLANES, LANES), 0) < half
    first = [lo, jnp.logical_not(lo)]
    coef = [(a_ref[s // 2, 2 * (s % 2)], a_ref[s // 2, 2 * (s % 2) + 1]) for s in range(nstream)]
    col = lambda i: slice(LANES * i, LANES * (i + 1))

    def advance(a, xr, xi, wr, wi):
        return a[0] * xr - a[1] * xi + wr, a[0] * xi + a[1] * xr + wi

    def swap(x):
        return pltpu.roll(x, half, 0)

    def step(i, carry):
        out = []
        for s in range(nstream):
            d = s % 2
            cr, ci = carry[2 * s], carry[2 * s + 1]
            vreg = i if d == 0 else nsteps - 1 - i
            rs = pl.ds(pl.multiple_of(vreg * SUBLANES, SUBLANES), SUBLANES)
            wr, wi = w_scr[rs, col(2 * s)], w_scr[rs, col(2 * s + 1)]
            r0r, r0i = swap(cr), swap(ci)
            t1r, t1i = advance(coef[s], r0r, r0i, wr, wi)
            r1r, r1i = swap(t1r), swap(t1i)
            t2r, t2i = advance(coef[s], r1r, r1i, wr, wi)
            s_scr[rs, col(2 * s)] = jnp.where(first[d], r0r, r1r)
            s_scr[rs, col(2 * s + 1)] = jnp.where(first[d], r0i, r1i)
            out += [jnp.where(first[d], t1r, t2r), jnp.where(first[d], t1i, t2i)]
        return tuple(out)

    z = jnp.zeros((SUBLANES, LANES), F32)
    lax.fori_loop(0, nsteps, step, (z,) * (2 * nstream))


S5_SLAB_GROUPS = LANES // SSM_CH
S5_ROW_CHUNK = 256
S5_SUB_ROWS = 128
S5_SCAN_PAIRS = 2
S5_TOK_PITCH = CHUNK + SUBLANES


def _s5_body(u_ref, wb_ref, t_ref, wct_ref, a_ref, y_ref, ur_scr, w_scr, s_scr, sall_scr, yr_scr, tok_scr):
    rows = ur_scr.shape[0]
    width = CHUNK * SSM_CH
    pw = PAIR * width
    npair = S5_SLAB_GROUPS // PAIR
    half_steps = CHUNK // 2
    tok_rows = S5_ROW_CHUNK * CHUNK

    nchunks = rows // S5_ROW_CHUNK
    nsub = S5_ROW_CHUNK // S5_SUB_ROWS
    sw = 4 * LANES
    row_chunk = lambda c: slice(S5_ROW_CHUNK * c, S5_ROW_CHUNK * (c + 1))
    tok_chunk = lambda c: slice(tok_rows * c, tok_rows * (c + 1))
    step_rows = lambda i, step: pl.ds(S5_TOK_PITCH * S5_SUB_ROWS * i + step, S5_SUB_ROWS, stride=S5_TOK_PITCH)

    def copy_chunk_rows(c, tok, to_tok):
        for r in range(S5_ROW_CHUNK):
            ext = slice(tok_rows * c + r * CHUNK, tok_rows * c + (r + 1) * CHUNK)
            pad = slice(r * S5_TOK_PITCH, r * S5_TOK_PITCH + CHUNK)
            if to_tok:
                tok[pad, :] = u_ref[ext, :].astype(F32)
            else:
                y_ref[ext, :] = tok[pad, :].astype(BF16)
    lane_blk = lambda g, h: slice(width * g + LANES * h, width * g + LANES * (h + 1))

    def relayout_in(c):
        tok = tok_scr.at[c % 2]
        copy_chunk_rows(c, tok, True)
        for i in range(nsub):
            for h in range(2):
                xs = [tok[step_rows(i, half_steps * h + kk), :] for kk in range(half_steps)]
                ys = _block_transpose(xs)
                for g in range(S5_SLAB_GROUPS):
                    r0 = S5_ROW_CHUNK * c + S5_SUB_ROWS * i
                    ur_scr[r0:r0 + S5_SUB_ROWS, lane_blk(g, h)] = ys[g].astype(BF16)

    def state_in(c, p0):
        for q in range(S5_SCAN_PAIRS):
            pp = p0 + q
            w_scr[row_chunk(c), sw * q:sw * (q + 1)] = jnp.dot(
                ur_scr[row_chunk(c), pw * pp:pw * (pp + 1)], wb_ref[pp], preferred_element_type=F32)

    for p0 in range(0, npair, S5_SCAN_PAIRS):
        for c in range(nchunks):
            if p0 == 0:
                relayout_in(c)
            state_in(c, p0)
        _s5_scan(a_ref.at[p0:p0 + S5_SCAN_PAIRS], w_scr, s_scr)
        sall_scr[:, sw * p0:sw * (p0 + S5_SCAN_PAIRS)] = s_scr[...].astype(BF16)

    def outputs(c):
        yr, tok = yr_scr.at[c % 2], tok_scr.at[c % 2]
        for pp in range(npair):
            y = lax.dot_general(sall_scr[row_chunk(c), sw * pp:sw * (pp + 1)], wct_ref[pp],
                                (((1,), (1,)), ((), ())), preferred_element_type=F32)
            for g in range(PAIR):
                lo = pw * pp + width * g
                yr[:, lo:lo + width] = y[:, width * g:width * (g + 1)] + jnp.dot(
                    ur_scr[row_chunk(c), lo:lo + width], t_ref[PAIR * pp + g], preferred_element_type=F32)
        for i in range(nsub):
            for h in range(2):
                xs = [yr[S5_SUB_ROWS * i:S5_SUB_ROWS * (i + 1), lane_blk(g, h)].astype(BF16)
                      for g in range(S5_SLAB_GROUPS)]
                ys = _block_transpose(xs)
                for kk in range(half_steps):
                    tok[step_rows(i, half_steps * h + kk), :] = ys[kk].astype(F32)
        copy_chunk_rows(c, tok, False)

    for c in range(nchunks):
        outputs(c)


def _s5(u_tok, wb, tsum, wct, a16):
    T, ssm_w = u_tok.shape
    rows = T // CHUNK
    assert rows % S5_ROW_CHUNK == 0 and S5_SLAB_GROUPS == CHUNK // 2
    width = CHUNK * SSM_CH
    pw = PAIR * width
    npair = S5_SLAB_GROUPS // PAIR
    slab = pl.BlockSpec((T, LANES), lambda s: (0, s))
    return pl.pallas_call(
        _s5_body,
        grid=(ssm_w // LANES,),
        in_specs=[slab,
                  pl.BlockSpec((npair, pw, 4 * LANES), lambda s: (s, 0, 0)),
                  pl.BlockSpec((S5_SLAB_GROUPS, width, width), lambda s: (s, 0, 0)),
                  pl.BlockSpec((npair, pw, 4 * LANES), lambda s: (s, 0, 0)),
                  pl.BlockSpec((npair, 4, SUBLANES, LANES), lambda s: (s, 0, 0, 0))],
        out_specs=slab,
        out_shape=jax.ShapeDtypeStruct((T, ssm_w), BF16),
        scratch_shapes=[pltpu.VMEM((rows, S5_SLAB_GROUPS * width), BF16),
                        pltpu.VMEM((rows, S5_SCAN_PAIRS * 4 * LANES), F32),
                        pltpu.VMEM((rows, S5_SCAN_PAIRS * 4 * LANES), F32),
                        pltpu.VMEM((rows, npair * 4 * LANES), BF16),
                        pltpu.VMEM((2, S5_ROW_CHUNK, S5_SLAB_GROUPS * width), F32),
                        pltpu.VMEM((2, S5_ROW_CHUNK * S5_TOK_PITCH, LANES), F32)],
        compiler_params=_params(("parallel",)),
        name="s5",
    )(u_tok, wb, tsum, wct, a16)


def _merge_body(attn_ref, ys_ref, u_ref, g_ref, x_ref, wba_ref, wglu_ref, bglu_ref, d_ref, wbs_ref,
                wout_ref, gpost_ref, gpre_ref, x1_ref, h2_ref):
    tm, d = x_ref.shape
    u = u_ref[...].reshape(tm, -1).astype(F32)
    y = ys_ref[...].reshape(tm, -1).astype(F32) + d_ref[...] * u
    y = _gelu_tanh(y)
    gl = jnp.dot(y.astype(BF16), wglu_ref[...], preferred_element_type=F32) + bglu_ref[...]
    s = y * jax.nn.sigmoid(gl)
    y_ssm = jnp.dot(s.astype(BF16), wbs_ref[...], preferred_element_type=F32)
    y_attn = jnp.dot(attn_ref[...], wba_ref[...], preferred_element_type=F32)
    merged = g_ref[:, :d].astype(F32) * y_attn + g_ref[:, d:].astype(F32) * y_ssm
    o = jnp.dot(merged.astype(BF16), wout_ref[...], preferred_element_type=F32)
    x1 = x_ref[...] + _rms(o, gpost_ref[...])
    x1_ref[...] = x1
    h2_ref[...] = _rms(x1, gpre_ref[...]).astype(BF16)


def _merge(attn, ys, u, gates, x2, wba, wglu, bglu, ssm_d, wbs, wout, gpost, gpre, seq, tm):
    T, D = x2.shape
    row = lambda w: pl.BlockSpec((tm, w), lambda i: (i, 0))
    chunked = _chunk_rows_spec(tm, seq // tm, u.shape[-1])
    consts = [wba, wglu, bglu, ssm_d, wbs, wout, gpost, gpre]
    return pl.pallas_call(
        _merge_body,
        grid=(T // tm,),
        in_specs=[row(attn.shape[1]), chunked, chunked, row(gates.shape[1]), row(D)]
                 + [_const_spec(c.shape) for c in consts],
        out_specs=[row(D), row(D)],
        out_shape=[jax.ShapeDtypeStruct((T, D), F32), jax.ShapeDtypeStruct((T, D), BF16)],
        compiler_params=_params(("parallel",)),
        name="merge",
    )(attn, ys, u, gates, x2, *consts)


def _ffn_body(tiles_per_seq, hp_ref, hc_ref, hn_ref, x1_ref, wg_ref, wu_ref, cw_ref, cb_ref, wd_ref,
              gn_ref, o_ref, hext, fscr):
    tm = hc_ref.shape[0]
    i = pl.program_id(0) % tiles_per_seq
    zero = jnp.zeros((), BF16)
    hext[0:HALO] = jnp.where(i == 0, zero, hp_ref[...])
    hext[HALO:HALO + tm] = hc_ref[...]
    hext[HALO + tm:] = jnp.where(i == tiles_per_seq - 1, zero, hn_ref[...])
    ext = tm + 2 * HALO
    for c in range(wg_ref.shape[1] // FF_TILE):
        sl = slice(FF_TILE * c, FF_TILE * (c + 1))
        g = jnp.dot(hext[...], wg_ref[:, sl], preferred_element_type=F32)
        up = jnp.dot(hc_ref[...], wu_ref[:, sl], preferred_element_type=F32)
        g_prev = pltpu.roll(g, 1, 0)[HALO:HALO + tm]
        g_next = pltpu.roll(g, ext - 1, 0)[HALO:HALO + tm]
        conv = (g_prev * cw_ref[0:1, sl] + g[HALO:HALO + tm] * cw_ref[1:2, sl]
                + g_next * cw_ref[2:3, sl] + cb_ref[:, sl])
        fscr[:, sl] = (_gelu_tanh(conv) * up).astype(BF16)
    o = jnp.dot(fscr[...], wd_ref[...], preferred_element_type=F32)
    o_ref[...] = x1_ref[...] + _rms(o, gn_ref[...])


def _ffn(h2, x1, wg, wu, cw, cb, wd, gn, seq, tm):
    T, D = x1.shape
    dff = wg.shape[1]
    assert dff % FF_TILE == 0 and tm % HALO == 0 and seq % tm == 0
    per = tm // HALO
    nh = T // HALO
    row = lambda w: pl.BlockSpec((tm, w), lambda i: (i, 0))
    once = lambda shape: pl.BlockSpec(shape, lambda i: (0,) * len(shape), pipeline_mode=pl.Buffered(1))
    return pl.pallas_call(
        functools.partial(_ffn_body, seq // tm),
        grid=(T // tm,),
        in_specs=[pl.BlockSpec((HALO, D), lambda i: (jnp.maximum(i * per - 1, 0), 0)),
                  row(D),
                  pl.BlockSpec((HALO, D), lambda i: (jnp.minimum((i + 1) * per, nh - 1), 0)),
                  row(D),
                  once(wg.shape), once(wu.shape), _const_spec(cw.shape), _const_spec(cb.shape),
                  once(wd.shape), _const_spec(gn.shape)],
        out_specs=row(D),
        out_shape=jax.ShapeDtypeStruct((T, D), F32),
        scratch_shapes=[pltpu.VMEM((tm + 2 * HALO, D), BF16), pltpu.VMEM((tm, dff), BF16)],
        compiler_params=_params(("parallel",)),
        name="ffn",
    )(h2, h2, h2, x1, wg, wu, cw, cb, wd, gn)


def _row_tile(seq):
    return min(1024, seq)


def _layer(x, p):
    bsz, seq, d = x.shape
    T = bsz * seq
    tm = _row_tile(seq)
    ssm_w = d // 2
    ngroups = ssm_w // SSM_CH
    row2 = lambda v: v.reshape(1, -1).astype(F32)

    npair = ngroups // PAIR
    both = lambda re, im: jnp.stack([jnp.stack([p[re + "_f"], p[im + "_f"]]),
                                     jnp.stack([p[re + "_b"], p[im + "_b"]])]).astype(F32)
    lam = both("lam_re", "lam_im").reshape(2, 2, npair, 1, LANES).transpose(0, 2, 1, 3, 4)
    ldt = jnp.repeat(jnp.stack([p["log_dt_f"], p["log_dt_b"]]).astype(F32), SSM_STATE, axis=-1)
    ldt = ldt.reshape(2, npair, 1, LANES)
    bt = both("b_re", "b_im").reshape(2, 2, npair, PAIR, SSM_STATE, SSM_CH)
    bt = bt.transpose(0, 2, 1, 5, 3, 4).reshape(2, npair, 2, SSM_CH, LANES)
    cc = both("c_re", "c_im").reshape(2, 2, npair, PAIR, SSM_CH, SSM_STATE)
    cc = cc.transpose(0, 2, 1, 4, 3, 5).reshape(2, npair, 2, SSM_CH, LANES)
    wb, wct, tsum, a, w_in = _s5prep(lam, ldt, bt, cc, p["w_in"])

    x2 = x.reshape(T, d)
    later = ("w_ffn_gate", "w_ffn_up", "w_ffn_down", "w_out", "w_branch_attn", "w_branch_ssm", "w_glu")
    q, k, ks, v, u, gates, *rounded = _inproj(x2, row2(p["norm_mix_pre"]), w_in, bsz, seq, tm,
                                              [p[name] for name in later])
    w16 = dict(zip(later, rounded))
    attn, = _attention(q, k, ks, v, jnp.asarray(_attn_bias()), p["attn_sink"].astype(F32), bsz, seq, [])

    assert bsz == SUBLANES // 2
    ys = _s5(u.reshape(T, ssm_w), wb, tsum, wct, a).reshape(u.shape)

    x1, h2 = _merge(attn, ys, u, gates, x2,
                    w16["w_branch_attn"], w16["w_glu"], row2(p["b_glu"]),
                    row2(p["ssm_d"]), w16["w_branch_ssm"], w16["w_out"],
                    row2(p["norm_mix_post"]), row2(p["norm_ffn_pre"]), seq, tm)

    out = _ffn(h2, x1, w16["w_ffn_gate"], w16["w_ffn_up"], p["conv_w"].astype(F32),
               row2(p["conv_b"]), w16["w_ffn_down"], row2(p["norm_ffn_post"]), seq, tm)
    return out.reshape(bsz, seq, d)


_PARAM_NAMES = (
    "norm_mix_pre", "w_in", "attn_sink",
    "lam_re_f", "lam_im_f", "log_dt_f", "b_re_f", "b_im_f", "c_re_f", "c_im_f",
    "lam_re_b", "lam_im_b", "log_dt_b", "b_re_b", "b_im_b", "c_re_b", "c_im_b",
    "ssm_d", "w_glu", "b_glu", "w_branch_attn", "w_branch_ssm", "w_out", "norm_mix_post",
    "norm_ffn_pre", "w_ffn_gate", "w_ffn_up", "conv_w", "conv_b", "w_ffn_down", "norm_ffn_post")


def kernel(x, norm_mix_pre, w_in, attn_sink, lam_re_f, lam_im_f, log_dt_f, b_re_f, b_im_f, c_re_f, c_im_f, lam_re_b, lam_im_b, log_dt_b, b_re_b, b_im_b, c_re_b, c_im_b, ssm_d, w_glu, b_glu, w_branch_attn, w_branch_ssm, w_out, norm_mix_post, norm_ffn_pre, w_ffn_gate, w_ffn_up, conv_w, conv_b, w_ffn_down, norm_ffn_post):
    stacked = dict(zip(_PARAM_NAMES, (
        norm_mix_pre, w_in, attn_sink,
        lam_re_f, lam_im_f, log_dt_f, b_re_f, b_im_f, c_re_f, c_im_f,
        lam_re_b, lam_im_b, log_dt_b, b_re_b, b_im_b, c_re_b, c_im_b,
        ssm_d, w_glu, b_glu, w_branch_attn, w_branch_ssm, w_out, norm_mix_post,
        norm_ffn_pre, w_ffn_gate, w_ffn_up, conv_w, conv_b, w_ffn_down, norm_ffn_post)))
    for layer in range(w_in.shape[0]):
        x = _layer(x, {name: value[layer] for name, value in stacked.items()})
    return x
```

```python
import functools
import math

import numpy as np
import jax
import jax.numpy as jnp
from jax import lax
from jax.experimental import pallas as pl
from jax.experimental.pallas import tpu as pltpu

F32 = jnp.float32
BF16 = jnp.bfloat16

N_HEADS = 8
N_KV_HEADS = 2
HEAD_DIM = 64
GROUP = N_HEADS // N_KV_HEADS
WINDOW = 128
BLOCK = 128
ATTN_SUB_BLOCKS = 8
SSM_CH = 16
SSM_STATE = 64
CHUNK = 16
PAIR = 2
N_BRANCH = 2
EPS = 1e-6
NEG_INF = -1e30
LOG2E = math.log2(math.e)

LANES = 128
SUBLANES = 8
BF16_SUBLANES = 2 * SUBLANES
HALO = BF16_SUBLANES
FF_TILE = 256
V7X_VMEM_BYTES = 64 * 1024 * 1024
VMEM_LIMIT = V7X_VMEM_BYTES * 7 // 8


def _gelu_tanh(x):
    return 0.5 * x * (1.0 + jnp.tanh(math.sqrt(2.0 / math.pi) * (x + 0.044715 * (x * x * x))))


def _rms(x, gain):
    return x * lax.rsqrt(jnp.mean(x * x, axis=-1, keepdims=True) + EPS) * gain


def _params(semantics):
    return pltpu.CompilerParams(dimension_semantics=semantics, vmem_limit_bytes=VMEM_LIMIT)


def _const_spec(shape):
    nd = len(shape)
    return pl.BlockSpec(shape, lambda *_: (0,) * nd)


def _inproj_body(splits, x_ref, g_ref, w_ref, q_ref, k_ref, ks_ref, v_ref, u_ref, gate_ref):
    h = _rms(x_ref[...], g_ref[...]).astype(BF16)

    def proj(i):
        return jnp.dot(h, w_ref[:, splits[i]:splits[i + 1]], preferred_element_type=F32)

    q_ref[...] = (proj(0) * (HEAD_DIM ** -0.5 * LOG2E)).astype(BF16)
    k = proj(1).astype(BF16)
    k_ref[...] = k
    ks_ref[...] = pltpu.roll(k, HEAD_DIM, 1)
    v_ref[...] = proj(2).astype(BF16)
    u_ref[...] = proj(3).astype(BF16).reshape(u_ref.shape)
    gate_ref[...] = jax.nn.sigmoid(proj(4)).astype(BF16)


def _chunk_rows_spec(tm, tiles_per_seq, width):
    return pl.BlockSpec((tm // CHUNK, None, CHUNK, width),
                        lambda i: (i % tiles_per_seq, i // tiles_per_seq, 0, 0))


def _inproj(x2, gain, w_in, bsz, seq, tm):
    T, D = x2.shape
    attn_w = N_HEADS * HEAD_DIM
    kv_w = N_KV_HEADS * HEAD_DIM
    widths = [attn_w, kv_w, kv_w, D // 2, N_BRANCH * D]
    splits = [0] + np.cumsum(widths).tolist()
    row = lambda w: pl.BlockSpec((tm, w), lambda i: (i, 0))
    out_widths = [attn_w, kv_w, kv_w, kv_w, D // 2, N_BRANCH * D]
    out_specs = [row(w) for w in out_widths]
    out_shape = [jax.ShapeDtypeStruct((T, w), BF16) for w in out_widths]
    out_specs[4] = _chunk_rows_spec(tm, seq // tm, out_widths[4])
    out_shape[4] = jax.ShapeDtypeStruct((seq // CHUNK, bsz, CHUNK, out_widths[4]), BF16)
    return pl.pallas_call(
        functools.partial(_inproj_body, tuple(splits)),
        grid=(T // tm,),
        in_specs=[row(D), _const_spec((1, D)), _const_spec(w_in.shape)],
        out_specs=out_specs,
        out_shape=out_shape,
        compiler_params=_params(("parallel",)),
        name="inproj",
    )(x2, gain, w_in)


def _cast_specs(weights, steps, step_of):
    specs = []
    for w in weights:
        nblocks = max(n for n in range(1, steps + 1)
                      if steps % n == 0 and w.shape[0] % (n * BF16_SUBLANES) == 0)
        hold = steps // nblocks
        specs.append(pl.BlockSpec((w.shape[0] // nblocks, w.shape[1]),
                                  lambda *ids, hold=hold: (step_of(*ids) // hold, 0)))
    return specs, specs, [jax.ShapeDtypeStruct(w.shape, BF16) for w in weights]


def _attn_body(ncast, q_ref, kp_ref, kc_ref, kn_ref, sp_ref, sc_ref, sn_ref, vp_ref, vc_ref, vn_ref,
               bias_first_ref, bias_mid_ref, bias_last_ref, sink_ref, *refs):
    o_ref = refs[ncast]
    for src, dst in zip(refs[:ncast], refs[ncast + 1:]):
        dst[...] = src[...].astype(BF16)
    bias_refs = [bias_first_ref] + [bias_mid_ref] * (ATTN_SUB_BLOCKS - 2) + [bias_last_ref]
    lane = lax.broadcasted_iota(jnp.int32, (1, LANES), 1)
    half_mask = [lane < HEAD_DIM, lane >= HEAD_DIM]
    zero = jnp.zeros((), BF16)
    kall = [jnp.concatenate([kp_ref[...], kc_ref[...], kn_ref[...]], axis=0),
            jnp.concatenate([sp_ref[...], sc_ref[...], sn_ref[...]], axis=0)]
    vall = jnp.concatenate([vp_ref[...], vc_ref[...], vn_ref[...]], axis=0)
    for b, bias_ref in enumerate(bias_refs):
        keys = slice(BLOCK * b, BLOCK * (b + 3))
        vcat = vall[keys]
        rows = slice(BLOCK * b, BLOCK * (b + 1))
        for j in range(N_HEADS // 2):
            qj = q_ref[rows, LANES * j:LANES * (j + 1)]
            g = (2 * j) // GROUP
            halves = []
            for e in range(2):
                h = 2 * j + e
                qm = jnp.where(half_mask[e], qj, zero)
                kcat = kall[0 if e == g else 1][keys]
                st = lax.dot_general(kcat, qm, (((1,), (1,)), ((), ())), preferred_element_type=F32)
                st = st + bias_ref[h]
                sink = sink_ref[h] * LOG2E
                m = jnp.maximum(jnp.max(st, axis=0, keepdims=True), sink)
                p = jnp.exp2(st - m)
                denom = jnp.sum(p, axis=0, keepdims=True) + jnp.exp2(sink - m)
                ot = lax.dot_general(vcat, p.astype(BF16), (((0,), (0,)), ((), ())),
                                     preferred_element_type=F32)
                halves.append(ot[HEAD_DIM * g:HEAD_DIM * (g + 1)] / denom)
            o_ref[rows, LANES * j:LANES * (j + 1)] = jnp.concatenate(halves, axis=0).T.astype(BF16)


def _attn_bias():
    t = np.arange(BLOCK)[None, :]
    s = np.arange(3 * BLOCK)[:, None] - BLOCK
    dist = np.abs(t - s).astype(np.float64)
    valid = dist <= WINDOW
    slopes = 2.0 ** (-8.0 * (np.arange(N_HEADS) + 1.0) / N_HEADS)
    out = np.zeros((3, N_HEADS, 3 * BLOCK, BLOCK), np.float32)
    for variant in range(3):
        ok = valid.copy()
        if variant == 0:
            ok[:BLOCK] = False
        if variant == 2:
            ok[2 * BLOCK:] = False
        for h in range(N_HEADS):
            out[variant, h] = np.where(ok, -slopes[h] * dist * LOG2E, NEG_INF)
    return out


def _attention(q, k, ks, v, bias, sink, bsz, seq, cast_views):
    nb = seq // BLOCK
    sub = ATTN_SUB_BLOCKS
    assert nb % sub == 0
    ns = nb // sub
    kvw = k.shape[1]
    aw = q.shape[1]

    def edge_spec(off):
        return pl.BlockSpec((BLOCK, kvw), lambda b, n: (b * nb + jnp.clip(sub * n + off, 0, nb - 1), 0))

    centre = pl.BlockSpec((sub * BLOCK, kvw), lambda b, n: (b * ns + n, 0))
    bias_spec = lambda pick: pl.BlockSpec((None,) + bias.shape[1:], lambda b, n: (pick(n), 0, 0, 0))
    first = bias_spec(lambda n: jnp.where(n == 0, 0, 1))
    middle = bias_spec(lambda n: 1)
    last = bias_spec(lambda n: jnp.where(n == ns - 1, 2, 1))
    qspec = pl.BlockSpec((sub * BLOCK, aw), lambda b, n: (b * ns + n, 0))
    cast_in, cast_out, cast_shape = _cast_specs(cast_views, bsz * ns, lambda b, n: b * ns + n)
    return pl.pallas_call(
        functools.partial(_attn_body, len(cast_views)),
        grid=(bsz, ns),
        in_specs=[qspec] + [edge_spec(-1), centre, edge_spec(sub)] * 3 + [
                  first, middle, last,
                  pl.BlockSpec(memory_space=pltpu.SMEM)] + cast_in,
        out_specs=[qspec] + cast_out,
        out_shape=[jax.ShapeDtypeStruct(q.shape, BF16)] + cast_shape,
        compiler_params=_params(("parallel", "arbitrary")),
        name="attn",
    )(q, k, k, k, ks, ks, ks, v, v, v, bias, bias, bias, sink, *cast_views)


def _s5prep_body(lam_ref, ldt_ref, bt_ref, c_ref, w32_ref, wb_ref, wct_ref, t_ref, a_ref, w16_ref):
    w16_ref[...] = w32_ref[...].astype(BF16)
    width = CHUNK * SSM_CH
    lane = lax.broadcasted_iota(jnp.int32, (1, LANES), 1)
    in_group = [lane < SSM_STATE, lane >= SSM_STATE]
    lane_t = lax.broadcasted_iota(jnp.int32, (SSM_CH, width), 1)
    contract = (((1,), (1,)), ((), ()))
    hi = lax.Precision.HIGHEST
    toeplitz = [[None] * CHUNK for _ in range(PAIR)]
    for d in range(2):
        lre, lim = lam_ref[d, 0], lam_ref[d, 1]
        dt = jnp.exp(ldt_ref[d])
        mag = jnp.exp(lre * dt)
        are, aim = mag * jnp.cos(lim * dt), mag * jnp.sin(lim * dt)
        den = lre * lre + lim * lim
        cfr = ((are - 1.0) * lre + aim * lim) / den
        cfi = (aim * lre - (are - 1.0) * lim) / den
        btr, bti = bt_ref[d, 0], bt_ref[d, 1]
        bbr = btr * cfr - bti * cfi
        bbi = btr * cfi + bti * cfr
        cre, cim = c_ref[d, 0], c_ref[d, 1]

        pwr, pwi = jnp.ones_like(are), jnp.zeros_like(are)
        e_r, e_i = [], []
        for p in range(CHUNK + 1):
            er = cre * pwr - cim * pwi
            ei = cre * pwi + cim * pwr
            if p < CHUNK:
                e_r.append(er)
                e_i.append(ei)
                k = CHUNK - 1 - p if d == 0 else p
                for ri, val in enumerate((bbr * pwr - bbi * pwi, bbr * pwi + bbi * pwr)):
                    for g in range(PAIR):
                        wb_ref[width * g + SSM_CH * k:width * g + SSM_CH * (k + 1),
                               LANES * (2 * d + ri):LANES * (2 * d + ri + 1)] = jnp.where(
                                   in_group[g], val, 0.0).astype(BF16)
            if p >= 1:
                r = p - 1 if d == 0 else CHUNK - p
                for ri, val in enumerate((er, -ei)):
                    for g in range(PAIR):
                        wct_ref[width * g + SSM_CH * r:width * g + SSM_CH * (r + 1),
                                LANES * (2 * d + ri):LANES * (2 * d + ri + 1)] = jnp.where(
                                    in_group[g], val, 0.0).astype(BF16)
            if p == CHUNK:
                a_ref[2 * d] = jnp.broadcast_to(pwr, a_ref.shape[1:])
                a_ref[2 * d + 1] = jnp.broadcast_to(pwi, a_ref.shape[1:])
            pwr, pwi = pwr * are - pwi * aim, pwr * aim + pwi * are

        if d == 1:
            e_r, e_i = e_r[::-1], e_i[::-1]
        ecat_r, ecat_i = jnp.concatenate(e_r, axis=0), jnp.concatenate(e_i, axis=0)
        for g in range(PAIR):
            gr, gi = jnp.where(in_group[g], bbr, 0.0), jnp.where(in_group[g], bbi, 0.0)
            krow = (lax.dot_general(gr, ecat_r, contract, precision=hi, preferred_element_type=F32)
                    - lax.dot_general(gi, ecat_i, contract, precision=hi, preferred_element_type=F32))
            for k in range(CHUNK):
                if d == 0:
                    shift, keep = SSM_CH * k, lane_t >= SSM_CH * k
                else:
                    shift, keep = (SSM_CH * (k + 1)) % width, lane_t < SSM_CH * (k + 1)
                blk = jnp.where(keep, pltpu.roll(krow, shift, 1) if shift else krow, 0.0)
                toeplitz[g][k] = blk if d == 0 else toeplitz[g][k] + blk
    for g in range(PAIR):
        t_ref[g] = jnp.concatenate(toeplitz[g], axis=0).astype(BF16)


def _s5prep(lam, ldt, bt, c, w32):
    npair = lam.shape[1]
    width = CHUNK * SSM_CH
    pw = PAIR * width
    cast_in, cast_out, cast_shape = _cast_specs([w32], npair, lambda i: i)
    return pl.pallas_call(
        _s5prep_body,
        grid=(npair,),
        in_specs=[pl.BlockSpec((2, None, 2, 1, LANES), lambda i: (0, i, 0, 0, 0)),
                  pl.BlockSpec((2, None, 1, LANES), lambda i: (0, i, 0, 0)),
                  pl.BlockSpec((2, None, 2, SSM_CH, LANES), lambda i: (0, i, 0, 0, 0)),
                  pl.BlockSpec((2, None, 2, SSM_CH, LANES), lambda i: (0, i, 0, 0, 0))] + cast_in,
        out_specs=[pl.BlockSpec((None, pw, 4 * LANES), lambda i: (i, 0, 0)),
                   pl.BlockSpec((None, pw, 4 * LANES), lambda i: (i, 0, 0)),
                   pl.BlockSpec((PAIR, width, width), lambda i: (i, 0, 0)),
                   pl.BlockSpec((None, 4, SUBLANES, LANES), lambda i: (i, 0, 0, 0))] + cast_out,
        out_shape=[jax.ShapeDtypeStruct((npair, pw, 4 * LANES), BF16),
                   jax.ShapeDtypeStruct((npair, pw, 4 * LANES), BF16),
                   jax.ShapeDtypeStruct((npair * PAIR, width, width), BF16),
                   jax.ShapeDtypeStruct((npair, 4, SUBLANES, LANES), F32)] + cast_shape,
        compiler_params=_params(("parallel",)),
        name="s5prep",
    )(lam, ldt, bt, c, w32)


def _block_transpose(xs):
    n = len(xs)
    assert n * SSM_CH == LANES and n & (n - 1) == 0
    blk = lax.broadcasted_iota(jnp.int32, xs[0].shape, 1) // SSM_CH
    d = n // 2
    while d:
        upper = (blk & d) != 0
        ys = list(xs)
        for a in range(n):
            if a & d:
                continue
            b = a + d
            ys[a] = jnp.where(upper, pltpu.roll(xs[b], SSM_CH * d, 1), xs[a])
            ys[b] = jnp.where(upper, xs[b], pltpu.roll(xs[a], LANES - SSM_CH * d, 1))
        xs = ys
        d //= 2
    return xs


def _s5_scan(a_ref, w_scr, s_scr):
    rows = w_scr.shape[0]
    nsteps = rows // SUBLANES
    nstream = w_scr.shape[1] // (2 * LANES)
    half = SUBLANES // 2
    lo = lax.broadcasted_iota(jnp.int32, (SUBLANES, LANES), 0) < half
    first = [lo, jnp.logical_not(lo)]
    coef = [(a_ref[s // 2, 2 * (s % 2)], a_ref[s // 2, 2 * (s % 2) + 1]) for s in range(nstream)]
    col = lambda i: slice(LANES * i, LANES * (i + 1))

    def advance(a, xr, xi, wr, wi):
        return a[0] * xr - a[1] * xi + wr, a[0] * xi + a[1] * xr + wi

    def swap(x):
        return pltpu.roll(x, half, 0)

    def step(i, carry):
        out = []
        for s in range(nstream):
            d = s % 2
            cr, ci = carry[2 * s], carry[2 * s + 1]
            vreg = i if d == 0 else nsteps - 1 - i
            rs = pl.ds(pl.multiple_of(vreg * SUBLANES, SUBLANES), SUBLANES)
            wr, wi = w_scr[rs, col(2 * s)], w_scr[rs, col(2 * s + 1)]
            r0r, r0i = swap(cr), swap(ci)
            t1r, t1i = advance(coef[s], r0r, r0i, wr, wi)
            r1r, r1i = swap(t1r), swap(t1i)
            t2r, t2i = advance(coef[s], r1r, r1i, wr, wi)
            s_scr[rs, col(2 * s)] = jnp.where(first[d], r0r, r1r)
            s_scr[rs, col(2 * s + 1)] = jnp.where(first[d], r0i, r1i)
            out += [jnp.where(first[d], t1r, t2r), jnp.where(first[d], t1i, t2i)]
        return tuple(out)

    z = jnp.zeros((SUBLANES, LANES), F32)
    lax.fori_loop(0, nsteps, step, (z,) * (2 * nstream))


S5_SLAB_GROUPS = LANES // SSM_CH
S5_ROW_CHUNK = 256
S5_SUB_ROWS = 256
S5_SCAN_PAIRS = 2
S5_TOK_PITCH = CHUNK + SUBLANES


def _s5_body(u_ref, wb_ref, t_ref, wct_ref, a_ref, y_ref, ur_scr, w_scr, s_scr, sall_scr, yr_scr, tok_scr):
    rows = ur_scr.shape[0]
    width = CHUNK * SSM_CH
    pw = PAIR * width
    npair = S5_SLAB_GROUPS // PAIR
    half_steps = CHUNK // 2
    tok_rows = S5_ROW_CHUNK * CHUNK

    nchunks = rows // S5_ROW_CHUNK
    nsub = S5_ROW_CHUNK // S5_SUB_ROWS
    sw = 4 * LANES
    row_chunk = lambda c: slice(S5_ROW_CHUNK * c, S5_ROW_CHUNK * (c + 1))
    tok_chunk = lambda c: slice(tok_rows * c, tok_rows * (c + 1))
    step_rows = lambda i, step: pl.ds(S5_TOK_PITCH * S5_SUB_ROWS * i + step, S5_SUB_ROWS, stride=S5_TOK_PITCH)

    def copy_chunk_rows(c, tok, to_tok):
        for r in range(S5_ROW_CHUNK):
            ext = slice(tok_rows * c + r * CHUNK, tok_rows * c + (r + 1) * CHUNK)
            pad = slice(r * S5_TOK_PITCH, r * S5_TOK_PITCH + CHUNK)
            if to_tok:
                tok[pad, :] = u_ref[ext, :].astype(F32)
            else:
                y_ref[ext, :] = tok[pad, :].astype(BF16)
    lane_blk = lambda g, h: slice(width * g + LANES * h, width * g + LANES * (h + 1))

    def relayout_in(c):
        tok = tok_scr.at[c % 2]
        copy_chunk_rows(c, tok, True)
        for i in range(nsub):
            for h in range(2):
                xs = [tok[step_rows(i, half_steps * h + kk), :] for kk in range(half_steps)]
                ys = _block_transpose(xs)
                for g in range(S5_SLAB_GROUPS):
                    r0 = S5_ROW_CHUNK * c + S5_SUB_ROWS * i
                    ur_scr[r0:r0 + S5_SUB_ROWS, lane_blk(g, h)] = ys[g].astype(BF16)

    def state_in(c, p0):
        for q in range(S5_SCAN_PAIRS):
            pp = p0 + q
            w_scr[row_chunk(c), sw * q:sw * (q + 1)] = jnp.dot(
                ur_scr[row_chunk(c), pw * pp:pw * (pp + 1)], wb_ref[pp], preferred_element_type=F32)

    for p0 in range(0, npair, S5_SCAN_PAIRS):
        for c in range(nchunks):
            if p0 == 0:
                relayout_in(c)
            state_in(c, p0)
        _s5_scan(a_ref.at[p0:p0 + S5_SCAN_PAIRS], w_scr, s_scr)
        sall_scr[:, sw * p0:sw * (p0 + S5_SCAN_PAIRS)] = s_scr[...].astype(BF16)

    def outputs(c):
        yr, tok = yr_scr.at[c % 2], tok_scr.at[c % 2]
        for pp in range(npair):
            y = lax.dot_general(sall_scr[row_chunk(c), sw * pp:sw * (pp + 1)], wct_ref[pp],
                                (((1,), (1,)), ((), ())), preferred_element_type=F32)
            for g in range(PAIR):
                lo = pw * pp + width * g
                yr[:, lo:lo + width] = y[:, width * g:width * (g + 1)] + jnp.dot(
                    ur_scr[row_chunk(c), lo:lo + width], t_ref[PAIR * pp + g], preferred_element_type=F32)
        for i in range(nsub):
            for h in range(2):
                xs = [yr[S5_SUB_ROWS * i:S5_SUB_ROWS * (i + 1), lane_blk(g, h)].astype(BF16)
                      for g in range(S5_SLAB_GROUPS)]
                ys = _block_transpose(xs)
                for kk in range(half_steps):
                    tok[step_rows(i, half_steps * h + kk), :] = ys[kk].astype(F32)
        copy_chunk_rows(c, tok, False)

    for c in range(nchunks):
        outputs(c)


def _s5(u_tok, wb, tsum, wct, a16):
    T, ssm_w = u_tok.shape
    rows = T // CHUNK
    assert rows % S5_ROW_CHUNK == 0 and S5_SLAB_GROUPS == CHUNK // 2
    width = CHUNK * SSM_CH
    pw = PAIR * width
    npair = S5_SLAB_GROUPS // PAIR
    slab = pl.BlockSpec((T, LANES), lambda s: (0, s))
    return pl.pallas_call(
        _s5_body,
        grid=(ssm_w // LANES,),
        in_specs=[slab,
                  pl.BlockSpec((npair, pw, 4 * LANES), lambda s: (s, 0, 0)),
                  pl.BlockSpec((S5_SLAB_GROUPS, width, width), lambda s: (s, 0, 0)),
                  pl.BlockSpec((npair, pw, 4 * LANES), lambda s: (s, 0, 0)),
                  pl.BlockSpec((npair, 4, SUBLANES, LANES), lambda s: (s, 0, 0, 0))],
        out_specs=slab,
        out_shape=jax.ShapeDtypeStruct((T, ssm_w), BF16),
        scratch_shapes=[pltpu.VMEM((rows, S5_SLAB_GROUPS * width), BF16),
                        pltpu.VMEM((rows, S5_SCAN_PAIRS * 4 * LANES), F32),
                        pltpu.VMEM((rows, S5_SCAN_PAIRS * 4 * LANES), F32),
                        pltpu.VMEM((rows, npair * 4 * LANES), BF16),
                        pltpu.VMEM((2, S5_ROW_CHUNK, S5_SLAB_GROUPS * width), F32),
                        pltpu.VMEM((2, S5_ROW_CHUNK * S5_TOK_PITCH, LANES), F32)],
        compiler_params=_params(("parallel",)),
        name="s5",
    )(u_tok, wb, tsum, wct, a16)


def _merge_body(attn_ref, ys_ref, u_ref, g_ref, x_ref, wba_ref, wglu_ref, bglu_ref, d_ref, wbs_ref,
                wout_ref, gpost_ref, gpre_ref, x1_ref, h2_ref):
    tm, d = x_ref.shape
    u = u_ref[...].reshape(tm, -1).astype(F32)
    y = ys_ref[...].reshape(tm, -1).astype(F32) + d_ref[...] * u
    y = _gelu_tanh(y)
    gl = jnp.dot(y.astype(BF16), wglu_ref[...], preferred_element_type=F32) + bglu_ref[...]
    s = y * jax.nn.sigmoid(gl)
    y_ssm = jnp.dot(s.astype(BF16), wbs_ref[...], preferred_element_type=F32)
    y_attn = jnp.dot(attn_ref[...], wba_ref[...], preferred_element_type=F32)
    merged = g_ref[:, :d].astype(F32) * y_attn + g_ref[:, d:].astype(F32) * y_ssm
    o = jnp.dot(merged.astype(BF16), wout_ref[...], preferred_element_type=F32)
    x1 = x_ref[...] + _rms(o, gpost_ref[...])
    x1_ref[...] = x1
    h2_ref[...] = _rms(x1, gpre_ref[...]).astype(BF16)


def _merge(attn, ys, u, gates, x2, wba, wglu, bglu, ssm_d, wbs, wout, gpost, gpre, seq, tm):
    T, D = x2.shape
    row = lambda w: pl.BlockSpec((tm, w), lambda i: (i, 0))
    chunked = _chunk_rows_spec(tm, seq // tm, u.shape[-1])
    consts = [wba, wglu, bglu, ssm_d, wbs, wout, gpost, gpre]
    return pl.pallas_call(
        _merge_body,
        grid=(T // tm,),
        in_specs=[row(attn.shape[1]), chunked, chunked, row(gates.shape[1]), row(D)]
                 + [_const_spec(c.shape) for c in consts],
        out_specs=[row(D), row(D)],
        out_shape=[jax.ShapeDtypeStruct((T, D), F32), jax.ShapeDtypeStruct((T, D), BF16)],
        compiler_params=_params(("parallel",)),
        name="merge",
    )(attn, ys, u, gates, x2, *consts)


def _ffn_body(tiles_per_seq, hp_ref, hc_ref, hn_ref, x1_ref, wg_ref, wu_ref, cw_ref, cb_ref, wd_ref,
              gn_ref, o_ref, hext, fscr):
    tm = hc_ref.shape[0]
    i = pl.program_id(0) % tiles_per_seq
    zero = jnp.zeros((), BF16)
    hext[0:HALO] = jnp.where(i == 0, zero, hp_ref[...])
    hext[HALO:HALO + tm] = hc_ref[...]
    hext[HALO + tm:] = jnp.where(i == tiles_per_seq - 1, zero, hn_ref[...])
    ext = tm + 2 * HALO
    for c in range(wg_ref.shape[1] // FF_TILE):
        sl = slice(FF_TILE * c, FF_TILE * (c + 1))
        g = jnp.dot(hext[...], wg_ref[:, sl], preferred_element_type=F32)
        up = jnp.dot(hc_ref[...], wu_ref[:, sl], preferred_element_type=F32)
        g_prev = pltpu.roll(g, 1, 0)[HALO:HALO + tm]
        g_next = pltpu.roll(g, ext - 1, 0)[HALO:HALO + tm]
        conv = (g_prev * cw_ref[0:1, sl] + g[HALO:HALO + tm] * cw_ref[1:2, sl]
                + g_next * cw_ref[2:3, sl] + cb_ref[:, sl])
        fscr[:, sl] = (_gelu_tanh(conv) * up).astype(BF16)
    o = jnp.dot(fscr[...], wd_ref[...], preferred_element_type=F32)
    o_ref[...] = x1_ref[...] + _rms(o, gn_ref[...])


def _ffn(h2, x1, wg, wu, cw, cb, wd, gn, seq, tm):
    T, D = x1.shape
    dff = wg.shape[1]
    assert dff % FF_TILE == 0 and tm % HALO == 0 and seq % tm == 0
    per = tm // HALO
    nh = T // HALO
    row = lambda w: pl.BlockSpec((tm, w), lambda i: (i, 0))
    once = lambda shape: pl.BlockSpec(shape, lambda i: (0,) * len(shape), pipeline_mode=pl.Buffered(1))
    return pl.pallas_call(
        functools.partial(_ffn_body, seq // tm),
        grid=(T // tm,),
        in_specs=[pl.BlockSpec((HALO, D), lambda i: (jnp.maximum(i * per - 1, 0), 0)),
                  row(D),
                  pl.BlockSpec((HALO, D), lambda i: (jnp.minimum((i + 1) * per, nh - 1), 0)),
                  row(D),
                  once(wg.shape), once(wu.shape), _const_spec(cw.shape), _const_spec(cb.shape),
                  once(wd.shape), _const_spec(gn.shape)],
        out_specs=row(D),
        out_shape=jax.ShapeDtypeStruct((T, D), F32),
        scratch_shapes=[pltpu.VMEM((tm + 2 * HALO, D), BF16), pltpu.VMEM((tm, dff), BF16)],
        compiler_params=_params(("parallel",)),
        name="ffn",
    )(h2, h2, h2, x1, wg, wu, cw, cb, wd, gn)


def _row_tile(seq):
    return min(1024, seq)


def _layer(x, p):
    bsz, seq, d = x.shape
    T = bsz * seq
    tm = _row_tile(seq)
    ssm_w = d // 2
    ngroups = ssm_w // SSM_CH
    row2 = lambda v: v.reshape(1, -1).astype(F32)

    npair = ngroups // PAIR
    both = lambda re, im: jnp.stack([jnp.stack([p[re + "_f"], p[im + "_f"]]),
                                     jnp.stack([p[re + "_b"], p[im + "_b"]])]).astype(F32)
    lam = both("lam_re", "lam_im").reshape(2, 2, npair, 1, LANES).transpose(0, 2, 1, 3, 4)
    ldt = jnp.repeat(jnp.stack([p["log_dt_f"], p["log_dt_b"]]).astype(F32), SSM_STATE, axis=-1)
    ldt = ldt.reshape(2, npair, 1, LANES)
    bt = both("b_re", "b_im").reshape(2, 2, npair, PAIR, SSM_STATE, SSM_CH)
    bt = bt.transpose(0, 2, 1, 5, 3, 4).reshape(2, npair, 2, SSM_CH, LANES)
    cc = both("c_re", "c_im").reshape(2, 2, npair, PAIR, SSM_CH, SSM_STATE)
    cc = cc.transpose(0, 2, 1, 4, 3, 5).reshape(2, npair, 2, SSM_CH, LANES)
    wb, wct, tsum, a, w_in = _s5prep(lam, ldt, bt, cc, p["w_in"])

    x2 = x.reshape(T, d)
    q, k, ks, v, u, gates = _inproj(x2, row2(p["norm_mix_pre"]), w_in, bsz, seq, tm)

    later = ("w_ffn_gate", "w_ffn_up", "w_ffn_down", "w_out", "w_branch_attn", "w_branch_ssm", "w_glu")
    attn, *rounded = _attention(q, k, ks, v, jnp.asarray(_attn_bias()), p["attn_sink"].astype(F32), bsz, seq,
                                [p[name] for name in later])
    w16 = dict(zip(later, rounded))

    assert bsz == SUBLANES // 2
    ys = _s5(u.reshape(T, ssm_w), wb, tsum, wct, a).reshape(u.shape)

    x1, h2 = _merge(attn, ys, u, gates, x2,
                    w16["w_branch_attn"], w16["w_glu"], row2(p["b_glu"]),
                    row2(p["ssm_d"]), w16["w_branch_ssm"], w16["w_out"],
                    row2(p["norm_mix_post"]), row2(p["norm_ffn_pre"]), seq, tm)

    out = _ffn(h2, x1, w16["w_ffn_gate"], w16["w_ffn_up"], p["conv_w"].astype(F32),
               row2(p["conv_b"]), w16["w_ffn_down"], row2(p["norm_ffn_post"]), seq, tm)
    return out.reshape(bsz, seq, d)


_PARAM_NAMES = (
    "norm_mix_pre", "w_in", "attn_sink",
    "lam_re_f", "lam_im_f", "log_dt_f", "b_re_f", "b_im_f", "c_re_f", "c_im_f",
    "lam_re_b", "lam_im_b", "log_dt_b", "b_re_b", "b_im_b", "c_re_b", "c_im_b",
    "ssm_d", "w_glu", "b_glu", "w_branch_attn", "w_branch_ssm", "w_out", "norm_mix_post",
    "norm_ffn_pre", "w_ffn_gate", "w_ffn_up", "conv_w", "conv_b", "w_ffn_down", "norm_ffn_post")


def kernel(x, norm_mix_pre, w_in, attn_sink, lam_re_f, lam_im_f, log_dt_f, b_re_f, b_im_f, c_re_f, c_im_f, lam_re_b, lam_im_b, log_dt_b, b_re_b, b_im_b, c_re_b, c_im_b, ssm_d, w_glu, b_glu, w_branch_attn, w_branch_ssm, w_out, norm_mix_post, norm_ffn_pre, w_ffn_gate, w_ffn_up, conv_w, conv_b, w_ffn_down, norm_ffn_post):
    stacked = dict(zip(_PARAM_NAMES, (
        norm_mix_pre, w_in, attn_sink,
        lam_re_f, lam_im_f, log_dt_f, b_re_f, b_im_f, c_re_f, c_im_f,
        lam_re_b, lam_im_b, log_dt_b, b_re_b, b_im_b, c_re_b, c_im_b,
        ssm_d, w_glu, b_glu, w_branch_attn, w_branch_ssm, w_out, norm_mix_post,
        norm_ffn_pre, w_ffn_gate, w_ffn_up, conv_w, conv_b, w_ffn_down, norm_ffn_post)))
    for layer in range(w_in.shape[0]):
        x = _layer(x, {name: value[layer] for name, value in stacked.items()})
    return x
```

```python
import functools
import math

import numpy as np
import jax
import jax.numpy as jnp
from jax import lax
from jax.experimental import pallas as pl
from jax.experimental.pallas import tpu as pltpu

F32 = jnp.float32
BF16 = jnp.bfloat16

N_HEADS = 8
N_KV_HEADS = 2
HEAD_DIM = 64
GROUP = N_HEADS // N_KV_HEADS
WINDOW = 128
BLOCK = 128
ATTN_SUB_BLOCKS = 16
SSM_CH = 16
SSM_STATE = 64
CHUNK = 16
PAIR = 2
N_BRANCH = 2
EPS = 1e-6
NEG_INF = -1e30
LOG2E = math.log2(math.e)

LANES = 128
SUBLANES = 8
BF16_SUBLANES = 2 * SUBLANES
HALO = BF16_SUBLANES
FF_TILE = 256
V7X_VMEM_BYTES = 64 * 1024 * 1024
VMEM_LIMIT = V7X_VMEM_BYTES * 7 // 8


def _gelu_tanh(x):
    return 0.5 * x * (1.0 + jnp.tanh(math.sqrt(2.0 / math.pi) * (x + 0.044715 * (x * x * x))))


def _rms(x, gain):
    return x * lax.rsqrt(jnp.mean(x * x, axis=-1, keepdims=True) + EPS) * gain


def _params(semantics):
    return pltpu.CompilerParams(dimension_semantics=semantics, vmem_limit_bytes=VMEM_LIMIT)


def _const_spec(shape):
    nd = len(shape)
    return pl.BlockSpec(shape, lambda *_: (0,) * nd)


def _inproj_body(splits, x_ref, g_ref, w_ref, q_ref, k_ref, ks_ref, v_ref, u_ref, gate_ref):
    h = _rms(x_ref[...], g_ref[...]).astype(BF16)

    def proj(i):
        return jnp.dot(h, w_ref[:, splits[i]:splits[i + 1]], preferred_element_type=F32)

    q_ref[...] = (proj(0) * (HEAD_DIM ** -0.5 * LOG2E)).astype(BF16)
    k = proj(1).astype(BF16)
    k_ref[...] = k
    ks_ref[...] = pltpu.roll(k, HEAD_DIM, 1)
    v_ref[...] = proj(2).astype(BF16)
    u_ref[...] = proj(3).astype(BF16).reshape(u_ref.shape)
    gate_ref[...] = jax.nn.sigmoid(proj(4)).astype(BF16)


def _chunk_rows_spec(tm, tiles_per_seq, width):
    return pl.BlockSpec((tm // CHUNK, None, CHUNK, width),
                        lambda i: (i % tiles_per_seq, i // tiles_per_seq, 0, 0))


def _inproj(x2, gain, w_in, bsz, seq, tm):
    T, D = x2.shape
    attn_w = N_HEADS * HEAD_DIM
    kv_w = N_KV_HEADS * HEAD_DIM
    widths = [attn_w, kv_w, kv_w, D // 2, N_BRANCH * D]
    splits = [0] + np.cumsum(widths).tolist()
    row = lambda w: pl.BlockSpec((tm, w), lambda i: (i, 0))
    out_widths = [attn_w, kv_w, kv_w, kv_w, D // 2, N_BRANCH * D]
    out_specs = [row(w) for w in out_widths]
    out_shape = [jax.ShapeDtypeStruct((T, w), BF16) for w in out_widths]
    out_specs[4] = _chunk_rows_spec(tm, seq // tm, out_widths[4])
    out_shape[4] = jax.ShapeDtypeStruct((seq // CHUNK, bsz, CHUNK, out_widths[4]), BF16)
    return pl.pallas_call(
        functools.partial(_inproj_body, tuple(splits)),
        grid=(T // tm,),
        in_specs=[row(D), _const_spec((1, D)), _const_spec(w_in.shape)],
        out_specs=out_specs,
        out_shape=out_shape,
        compiler_params=_params(("parallel",)),
        name="inproj",
    )(x2, gain, w_in)


def _cast_specs(weights, steps, step_of):
    specs = []
    for w in weights:
        nblocks = max(n for n in range(1, steps + 1)
                      if steps % n == 0 and w.shape[0] % (n * BF16_SUBLANES) == 0)
        hold = steps // nblocks
        specs.append(pl.BlockSpec((w.shape[0] // nblocks, w.shape[1]),
                                  lambda *ids, hold=hold: (step_of(*ids) // hold, 0)))
    return specs, specs, [jax.ShapeDtypeStruct(w.shape, BF16) for w in weights]


def _attn_body(ncast, q_ref, kp_ref, kc_ref, kn_ref, sp_ref, sc_ref, sn_ref, vp_ref, vc_ref, vn_ref,
               bias_first_ref, bias_mid_ref, bias_last_ref, sink_ref, *refs):
    o_ref = refs[ncast]
    for src, dst in zip(refs[:ncast], refs[ncast + 1:]):
        dst[...] = src[...].astype(BF16)
    bias_refs = [bias_first_ref] + [bias_mid_ref] * (ATTN_SUB_BLOCKS - 2) + [bias_last_ref]
    lane = lax.broadcasted_iota(jnp.int32, (1, LANES), 1)
    half_mask = [lane < HEAD_DIM, lane >= HEAD_DIM]
    zero = jnp.zeros((), BF16)
    kall = [jnp.concatenate([kp_ref[...], kc_ref[...], kn_ref[...]], axis=0),
            jnp.concatenate([sp_ref[...], sc_ref[...], sn_ref[...]], axis=0)]
    vall = jnp.concatenate([vp_ref[...], vc_ref[...], vn_ref[...]], axis=0)
    for b, bias_ref in enumerate(bias_refs):
        keys = slice(BLOCK * b, BLOCK * (b + 3))
        vcat = vall[keys]
        rows = slice(BLOCK * b, BLOCK * (b + 1))
        for j in range(N_HEADS // 2):
            qj = q_ref[rows, LANES * j:LANES * (j + 1)]
            g = (2 * j) // GROUP
            halves = []
            for e in range(2):
                h = 2 * j + e
                qm = jnp.where(half_mask[e], qj, zero)
                kcat = kall[0 if e == g else 1][keys]
                st = lax.dot_general(kcat, qm, (((1,), (1,)), ((), ())), preferred_element_type=F32)
                st = st + bias_ref[h]
                sink = sink_ref[h] * LOG2E
                m = jnp.maximum(jnp.max(st, axis=0, keepdims=True), sink)
                p = jnp.exp2(st - m)
                denom = jnp.sum(p, axis=0, keepdims=True) + jnp.exp2(sink - m)
                ot = lax.dot_general(vcat, p.astype(BF16), (((0,), (0,)), ((), ())),
                                     preferred_element_type=F32)
                halves.append(ot[HEAD_DIM * g:HEAD_DIM * (g + 1)] / denom)
            o_ref[rows, LANES * j:LANES * (j + 1)] = jnp.concatenate(halves, axis=0).T.astype(BF16)


def _attn_bias():
    t = np.arange(BLOCK)[None, :]
    s = np.arange(3 * BLOCK)[:, None] - BLOCK
    dist = np.abs(t - s).astype(np.float64)
    valid = dist <= WINDOW
    slopes = 2.0 ** (-8.0 * (np.arange(N_HEADS) + 1.0) / N_HEADS)
    out = np.zeros((3, N_HEADS, 3 * BLOCK, BLOCK), np.float32)
    for variant in range(3):
        ok = valid.copy()
        if variant == 0:
            ok[:BLOCK] = False
        if variant == 2:
            ok[2 * BLOCK:] = False
        for h in range(N_HEADS):
            out[variant, h] = np.where(ok, -slopes[h] * dist * LOG2E, NEG_INF)
    return out


def _attention(q, k, ks, v, bias, sink, bsz, seq, cast_views):
    nb = seq // BLOCK
    sub = ATTN_SUB_BLOCKS
    assert nb % sub == 0
    ns = nb // sub
    kvw = k.shape[1]
    aw = q.shape[1]

    def edge_spec(off):
        return pl.BlockSpec((BLOCK, kvw), lambda b, n: (b * nb + jnp.clip(sub * n + off, 0, nb - 1), 0))

    centre = pl.BlockSpec((sub * BLOCK, kvw), lambda b, n: (b * ns + n, 0))
    bias_spec = lambda pick: pl.BlockSpec((None,) + bias.shape[1:], lambda b, n: (pick(n), 0, 0, 0))
    first = bias_spec(lambda n: jnp.where(n == 0, 0, 1))
    middle = bias_spec(lambda n: 1)
    last = bias_spec(lambda n: jnp.where(n == ns - 1, 2, 1))
    qspec = pl.BlockSpec((sub * BLOCK, aw), lambda b, n: (b * ns + n, 0))
    cast_in, cast_out, cast_shape = _cast_specs(cast_views, bsz * ns, lambda b, n: b * ns + n)
    return pl.pallas_call(
        functools.partial(_attn_body, len(cast_views)),
        grid=(bsz, ns),
        in_specs=[qspec] + [edge_spec(-1), centre, edge_spec(sub)] * 3 + [
                  first, middle, last,
                  pl.BlockSpec(memory_space=pltpu.SMEM)] + cast_in,
        out_specs=[qspec] + cast_out,
        out_shape=[jax.ShapeDtypeStruct(q.shape, BF16)] + cast_shape,
        compiler_params=_params(("parallel", "arbitrary")),
        name="attn",
    )(q, k, k, k, ks, ks, ks, v, v, v, bias, bias, bias, sink, *cast_views)


def _s5prep_body(lam_ref, ldt_ref, bt_ref, c_ref, w32_ref, wb_ref, wct_ref, t_ref, a_ref, w16_ref):
    w16_ref[...] = w32_ref[...].astype(BF16)
    width = CHUNK * SSM_CH
    lane = lax.broadcasted_iota(jnp.int32, (1, LANES), 1)
    in_group = [lane < SSM_STATE, lane >= SSM_STATE]
    lane_t = lax.broadcasted_iota(jnp.int32, (SSM_CH, width), 1)
    contract = (((1,), (1,)), ((), ()))
    hi = lax.Precision.HIGHEST
    toeplitz = [[None] * CHUNK for _ in range(PAIR)]
    for d in range(2):
        lre, lim = lam_ref[d, 0], lam_ref[d, 1]
        dt = jnp.exp(ldt_ref[d])
        mag = jnp.exp(lre * dt)
        are, aim = mag * jnp.cos(lim * dt), mag * jnp.sin(lim * dt)
        den = lre * lre + lim * lim
        cfr = ((are - 1.0) * lre + aim * lim) / den
        cfi = (aim * lre - (are - 1.0) * lim) / den
        btr, bti = bt_ref[d, 0], bt_ref[d, 1]
        bbr = btr * cfr - bti * cfi
        bbi = btr * cfi + bti * cfr
        cre, cim = c_ref[d, 0], c_ref[d, 1]

        pwr, pwi = jnp.ones_like(are), jnp.zeros_like(are)
        e_r, e_i = [], []
        for p in range(CHUNK + 1):
            er = cre * pwr - cim * pwi
            ei = cre * pwi + cim * pwr
            if p < CHUNK:
                e_r.append(er)
                e_i.append(ei)
                k = CHUNK - 1 - p if d == 0 else p
                for ri, val in enumerate((bbr * pwr - bbi * pwi, bbr * pwi + bbi * pwr)):
                    for g in range(PAIR):
                        wb_ref[width * g + SSM_CH * k:width * g + SSM_CH * (k + 1),
                               LANES * (2 * d + ri):LANES * (2 * d + ri + 1)] = jnp.where(
                                   in_group[g], val, 0.0).astype(BF16)
            if p >= 1:
                r = p - 1 if d == 0 else CHUNK - p
                for ri, val in enumerate((er, -ei)):
                    for g in range(PAIR):
                        wct_ref[width * g + SSM_CH * r:width * g + SSM_CH * (r + 1),
                                LANES * (2 * d + ri):LANES * (2 * d + ri + 1)] = jnp.where(
                                    in_group[g], val, 0.0).astype(BF16)
            if p == CHUNK:
                a_ref[2 * d] = jnp.broadcast_to(pwr, a_ref.shape[1:])
                a_ref[2 * d + 1] = jnp.broadcast_to(pwi, a_ref.shape[1:])
            pwr, pwi = pwr * are - pwi * aim, pwr * aim + pwi * are

        if d == 1:
            e_r, e_i = e_r[::-1], e_i[::-1]
        ecat_r, ecat_i = jnp.concatenate(e_r, axis=0), jnp.concatenate(e_i, axis=0)
        for g in range(PAIR):
            gr, gi = jnp.where(in_group[g], bbr, 0.0), jnp.where(in_group[g], bbi, 0.0)
            krow = (lax.dot_general(gr, ecat_r, contract, precision=hi, preferred_element_type=F32)
                    - lax.dot_general(gi, ecat_i, contract, precision=hi, preferred_element_type=F32))
            for k in range(CHUNK):
                if d == 0:
                    shift, keep = SSM_CH * k, lane_t >= SSM_CH * k
                else:
                    shift, keep = (SSM_CH * (k + 1)) % width, lane_t < SSM_CH * (k + 1)
                blk = jnp.where(keep, pltpu.roll(krow, shift, 1) if shift else krow, 0.0)
                toeplitz[g][k] = blk if d == 0 else toeplitz[g][k] + blk
    for g in range(PAIR):
        t_ref[g] = jnp.concatenate(toeplitz[g], axis=0).astype(BF16)


def _s5prep(lam, ldt, bt, c, w32):
    npair = lam.shape[1]
    width = CHUNK * SSM_CH
    pw = PAIR * width
    cast_in, cast_out, cast_shape = _cast_specs([w32], npair, lambda i: i)
    return pl.pallas_call(
        _s5prep_body,
        grid=(npair,),
        in_specs=[pl.BlockSpec((2, None, 2, 1, LANES), lambda i: (0, i, 0, 0, 0)),
                  pl.BlockSpec((2, None, 1, LANES), lambda i: (0, i, 0, 0)),
                  pl.BlockSpec((2, None, 2, SSM_CH, LANES), lambda i: (0, i, 0, 0, 0)),
                  pl.BlockSpec((2, None, 2, SSM_CH, LANES), lambda i: (0, i, 0, 0, 0))] + cast_in,
        out_specs=[pl.BlockSpec((None, pw, 4 * LANES), lambda i: (i, 0, 0)),
                   pl.BlockSpec((None, pw, 4 * LANES), lambda i: (i, 0, 0)),
                   pl.BlockSpec((PAIR, width, width), lambda i: (i, 0, 0)),
                   pl.BlockSpec((None, 4, SUBLANES, LANES), lambda i: (i, 0, 0, 0))] + cast_out,
        out_shape=[jax.ShapeDtypeStruct((npair, pw, 4 * LANES), BF16),
                   jax.ShapeDtypeStruct((npair, pw, 4 * LANES), BF16),
                   jax.ShapeDtypeStruct((npair * PAIR, width, width), BF16),
                   jax.ShapeDtypeStruct((npair, 4, SUBLANES, LANES), F32)] + cast_shape,
        compiler_params=_params(("parallel",)),
        name="s5prep",
    )(lam, ldt, bt, c, w32)


def _block_transpose(xs):
    n = len(xs)
    assert n * SSM_CH == LANES and n & (n - 1) == 0
    blk = lax.broadcasted_iota(jnp.int32, xs[0].shape, 1) // SSM_CH
    d = n // 2
    while d:
        upper = (blk & d) != 0
        ys = list(xs)
        for a in range(n):
            if a & d:
                continue
            b = a + d
            ys[a] = jnp.where(upper, pltpu.roll(xs[b], SSM_CH * d, 1), xs[a])
            ys[b] = jnp.where(upper, xs[b], pltpu.roll(xs[a], LANES - SSM_CH * d, 1))
        xs = ys
        d //= 2
    return xs


def _s5_scan(a_ref, w_scr, s_scr):
    rows = w_scr.shape[0]
    nsteps = rows // SUBLANES
    nstream = w_scr.shape[1] // (2 * LANES)
    half = SUBLANES // 2
    lo = lax.broadcasted_iota(jnp.int32, (SUBLANES, LANES), 0) < half
    first = [lo, jnp.logical_not(lo)]
    coef = [(a_ref[s // 2, 2 * (s % 2)], a_ref[s // 2, 2 * (s % 2) + 1]) for s in range(nstream)]
    col = lambda i: slice(LANES * i, LANES * (i + 1))

    def advance(a, xr, xi, wr, wi):
        return a[0] * xr - a[1] * xi + wr, a[0] * xi + a[1] * xr + wi

    def swap(x):
        return pltpu.roll(x, half, 0)

    def step(i, carry):
        out = []
        for s in range(nstream):
            d = s % 2
            cr, ci = carry[2 * s], carry[2 * s + 1]
            vreg = i if d == 0 else nsteps - 1 - i
            rs = pl.ds(pl.multiple_of(vreg * SUBLANES, SUBLANES), SUBLANES)
            wr, wi = w_scr[rs, col(2 * s)], w_scr[rs, col(2 * s + 1)]
            r0r, r0i = swap(cr), swap(ci)
            t1r, t1i = advance(coef[s], r0r, r0i, wr, wi)
            r1r, r1i = swap(t1r), swap(t1i)
            t2r, t2i = advance(coef[s], r1r, r1i, wr, wi)
            s_scr[rs, col(2 * s)] = jnp.where(first[d], r0r, r1r)
            s_scr[rs, col(2 * s + 1)] = jnp.where(first[d], r0i, r1i)
            out += [jnp.where(first[d], t1r, t2r), jnp.where(first[d], t1i, t2i)]
        return tuple(out)

    z = jnp.zeros((SUBLANES, LANES), F32)
    lax.fori_loop(0, nsteps, step, (z,) * (2 * nstream))


S5_SLAB_GROUPS = LANES // SSM_CH
S5_ROW_CHUNK = 256
S5_SUB_ROWS = 256
S5_SCAN_PAIRS = 2
S5_TOK_PITCH = CHUNK + SUBLANES


def _s5_body(u_ref, wb_ref, t_ref, wct_ref, a_ref, y_ref, ur_scr, w_scr, s_scr, sall_scr, yr_scr, tok_scr):
    rows = ur_scr.shape[0]
    width = CHUNK * SSM_CH
    pw = PAIR * width
    npair = S5_SLAB_GROUPS // PAIR
    half_steps = CHUNK // 2
    tok_rows = S5_ROW_CHUNK * CHUNK

    nchunks = rows // S5_ROW_CHUNK
    nsub = S5_ROW_CHUNK // S5_SUB_ROWS
    sw = 4 * LANES
    row_chunk = lambda c: slice(S5_ROW_CHUNK * c, S5_ROW_CHUNK * (c + 1))
    tok_chunk = lambda c: slice(tok_rows * c, tok_rows * (c + 1))
    step_rows = lambda i, step: pl.ds(S5_TOK_PITCH * S5_SUB_ROWS * i + step, S5_SUB_ROWS, stride=S5_TOK_PITCH)

    def copy_chunk_rows(c, tok, to_tok):
        for r in range(S5_ROW_CHUNK):
            ext = slice(tok_rows * c + r * CHUNK, tok_rows * c + (r + 1) * CHUNK)
            pad = slice(r * S5_TOK_PITCH, r * S5_TOK_PITCH + CHUNK)
            if to_tok:
                tok[pad, :] = u_ref[ext, :].astype(F32)
            else:
                y_ref[ext, :] = tok[pad, :].astype(BF16)
    lane_blk = lambda g, h: slice(width * g + LANES * h, width * g + LANES * (h + 1))

    def relayout_in(c):
        tok = tok_scr.at[c % 2]
        copy_chunk_rows(c, tok, True)
        for i in range(nsub):
            for h in range(2):
                xs = [tok[step_rows(i, half_steps * h + kk), :] for kk in range(half_steps)]
                ys = _block_transpose(xs)
                for g in range(S5_SLAB_GROUPS):
                    r0 = S5_ROW_CHUNK * c + S5_SUB_ROWS * i
                    ur_scr[r0:r0 + S5_SUB_ROWS, lane_blk(g, h)] = ys[g].astype(BF16)

    def state_in(c, p0):
        for q in range(S5_SCAN_PAIRS):
            pp = p0 + q
            w_scr[row_chunk(c), sw * q:sw * (q + 1)] = jnp.dot(
                ur_scr[row_chunk(c), pw * pp:pw * (pp + 1)], wb_ref[pp], preferred_element_type=F32)

    for p0 in range(0, npair, S5_SCAN_PAIRS):
        for c in range(nchunks):
            if p0 == 0:
                relayout_in(c)
            state_in(c, p0)
        _s5_scan(a_ref.at[p0:p0 + S5_SCAN_PAIRS], w_scr, s_scr)
        sall_scr[:, sw * p0:sw * (p0 + S5_SCAN_PAIRS)] = s_scr[...].astype(BF16)

    def outputs(c):
        yr, tok = yr_scr.at[c % 2], tok_scr.at[c % 2]
        for pp in range(npair):
            y = lax.dot_general(sall_scr[row_chunk(c), sw * pp:sw * (pp + 1)], wct_ref[pp],
                                (((1,), (1,)), ((), ())), preferred_element_type=F32)
            for g in range(PAIR):
                lo = pw * pp + width * g
                yr[:, lo:lo + width] = y[:, width * g:width * (g + 1)] + jnp.dot(
                    ur_scr[row_chunk(c), lo:lo + width], t_ref[PAIR * pp + g], preferred_element_type=F32)
        for i in range(nsub):
            for h in range(2):
                xs = [yr[S5_SUB_ROWS * i:S5_SUB_ROWS * (i + 1), lane_blk(g, h)].astype(BF16)
                      for g in range(S5_SLAB_GROUPS)]
                ys = _block_transpose(xs)
                for kk in range(half_steps):
                    tok[step_rows(i, half_steps * h + kk), :] = ys[kk].astype(F32)
        copy_chunk_rows(c, tok, False)

    for c in range(nchunks):
        outputs(c)


def _s5(u_tok, wb, tsum, wct, a16):
    T, ssm_w = u_tok.shape
    rows = T // CHUNK
    assert rows % S5_ROW_CHUNK == 0 and S5_SLAB_GROUPS == CHUNK // 2
    width = CHUNK * SSM_CH
    pw = PAIR * width
    npair = S5_SLAB_GROUPS // PAIR
    slab = pl.BlockSpec((T, LANES), lambda s: (0, s))
    return pl.pallas_call(
        _s5_body,
        grid=(ssm_w // LANES,),
        in_specs=[slab,
                  pl.BlockSpec((npair, pw, 4 * LANES), lambda s: (s, 0, 0)),
                  pl.BlockSpec((S5_SLAB_GROUPS, width, width), lambda s: (s, 0, 0)),
                  pl.BlockSpec((npair, pw, 4 * LANES), lambda s: (s, 0, 0)),
                  pl.BlockSpec((npair, 4, SUBLANES, LANES), lambda s: (s, 0, 0, 0))],
        out_specs=slab,
        out_shape=jax.ShapeDtypeStruct((T, ssm_w), BF16),
        scratch_shapes=[pltpu.VMEM((rows, S5_SLAB_GROUPS * width), BF16),
                        pltpu.VMEM((rows, S5_SCAN_PAIRS * 4 * LANES), F32),
                        pltpu.VMEM((rows, S5_SCAN_PAIRS * 4 * LANES), F32),
                        pltpu.VMEM((rows, npair * 4 * LANES), BF16),
                        pltpu.VMEM((2, S5_ROW_CHUNK, S5_SLAB_GROUPS * width), F32),
                        pltpu.VMEM((2, S5_ROW_CHUNK * S5_TOK_PITCH, LANES), F32)],
        compiler_params=_params(("parallel",)),
        name="s5",
    )(u_tok, wb, tsum, wct, a16)


def _merge_body(attn_ref, ys_ref, u_ref, g_ref, x_ref, wba_ref, wglu_ref, bglu_ref, d_ref, wbs_ref,
                wout_ref, gpost_ref, gpre_ref, x1_ref, h2_ref):
    tm, d = x_ref.shape
    u = u_ref[...].reshape(tm, -1).astype(F32)
    y = ys_ref[...].reshape(tm, -1).astype(F32) + d_ref[...] * u
    y = _gelu_tanh(y)
    gl = jnp.dot(y.astype(BF16), wglu_ref[...], preferred_element_type=F32) + bglu_ref[...]
    s = y * jax.nn.sigmoid(gl)
    y_ssm = jnp.dot(s.astype(BF16), wbs_ref[...], preferred_element_type=F32)
    y_attn = jnp.dot(attn_ref[...], wba_ref[...], preferred_element_type=F32)
    merged = g_ref[:, :d].astype(F32) * y_attn + g_ref[:, d:].astype(F32) * y_ssm
    o = jnp.dot(merged.astype(BF16), wout_ref[...], preferred_element_type=F32)
    x1 = x_ref[...] + _rms(o, gpost_ref[...])
    x1_ref[...] = x1
    h2_ref[...] = _rms(x1, gpre_ref[...]).astype(BF16)


def _merge(attn, ys, u, gates, x2, wba, wglu, bglu, ssm_d, wbs, wout, gpost, gpre, seq, tm):
    T, D = x2.shape
    row = lambda w: pl.BlockSpec((tm, w), lambda i: (i, 0))
    chunked = _chunk_rows_spec(tm, seq // tm, u.shape[-1])
    consts = [wba, wglu, bglu, ssm_d, wbs, wout, gpost, gpre]
    return pl.pallas_call(
        _merge_body,
        grid=(T // tm,),
        in_specs=[row(attn.shape[1]), chunked, chunked, row(gates.shape[1]), row(D)]
                 + [_const_spec(c.shape) for c in consts],
        out_specs=[row(D), row(D)],
        out_shape=[jax.ShapeDtypeStruct((T, D), F32), jax.ShapeDtypeStruct((T, D), BF16)],
        compiler_params=_params(("parallel",)),
        name="merge",
    )(attn, ys, u, gates, x2, *consts)


def _ffn_body(tiles_per_seq, hp_ref, hc_ref, hn_ref, x1_ref, wg_ref, wu_ref, cw_ref, cb_ref, wd_ref,
              gn_ref, o_ref, hext, fscr):
    tm = hc_ref.shape[0]
    i = pl.program_id(0) % tiles_per_seq
    zero = jnp.zeros((), BF16)
    hext[0:HALO] = jnp.where(i == 0, zero, hp_ref[...])
    hext[HALO:HALO + tm] = hc_ref[...]
    hext[HALO + tm:] = jnp.where(i == tiles_per_seq - 1, zero, hn_ref[...])
    ext = tm + 2 * HALO
    for c in range(wg_ref.shape[1] // FF_TILE):
        sl = slice(FF_TILE * c, FF_TILE * (c + 1))
        g = jnp.dot(hext[...], wg_ref[:, sl], preferred_element_type=F32)
        up = jnp.dot(hc_ref[...], wu_ref[:, sl], preferred_element_type=F32)
        g_prev = pltpu.roll(g, 1, 0)[HALO:HALO + tm]
        g_next = pltpu.roll(g, ext - 1, 0)[HALO:HALO + tm]
        conv = (g_prev * cw_ref[0:1, sl] + g[HALO:HALO + tm] * cw_ref[1:2, sl]
                + g_next * cw_ref[2:3, sl] + cb_ref[:, sl])
        fscr[:, sl] = (_gelu_tanh(conv) * up).astype(BF16)
    o = jnp.dot(fscr[...], wd_ref[...], preferred_element_type=F32)
    o_ref[...] = x1_ref[...] + _rms(o, gn_ref[...])


def _ffn(h2, x1, wg, wu, cw, cb, wd, gn, seq, tm):
    T, D = x1.shape
    dff = wg.shape[1]
    assert dff % FF_TILE == 0 and tm % HALO == 0 and seq % tm == 0
    per = tm // HALO
    nh = T // HALO
    row = lambda w: pl.BlockSpec((tm, w), lambda i: (i, 0))
    once = lambda shape: pl.BlockSpec(shape, lambda i: (0,) * len(shape), pipeline_mode=pl.Buffered(1))
    return pl.pallas_call(
        functools.partial(_ffn_body, seq // tm),
        grid=(T // tm,),
        in_specs=[pl.BlockSpec((HALO, D), lambda i: (jnp.maximum(i * per - 1, 0), 0)),
                  row(D),
                  pl.BlockSpec((HALO, D), lambda i: (jnp.minimum((i + 1) * per, nh - 1), 0)),
                  row(D),
                  once(wg.shape), once(wu.shape), _const_spec(cw.shape), _const_spec(cb.shape),
                  once(wd.shape), _const_spec(gn.shape)],
        out_specs=row(D),
        out_shape=jax.ShapeDtypeStruct((T, D), F32),
        scratch_shapes=[pltpu.VMEM((tm + 2 * HALO, D), BF16), pltpu.VMEM((tm, dff), BF16)],
        compiler_params=_params(("parallel",)),
        name="ffn",
    )(h2, h2, h2, x1, wg, wu, cw, cb, wd, gn)


def _row_tile(seq):
    return min(1024, seq)


def _layer(x, p):
    bsz, seq, d = x.shape
    T = bsz * seq
    tm = _row_tile(seq)
    ssm_w = d // 2
    ngroups = ssm_w // SSM_CH
    row2 = lambda v: v.reshape(1, -1).astype(F32)

    npair = ngroups // PAIR
    both = lambda re, im: jnp.stack([jnp.stack([p[re + "_f"], p[im + "_f"]]),
                                     jnp.stack([p[re + "_b"], p[im + "_b"]])]).astype(F32)
    lam = both("lam_re", "lam_im").reshape(2, 2, npair, 1, LANES).transpose(0, 2, 1, 3, 4)
    ldt = jnp.repeat(jnp.stack([p["log_dt_f"], p["log_dt_b"]]).astype(F32), SSM_STATE, axis=-1)
    ldt = ldt.reshape(2, npair, 1, LANES)
    bt = both("b_re", "b_im").reshape(2, 2, npair, PAIR, SSM_STATE, SSM_CH)
    bt = bt.transpose(0, 2, 1, 5, 3, 4).reshape(2, npair, 2, SSM_CH, LANES)
    cc = both("c_re", "c_im").reshape(2, 2, npair, PAIR, SSM_CH, SSM_STATE)
    cc = cc.transpose(0, 2, 1, 4, 3, 5).reshape(2, npair, 2, SSM_CH, LANES)
    wb, wct, tsum, a, w_in = _s5prep(lam, ldt, bt, cc, p["w_in"])

    x2 = x.reshape(T, d)
    q, k, ks, v, u, gates = _inproj(x2, row2(p["norm_mix_pre"]), w_in, bsz, seq, tm)

    later = ("w_ffn_gate", "w_ffn_up", "w_ffn_down", "w_out", "w_branch_attn", "w_branch_ssm", "w_glu")
    attn, *rounded = _attention(q, k, ks, v, jnp.asarray(_attn_bias()), p["attn_sink"].astype(F32), bsz, seq,
                                [p[name] for name in later])
    w16 = dict(zip(later, rounded))

    assert bsz == SUBLANES // 2
    ys = _s5(u.reshape(T, ssm_w), wb, tsum, wct, a).reshape(u.shape)

    x1, h2 = _merge(attn, ys, u, gates, x2,
                    w16["w_branch_attn"], w16["w_glu"], row2(p["b_glu"]),
                    row2(p["ssm_d"]), w16["w_branch_ssm"], w16["w_out"],
                    row2(p["norm_mix_post"]), row2(p["norm_ffn_pre"]), seq, tm)

    out = _ffn(h2, x1, w16["w_ffn_gate"], w16["w_ffn_up"], p["conv_w"].astype(F32),
               row2(p["conv_b"]), w16["w_ffn_down"], row2(p["norm_ffn_post"]), seq, tm)
    return out.reshape(bsz, seq, d)


_PARAM_NAMES = (
    "norm_mix_pre", "w_in", "attn_sink",
    "lam_re_f", "lam_im_f", "log_dt_f", "b_re_f", "b_im_f", "c_re_f", "c_im_f",
    "lam_re_b", "lam_im_b", "log_dt_b", "b_re_b", "b_im_b", "c_re_b", "c_im_b",
    "ssm_d", "w_glu", "b_glu", "w_branch_attn", "w_branch_ssm", "w_out", "norm_mix_post",
    "norm_ffn_pre", "w_ffn_gate", "w_ffn_up", "conv_w", "conv_b", "w_ffn_down", "norm_ffn_post")


def kernel(x, norm_mix_pre, w_in, attn_sink, lam_re_f, lam_im_f, log_dt_f, b_re_f, b_im_f, c_re_f, c_im_f, lam_re_b, lam_im_b, log_dt_b, b_re_b, b_im_b, c_re_b, c_im_b, ssm_d, w_glu, b_glu, w_branch_attn, w_branch_ssm, w_out, norm_mix_post, norm_ffn_pre, w_ffn_gate, w_ffn_up, conv_w, conv_b, w_ffn_down, norm_ffn_post):
    stacked = dict(zip(_PARAM_NAMES, (
        norm_mix_pre, w_in, attn_sink,
        lam_re_f, lam_im_f, log_dt_f, b_re_f, b_im_f, c_re_f, c_im_f,
        lam_re_b, lam_im_b, log_dt_b, b_re_b, b_im_b, c_re_b, c_im_b,
        ssm_d, w_glu, b_glu, w_branch_attn, w_branch_ssm, w_out, norm_mix_post,
        norm_ffn_pre, w_ffn_gate, w_ffn_up, conv_w, conv_b, w_ffn_down, norm_ffn_post)))
    for layer in range(w_in.shape[0]):
        x = _layer(x, {name: value[layer] for name, value in stacked.items()})
    return x
```
